```python
import jax
import jax.numpy as jnp
from jax import lax
import numpy as np

D_MODEL = 1024
BATCH = 8
SEQ = 8192
DEPTH = 2

D_FF = 2816
LN_EPS = 1e-5
RMS_EPS = 1e-6
DEEPNORM_ALPHA = (2 * DEPTH) ** 0.25
DEEPNORM_BETA = (8 * DEPTH) ** -0.25
HALF_STEP = 0.5
N_EVEN = (DEPTH + 1) // 2
N_ODD = DEPTH // 2
NEG = -1e30
BIG = 1e9
ALIBI_MAX_BIAS = 8.0

MLA_HEADS = 8
MLA_Q_RANK = 384
MLA_KV_RANK = 256
MLA_NOPE = 64
MLA_ROPE = 32
MLA_V = 64
ROPE_THETA = 10000.0
MLA_BLOCK = 128

NSA_HEADS = 8
NSA_KV_GROUPS = 2
NSA_HEADS_PER_GROUP = NSA_HEADS // NSA_KV_GROUPS
NSA_HEAD_DIM = 64
CMP_LEN = 32
CMP_STRIDE = 16
CMP_HIDDEN = 256
SEL_BLOCK = 64
SEL_TOPK = 16
WIN = 512
NSA_CHUNK = SEL_BLOCK

DIL_PATTERNS = ((128, 1), (512, 4), (2048, 16))
DIL_GROUPS = 3
DIL_HEADS_PER_GROUP = 4
DIL_HEAD_DIM = 128
DIL_CHUNK = 128
DIL_MAX_WIN = 2048

NSA_KV_WIDTH = NSA_KV_GROUPS * NSA_HEAD_DIM
EVEN_SPLITS = (MLA_Q_RANK, MLA_KV_RANK, MLA_ROPE, NSA_HEADS * NSA_HEAD_DIM) + (NSA_KV_WIDTH,) * 6 + (3 * NSA_HEADS,)
EVEN_IN = sum(EVEN_SPLITS)
EVEN_OUT = MLA_HEADS * MLA_V + NSA_HEADS * NSA_HEAD_DIM
ODD_IN = 3 * DIL_GROUPS * DIL_HEADS_PER_GROUP * DIL_HEAD_DIM
ODD_OUT = DIL_HEADS_PER_GROUP * DIL_HEAD_DIM

kernel_name = "hybrid_mla_nsa_dilated_macaron_deepnorm"


def layer_norm(x, g, b):
    x32 = x.astype(jnp.float32)
    mu = jnp.mean(x32, axis=-1, keepdims=True)
    var = jnp.mean(jnp.square(x32 - mu), axis=-1, keepdims=True)
    y = (x32 - mu) * lax.rsqrt(var + LN_EPS)
    return (y * g.astype(jnp.float32) + b.astype(jnp.float32)).astype(x.dtype)


def rms_norm(x, g):
    x32 = x.astype(jnp.float32)
    y = x32 * lax.rsqrt(jnp.mean(x32 * x32, axis=-1, keepdims=True) + RMS_EPS)
    return (y * g.astype(jnp.float32)).astype(x.dtype)


def swiglu(x, w_gate, w_up, w_down):
    return (jax.nn.silu(x @ w_gate) * (x @ w_up)) @ w_down


def alibi_slopes(n):
    return 2.0 ** (-ALIBI_MAX_BIAS * jnp.arange(1, n + 1, dtype=jnp.float32) / n)


def masked_softmax(s, mask):
    s = jnp.where(mask, s, NEG)
    m = jnp.max(s, axis=-1, keepdims=True)
    e = jnp.where(mask, jnp.exp(s - m), 0.0)
    den = jnp.sum(e, axis=-1, keepdims=True)
    return e / jnp.maximum(den, 1e-30)


def rope(x, pos):
    half = x.shape[-1] // 2
    inv = ROPE_THETA ** (-jnp.arange(half, dtype=jnp.float32) / half)
    ang = pos.astype(jnp.float32)[..., None] * inv
    ang = ang.reshape(ang.shape[:2] + (1,) * (x.ndim - 3) + (half,))
    cos, sin = jnp.cos(ang), jnp.sin(ang)
    x32 = x.astype(jnp.float32)
    x1, x2 = x32[..., :half], x32[..., half:]
    return jnp.concatenate([x1 * cos - x2 * sin, x2 * cos + x1 * sin], axis=-1).astype(x.dtype)


def mla_attention(c_q, c_kv, k_pe, pos, q_norm_g, kv_norm_g, w_uq, w_uk, w_uv):
    B, S, _ = c_kv.shape
    cq = rms_norm(c_q, q_norm_g)
    ckv = rms_norm(c_kv, kv_norm_g)
    q = (cq @ w_uq).reshape(B, S, MLA_HEADS, MLA_NOPE + MLA_ROPE)
    q_nope = q[..., :MLA_NOPE]
    q_pe = rope(q[..., MLA_NOPE:], pos)
    k_nope = (ckv @ w_uk).reshape(B, S, MLA_HEADS, MLA_NOPE)
    v = (ckv @ w_uv).reshape(B, S, MLA_HEADS, MLA_V)
    k_rot = rope(k_pe, pos)
    scale = (MLA_NOPE + MLA_ROPE) ** -0.5
    key_idx = jnp.arange(S)

    def block(i):
        start = i * MLA_BLOCK
        qn = lax.dynamic_slice_in_dim(q_nope, start, MLA_BLOCK, axis=1)
        qp = lax.dynamic_slice_in_dim(q_pe, start, MLA_BLOCK, axis=1)
        s = (jnp.einsum('bqhd,bkhd->bhqk', qn, k_nope)
             + jnp.einsum('bqhr,bkr->bhqk', qp, k_rot)).astype(jnp.float32) * scale
        q_idx = start + jnp.arange(MLA_BLOCK)
        p = masked_softmax(s, q_idx[:, None] >= key_idx[None, :])
        return jnp.einsum('bhqk,bkhd->bqhd', p.astype(v.dtype), v)

    out = lax.map(block, jnp.arange(S // MLA_BLOCK))
    return out.transpose(1, 0, 2, 3, 4).reshape(B, S, MLA_HEADS * MLA_V)


def nsa_attention(q, k_cmp, v_cmp, k_sel, v_sel, k_win, v_win, gate_logits, pos,
                  cmp_pos, cmp_k_w1, cmp_k_w2, cmp_v_w1, cmp_v_w2):
    B, S, _ = q.shape
    G, HPG, DH = NSA_KV_GROUPS, NSA_HEADS_PER_GROUP, NSA_HEAD_DIM
    NC = S // CMP_STRIDE - 1
    NS = S // SEL_BLOCK
    SEL_K = min(SEL_TOPK, NS)
    scale = DH ** -0.5
    q = q.reshape(B, S, G, HPG, DH)
    slopes = alibi_slopes(NSA_HEADS).reshape(G, HPG)[:, :, None, None]
    pos_f = pos.astype(jnp.float32)

    def compress(kv, w1, w2):
        halves = kv.reshape(B, S // CMP_STRIDE, CMP_STRIDE, G, DH)
        blocks = jnp.concatenate([halves[:, :-1], halves[:, 1:]], axis=2) + cmp_pos[:, None, :]
        flat = blocks.transpose(0, 1, 3, 2, 4).reshape(B, NC, G, CMP_LEN * DH)
        return jax.nn.gelu(flat @ w1) @ w2

    kc = compress(k_cmp, cmp_k_w1, cmp_k_w2)
    vc = compress(v_cmp, cmp_v_w1, cmp_v_w2)
    cend_idx = jnp.arange(NC) * CMP_STRIDE + (CMP_LEN - 1)
    pos_cend = pos_f[:, CMP_LEN - 1::CMP_STRIDE]
    cstart = jnp.arange(NC)[:, None] * CMP_STRIDE
    sstart = jnp.arange(NS)[None, :] * SEL_BLOCK
    overlap = ((cstart < sstart + SEL_BLOCK) & (cstart + CMP_LEN > sstart)).astype(jnp.float32)

    ks = k_sel.reshape(B, NS, SEL_BLOCK, G, DH).transpose(0, 3, 1, 2, 4)
    vs = v_sel.reshape(B, NS, SEL_BLOCK, G, DH).transpose(0, 3, 1, 2, 4)
    pos_sel = pos_f.reshape(B, NS, SEL_BLOCK)
    pad = ((0, 0), (WIN, 0), (0, 0), (0, 0))
    kw = jnp.pad(k_win.reshape(B, S, G, DH), pad)
    vw = jnp.pad(v_win.reshape(B, S, G, DH), pad)
    pos_win = jnp.pad(pos_f, ((0, 0), (WIN, 0)))
    gates = jax.nn.sigmoid(gate_logits.reshape(B, S, 3, G, HPG))
    b_ix = jnp.arange(B)[:, None, None, None]
    g_ix = jnp.arange(G)[None, :, None, None]
    blk = jnp.arange(NS)
    J = SEL_K * SEL_BLOCK
    L = WIN + NSA_CHUNK

    def chunk(c):
        start = c * NSA_CHUNK
        t = start + jnp.arange(NSA_CHUNK)
        qc = lax.dynamic_slice_in_dim(q, start, NSA_CHUNK, axis=1)
        pq = lax.dynamic_slice_in_dim(pos_f, start, NSA_CHUNK, axis=1)
        s = jnp.einsum('bqgnd,bkgd->bgnqk', qc, kc).astype(jnp.float32) * scale
        s = s - slopes * (pq[:, :, None] - pos_cend[:, None, :])[:, None, None]
        p_c = masked_softmax(s, cend_idx[None, :] <= t[:, None])
        o_c = jnp.einsum('bgnqk,bkgd->bqgnd', p_c.astype(vc.dtype), vc)
        imp = jnp.einsum('bgnqk,ks->bgqs', p_c, overlap)
        imp = jnp.where((blk == 0) | (blk == c), BIG, imp)
        imp = jnp.where(blk <= c, imp, NEG)
        _, idx = lax.top_k(imp, SEL_K)
        valid = idx <= c
        kg = ks[b_ix, g_ix, idx].reshape(B, G, NSA_CHUNK, J, DH)
        vg = vs[b_ix, g_ix, idx].reshape(B, G, NSA_CHUNK, J, DH)
        kpos = pos_sel[b_ix, idx].reshape(B, G, NSA_CHUNK, J)
        kidx = idx[..., None] * SEL_BLOCK + jnp.arange(SEL_BLOCK)
        mask_s = (valid[..., None] & (kidx <= t[:, None, None])).reshape(B, G, NSA_CHUNK, J)
        s = jnp.einsum('bqgnd,bgqjd->bgnqj', qc, kg).astype(jnp.float32) * scale
        s = s - slopes * (pq[:, None, :, None] - kpos)[:, :, None]
        p_s = masked_softmax(s, mask_s[:, :, None])
        o_s = jnp.einsum('bgnqj,bgqjd->bqgnd', p_s.astype(vg.dtype), vg)
        kwc = lax.dynamic_slice_in_dim(kw, start, L, axis=1)
        vwc = lax.dynamic_slice_in_dim(vw, start, L, axis=1)
        pwc = lax.dynamic_slice_in_dim(pos_win, start, L, axis=1)
        widx = start - WIN + jnp.arange(L)
        diff = t[:, None] - widx[None, :]
        mask_w = (widx[None, :] >= 0) & (diff >= 0) & (diff < WIN)
        s = jnp.einsum('bqgnd,bkgd->bgnqk', qc, kwc).astype(jnp.float32) * scale
        s = s - slopes * (pq[:, :, None] - pwc[:, None, :])[:, None, None]
        p_w = masked_softmax(s, mask_w)
        o_w = jnp.einsum('bgnqk,bkgd->bqgnd', p_w.astype(vwc.dtype), vwc)
        gc = lax.dynamic_slice_in_dim(gates, start, NSA_CHUNK, axis=1)
        return (gc[:, :, 0, :, :, None] * o_c + gc[:, :, 1, :, :, None] * o_s
                + gc[:, :, 2, :, :, None] * o_w)

    out = lax.map(chunk, jnp.arange(S // NSA_CHUNK))
    return out.transpose(1, 0, 2, 3, 4, 5).reshape(B, S, NSA_HEADS * DH)


def dilated_attention(qkv, pos):
    B, S, _ = qkv.shape
    GH, DH, Q = DIL_HEADS_PER_GROUP, DIL_HEAD_DIM, DIL_CHUNK
    scale = DH ** -0.5
    qkv = qkv.reshape(B, S, 3, DIL_GROUPS, GH, DH)
    pad = ((0, 0), (DIL_MAX_WIN, 0), (0, 0), (0, 0))
    pos_f = pos.astype(jnp.float32)
    pos_pad = jnp.pad(pos_f, ((0, 0), (DIL_MAX_WIN, 0)))
    slopes = alibi_slopes(DIL_GROUPS * GH).reshape(DIL_GROUPS, GH)
    qs = [qkv[:, :, 0, g] for g in range(DIL_GROUPS)]
    ks = [jnp.pad(qkv[:, :, 1, g], pad) for g in range(DIL_GROUPS)]
    vs = [jnp.pad(qkv[:, :, 2, g], pad) for g in range(DIL_GROUPS)]

    def chunk(c):
        start = c * Q
        pq = lax.dynamic_slice_in_dim(pos_f, start, Q, axis=1)
        outs, lses = [], []
        for g, (w, dil) in enumerate(DIL_PATTERNS):
            L = w + Q
            off = start + DIL_MAX_WIN - w
            qg = lax.dynamic_slice_in_dim(qs[g], start, Q, axis=1).reshape(B, Q // dil, dil, GH, DH)
            kg = lax.dynamic_slice_in_dim(ks[g], off, L, axis=1).reshape(B, L // dil, dil, GH, DH)
            vg = lax.dynamic_slice_in_dim(vs[g], off, L, axis=1).reshape(B, L // dil, dil, GH, DH)
            pk = lax.dynamic_slice_in_dim(pos_pad, off, L, axis=1).reshape(B, L // dil, dil)
            dist = (pq.reshape(B, Q // dil, dil).transpose(0, 2, 1)[:, :, :, None]
                    - pk.transpose(0, 2, 1)[:, :, None, :])
            s = jnp.einsum('bjrhd,birhd->bhrji', qg, kg).astype(jnp.float32) * scale
            s = s - slopes[g][:, None, None, None] * dist[:, None]
            jj = jnp.arange(Q // dil)[:, None]
            ii = jnp.arange(L // dil)[None, :]
            kidx = start - w + ii * dil + jnp.arange(dil)[:, None, None]
            mask = (ii >= jj) & (ii <= jj + w // dil) & (kidx >= 0)
            s = jnp.where(mask, s, NEG)
            m = jnp.max(s, axis=-1, keepdims=True)
            e = jnp.exp(s - m)
            den = jnp.sum(e, axis=-1, keepdims=True)
            o = jnp.einsum('bhrji,birhd->bjrhd', (e / den).astype(vg.dtype), vg)
            outs.append(o.reshape(B, Q, GH, DH))
            lses.append((m + jnp.log(den))[..., 0].transpose(0, 3, 2, 1).reshape(B, Q, GH))
        wts = jax.nn.softmax(jnp.stack(lses), axis=0)
        out = jnp.einsum('gbqh,gbqhd->bqhd', wts.astype(outs[0].dtype), jnp.stack(outs))
        return out.reshape(B, Q, GH * DH)

    out = lax.map(chunk, jnp.arange(S // Q))
    return out.transpose(1, 0, 2, 3).reshape(B, S, ODD_OUT)


def even_mixer(x, pos, w_in, q_norm_g, kv_norm_g, w_uq, w_uk, w_uv,
               cmp_pos, cmp_k_w1, cmp_k_w2, cmp_v_w1, cmp_v_w2, w_out):
    h = x @ w_in
    cuts = np.cumsum(EVEN_SPLITS)[:-1].tolist()
    c_q, c_kv, k_pe, q, k_c, v_c, k_s, v_s, k_w, v_w, gate_logits = jnp.split(h, cuts, axis=-1)
    o_mla = mla_attention(c_q, c_kv, k_pe, pos, q_norm_g, kv_norm_g, w_uq, w_uk, w_uv)
    o_nsa = nsa_attention(q, k_c, v_c, k_s, v_s, k_w, v_w, gate_logits, pos,
                          cmp_pos, cmp_k_w1, cmp_k_w2, cmp_v_w1, cmp_v_w2)
    return jnp.concatenate([o_mla, o_nsa], axis=-1) @ w_out


def odd_mixer(x, pos, w_in, w_out):
    return dilated_attention(x @ w_in, pos) @ w_out


def setup_inputs(seed: int = 0) -> dict:
    key = jax.random.key(seed)
    keys = iter(jax.random.split(key, 40))
    f32 = jnp.float32

    def nrm(shape, scale):
        return jax.random.normal(next(keys), shape, f32) * scale

    def gain(shape):
        return 1.0 + nrm(shape, 0.02)

    x = jax.random.normal(next(keys), (BATCH, SEQ, D_MODEL), f32)
    offset = jax.random.randint(next(keys), (BATCH, 1), 0, 4096, dtype=jnp.int32)
    positions = offset + jnp.arange(SEQ, dtype=jnp.int32)[None, :]
    return {
        "x": x,
        "positions": positions,
        "ln1_g": gain((DEPTH, D_MODEL)),
        "ln1_b": nrm((DEPTH, D_MODEL), 0.02),
        "ffn1_w_gate": nrm((DEPTH, D_MODEL, D_FF), D_MODEL ** -0.5),
        "ffn1_w_up": nrm((DEPTH, D_MODEL, D_FF), D_MODEL ** -0.5),
        "ffn1_w_down": nrm((DEPTH, D_FF, D_MODEL), DEEPNORM_BETA * D_FF ** -0.5),
        "mix_in_even": nrm((N_EVEN, D_MODEL, EVEN_IN), D_MODEL ** -0.5),
        "mla_q_norm": gain((N_EVEN, MLA_Q_RANK)),
        "mla_kv_norm": gain((N_EVEN, MLA_KV_RANK)),
        "mla_w_uq": nrm((N_EVEN, MLA_Q_RANK, MLA_HEADS * (MLA_NOPE + MLA_ROPE)), MLA_Q_RANK ** -0.5),
        "mla_w_uk": nrm((N_EVEN, MLA_KV_RANK, MLA_HEADS * MLA_NOPE), MLA_KV_RANK ** -0.5),
        "mla_w_uv": nrm((N_EVEN, MLA_KV_RANK, MLA_HEADS * MLA_V), MLA_KV_RANK ** -0.5),
        "nsa_cmp_pos": nrm((N_EVEN, CMP_LEN, NSA_HEAD_DIM), 0.5),
        "nsa_cmp_k_w1": nrm((N_EVEN, CMP_LEN * NSA_HEAD_DIM, CMP_HIDDEN), (CMP_LEN * NSA_HEAD_DIM) ** -0.5),
        "nsa_cmp_k_w2": nrm((N_EVEN, CMP_HIDDEN, NSA_HEAD_DIM), CMP_HIDDEN ** -0.5),
        "nsa_cmp_v_w1": nrm((N_EVEN, CMP_LEN * NSA_HEAD_DIM, CMP_HIDDEN), (CMP_LEN * NSA_HEAD_DIM) ** -0.5),
        "nsa_cmp_v_w2": nrm((N_EVEN, CMP_HIDDEN, NSA_HEAD_DIM), CMP_HIDDEN ** -0.5),
        "mix_out_even": nrm((N_EVEN, EVEN_OUT, D_MODEL), DEEPNORM_BETA * EVEN_OUT ** -0.5),
        "mix_in_odd": nrm((N_ODD, D_MODEL, ODD_IN), D_MODEL ** -0.5),
        "mix_out_odd": nrm((N_ODD, ODD_OUT, D_MODEL), DEEPNORM_BETA * ODD_OUT ** -0.5),
        "ln2_g": gain((DEPTH, D_MODEL)),
        "ln2_b": nrm((DEPTH, D_MODEL), 0.02),
        "ffn2_w_gate": nrm((DEPTH, D_MODEL, D_FF), D_MODEL ** -0.5),
        "ffn2_w_up": nrm((DEPTH, D_MODEL, D_FF), D_MODEL ** -0.5),
        "ffn2_w_down": nrm((DEPTH, D_FF, D_MODEL), DEEPNORM_BETA * D_FF ** -0.5),
        "ln3_g": gain((DEPTH, D_MODEL)),
        "ln3_b": nrm((DEPTH, D_MODEL), 0.02),
    }


def reference(x, positions, ln1_g, ln1_b, ffn1_w_gate, ffn1_w_up, ffn1_w_down,
              mix_in_even, mla_q_norm, mla_kv_norm, mla_w_uq, mla_w_uk, mla_w_uv,
              nsa_cmp_pos, nsa_cmp_k_w1, nsa_cmp_k_w2, nsa_cmp_v_w1, nsa_cmp_v_w2, mix_out_even,
              mix_in_odd, mix_out_odd, ln2_g, ln2_b,
              ffn2_w_gate, ffn2_w_up, ffn2_w_down, ln3_g, ln3_b):
    for i in range(DEPTH):
        j = i // 2
        x = layer_norm(DEEPNORM_ALPHA * x + HALF_STEP * swiglu(x, ffn1_w_gate[i], ffn1_w_up[i], ffn1_w_down[i]),
                       ln1_g[i], ln1_b[i])
        if i % 2 == 0:
            mix = even_mixer(x, positions, mix_in_even[j], mla_q_norm[j], mla_kv_norm[j],
                             mla_w_uq[j], mla_w_uk[j], mla_w_uv[j], nsa_cmp_pos[j],
                             nsa_cmp_k_w1[j], nsa_cmp_k_w2[j], nsa_cmp_v_w1[j], nsa_cmp_v_w2[j],
                             mix_out_even[j])
        else:
            mix = odd_mixer(x, positions, mix_in_odd[j], mix_out_odd[j])
        x = layer_norm(DEEPNORM_ALPHA * x + mix, ln2_g[i], ln2_b[i])
        x = layer_norm(DEEPNORM_ALPHA * x + HALF_STEP * swiglu(x, ffn2_w_gate[i], ffn2_w_up[i], ffn2_w_down[i]),
                       ln3_g[i], ln3_b[i])
    return x
```

```python
import functools

import numpy as np
import jax
import jax.numpy as jnp
from jax import lax
from jax.experimental import pallas as pl
from jax.experimental.pallas import tpu as pltpu

F32 = jnp.float32
BF16 = jnp.bfloat16

DEPTH = 2
LN_EPS = 1e-5
RMS_EPS = 1e-6
ALPHA = (2 * DEPTH) ** 0.25
HALF_STEP = 0.5
NEG = -1e30
BIG = 1e9
REMOVED = -3.0e38
ALIBI_MAX_BIAS = 8.0
LANE = 128

MLA_HEADS = 8
MLA_Q_RANK = 384
MLA_KV_RANK = 256
MLA_NOPE = 64
MLA_ROPE = 32
MLA_V = 64
ROPE_THETA = 10000.0
MLA_SCALE = (MLA_NOPE + MLA_ROPE) ** -0.5

NSA_HEADS = 8
NSA_GROUPS = 2
NSA_HPG = 4
NSA_DH = 64
CMP_LEN = 32
CMP_STRIDE = 16
CMP_HIDDEN = 256
SEL_BLOCK = 64
SEL_TOPK = 16
WIN = 512
NSA_SCALE = NSA_DH ** -0.5
NSA_SLOPES = tuple(2.0 ** (-ALIBI_MAX_BIAS * (i + 1) / NSA_HEADS) for i in range(NSA_HEADS))

DIL_PATTERNS = ((128, 1), (512, 4), (2048, 16))
DIL_GROUPS = 3
DIL_GH = 4
DIL_DH = 128
DIL_SCALE = DIL_DH ** -0.5

VMEM_LIMIT = 48 * 1024 * 1024


def _iota(shape, dim):
    return lax.broadcasted_iota(jnp.int32, shape, dim)


def _shr(x, pow2):
    return jnp.right_shift(x, int(pow2).bit_length() - 1)


def _dot(a, b):
    return jnp.dot(a, b, preferred_element_type=F32)


def _dot_nt(a, b):
    return lax.dot_general(a, b, (((1,), (1,)), ((), ())), preferred_element_type=F32)


def _const_spec(shape):
    zeros = (0,) * len(shape)
    return pl.BlockSpec(shape, lambda *_: zeros, pipeline_mode=pl.Buffered(1))


def _params(sem):
    return pltpu.CompilerParams(dimension_semantics=sem, vmem_limit_bytes=VMEM_LIMIT)


def _layer_norm(z, g, b):
    mu = jnp.mean(z, axis=-1, keepdims=True)
    zc = z - mu
    var = jnp.mean(zc * zc, axis=-1, keepdims=True)
    return zc * lax.rsqrt(var + LN_EPS) * g + b


def _rms_norm(z, g):
    return z * lax.rsqrt(jnp.mean(z * z, axis=-1, keepdims=True) + RMS_EPS) * g


def _online_softmax_step(s, mask, v, m_ref, l_ref, acc_ref, h, guard):
    if mask is not None:
        s = jnp.where(mask, s, NEG)
    m_old = m_ref[h]
    m_new = jnp.maximum(m_old, jnp.max(s, axis=-1, keepdims=True))
    p = jnp.exp(s - m_new)
    if guard:
        p = jnp.where(mask, p, 0.0)
    alpha = jnp.exp(m_old - m_new)
    l_ref[h] = alpha * l_ref[h] + jnp.sum(p, axis=-1, keepdims=True)
    acc_ref[h] = alpha * acc_ref[h] + _dot(p.astype(BF16), v)
    m_ref[h] = m_new


def _init_state(m_ref, l_ref, acc_ref):
    m_ref[...] = jnp.full(m_ref.shape, NEG, F32)
    l_ref[...] = jnp.zeros(l_ref.shape, F32)
    acc_ref[...] = jnp.zeros(acc_ref.shape, F32)


def _ffn_ln_kernel(x_ref, wg_ref, wu_ref, wd_ref, g_ref, b_ref, o_ref, *, n_chunks):
    x = x_ref[...]
    xb = x.astype(BF16)
    c = wg_ref.shape[1] // n_chunks
    y = None
    for i in range(n_chunks):
        gt = _dot(xb, wg_ref[:, i * c:(i + 1) * c])
        up = _dot(xb, wu_ref[:, i * c:(i + 1) * c])
        h = (gt * jax.nn.sigmoid(gt) * up).astype(BF16)
        part = _dot(h, wd_ref[i * c:(i + 1) * c, :])
        y = part if y is None else y + part
    o_ref[...] = _layer_norm(ALPHA * x + HALF_STEP * y, g_ref[...], b_ref[...])


def ffn_ln(x2, wg, wu, wd, g, b, tm=512):
    m, d = x2.shape
    tm = min(tm, m)
    row = pl.BlockSpec((tm, d), lambda i: (i, 0))
    return pl.pallas_call(
        functools.partial(_ffn_ln_kernel, n_chunks=2),
        grid=(m // tm,),
        in_specs=[row, _const_spec(wg.shape), _const_spec(wu.shape), _const_spec(wd.shape),
                  _const_spec((1, d)), _const_spec((1, d))],
        out_specs=row,
        out_shape=jax.ShapeDtypeStruct((m, d), F32),
        compiler_params=_params(("parallel",)),
        name="ffn_ln",
    )(x2, wg, wu, wd, g.reshape(1, d), b.reshape(1, d))


EVEN_X_COLS = (MLA_Q_RANK, MLA_KV_RANK, LANE, LANE, NSA_HEADS * LANE, 6 * LANE, LANE)
EVEN_X_OFFS = tuple(int(v) for v in np.cumsum((0,) + EVEN_X_COLS))


def _even_proj_kernel(x_ref, cos_ref, sin_ref, wx_ref, qg_ref, kvg_ref, wuq_ref, wuqs_ref, wuk_ref, wuv_ref,
                      qm_ref, km_ref, vm_ref, qn_ref, kv_ref, gate_ref):
    xb = x_ref[...].astype(BF16)
    cos = cos_ref[...]
    sin = sin_ref[...]

    def xdot(i):
        return _dot(xb, wx_ref[:, EVEN_X_OFFS[i]:EVEN_X_OFFS[i + 1]])

    cq = _rms_norm(xdot(0), qg_ref[...]).astype(BF16)
    ckv = _rms_norm(xdot(1), kvg_ref[...]).astype(BF16)
    k_rot = xdot(2) * cos + xdot(3) * sin
    for h in range(MLA_HEADS):
        sl = slice(h * LANE, (h + 1) * LANE)
        q = _dot(cq, wuq_ref[:, sl]) * cos + _dot(cq, wuqs_ref[:, sl]) * sin
        qm_ref[:, sl] = (q * MLA_SCALE).astype(BF16)
        km_ref[:, sl] = (_dot(ckv, wuk_ref[:, sl]) + k_rot).astype(BF16)
        vm_ref[:, sl] = _dot(ckv, wuv_ref[:, sl]).astype(BF16)
    qn_ref[...] = (xdot(4) * NSA_SCALE).astype(BF16)
    kv_ref[...] = xdot(5).astype(BF16)
    gate_ref[...] = jax.nn.sigmoid(xdot(6))


def _head_blocks(w, n_heads, width, offset=0):
    k = w.shape[0]
    w = w.reshape(k, n_heads, width)
    w = jnp.pad(w, ((0, 0), (0, 0), (offset, LANE - width - offset)))
    return w.reshape(k, n_heads * LANE)


def even_proj(x2, cos_t, sin_t, w_in, q_norm_g, kv_norm_g, w_uq, w_uk, w_uv, tm=256):
    m, d = x2.shape
    tm = min(tm, m)
    half = MLA_ROPE // 2
    cuts = np.cumsum((MLA_Q_RANK, MLA_KV_RANK, MLA_ROPE, NSA_HEADS * NSA_DH) + (NSA_GROUPS * NSA_DH,) * 6)
    cuts = [0] + [int(c) for c in cuts]
    w_cq, w_ckv, w_kpe, w_q = (w_in[:, cuts[i]:cuts[i + 1]] for i in range(4))
    w_kv6 = w_in[:, cuts[4]:cuts[10]]
    w_gate = w_in[:, cuts[10]:]
    w_kpe_sw = jnp.concatenate([-w_kpe[:, half:], w_kpe[:, :half]], axis=1)
    kpe_blk = jnp.pad(w_kpe, ((0, 0), (MLA_NOPE, LANE - MLA_NOPE - MLA_ROPE)))
    kpe_sw_blk = jnp.pad(w_kpe_sw, ((0, 0), (MLA_NOPE, LANE - MLA_NOPE - MLA_ROPE)))
    wq = w_q.reshape(d, NSA_GROUPS, NSA_HPG * NSA_DH)
    wq_blk = jnp.concatenate([_head_blocks(wq[:, g], NSA_HPG, NSA_DH, g * NSA_DH) for g in range(NSA_GROUPS)], axis=1)
    w_gate_blk = jnp.pad(w_gate, ((0, 0), (0, LANE - w_gate.shape[1])))
    wx = jnp.concatenate([w_cq, w_ckv, kpe_blk, kpe_sw_blk, wq_blk, w_kv6, w_gate_blk], axis=1).astype(BF16)

    qd = MLA_NOPE + MLA_ROPE
    uq = w_uq.reshape(MLA_Q_RANK, MLA_HEADS, qd)
    uq_sw = jnp.concatenate([jnp.zeros_like(uq[..., :MLA_NOPE]), -uq[..., MLA_NOPE + half:], uq[..., MLA_NOPE:MLA_NOPE + half]], axis=-1)
    wuq = _head_blocks(uq.reshape(MLA_Q_RANK, -1), MLA_HEADS, qd).astype(BF16)
    wuqs = _head_blocks(uq_sw.reshape(MLA_Q_RANK, -1), MLA_HEADS, qd).astype(BF16)
    wuk = _head_blocks(w_uk, MLA_HEADS, MLA_NOPE).astype(BF16)
    wuv = _head_blocks(w_uv, MLA_HEADS, MLA_V).astype(BF16)

    hw = MLA_HEADS * LANE
    row = lambda w: pl.BlockSpec((tm, w), lambda i: (i, 0))
    outs = pl.pallas_call(
        _even_proj_kernel,
        grid=(m // tm,),
        in_specs=[row(d), row(LANE), row(LANE), _const_spec(wx.shape), _const_spec((1, MLA_Q_RANK)),
                  _const_spec((1, MLA_KV_RANK)), _const_spec(wuq.shape), _const_spec(wuqs.shape),
                  _const_spec(wuk.shape), _const_spec(wuv.shape)],
        out_specs=[row(hw), row(hw), row(hw), row(hw), row(6 * LANE), row(LANE)],
        out_shape=[jax.ShapeDtypeStruct((m, hw), BF16)] * 4
        + [jax.ShapeDtypeStruct((m, 6 * LANE), BF16), jax.ShapeDtypeStruct((m, LANE), F32)],
        compiler_params=_params(("parallel",)),
        name="even_proj",
    )(x2, cos_t, sin_t, wx, q_norm_g.reshape(1, -1), kv_norm_g.reshape(1, -1), wuq, wuqs, wuk, wuv)
    return outs


def _causal_pairs(nq, tq, tk):
    qi, ki = [], []
    for i in range(nq):
        for j in range(((i + 1) * tq - 1) // tk + 1):
            qi.append(i)
            ki.append(j)
    return jnp.asarray(qi, jnp.int32), jnp.asarray(ki, jnp.int32)


def _mla_kernel(qi_tab, ki_tab, q_ref, k_ref, v_ref, o_ref, m_ref, l_ref, acc_ref, *, tq, tk):
    p = pl.program_id(1)
    qi = qi_tab[p]
    ki = ki_tab[p]

    @pl.when(ki == 0)
    def _():
        _init_state(m_ref, l_ref, acc_ref)

    def run(masked):
        mask = None
        if masked:
            mask = (qi * tq + _iota((tq, tk), 0)) >= (ki * tk + _iota((tq, tk), 1))
        for h in range(MLA_HEADS):
            sl = slice(h * LANE, (h + 1) * LANE)
            s = _dot_nt(q_ref[0, :, sl], k_ref[0, :, sl])
            _online_softmax_step(s, mask, v_ref[0, :, sl], m_ref, l_ref, acc_ref, h, guard=False)

    crosses = (ki + 1) * tk - 1 > qi * tq

    @pl.when(crosses)
    def _():
        run(True)

    @pl.when(jnp.logical_not(crosses))
    def _():
        run(False)

    @pl.when(ki == ((qi + 1) * tq - 1) // tk)
    def _():
        for h in range(MLA_HEADS):
            sl = slice(h * LANE, (h + 1) * LANE)
            o_ref[0, :, sl] = (acc_ref[h] / jnp.maximum(l_ref[h], 1e-30)).astype(BF16)


def mla_attention(q, k, v, tq=256, tk=512):
    b, s, hw = q.shape
    tq, tk = min(tq, s), min(tk, s)
    qi_tab, ki_tab = _causal_pairs(s // tq, tq, tk)
    qspec = pl.BlockSpec((1, tq, hw), lambda bb, p, qt, kt: (bb, qt[p], 0))
    kspec = pl.BlockSpec((1, tk, hw), lambda bb, p, qt, kt: (bb, kt[p], 0))
    return pl.pallas_call(
        functools.partial(_mla_kernel, tq=tq, tk=tk),
        grid_spec=pltpu.PrefetchScalarGridSpec(
            num_scalar_prefetch=2,
            grid=(b, int(qi_tab.shape[0])),
            in_specs=[qspec, kspec, kspec],
            out_specs=qspec,
            scratch_shapes=[pltpu.VMEM((MLA_HEADS, tq, 1), F32), pltpu.VMEM((MLA_HEADS, tq, 1), F32),
                            pltpu.VMEM((MLA_HEADS, tq, LANE), F32)],
        ),
        out_shape=jax.ShapeDtypeStruct((b, s, hw), BF16),
        compiler_params=_params(("parallel", "arbitrary")),
        name="mla_attention",
    )(qi_tab, ki_tab, q, k, v)


def _compress_kernel(h_ref, pos_ref, w1_ref, w2a_ref, w2b_ref, o_ref):
    n16 = h_ref.shape[2]
    half = w1_ref.shape[0] // 2
    bias = _dot(pos_ref[...], w1_ref[...])[0:1]
    out = None
    for g in range(NSA_GROUPS):
        hg = h_ref[0, g]
        first = _dot(hg, w1_ref[:half, :])
        second = _dot(hg, w1_ref[half:, :])
        hid = first + pltpu.roll(second, n16 - 1, 0) + bias
        act = jax.nn.gelu(hid).astype(BF16)
        part = _dot(act, (w2a_ref if g == 0 else w2b_ref)[...])
        out = part if out is None else out + part
    o_ref[0] = out.astype(BF16)


def nsa_compress(kv, cmp_pos, w1, w2):
    b, s, _ = kv.shape
    n16 = s // CMP_STRIDE
    h = kv.reshape(b, n16, CMP_STRIDE, NSA_GROUPS, NSA_DH).transpose(0, 3, 1, 2, 4)
    h = h.reshape(b, NSA_GROUPS, n16, CMP_STRIDE * NSA_DH)
    pos = jnp.broadcast_to(cmp_pos.reshape(1, CMP_LEN * NSA_DH), (8, CMP_LEN * NSA_DH)).astype(BF16)
    w2a = jnp.pad(w2, ((0, 0), (0, NSA_DH))).astype(BF16)
    w2b = jnp.pad(w2, ((0, 0), (NSA_DH, 0))).astype(BF16)
    w1 = w1.astype(BF16)
    return pl.pallas_call(
        _compress_kernel,
        grid=(b,),
        in_specs=[pl.BlockSpec((1,) + h.shape[1:], lambda i: (i, 0, 0, 0)), _const_spec(pos.shape),
                  _const_spec(w1.shape), _const_spec(w2a.shape), _const_spec(w2b.shape)],
        out_specs=pl.BlockSpec((1, n16, LANE), lambda i: (i, 0, 0)),
        out_shape=jax.ShapeDtypeStruct((b, n16, LANE), BF16),
        compiler_params=_params(("parallel",)),
        name="nsa_compress",
    )(h, pos, w1, w2a, w2b)


def _topk_mask_t(x, k):
    n = x.shape[0]
    ridx = _iota(x.shape, 0).astype(F32)
    sel = jnp.zeros(x.shape, F32)
    for _ in range(k):
        m = jnp.max(x, axis=0, keepdims=True)
        first = jnp.min(jnp.where(x == m, ridx, float(n)), axis=0, keepdims=True)
        hit = ridx == first
        sel = jnp.where(hit, 1.0, sel)
        x = jnp.where(hit, REMOVED, x)
    return sel


def _cmp_topk_kernel(q_ref, kc_ref, vc_ref, pq_ref, pc_ref, gate_ref, oc_ref, sel_ref, *, tq, ncp, ns, topk):
    qi = pl.program_id(1)
    t = qi * tq + _iota((tq, 1), 0)
    cend = _iota((1, ncp), 1) * CMP_STRIDE + (CMP_LEN - 1)
    mask = cend <= t
    dist = pq_ref[0] - pc_ref[0]
    kc = kc_ref[0]
    vc = vc_ref[0]
    gates = gate_ref[0]
    lane_group = _shr(_iota((tq, LANE), 1), NSA_DH)
    cstart = _iota((ncp, ns), 0) * CMP_STRIDE
    sstart = _iota((ncp, ns), 1) * SEL_BLOCK
    overlap = jnp.where((cstart < sstart + SEL_BLOCK) & (cstart + CMP_LEN > sstart)
                        & (cstart < (ncp - 1) * CMP_STRIDE), 1.0, 0.0).astype(BF16)
    blk = _iota((tq, ns), 1)
    chunk = _shr(t, SEL_BLOCK)
    for g in range(NSA_GROUPS):
        psum = jnp.zeros((tq, ncp), F32)
        for n in range(NSA_HPG):
            h = g * NSA_HPG + n
            sl = slice(h * LANE, (h + 1) * LANE)
            s = _dot_nt(q_ref[0, :, sl], kc) - NSA_SLOPES[h] * dist
            s = jnp.where(mask, s, NEG)
            e = jnp.where(mask, jnp.exp(s - jnp.max(s, axis=-1, keepdims=True)), 0.0)
            p = e / jnp.maximum(jnp.sum(e, axis=-1, keepdims=True), 1e-30)
            psum = psum + p
            o = _dot(p.astype(BF16), vc) * gates[:, h:h + 1]
            oc_ref[0, :, sl] = jnp.where(lane_group == g, o, 0.0).astype(BF16)
        hi = psum.astype(BF16)
        r1 = psum - hi.astype(F32)
        mid = r1.astype(BF16)
        lo = (r1 - mid.astype(F32)).astype(BF16)
        imp = _dot(hi, overlap) + _dot(mid, overlap) + _dot(lo, overlap)
        imp = jnp.where((blk == 0) | (blk == chunk), BIG, imp)
        imp = jnp.where(blk <= chunk, imp, NEG)
        sel = _topk_mask_t(imp.T, topk).T
        sel_ref[0, g] = jnp.where(blk <= chunk, sel, 0.0).astype(BF16)


def nsa_cmp_topk(qn, kc, vc, pq, pos_cend, gates, tq=128):
    b, s, hw = qn.shape
    tq = min(tq, s)
    ncp = kc.shape[1]
    ns = s // SEL_BLOCK
    return pl.pallas_call(
        functools.partial(_cmp_topk_kernel, tq=tq, ncp=ncp, ns=ns, topk=min(SEL_TOPK, ns)),
        grid=(b, s // tq),
        in_specs=[pl.BlockSpec((1, tq, hw), lambda i, j: (i, j, 0)),
                  pl.BlockSpec((1, ncp, LANE), lambda i, j: (i, 0, 0)),
                  pl.BlockSpec((1, ncp, LANE), lambda i, j: (i, 0, 0)),
                  pl.BlockSpec((1, tq, 1), lambda i, j: (i, j, 0)),
                  pl.BlockSpec((1, 1, ncp), lambda i, j: (i, 0, 0)),
                  pl.BlockSpec((1, tq, LANE), lambda i, j: (i, j, 0))],
        out_specs=[pl.BlockSpec((1, tq, hw), lambda i, j: (i, j, 0)),
                   pl.BlockSpec((1, NSA_GROUPS, tq, ns), lambda i, j: (i, 0, j, 0))],
        out_shape=[jax.ShapeDtypeStruct((b, s, hw), BF16), jax.ShapeDtypeStruct((b, NSA_GROUPS, s, ns), BF16)],
        compiler_params=_params(("parallel", "parallel")),
        name="nsa_cmp_topk",
    )(qn, kc, vc, pq, pos_cend, gates)


def _nsa_finalize(o_ref, gate_ref, m_ref, l_ref, acc_ref, branch):
    lane_group = _shr(_iota(acc_ref.shape[1:], 1), NSA_DH)
    gates = gate_ref[0]
    for h in range(NSA_HEADS):
        col = branch * NSA_HEADS + h
        o = acc_ref[h] / jnp.maximum(l_ref[h], 1e-30) * gates[:, col:col + 1]
        o_ref[0, :, h * LANE:(h + 1) * LANE] = jnp.where(lane_group == h // NSA_HPG, o, 0.0).astype(BF16)


def _sel_attn_kernel(qi_tab, ki_tab, flags, q_ref, k_ref, v_ref, sel_ref, pq_ref, pk_ref, gate_ref, o_ref,
                     m_ref, l_ref, acc_ref, *, tq, tk, nq, nk):
    b = pl.program_id(0)
    p = pl.program_id(1)
    qi = qi_tab[p]
    ki = ki_tab[p]

    @pl.when(ki == 0)
    def _():
        _init_state(m_ref, l_ref, acc_ref)

    ns = sel_ref.shape[3]
    kidx = ki * tk + _iota((tq, tk), 1)
    causal = kidx <= qi * tq + _iota((tq, tk), 0)
    for g in range(NSA_GROUPS):
        @pl.when(flags[((b * NSA_GROUPS + g) * nq + qi) * nk + ki] > 0)
        def _(g=g):
            expand = jnp.where(_iota((ns, tk), 0) == _shr(ki * tk + _iota((ns, tk), 1), SEL_BLOCK), 1.0, 0.0)
            picked = _dot(sel_ref[0, g], expand.astype(BF16))
            mask = jnp.where(causal, picked, 0.0) > 0.5
            dist = pq_ref[0] - pk_ref[0]
            k = k_ref[0]
            v = v_ref[0]
            for n in range(NSA_HPG):
                h = g * NSA_HPG + n
                s = _dot_nt(q_ref[0, :, h * LANE:(h + 1) * LANE], k) - NSA_SLOPES[h] * dist
                _online_softmax_step(s, mask, v, m_ref, l_ref, acc_ref, h, guard=True)

    @pl.when(ki == ((qi + 1) * tq - 1) // tk)
    def _():
        _nsa_finalize(o_ref, gate_ref, m_ref, l_ref, acc_ref, branch=1)


def nsa_selected(qn, ks, vs, sel, pq, pk, gates, tq=256, tk=512):
    b, s, hw = qn.shape
    tq, tk = min(tq, s), min(tk, s)
    nq, nk = s // tq, s // tk
    ns = s // SEL_BLOCK
    qi_tab, ki_tab = _causal_pairs(nq, tq, tk)
    flags = sel.reshape(b, NSA_GROUPS, nq, tq, nk, tk // SEL_BLOCK).max(axis=(3, 5))
    flags = (flags > 0).astype(jnp.int32).reshape(-1)
    imap_q = lambda bb, p, qt, kt, fl: (bb, qt[p], 0)
    imap_k = lambda bb, p, qt, kt, fl: (bb, kt[p], 0)
    return pl.pallas_call(
        functools.partial(_sel_attn_kernel, tq=tq, tk=tk, nq=nq, nk=nk),
        grid_spec=pltpu.PrefetchScalarGridSpec(
            num_scalar_prefetch=3,
            grid=(b, int(qi_tab.shape[0])),
            in_specs=[pl.BlockSpec((1, tq, hw), imap_q), pl.BlockSpec((1, tk, LANE), imap_k),
                      pl.BlockSpec((1, tk, LANE), imap_k),
                      pl.BlockSpec((1, NSA_GROUPS, tq, ns), lambda bb, p, qt, kt, fl: (bb, 0, qt[p], 0)),
                      pl.BlockSpec((1, tq, 1), imap_q),
                      pl.BlockSpec((1, 1, tk), lambda bb, p, qt, kt, fl: (bb, 0, kt[p])),
                      pl.BlockSpec((1, tq, LANE), imap_q)],
            out_specs=pl.BlockSpec((1, tq, hw), imap_q),
            scratch_shapes=[pltpu.VMEM((NSA_HEADS, tq, 1), F32), pltpu.VMEM((NSA_HEADS, tq, 1), F32),
                            pltpu.VMEM((NSA_HEADS, tq, LANE), F32)],
        ),
        out_shape=jax.ShapeDtypeStruct((b, s, hw), BF16),
        compiler_params=_params(("parallel", "arbitrary")),
        name="nsa_selected",
    )(qi_tab, ki_tab, flags, qn, ks, vs, sel, pq, pk, gates)


def _win_attn_kernel(q_ref, k_ref, v_ref, pq_ref, pk_ref, gate_ref, o_ref, m_ref, l_ref, acc_ref, *, tq, tk, nkt):
    qi = pl.program_id(1)
    j = pl.program_id(2)
    kt = qi * (tq // tk) - (WIN // tk) + j

    @pl.when(j == 0)
    def _():
        _init_state(m_ref, l_ref, acc_ref)

    @pl.when(kt >= 0)
    def _():
        diff = (qi * tq + _iota((tq, tk), 0)) - (kt * tk + _iota((tq, tk), 1))
        mask = (diff >= 0) & (diff < WIN)
        dist = pq_ref[0] - pk_ref[0]
        k = k_ref[0]
        v = v_ref[0]
        for h in range(NSA_HEADS):
            s = _dot_nt(q_ref[0, :, h * LANE:(h + 1) * LANE], k) - NSA_SLOPES[h] * dist
            _online_softmax_step(s, mask, v, m_ref, l_ref, acc_ref, h, guard=True)

    @pl.when(j == nkt - 1)
    def _():
        _nsa_finalize(o_ref, gate_ref, m_ref, l_ref, acc_ref, branch=2)


def nsa_window(qn, kw, vw, pq, pk, gates, tq=256, tk=256):
    b, s, hw = qn.shape
    tq, tk = min(tq, s), min(tk, s, WIN)
    nkt = WIN // tk + tq // tk
    kmap = lambda bb, i, j: jnp.maximum(i * (tq // tk) - (WIN // tk) + j, 0)
    return pl.pallas_call(
        functools.partial(_win_attn_kernel, tq=tq, tk=tk, nkt=nkt),
        grid=(b, s // tq, nkt),
        in_specs=[pl.BlockSpec((1, tq, hw), lambda bb, i, j: (bb, i, 0)),
                  pl.BlockSpec((1, tk, LANE), lambda bb, i, j: (bb, kmap(bb, i, j), 0)),
                  pl.BlockSpec((1, tk, LANE), lambda bb, i, j: (bb, kmap(bb, i, j), 0)),
                  pl.BlockSpec((1, tq, 1), lambda bb, i, j: (bb, i, 0)),
                  pl.BlockSpec((1, 1, tk), lambda bb, i, j: (bb, 0, kmap(bb, i, j))),
                  pl.BlockSpec((1, tq, LANE), lambda bb, i, j: (bb, i, 0))],
        out_specs=pl.BlockSpec((1, tq, hw), lambda bb, i, j: (bb, i, 0)),
        out_shape=jax.ShapeDtypeStruct((b, s, hw), BF16),
        scratch_shapes=[pltpu.VMEM((NSA_HEADS, tq, 1), F32), pltpu.VMEM((NSA_HEADS, tq, 1), F32),
                        pltpu.VMEM((NSA_HEADS, tq, LANE), F32)],
        compiler_params=_params(("parallel", "parallel", "arbitrary")),
        name="nsa_window",
    )(qn, kw, vw, pq, pk, gates)


def _out_ln_kernel(*refs, n_in, groups):
    x_ref = refs[0]
    ins = refs[1:1 + n_in]
    ws = refs[1 + n_in:1 + n_in + len(groups)]
    g_ref, b_ref, o_ref = refs[1 + n_in + len(groups):]
    mix = None
    for w_ref, members in zip(ws, groups):
        a = ins[members[0]][...]
        if len(members) > 1:
            a = a.astype(F32)
            for i in members[1:]:
                a = a + ins[i][...].astype(F32)
            a = a.astype(BF16)
        part = _dot(a, w_ref[...])
        mix = part if mix is None else mix + part
    o_ref[...] = _layer_norm(ALPHA * x_ref[...] + mix, g_ref[...], b_ref[...])


def out_ln(x2, ins, ws, groups, g, b, tm=512):
    m, d = x2.shape
    tm = min(tm, m)
    row = lambda w: pl.BlockSpec((tm, w), lambda i: (i, 0))
    return pl.pallas_call(
        functools.partial(_out_ln_kernel, n_in=len(ins), groups=groups),
        grid=(m // tm,),
        in_specs=[row(d)] + [row(a.shape[1]) for a in ins] + [_const_spec(w.shape) for w in ws]
        + [_const_spec((1, d)), _const_spec((1, d))],
        out_specs=row(d),
        out_shape=jax.ShapeDtypeStruct((m, d), F32),
        compiler_params=_params(("parallel",)),
        name="out_ln",
    )(x2, *ins, *ws, g.reshape(1, d), b.reshape(1, d))


def _odd_proj_kernel(x_ref, w_ref, o_ref, *, n_q_cols):
    xb = x_ref[...].astype(BF16)
    width = DIL_GH * DIL_DH
    for c in range(w_ref.shape[1] // width):
        y = _dot(xb, w_ref[:, c * width:(c + 1) * width])
        if c * width < n_q_cols:
            y = y * DIL_SCALE
        o_ref[:, c * width:(c + 1) * width] = y.astype(BF16)


def odd_proj(x2, w, tm=256):
    m, d = x2.shape
    tm = min(tm, m)
    n = w.shape[1]
    return pl.pallas_call(
        functools.partial(_odd_proj_kernel, n_q_cols=DIL_GROUPS * DIL_GH * DIL_DH),
        grid=(m // tm,),
        in_specs=[pl.BlockSpec((tm, d), lambda i: (i, 0)), _const_spec(w.shape)],
        out_specs=pl.BlockSpec((tm, n), lambda i: (i, 0)),
        out_shape=jax.ShapeDtypeStruct((m, n), BF16),
        compiler_params=_params(("parallel",)),
        name="odd_proj",
    )(x2, w)


def _dil_steps(tq, tk):
    grp, off = [], []
    for g, (w, _) in enumerate(DIL_PATTERNS):
        back = -(-w // tk)
        for o in range(-back, tq // tk):
            grp.append(g)
            off.append(o)
    return grp, off


def _dil_attn_kernel(grp_tab, off_tab, win_tab, dmask_tab, slope_tab, q_ref, k_ref, v_ref, pq_ref, pk_ref, o_ref,
                     m_ref, l_ref, acc_ref, *, tq, tk, nsteps):
    qi = pl.program_id(1)
    j = pl.program_id(2)
    g = grp_tab[j]
    kt = qi * (tq // tk) + off_tab[j]

    @pl.when(j == 0)
    def _():
        _init_state(m_ref, l_ref, acc_ref)

    @pl.when(kt >= 0)
    def _():
        diff = (qi * tq + _iota((tq, tk), 0)) - (kt * tk + _iota((tq, tk), 1))
        ok = jnp.where((diff >= 0) & (diff <= win_tab[g]), jnp.bitwise_and(diff, dmask_tab[g]), 1)
        mask = ok == 0
        dist = pq_ref[0] - pk_ref[0]
        for h in range(DIL_GH):
            sl = slice(h * DIL_DH, (h + 1) * DIL_DH)
            s = _dot_nt(q_ref[0, :, sl], k_ref[0, :, sl]) - slope_tab[g * DIL_GH + h] * dist
            _online_softmax_step(s, mask, v_ref[0, :, sl], m_ref, l_ref, acc_ref, h, guard=True)

    @pl.when(j == nsteps - 1)
    def _():
        for h in range(DIL_GH):
            o_ref[0, :, h * DIL_DH:(h + 1) * DIL_DH] = (acc_ref[h] / l_ref[h]).astype(BF16)


def dilated_attention(qkv, pq, pk, tq=512, tk=512):
    b, s, _ = qkv.shape
    tq, tk = min(tq, s), min(tk, s)
    grp, off = _dil_steps(tq, tk)
    nsteps = len(grp)
    width = DIL_GH * DIL_DH
    n_slopes = DIL_GROUPS * DIL_GH
    slopes = 2.0 ** (-ALIBI_MAX_BIAS * jnp.arange(1, n_slopes + 1, dtype=F32) / n_slopes)
    tabs = (jnp.asarray(grp, jnp.int32), jnp.asarray(off, jnp.int32),
            jnp.asarray([w for w, _ in DIL_PATTERNS], jnp.int32),
            jnp.asarray([d - 1 for _, d in DIL_PATTERNS], jnp.int32), slopes)
    ktile = lambda i, j, gt, ot: jnp.maximum(i * (tq // tk) + ot[j], 0)
    return pl.pallas_call(
        functools.partial(_dil_attn_kernel, tq=tq, tk=tk, nsteps=nsteps),
        grid_spec=pltpu.PrefetchScalarGridSpec(
            num_scalar_prefetch=5,
            grid=(b, s // tq, nsteps),
            in_specs=[pl.BlockSpec((1, tq, width), lambda bb, i, j, gt, ot, *_: (bb, i, gt[j])),
                      pl.BlockSpec((1, tk, width), lambda bb, i, j, gt, ot, *_: (bb, ktile(i, j, gt, ot), DIL_GROUPS + gt[j])),
                      pl.BlockSpec((1, tk, width), lambda bb, i, j, gt, ot, *_: (bb, ktile(i, j, gt, ot), 2 * DIL_GROUPS + gt[j])),
                      pl.BlockSpec((1, tq, 1), lambda bb, i, j, *_: (bb, i, 0)),
                      pl.BlockSpec((1, 1, tk), lambda bb, i, j, gt, ot, *_: (bb, 0, ktile(i, j, gt, ot)))],
            out_specs=pl.BlockSpec((1, tq, width), lambda bb, i, j, *_: (bb, i, 0)),
            scratch_shapes=[pltpu.VMEM((DIL_GH, tq, 1), F32), pltpu.VMEM((DIL_GH, tq, 1), F32),
                            pltpu.VMEM((DIL_GH, tq, DIL_DH), F32)],
        ),
        out_shape=jax.ShapeDtypeStruct((b, s, width), BF16),
        compiler_params=_params(("parallel", "parallel", "arbitrary")),
        name="dilated_attention",
    )(*tabs, qkv, qkv, qkv, pq, pk)


def even_mixer_ln(x2, b, s, pos_f, w_in, q_norm_g, kv_norm_g, w_uq, w_uk, w_uv,
                  cmp_pos, cmp_k_w1, cmp_k_w2, cmp_v_w1, cmp_v_w2, w_out, ln_g, ln_b):
    m = b * s
    half = MLA_ROPE // 2
    inv = ROPE_THETA ** (-jnp.arange(half, dtype=F32) / half)
    ang = (pos_f[..., None] * inv).reshape(m, half)
    ones = jnp.ones((m, MLA_NOPE), F32)
    zeros = jnp.zeros((m, LANE - MLA_NOPE - MLA_ROPE), F32)
    cos_t = jnp.concatenate([ones, jnp.cos(ang), jnp.cos(ang), zeros], axis=1)
    sin_t = jnp.concatenate([0.0 * ones, jnp.sin(ang), jnp.sin(ang), zeros], axis=1)

    qm, km, vm, qn, kv6, gates = even_proj(x2, cos_t, sin_t, w_in, q_norm_g, kv_norm_g, w_uq, w_uk, w_uv)
    hw = MLA_HEADS * LANE
    o_mla = mla_attention(qm.reshape(b, s, hw), km.reshape(b, s, hw), vm.reshape(b, s, hw))

    kv6 = kv6.reshape(b, s, 6, LANE)
    k_c, v_c, k_s, v_s, k_w, v_w = (kv6[:, :, i] for i in range(6))
    kc = nsa_compress(k_c, cmp_pos, cmp_k_w1, cmp_k_w2)
    vc = nsa_compress(v_c, cmp_pos, cmp_v_w1, cmp_v_w2)
    ncp = s // CMP_STRIDE
    pos_cend = jnp.pad(pos_f[:, CMP_LEN - 1::CMP_STRIDE], ((0, 0), (0, 1))).reshape(b, 1, ncp)
    pq = pos_f.reshape(b, s, 1)
    pk = pos_f.reshape(b, 1, s)
    qn = qn.reshape(b, s, hw)
    gates = gates.reshape(b, s, LANE)
    o_c, sel = nsa_cmp_topk(qn, kc, vc, pq, pos_cend, gates)
    o_s = nsa_selected(qn, k_s, v_s, sel, pq, pk, gates)
    o_w = nsa_window(qn, k_w, v_w, pq, pk, gates)

    w_mla = jnp.pad(w_out[:MLA_HEADS * MLA_V].reshape(MLA_HEADS, MLA_V, -1), ((0, 0), (0, LANE - MLA_V), (0, 0)))
    w_nsa = w_out[MLA_HEADS * MLA_V:].reshape(NSA_GROUPS, NSA_HPG, NSA_DH, -1)
    w_nsa = jnp.concatenate([jnp.pad(w_nsa[g], ((0, 0), (g * NSA_DH, LANE - NSA_DH - g * NSA_DH), (0, 0)))
                             for g in range(NSA_GROUPS)], axis=0)
    d = x2.shape[1]
    ins = [o_mla.reshape(m, hw), o_c.reshape(m, hw), o_s.reshape(m, hw), o_w.reshape(m, hw)]
    ws = [w_mla.reshape(hw, d).astype(BF16), w_nsa.reshape(hw, d).astype(BF16)]
    return out_ln(x2, ins, ws, ((0,), (1, 2, 3)), ln_g, ln_b)


def odd_mixer_ln(x2, b, s, pos_f, w_in, w_out, ln_g, ln_b):
    m = b * s
    qkv = odd_proj(x2, w_in.astype(BF16)).reshape(b, s, -1)
    o = dilated_attention(qkv, pos_f.reshape(b, s, 1), pos_f.reshape(b, 1, s))
    return out_ln(x2, [o.reshape(m, -1)], [w_out.astype(BF16)], ((0,),), ln_g, ln_b)


def kernel(x, positions, ln1_g, ln1_b, ffn1_w_gate, ffn1_w_up, ffn1_w_down, mix_in_even, mla_q_norm, mla_kv_norm, mla_w_uq, mla_w_uk, mla_w_uv, nsa_cmp_pos, nsa_cmp_k_w1, nsa_cmp_k_w2, nsa_cmp_v_w1, nsa_cmp_v_w2, mix_out_even, mix_in_odd, mix_out_odd, ln2_g, ln2_b, ffn2_w_gate, ffn2_w_up, ffn2_w_down, ln3_g, ln3_b):
    b, s, d = x.shape
    x2 = x.reshape(b * s, d)
    pos_f = positions.astype(F32)
    for i in range(DEPTH):
        j = i // 2
        x2 = ffn_ln(x2, ffn1_w_gate[i].astype(BF16), ffn1_w_up[i].astype(BF16), ffn1_w_down[i].astype(BF16),
                    ln1_g[i], ln1_b[i])
        if i % 2 == 0:
            x2 = even_mixer_ln(x2, b, s, pos_f, mix_in_even[j], mla_q_norm[j], mla_kv_norm[j], mla_w_uq[j],
                               mla_w_uk[j], mla_w_uv[j], nsa_cmp_pos[j], nsa_cmp_k_w1[j], nsa_cmp_k_w2[j],
                               nsa_cmp_v_w1[j], nsa_cmp_v_w2[j], mix_out_even[j], ln2_g[i], ln2_b[i])
        else:
            x2 = odd_mixer_ln(x2, b, s, pos_f, mix_in_odd[j], mix_out_odd[j], ln2_g[i], ln2_b[i])
        x2 = ffn_ln(x2, ffn2_w_gate[i].astype(BF16), ffn2_w_up[i].astype(BF16), ffn2_w_down[i].astype(BF16),
                    ln3_g[i], ln3_b[i])
    return x2.reshape(b, s, d)
```

```python
import functools
import math

import numpy as np
import jax
import jax.numpy as jnp
from jax import lax
from jax.experimental import pallas as pl
from jax.experimental.pallas import tpu as pltpu

F32 = jnp.float32
BF16 = jnp.bfloat16

DEPTH = 2
LN_EPS = 1e-5
RMS_EPS = 1e-6
ALPHA = (2 * DEPTH) ** 0.25
HALF_STEP = 0.5
NEG = -1e30
BIG = 1e9
MASK_BIG = 1e30
REMOVED = -3.0e38
ALIBI_MAX_BIAS = 8.0
LOG2E = math.log2(math.e)
LN2 = math.log(2.0)
LANE = 128

MLA_HEADS = 8
MLA_Q_RANK = 384
MLA_KV_RANK = 256
MLA_NOPE = 64
MLA_ROPE = 32
MLA_V = 64
ROPE_THETA = 10000.0
MLA_SCALE = (MLA_NOPE + MLA_ROPE) ** -0.5

NSA_HEADS = 8
NSA_GROUPS = 2
NSA_HPG = 4
NSA_DH = 64
CMP_LEN = 32
CMP_STRIDE = 16
CMP_HIDDEN = 256
SEL_BLOCK = 64
SEL_TOPK = 16
WIN = 512
NSA_SCALE = NSA_DH ** -0.5
NSA_SLOPES = tuple(2.0 ** (-ALIBI_MAX_BIAS * (i + 1) / NSA_HEADS) for i in range(NSA_HEADS))
ONES_LANE = 64
BIAS_LANES = (64, 65, 66)

DIL_PATTERNS = ((128, 1), (512, 4), (2048, 16))
DIL_GROUPS = 3
DIL_GH = 4
DIL_DH = 128
DIL_SCALE = DIL_DH ** -0.5
DIL_SPAN = 128

VMEM_LIMIT = 48 * 1024 * 1024


def _iota(shape, dim):
    return lax.broadcasted_iota(jnp.int32, shape, dim)


def _shr(x, pow2):
    return jnp.right_shift(x, int(pow2).bit_length() - 1)


def _dot(a, b):
    return jnp.dot(a, b, preferred_element_type=F32)


def _dot_nt(a, b):
    return lax.dot_general(a, b, (((1,), (1,)), ((), ())), preferred_element_type=F32)


def _const_spec(shape):
    zeros = (0,) * len(shape)
    return pl.BlockSpec(shape, lambda *_: zeros, pipeline_mode=pl.Buffered(1))


def _params(sem):
    return pltpu.CompilerParams(dimension_semantics=sem, vmem_limit_bytes=VMEM_LIMIT)


def _layer_norm(z, g, b):
    mu = jnp.mean(z, axis=-1, keepdims=True)
    zc = z - mu
    var = jnp.mean(zc * zc, axis=-1, keepdims=True)
    return zc * lax.rsqrt(var + LN_EPS) * g + b


def _rms_norm(z, g):
    return z * lax.rsqrt(jnp.mean(z * z, axis=-1, keepdims=True) + RMS_EPS) * g


def _bias_pieces(d, lane):
    x = jnp.broadcast_to(d, lane.shape)
    hi = x.astype(BF16).astype(F32)
    r = x - hi
    mid = r.astype(BF16).astype(F32)
    lo = (r - mid).astype(BF16).astype(F32)
    return jnp.where(lane == BIAS_LANES[0], hi,
                     jnp.where(lane == BIAS_LANES[1], mid, jnp.where(lane == BIAS_LANES[2], lo, 0.0)))


def _flash_update(s, v, m_ref, acc_ref, idx):
    m_old = m_ref[idx]
    m_new = jnp.maximum(m_old, jnp.max(s, axis=-1, keepdims=True))
    p = jnp.exp2(s - m_new).astype(BF16)
    acc_ref[idx] = jnp.exp2(m_old - m_new) * acc_ref[idx] + _dot(p, v)
    m_ref[idx] = m_new


def _init_state(m_ref, acc_ref):
    m_ref[...] = jnp.full(m_ref.shape, NEG, F32)
    acc_ref[...] = jnp.zeros(acc_ref.shape, F32)


def _normalized(acc):
    lane = _iota(acc.shape, 1)
    o = acc / jnp.maximum(acc[:, ONES_LANE:ONES_LANE + 1], 1e-30)
    return jnp.where(lane < ONES_LANE, o, 0.0)


def _ffn_ln_kernel(x_ref, wg_ref, wu_ref, wd_ref, g_ref, b_ref, o_ref, *, n_chunks):
    x = x_ref[...]
    xb = x.astype(BF16)
    c = wg_ref.shape[1] // n_chunks
    y = None
    for i in range(n_chunks):
        gt = _dot(xb, wg_ref[:, i * c:(i + 1) * c])
        up = _dot(xb, wu_ref[:, i * c:(i + 1) * c])
        h = (gt * jax.nn.sigmoid(gt) * up).astype(BF16)
        part = _dot(h, wd_ref[i * c:(i + 1) * c, :])
        y = part if y is None else y + part
    o_ref[...] = _layer_norm(ALPHA * x + HALF_STEP * y, g_ref[...], b_ref[...])


def ffn_ln(x2, wg, wu, wd, g, b, tm=512):
    m, d = x2.shape
    tm = min(tm, m)
    row = pl.BlockSpec((tm, d), lambda i: (i, 0))
    return pl.pallas_call(
        functools.partial(_ffn_ln_kernel, n_chunks=2),
        grid=(m // tm,),
        in_specs=[row, _const_spec(wg.shape), _const_spec(wu.shape), _const_spec(wd.shape),
                  _const_spec((1, d)), _const_spec((1, d))],
        out_specs=row,
        out_shape=jax.ShapeDtypeStruct((m, d), F32),
        compiler_params=_params(("parallel",)),
        name="ffn_ln",
    )(x2, wg, wu, wd, g.reshape(1, d), b.reshape(1, d))


HW = MLA_HEADS * LANE
EVEN_X_COLS = (MLA_Q_RANK, MLA_KV_RANK, LANE, LANE, HW, 2 * LANE, 8 * LANE, LANE)
EVEN_X_OFFS = tuple(int(v) for v in np.cumsum((0,) + EVEN_X_COLS))


def _even_proj_kernel(x_ref, cos_ref, sin_ref, dpos_ref, wx_ref, qg_ref, kvg_ref, wuq_ref, wuqs_ref, wuk_ref,
                      wuv_ref, slope_ref, qm_ref, km_ref, vm_ref, qn_ref, kvc_ref, ks_ref, kw_ref, vs_ref, vw_ref,
                      gate_ref, *, tiles_per_seq):
    tm = x_ref.shape[0]
    xb = x_ref[...].astype(BF16)
    cos = cos_ref[...]
    sin = sin_ref[...]
    lane = _iota((tm, LANE), 1)
    ones_lane = jnp.where(lane == ONES_LANE, 1.0, 0.0)
    pos_term = _bias_pieces(dpos_ref[...], lane)
    tok = (pl.program_id(0) % tiles_per_seq) * tm + _iota((tm, LANE), 0)
    block_onehot = jnp.where(lane == _shr(tok, SEL_BLOCK), MASK_BIG, 0.0).astype(BF16)

    def xdot(i):
        return _dot(xb, wx_ref[:, EVEN_X_OFFS[i]:EVEN_X_OFFS[i + 1]])

    cq = _rms_norm(xdot(0), qg_ref[...]).astype(BF16)
    ckv = _rms_norm(xdot(1), kvg_ref[...]).astype(BF16)
    k_rot = xdot(2) * cos + xdot(3) * sin
    for h in range(MLA_HEADS):
        sl = slice(h * LANE, (h + 1) * LANE)
        q = _dot(cq, wuq_ref[:, sl]) * cos + _dot(cq, wuqs_ref[:, sl]) * sin
        qm_ref[:, sl] = (q * (MLA_SCALE * LOG2E)).astype(BF16)
        km_ref[:, sl] = (_dot(ckv, wuk_ref[:, sl]) + k_rot).astype(BF16)
        vm_ref[:, sl] = (_dot(ckv, wuv_ref[:, sl]) + ones_lane).astype(BF16)
    qn = xdot(4) * (NSA_SCALE * LOG2E) + slope_ref[...]
    for h in range(NSA_HEADS):
        qn_ref[h] = qn[:, h * LANE:(h + 1) * LANE].astype(BF16)
    kvc_ref[...] = xdot(5).astype(BF16)
    kv8 = xdot(6)
    blk = lambda i: kv8[:, i * LANE:(i + 1) * LANE]
    for g in range(NSA_GROUPS):
        ks_ref[:, 2 * g * LANE:(2 * g + 1) * LANE] = (blk(g) + pos_term).astype(BF16)
        ks_ref[:, (2 * g + 1) * LANE:(2 * g + 2) * LANE] = block_onehot
        kw_ref[:, g * LANE:(g + 1) * LANE] = (blk(2 + g) + pos_term).astype(BF16)
        vs_ref[:, g * LANE:(g + 1) * LANE] = (blk(4 + g) + ones_lane).astype(BF16)
        vw_ref[:, g * LANE:(g + 1) * LANE] = (blk(6 + g) + ones_lane).astype(BF16)
    gate_ref[...] = jax.nn.sigmoid(xdot(7))


def _head_blocks(w, n_heads, width):
    k = w.shape[0]
    w = jnp.pad(w.reshape(k, n_heads, width), ((0, 0), (0, 0), (0, LANE - width)))
    return w.reshape(k, n_heads * LANE)


def even_proj(x2, cos_t, sin_t, dpos, seq, w_in, q_norm_g, kv_norm_g, w_uq, w_uk, w_uv, tm=256):
    m, d = x2.shape
    tm = min(tm, seq)
    half = MLA_ROPE // 2
    gw = NSA_GROUPS * NSA_DH
    cuts = np.cumsum((MLA_Q_RANK, MLA_KV_RANK, MLA_ROPE, NSA_HEADS * NSA_DH) + (gw,) * 6)
    cuts = [0] + [int(c) for c in cuts]
    w_cq, w_ckv, w_kpe, w_q = (w_in[:, cuts[i]:cuts[i + 1]] for i in range(4))
    w_kc, w_vc, w_ks, w_vs, w_kw, w_vw = (w_in[:, cuts[4 + i]:cuts[5 + i]] for i in range(6))
    w_gate = w_in[:, cuts[10]:]
    w_kpe_sw = jnp.concatenate([-w_kpe[:, half:], w_kpe[:, :half]], axis=1)
    rope_pad = ((0, 0), (MLA_NOPE, LANE - MLA_NOPE - MLA_ROPE))
    w_gate_blk = jnp.pad(w_gate, ((0, 0), (0, LANE - w_gate.shape[1])))
    wx = jnp.concatenate(
        [w_cq, w_ckv, jnp.pad(w_kpe, rope_pad), jnp.pad(w_kpe_sw, rope_pad), _head_blocks(w_q, NSA_HEADS, NSA_DH),
         w_kc, w_vc] + [_head_blocks(w, NSA_GROUPS, NSA_DH) for w in (w_ks, w_kw, w_vs, w_vw)] + [w_gate_blk],
        axis=1).astype(BF16)

    qd = MLA_NOPE + MLA_ROPE
    uq = w_uq.reshape(MLA_Q_RANK, MLA_HEADS, qd)
    uq_sw = jnp.concatenate([jnp.zeros_like(uq[..., :MLA_NOPE]), -uq[..., MLA_NOPE + half:],
                             uq[..., MLA_NOPE:MLA_NOPE + half]], axis=-1)
    wuq = _head_blocks(uq.reshape(MLA_Q_RANK, -1), MLA_HEADS, qd).astype(BF16)
    wuqs = _head_blocks(uq_sw.reshape(MLA_Q_RANK, -1), MLA_HEADS, qd).astype(BF16)
    wuk = _head_blocks(w_uk, MLA_HEADS, MLA_NOPE).astype(BF16)
    wuv = _head_blocks(w_uv, MLA_HEADS, MLA_V).astype(BF16)
    slope_row = np.zeros((1, HW), np.float32)
    for h in range(NSA_HEADS):
        for ln in BIAS_LANES:
            slope_row[0, h * LANE + ln] = NSA_SLOPES[h]

    row = lambda w: pl.BlockSpec((tm, w), lambda i: (i, 0))
    sds = jax.ShapeDtypeStruct
    return pl.pallas_call(
        functools.partial(_even_proj_kernel, tiles_per_seq=seq // tm),
        grid=(m // tm,),
        in_specs=[row(d), row(LANE), row(LANE), row(1), _const_spec(wx.shape), _const_spec((1, MLA_Q_RANK)),
                  _const_spec((1, MLA_KV_RANK)), _const_spec(wuq.shape), _const_spec(wuqs.shape),
                  _const_spec(wuk.shape), _const_spec(wuv.shape), _const_spec((1, HW))],
        out_specs=[row(HW), row(HW), row(HW), pl.BlockSpec((NSA_HEADS, tm, LANE), lambda i: (0, i, 0)),
                   row(2 * LANE), row(4 * LANE), row(2 * LANE), row(2 * LANE), row(2 * LANE), row(LANE)],
        out_shape=[sds((m, HW), BF16)] * 3 + [sds((NSA_HEADS, m, LANE), BF16), sds((m, 2 * LANE), BF16),
                                              sds((m, 4 * LANE), BF16), sds((m, 2 * LANE), BF16),
                                              sds((m, 2 * LANE), BF16), sds((m, 2 * LANE), BF16), sds((m, LANE), F32)],
        compiler_params=_params(("parallel",)),
        name="even_proj",
    )(x2, cos_t, sin_t, dpos, wx, q_norm_g.reshape(1, -1), kv_norm_g.reshape(1, -1), wuq, wuqs, wuk, wuv,
      jnp.asarray(slope_row))


def _causal_pairs(nq, tq, tk):
    qi, ki = [], []
    for i in range(nq):
        for j in range(((i + 1) * tq - 1) // tk + 1):
            qi.append(i)
            ki.append(j)
    return jnp.asarray(qi, jnp.int32), jnp.asarray(ki, jnp.int32)


def _mla_kernel(qi_tab, ki_tab, q_ref, k_ref, v_ref, o_ref, m_ref, acc_ref, *, tq, tk):
    p = pl.program_id(1)
    qi = qi_tab[p]
    ki = ki_tab[p]

    @pl.when(ki == 0)
    def _():
        _init_state(m_ref, acc_ref)

    def run(masked):
        if masked:
            mask = (qi * tq + _iota((tq, tk), 0)) >= (ki * tk + _iota((tq, tk), 1))
        scores = lambda h: _dot_nt(q_ref[0, :, h * LANE:(h + 1) * LANE], k_ref[0, :, h * LANE:(h + 1) * LANE])
        s_next = scores(0)
        for h in range(MLA_HEADS):
            s = s_next
            if h + 1 < MLA_HEADS:
                s_next = scores(h + 1)
            if masked:
                s = jnp.where(mask, s, NEG)
            _flash_update(s, v_ref[0, :, h * LANE:(h + 1) * LANE], m_ref, acc_ref, h)

    crosses = (ki + 1) * tk - 1 > qi * tq

    @pl.when(crosses)
    def _():
        run(True)

    @pl.when(jnp.logical_not(crosses))
    def _():
        run(False)

    @pl.when(ki == ((qi + 1) * tq - 1) // tk)
    def _():
        for h in range(MLA_HEADS):
            o_ref[0, :, h * LANE:(h + 1) * LANE] = _normalized(acc_ref[h]).astype(BF16)


def mla_attention(q, k, v, tq=512, tk=512):
    b, s, hw = q.shape
    tq, tk = min(tq, s), min(tk, s)
    qi_tab, ki_tab = _causal_pairs(s // tq, tq, tk)
    qspec = pl.BlockSpec((1, tq, hw), lambda bb, p, qt, kt: (bb, qt[p], 0))
    kspec = pl.BlockSpec((1, tk, hw), lambda bb, p, qt, kt: (bb, kt[p], 0))
    return pl.pallas_call(
        functools.partial(_mla_kernel, tq=tq, tk=tk),
        grid_spec=pltpu.PrefetchScalarGridSpec(
            num_scalar_prefetch=2,
            grid=(b, int(qi_tab.shape[0])),
            in_specs=[qspec, kspec, kspec],
            out_specs=qspec,
            scratch_shapes=[pltpu.VMEM((MLA_HEADS, tq, 1), F32), pltpu.VMEM((MLA_HEADS, tq, LANE), F32)],
        ),
        out_shape=jax.ShapeDtypeStruct((b, s, hw), BF16),
        compiler_params=_params(("parallel", "arbitrary")),
        name="mla_attention",
    )(qi_tab, ki_tab, q, k, v)


def _compress_kernel(h_ref, pos_ref, w1_ref, w2_ref, ext_ref, o_ref, *, is_value):
    n16 = h_ref.shape[2]
    half = w1_ref.shape[0] // 2
    lane = _iota((n16, LANE), 1)
    if is_value:
        extra = jnp.where(lane == ONES_LANE, 1.0, 0.0)
    else:
        extra = _bias_pieces(ext_ref[0], lane)
    bias = _dot(pos_ref[...], w1_ref[...])[0:1]
    for g in range(NSA_GROUPS):
        hg = h_ref[0, g]
        first = _dot(hg, w1_ref[:half, :])
        second = _dot(hg, w1_ref[half:, :])
        hid = first + pltpu.roll(second, n16 - 1, 0) + bias
        act = jax.nn.gelu(hid).astype(BF16)
        o_ref[0, g] = (_dot(act, w2_ref[...]) + extra).astype(BF16)


def nsa_compress(kv, cmp_pos, w1, w2, dpos_cend, is_value):
    b, s, _ = kv.shape
    n16 = s // CMP_STRIDE
    h = kv.reshape(b, n16, CMP_STRIDE, NSA_GROUPS, NSA_DH).transpose(0, 3, 1, 2, 4)
    h = h.reshape(b, NSA_GROUPS, n16, CMP_STRIDE * NSA_DH)
    pos = jnp.broadcast_to(cmp_pos.reshape(1, CMP_LEN * NSA_DH), (8, CMP_LEN * NSA_DH)).astype(BF16)
    w2p = jnp.pad(w2, ((0, 0), (0, LANE - NSA_DH))).astype(BF16)
    w1 = w1.astype(BF16)
    return pl.pallas_call(
        functools.partial(_compress_kernel, is_value=is_value),
        grid=(b,),
        in_specs=[pl.BlockSpec((1,) + h.shape[1:], lambda i: (i, 0, 0, 0)), _const_spec(pos.shape),
                  _const_spec(w1.shape), _const_spec(w2p.shape), pl.BlockSpec((1, n16, 1), lambda i: (i, 0, 0))],
        out_specs=pl.BlockSpec((1, NSA_GROUPS, n16, LANE), lambda i: (i, 0, 0, 0)),
        out_shape=jax.ShapeDtypeStruct((b, NSA_GROUPS, n16, LANE), BF16),
        compiler_params=_params(("parallel",)),
        name="nsa_compress",
    )(h, pos, w1, w2p, dpos_cend)


def _topk_mask_t(x, k):
    n = x.shape[0]
    ridx = _iota(x.shape, 0).astype(F32)
    sel = jnp.zeros(x.shape, F32)
    for _ in range(k):
        m = jnp.max(x, axis=0, keepdims=True)
        first = jnp.min(jnp.where(x == m, ridx, float(n)), axis=0, keepdims=True)
        hit = ridx == first
        sel = jnp.where(hit, 1.0, sel)
        x = jnp.where(hit, REMOVED, x)
    return sel


def _cmp_topk_kernel(q_ref, kc_ref, vc_ref, gate_ref, oc_ref, selm1_ref, *, tq, ncp, topk):
    qi = pl.program_id(1)
    t = qi * tq + _iota((tq, 1), 0)
    cend = _iota((1, ncp), 1) * CMP_STRIDE + (CMP_LEN - 1)
    mask = (cend <= t)[None]
    gates = gate_ref[0]
    cstart = _iota((ncp, LANE), 0) * CMP_STRIDE
    sstart = _iota((ncp, LANE), 1) * SEL_BLOCK
    overlap = jnp.where((cstart < sstart + SEL_BLOCK) & (cstart + CMP_LEN > sstart)
                        & (cstart < (ncp - 1) * CMP_STRIDE), 1.0, 0.0).astype(BF16)
    blk = _iota((tq, LANE), 1)
    chunk = _shr(t, SEL_BLOCK)
    for g in range(NSA_GROUPS):
        q = q_ref[g * NSA_HPG:(g + 1) * NSA_HPG].reshape(NSA_HPG * tq, LANE)
        s = _dot_nt(q, kc_ref[0, g]).reshape(NSA_HPG, tq, ncp)
        s = jnp.where(mask, s, NEG)
        e = jnp.where(mask, jnp.exp2(s - jnp.max(s, axis=-1, keepdims=True)), 0.0)
        p = e / jnp.maximum(jnp.sum(e, axis=-1, keepdims=True), 1e-30)
        psum = jnp.sum(p, axis=0)
        o = _dot(p.reshape(NSA_HPG * tq, ncp).astype(BF16), vc_ref[0, g])
        for n in range(NSA_HPG):
            h = g * NSA_HPG + n
            oh = jnp.where(blk < NSA_DH, o[n * tq:(n + 1) * tq] * gates[:, h:h + 1], 0.0)
            oc_ref[0, :, h * LANE:(h + 1) * LANE] = oh.astype(BF16)
        hi = psum.astype(BF16)
        r1 = psum - hi.astype(F32)
        mid = r1.astype(BF16)
        lo = (r1 - mid.astype(F32)).astype(BF16)
        imp = _dot(hi, overlap) + _dot(mid, overlap) + _dot(lo, overlap)
        imp = jnp.where((blk == 0) | (blk == chunk), BIG, imp)
        imp = jnp.where(blk <= chunk, imp, NEG)
        sel = _topk_mask_t(imp.T, topk).T
        selm1_ref[0, g] = (jnp.where(blk <= chunk, sel, 0.0) - 1.0).astype(BF16)


def nsa_cmp_topk(qn, kc, vc, gates, tq=128):
    _, b, s, _ = qn.shape
    tq = min(tq, s)
    ncp = kc.shape[2]
    assert s // SEL_BLOCK <= LANE
    topk = min(SEL_TOPK, s // SEL_BLOCK)
    kspec = pl.BlockSpec((1, NSA_GROUPS, ncp, LANE), lambda i, j: (i, 0, 0, 0))
    return pl.pallas_call(
        functools.partial(_cmp_topk_kernel, tq=tq, ncp=ncp, topk=topk),
        grid=(b, s // tq),
        in_specs=[pl.BlockSpec((NSA_HEADS, None, tq, LANE), lambda i, j: (0, i, j, 0)), kspec, kspec,
                  pl.BlockSpec((1, tq, LANE), lambda i, j: (i, j, 0))],
        out_specs=[pl.BlockSpec((1, tq, HW), lambda i, j: (i, j, 0)),
                   pl.BlockSpec((1, NSA_GROUPS, tq, LANE), lambda i, j: (i, 0, j, 0))],
        out_shape=[jax.ShapeDtypeStruct((b, s, HW), BF16), jax.ShapeDtypeStruct((b, NSA_GROUPS, s, LANE), BF16)],
        compiler_params=_params(("parallel", "parallel")),
        name="nsa_cmp_topk",
    )(qn, kc, vc, gates)


def _nsa_write(o_ref, gate_ref, acc_ref, tq, branch):
    gates = gate_ref[0]
    for h in range(NSA_HEADS):
        g, n = divmod(h, NSA_HPG)
        col = branch * NSA_HEADS + h
        o = _normalized(acc_ref[g, n * tq:(n + 1) * tq]) * gates[:, col:col + 1]
        o_ref[0, :, h * LANE:(h + 1) * LANE] = o.astype(BF16)


def _sel_attn_kernel(qi_tab, ki_tab, flags, q_ref, k_ref, v_ref, selm1_ref, gate_ref, o_ref,
                     lhs_ref, m_ref, acc_ref, *, tq, tk, nq, nk):
    b = pl.program_id(0)
    p = pl.program_id(1)
    qi = qi_tab[p]
    ki = ki_tab[p]
    last = ((qi + 1) * tq - 1) // tk

    @pl.when(ki == 0)
    def _():
        _init_state(m_ref, acc_ref)
        for g in range(NSA_GROUPS):
            lhs_ref[g, :, :LANE] = q_ref[g * NSA_HPG:(g + 1) * NSA_HPG].reshape(NSA_HPG * tq, LANE)
            lhs_ref[g, :, LANE:] = jnp.concatenate([selm1_ref[0, g]] * NSA_HPG, axis=0)

    def run(g, masked):
        s = _dot_nt(lhs_ref[g], k_ref[0, :, 2 * g * LANE:(2 * g + 2) * LANE])
        if masked:
            causal = (ki * tk + _iota((tq, tk), 1)) <= (qi * tq + _iota((tq, tk), 0))
            s = jnp.where(causal[None], s.reshape(NSA_HPG, tq, tk), NEG).reshape(NSA_HPG * tq, tk)
        _flash_update(s, v_ref[0, :, g * LANE:(g + 1) * LANE], m_ref, acc_ref, g)

    for g in range(NSA_GROUPS):
        active = flags[((b * NSA_GROUPS + g) * nq + qi) * nk + ki] > 0

        @pl.when(active & (ki == last))
        def _(g=g):
            run(g, True)

        @pl.when(active & (ki != last))
        def _(g=g):
            run(g, False)

    @pl.when(ki == last)
    def _():
        _nsa_write(o_ref, gate_ref, acc_ref, tq, branch=1)


def nsa_selected(qn, ks, vs, selm1, gates, tq=256, tk=512):
    _, b, s, _ = qn.shape
    tq, tk = min(tq, s), min(tk, s)
    nq, nk = s // tq, s // tk
    qi_tab, ki_tab = _causal_pairs(nq, tq, tk)
    bpt = tk // SEL_BLOCK
    flags = (selm1.reshape(b, NSA_GROUPS, nq, tq, LANE // bpt, bpt) > -0.5).any(axis=(3, 5))[..., :nk]
    flags = flags.astype(jnp.int32).reshape(-1)
    imap_q = lambda bb, p, qt, kt, fl: (bb, qt[p], 0)
    imap_k = lambda bb, p, qt, kt, fl: (bb, kt[p], 0)
    rows = NSA_HPG * tq
    return pl.pallas_call(
        functools.partial(_sel_attn_kernel, tq=tq, tk=tk, nq=nq, nk=nk),
        grid_spec=pltpu.PrefetchScalarGridSpec(
            num_scalar_prefetch=3,
            grid=(b, int(qi_tab.shape[0])),
            in_specs=[pl.BlockSpec((NSA_HEADS, None, tq, LANE), lambda bb, p, qt, kt, fl: (0, bb, qt[p], 0)),
                      pl.BlockSpec((1, tk, 4 * LANE), imap_k), pl.BlockSpec((1, tk, 2 * LANE), imap_k),
                      pl.BlockSpec((1, NSA_GROUPS, tq, LANE), lambda bb, p, qt, kt, fl: (bb, 0, qt[p], 0)),
                      pl.BlockSpec((1, tq, LANE), imap_q)],
            out_specs=pl.BlockSpec((1, tq, HW), imap_q),
            scratch_shapes=[pltpu.VMEM((NSA_GROUPS, rows, 2 * LANE), BF16), pltpu.VMEM((NSA_GROUPS, rows, 1), F32),
                            pltpu.VMEM((NSA_GROUPS, rows, LANE), F32)],
        ),
        out_shape=jax.ShapeDtypeStruct((b, s, HW), BF16),
        compiler_params=_params(("parallel", "arbitrary")),
        name="nsa_selected",
    )(qi_tab, ki_tab, flags, qn, ks, vs, selm1, gates)


def _win_attn_kernel(q_ref, k_ref, v_ref, gate_ref, o_ref, m_ref, acc_ref, *, tq, nkt):
    qi = pl.program_id(1)
    j = pl.program_id(2)
    kt = qi - j

    @pl.when(j == 0)
    def _():
        _init_state(m_ref, acc_ref)

    @pl.when(kt >= 0)
    def _():
        diff = (qi * tq + _iota((tq, tq), 0)) - (kt * tq + _iota((tq, tq), 1))
        mask = ((diff >= 0) & (diff < WIN))[None]
        for g in range(NSA_GROUPS):
            q = q_ref[g * NSA_HPG:(g + 1) * NSA_HPG].reshape(NSA_HPG * tq, LANE)
            s = _dot_nt(q, k_ref[0, :, g * LANE:(g + 1) * LANE]).reshape(NSA_HPG, tq, tq)
            s = jnp.where(mask, s, NEG).reshape(NSA_HPG * tq, tq)
            _flash_update(s, v_ref[0, :, g * LANE:(g + 1) * LANE], m_ref, acc_ref, g)

    @pl.when(j == nkt - 1)
    def _():
        _nsa_write(o_ref, gate_ref, acc_ref, tq, branch=2)


def nsa_window(qn, kw, vw, gates, tq=512):
    _, b, s, _ = qn.shape
    tq = min(tq, s, WIN)
    nkt = WIN // tq + 1
    kmap = lambda bb, i, j: (bb, jnp.maximum(i - j, 0), 0)
    rows = NSA_HPG * tq
    return pl.pallas_call(
        functools.partial(_win_attn_kernel, tq=tq, nkt=nkt),
        grid=(b, s // tq, nkt),
        in_specs=[pl.BlockSpec((NSA_HEADS, None, tq, LANE), lambda bb, i, j: (0, bb, i, 0)),
                  pl.BlockSpec((1, tq, 2 * LANE), kmap), pl.BlockSpec((1, tq, 2 * LANE), kmap),
                  pl.BlockSpec((1, tq, LANE), lambda bb, i, j: (bb, i, 0))],
        out_specs=pl.BlockSpec((1, tq, HW), lambda bb, i, j: (bb, i, 0)),
        out_shape=jax.ShapeDtypeStruct((b, s, HW), BF16),
        scratch_shapes=[pltpu.VMEM((NSA_GROUPS, rows, 1), F32), pltpu.VMEM((NSA_GROUPS, rows, LANE), F32)],
        compiler_params=_params(("parallel", "parallel", "arbitrary")),
        name="nsa_window",
    )(qn, kw, vw, gates)


def _even_out_ln_kernel(x_ref, om_ref, oc_ref, os_ref, ow_ref, wm_ref, wn_ref, g_ref, b_ref, o_ref):
    nsa = (oc_ref[...].astype(F32) + os_ref[...].astype(F32) + ow_ref[...].astype(F32)).astype(BF16)
    mix = _dot(om_ref[...], wm_ref[...]) + _dot(nsa, wn_ref[...])
    o_ref[...] = _layer_norm(ALPHA * x_ref[...] + mix, g_ref[...], b_ref[...])


def even_out_ln(x2, o_mla, o_c, o_s, o_w, w_mla, w_nsa, g, b, tm=512):
    m, d = x2.shape
    tm = min(tm, m)
    row = lambda w: pl.BlockSpec((tm, w), lambda i: (i, 0))
    return pl.pallas_call(
        _even_out_ln_kernel,
        grid=(m // tm,),
        in_specs=[row(d)] + [row(HW)] * 4 + [_const_spec(w_mla.shape), _const_spec(w_nsa.shape),
                                             _const_spec((1, d)), _const_spec((1, d))],
        out_specs=row(d),
        out_shape=jax.ShapeDtypeStruct((m, d), F32),
        compiler_params=_params(("parallel",)),
        name="even_out_ln",
    )(x2, o_mla, o_c, o_s, o_w, w_mla, w_nsa, g.reshape(1, d), b.reshape(1, d))


def _odd_out_ln_kernel(x_ref, o0_ref, o1_ref, o2_ref, l0_ref, l1_ref, l2_ref, w_ref, g_ref, b_ref, o_ref):
    outs = (o0_ref, o1_ref, o2_ref)
    lses = [r[...] for r in (l0_ref, l1_ref, l2_ref)]
    top = jnp.maximum(jnp.maximum(lses[0], lses[1]), lses[2])
    es = [jnp.exp(l - top) for l in lses]
    den = es[0] + es[1] + es[2]
    wts = [e / den for e in es]
    cols = []
    for h in range(DIL_GH):
        sl = slice(h * DIL_DH, (h + 1) * DIL_DH)
        merged = None
        for gi in range(DIL_GROUPS):
            term = wts[gi][:, h:h + 1] * outs[gi][:, sl].astype(F32)
            merged = term if merged is None else merged + term
        cols.append(merged.astype(BF16))
    mix = _dot(jnp.concatenate(cols, axis=1), w_ref[...])
    o_ref[...] = _layer_norm(ALPHA * x_ref[...] + mix, g_ref[...], b_ref[...])


def odd_out_ln(x2, outs, lses, w, g, b, tm=512):
    m, d = x2.shape
    tm = min(tm, m)
    row = lambda wd: pl.BlockSpec((tm, wd), lambda i: (i, 0))
    width = DIL_GH * DIL_DH
    return pl.pallas_call(
        _odd_out_ln_kernel,
        grid=(m // tm,),
        in_specs=[row(d)] + [row(width)] * 3 + [row(LANE)] * 3 + [_const_spec(w.shape), _const_spec((1, d)),
                                                                   _const_spec((1, d))],
        out_specs=row(d),
        out_shape=jax.ShapeDtypeStruct((m, d), F32),
        compiler_params=_params(("parallel",)),
        name="odd_out_ln",
    )(x2, *outs, *lses, w, g.reshape(1, d), b.reshape(1, d))


def _odd_proj_kernel(x_ref, w_ref, o_ref, *, n_q_cols):
    xb = x_ref[...].astype(BF16)
    width = DIL_GH * DIL_DH
    for c in range(w_ref.shape[1] // width):
        y = _dot(xb, w_ref[:, c * width:(c + 1) * width])
        if c * width < n_q_cols:
            y = y * (DIL_SCALE * LOG2E)
        o_ref[:, c * width:(c + 1) * width] = y.astype(BF16)


def odd_proj(x2, w, tm=256):
    m, d = x2.shape
    tm = min(tm, m)
    n = w.shape[1]
    return pl.pallas_call(
        functools.partial(_odd_proj_kernel, n_q_cols=DIL_GROUPS * DIL_GH * DIL_DH),
        grid=(m // tm,),
        in_specs=[pl.BlockSpec((tm, d), lambda i: (i, 0)), _const_spec(w.shape)],
        out_specs=pl.BlockSpec((tm, n), lambda i: (i, 0)),
        out_shape=jax.ShapeDtypeStruct((m, n), BF16),
        compiler_params=_params(("parallel",)),
        name="odd_proj",
    )(x2, w)


def _dil_attn_kernel(slope_tab, pos0_tab, q_ref, kc_ref, kp_ref, vc_ref, vp_ref, pc_ref, pp_ref, pq_ref, o_ref,
                     lse_ref, *, group, tq):
    sub = DIL_SPAN
    b = pl.program_id(0)
    jt = pl.program_id(2)
    row = _iota((sub, 2 * sub), 0)
    col = _iota((sub, 2 * sub), 1)
    band = (col >= row) & (col <= row + sub)
    band_bias = jnp.where(band, 0.0, NEG)
    start_bias = jnp.where(band & (col >= sub), 0.0, NEG)
    pos0 = pos0_tab[b]
    lane = _iota((sub, LANE), 1)
    for i in range(tq // sub):
        cur = slice(i * sub, (i + 1) * sub)
        if i == 0:
            mask_bias = jnp.where(jt == 0, start_bias, band_bias)
            pk = jnp.concatenate([pp_ref[...], pc_ref[:, :sub]], axis=1)
        else:
            mask_bias = band_bias
            pk = pc_ref[:, (i - 1) * sub:(i + 1) * sub]
        dpos = pk - pos0
        dq = (pq_ref[cur, :] - pos0) * LN2
        lse_tile = jnp.zeros((sub, LANE), F32)
        for h in range(DIL_GH):
            hs = slice(h * DIL_DH, (h + 1) * DIL_DH)
            slope = slope_tab[group * DIL_GH + h]
            if i == 0:
                k = jnp.concatenate([kp_ref[:, hs], kc_ref[:sub, hs]], axis=0)
                v = jnp.concatenate([vp_ref[:, hs], vc_ref[:sub, hs]], axis=0)
            else:
                k = kc_ref[(i - 1) * sub:(i + 1) * sub, hs]
                v = vc_ref[(i - 1) * sub:(i + 1) * sub, hs]
            s = _dot_nt(q_ref[cur, hs], k) + (mask_bias + slope * dpos)
            m = jnp.max(s, axis=-1, keepdims=True)
            p = jnp.exp2(s - m)
            den = jnp.sum(p, axis=-1, keepdims=True)
            o_ref[cur, hs] = (_dot(p.astype(BF16), v) / den).astype(BF16)
            lse_tile = jnp.where(lane == h, m * LN2 + jnp.log(den) - slope * dq, lse_tile)
        lse_ref[cur, :] = lse_tile


def dilated_group_attention(qkv, pos_f, group, tq=512):
    b, s, n = qkv.shape
    w, d = DIL_PATTERNS[group]
    assert w // d == DIL_SPAN
    sd = s // d
    tq = min(tq, sd)
    sub = DIL_SPAN
    width = DIL_GH * DIL_DH
    ncb = n // width
    n_slopes = DIL_GROUPS * DIL_GH
    slopes = 2.0 ** (-ALIBI_MAX_BIAS * jnp.arange(1, n_slopes + 1, dtype=F32) / n_slopes) * LOG2E
    pos0 = pos_f[:, 0]
    posc = pos_f.reshape(b, sd, d).transpose(0, 2, 1).reshape(b, d, 1, sd)
    view = qkv.reshape(b, sd, d * n)
    r = tq // sub
    cur = lambda c: pl.BlockSpec((None, tq, width), lambda bb, rr, j, *_: (bb, j, rr * ncb + c))
    prev = lambda c: pl.BlockSpec((None, sub, width),
                                  lambda bb, rr, j, *_: (bb, jnp.maximum(j * r - 1, 0), rr * ncb + c))
    out, lse = pl.pallas_call(
        functools.partial(_dil_attn_kernel, group=group, tq=tq),
        grid_spec=pltpu.PrefetchScalarGridSpec(
            num_scalar_prefetch=2,
            grid=(b, d, sd // tq),
            in_specs=[cur(group), cur(DIL_GROUPS + group), prev(DIL_GROUPS + group),
                      cur(2 * DIL_GROUPS + group), prev(2 * DIL_GROUPS + group),
                      pl.BlockSpec((None, None, 1, tq), lambda bb, rr, j, *_: (bb, rr, 0, j)),
                      pl.BlockSpec((None, None, 1, sub), lambda bb, rr, j, *_: (bb, rr, 0, jnp.maximum(j * r - 1, 0))),
                      pl.BlockSpec((None, None, tq, 1), lambda bb, rr, j, *_: (bb, rr, j, 0))],
            out_specs=[pl.BlockSpec((None, tq, width), lambda bb, rr, j, *_: (bb, j, rr)),
                       pl.BlockSpec((None, tq, LANE), lambda bb, rr, j, *_: (bb, j, rr))],
        ),
        out_shape=[jax.ShapeDtypeStruct((b, sd, d * width), BF16), jax.ShapeDtypeStruct((b, sd, d * LANE), F32)],
        compiler_params=_params(("parallel", "parallel", "parallel")),
        name=f"dilated_attention_g{group}",
    )(slopes, pos0, view, view, view, view, view, posc, posc, posc.reshape(b, d, sd, 1))
    return out.reshape(b * s, width), lse.reshape(b * s, LANE)


def even_mixer_ln(x2, b, s, pos_f, w_in, q_norm_g, kv_norm_g, w_uq, w_uk, w_uv,
                  cmp_pos, cmp_k_w1, cmp_k_w2, cmp_v_w1, cmp_v_w2, w_out, ln_g, ln_b):
    m = b * s
    d = x2.shape[1]
    half = MLA_ROPE // 2
    inv = ROPE_THETA ** (-jnp.arange(half, dtype=F32) / half)
    ang = (pos_f[..., None] * inv).reshape(m, half)
    ones = jnp.ones((m, MLA_NOPE), F32)
    zeros = jnp.zeros((m, LANE - MLA_NOPE - MLA_ROPE), F32)
    cos_t = jnp.concatenate([ones, jnp.cos(ang), jnp.cos(ang), zeros], axis=1)
    sin_t = jnp.concatenate([0.0 * ones, jnp.sin(ang), jnp.sin(ang), zeros], axis=1)
    dpos = (pos_f - pos_f[:, :1]) * LOG2E
    ncp = s // CMP_STRIDE
    dpos_cend = jnp.pad(dpos[:, CMP_LEN - 1::CMP_STRIDE], ((0, 0), (0, 1))).reshape(b, ncp, 1)

    qm, km, vm, qn, kvc, ks, kw, vs, vw, gates = even_proj(
        x2, cos_t, sin_t, dpos.reshape(m, 1), s, w_in, q_norm_g, kv_norm_g, w_uq, w_uk, w_uv)
    o_mla = mla_attention(qm.reshape(b, s, HW), km.reshape(b, s, HW), vm.reshape(b, s, HW))

    kvc = kvc.reshape(b, s, 2, LANE)
    kc = nsa_compress(kvc[:, :, 0], cmp_pos, cmp_k_w1, cmp_k_w2, dpos_cend, is_value=False)
    vc = nsa_compress(kvc[:, :, 1], cmp_pos, cmp_v_w1, cmp_v_w2, dpos_cend, is_value=True)
    qn = qn.reshape(NSA_HEADS, b, s, LANE)
    gates = gates.reshape(b, s, LANE)
    o_c, selm1 = nsa_cmp_topk(qn, kc, vc, gates)
    o_s = nsa_selected(qn, ks.reshape(b, s, 4 * LANE), vs.reshape(b, s, 2 * LANE), selm1, gates)
    o_w = nsa_window(qn, kw.reshape(b, s, 2 * LANE), vw.reshape(b, s, 2 * LANE), gates)

    pad_rows = lambda w, n, width: jnp.pad(w.reshape(n, width, d), ((0, 0), (0, LANE - width), (0, 0))).reshape(n * LANE, d)
    w_mla = pad_rows(w_out[:MLA_HEADS * MLA_V], MLA_HEADS, MLA_V).astype(BF16)
    w_nsa = pad_rows(w_out[MLA_HEADS * MLA_V:], NSA_HEADS, NSA_DH).astype(BF16)
    return even_out_ln(x2, o_mla.reshape(m, HW), o_c.reshape(m, HW), o_s.reshape(m, HW), o_w.reshape(m, HW),
                       w_mla, w_nsa, ln_g, ln_b)


def odd_mixer_ln(x2, b, s, pos_f, w_in, w_out, ln_g, ln_b):
    qkv = odd_proj(x2, w_in.astype(BF16)).reshape(b, s, -1)
    parts = [dilated_group_attention(qkv, pos_f, g) for g in range(DIL_GROUPS)]
    return odd_out_ln(x2, [p[0] for p in parts], [p[1] for p in parts], w_out.astype(BF16), ln_g, ln_b)


def kernel(x, positions, ln1_g, ln1_b, ffn1_w_gate, ffn1_w_up, ffn1_w_down, mix_in_even, mla_q_norm, mla_kv_norm, mla_w_uq, mla_w_uk, mla_w_uv, nsa_cmp_pos, nsa_cmp_k_w1, nsa_cmp_k_w2, nsa_cmp_v_w1, nsa_cmp_v_w2, mix_out_even, mix_in_odd, mix_out_odd, ln2_g, ln2_b, ffn2_w_gate, ffn2_w_up, ffn2_w_down, ln3_g, ln3_b):
    b, s, d = x.shape
    x2 = x.reshape(b * s, d)
    pos_f = positions.astype(F32)
    for i in range(DEPTH):
        j = i // 2
        x2 = ffn_ln(x2, ffn1_w_gate[i].astype(BF16), ffn1_w_up[i].astype(BF16), ffn1_w_down[i].astype(BF16),
                    ln1_g[i], ln1_b[i])
        if i % 2 == 0:
            x2 = even_mixer_ln(x2, b, s, pos_f, mix_in_even[j], mla_q_norm[j], mla_kv_norm[j], mla_w_uq[j],
                               mla_w_uk[j], mla_w_uv[j], nsa_cmp_pos[j], nsa_cmp_k_w1[j], nsa_cmp_k_w2[j],
                               nsa_cmp_v_w1[j], nsa_cmp_v_w2[j], mix_out_even[j], ln2_g[i], ln2_b[i])
        else:
            x2 = odd_mixer_ln(x2, b, s, pos_f, mix_in_odd[j], mix_out_odd[j], ln2_g[i], ln2_b[i])
        x2 = ffn_ln(x2, ffn2_w_gate[i].astype(BF16), ffn2_w_up[i].astype(BF16), ffn2_w_down[i].astype(BF16),
                    ln3_g[i], ln3_b[i])
    return x2.reshape(b, s, d)
```

```python
import functools
import math

import numpy as np
import jax
import jax.numpy as jnp
from jax import lax
from jax.experimental import pallas as pl
from jax.experimental.pallas import tpu as pltpu

F32 = jnp.float32
BF16 = jnp.bfloat16

DEPTH = 2
LN_EPS = 1e-5
RMS_EPS = 1e-6
ALPHA = (2 * DEPTH) ** 0.25
HALF_STEP = 0.5
NEG = -1e30
BIG = 1e9
MASK_BIG = 1e30
REMOVED = -3.0e38
ALIBI_MAX_BIAS = 8.0
LOG2E = math.log2(math.e)
LN2 = math.log(2.0)
LANE = 128

MLA_HEADS = 8
MLA_Q_RANK = 384
MLA_KV_RANK = 256
MLA_NOPE = 64
MLA_ROPE = 32
MLA_V = 64
ROPE_THETA = 10000.0
MLA_SCALE = (MLA_NOPE + MLA_ROPE) ** -0.5

NSA_HEADS = 8
NSA_GROUPS = 2
NSA_HPG = 4
NSA_DH = 64
CMP_LEN = 32
CMP_STRIDE = 16
CMP_HIDDEN = 256
SEL_BLOCK = 64
SEL_TOPK = 16
WIN = 512
NSA_SCALE = NSA_DH ** -0.5
NSA_SLOPES = tuple(2.0 ** (-ALIBI_MAX_BIAS * (i + 1) / NSA_HEADS) for i in range(NSA_HEADS))
ONES_LANE = 64
BIAS_LANES = (64, 65, 66)

DIL_PATTERNS = ((128, 1), (512, 4), (2048, 16))
DIL_GROUPS = 3
DIL_GH = 4
DIL_DH = 128
DIL_SCALE = DIL_DH ** -0.5
DIL_SPAN = 128

VMEM_LIMIT = 48 * 1024 * 1024


def _iota(shape, dim):
    return lax.broadcasted_iota(jnp.int32, shape, dim)


def _shr(x, pow2):
    return jnp.right_shift(x, int(pow2).bit_length() - 1)


def _dot(a, b):
    return jnp.dot(a, b, preferred_element_type=F32)


def _dot_nt(a, b):
    return lax.dot_general(a, b, (((1,), (1,)), ((), ())), preferred_element_type=F32)


def _const_spec(shape):
    zeros = (0,) * len(shape)
    return pl.BlockSpec(shape, lambda *_: zeros, pipeline_mode=pl.Buffered(1))


def _params(sem):
    return pltpu.CompilerParams(dimension_semantics=sem, vmem_limit_bytes=VMEM_LIMIT)


def _layer_norm(z, g, b):
    mu = jnp.mean(z, axis=-1, keepdims=True)
    zc = z - mu
    var = jnp.mean(zc * zc, axis=-1, keepdims=True)
    return zc * lax.rsqrt(var + LN_EPS) * g + b


def _rms_norm(z, g):
    return z * lax.rsqrt(jnp.mean(z * z, axis=-1, keepdims=True) + RMS_EPS) * g


def _bias_pieces(d, lane):
    x = jnp.broadcast_to(d, lane.shape)
    hi = x.astype(BF16).astype(F32)
    r = x - hi
    mid = r.astype(BF16).astype(F32)
    lo = (r - mid).astype(BF16).astype(F32)
    return jnp.where(lane == BIAS_LANES[0], hi,
                     jnp.where(lane == BIAS_LANES[1], mid, jnp.where(lane == BIAS_LANES[2], lo, 0.0)))


def _flash_update(s, v, m_ref, acc_ref, idx):
    m_old = m_ref[idx]
    m_new = jnp.maximum(m_old, jnp.max(s, axis=-1, keepdims=True))
    p = jnp.exp2(s - jnp.tile(m_new, (1, s.shape[1] // LANE))).astype(BF16)
    acc_ref[idx] = jnp.exp2(m_old - m_new) * acc_ref[idx] + _dot(p, v)
    m_ref[idx] = m_new


def _init_state(m_ref, acc_ref):
    m_ref[...] = jnp.full(m_ref.shape, NEG, F32)
    acc_ref[...] = jnp.zeros(acc_ref.shape, F32)


def _normalized(acc):
    lane = _iota(acc.shape, 1)
    o = acc / jnp.maximum(acc[:, ONES_LANE:ONES_LANE + 1], 1e-30)
    return jnp.where(lane < ONES_LANE, o, 0.0)


def _ffn_ln_kernel(x_ref, wg_ref, wu_ref, wd_ref, g_ref, b_ref, o_ref, *, n_chunks):
    x = x_ref[...]
    xb = x.astype(BF16)
    c = wg_ref.shape[1] // n_chunks
    y = None
    for i in range(n_chunks):
        gt = _dot(xb, wg_ref[:, i * c:(i + 1) * c])
        up = _dot(xb, wu_ref[:, i * c:(i + 1) * c])
        h = (gt * jax.nn.sigmoid(gt) * up).astype(BF16)
        part = _dot(h, wd_ref[i * c:(i + 1) * c, :])
        y = part if y is None else y + part
    o_ref[...] = _layer_norm(ALPHA * x + HALF_STEP * y, g_ref[...], b_ref[...])


def ffn_ln(x2, wg, wu, wd, g, b, tm=512):
    m, d = x2.shape
    tm = min(tm, m)
    row = pl.BlockSpec((tm, d), lambda i: (i, 0))
    return pl.pallas_call(
        functools.partial(_ffn_ln_kernel, n_chunks=2),
        grid=(m // tm,),
        in_specs=[row, _const_spec(wg.shape), _const_spec(wu.shape), _const_spec(wd.shape),
                  _const_spec((1, d)), _const_spec((1, d))],
        out_specs=row,
        out_shape=jax.ShapeDtypeStruct((m, d), F32),
        compiler_params=_params(("parallel",)),
        name="ffn_ln",
    )(x2, wg, wu, wd, g.reshape(1, d), b.reshape(1, d))


HW = MLA_HEADS * LANE
EVEN_X_COLS = (MLA_Q_RANK, MLA_KV_RANK, LANE, LANE, HW, 2 * LANE, 8 * LANE, LANE)
EVEN_X_OFFS = tuple(int(v) for v in np.cumsum((0,) + EVEN_X_COLS))


def _even_proj_kernel(x_ref, cos_ref, sin_ref, dpos_ref, wx_ref, qg_ref, kvg_ref, wuq_ref, wuqs_ref, wuk_ref,
                      wuv_ref, slope_ref, qm_ref, km_ref, vm_ref, qn_ref, kvc_ref, ks_ref, kw_ref, vs_ref, vw_ref,
                      gate_ref, *, tiles_per_seq):
    tm = x_ref.shape[0]
    xb = x_ref[...].astype(BF16)
    cos = cos_ref[...]
    sin = sin_ref[...]
    lane = _iota((tm, LANE), 1)
    ones_lane = jnp.where(lane == ONES_LANE, 1.0, 0.0)
    pos_term = _bias_pieces(dpos_ref[...], lane)
    tok = (pl.program_id(0) % tiles_per_seq) * tm + _iota((tm, LANE), 0)
    block_onehot = jnp.where(lane == _shr(tok, SEL_BLOCK), MASK_BIG, 0.0).astype(BF16)

    def xdot(i):
        return _dot(xb, wx_ref[:, EVEN_X_OFFS[i]:EVEN_X_OFFS[i + 1]])

    cq = _rms_norm(xdot(0), qg_ref[...]).astype(BF16)
    ckv = _rms_norm(xdot(1), kvg_ref[...]).astype(BF16)
    k_rot = xdot(2) * cos + xdot(3) * sin
    for h in range(MLA_HEADS):
        sl = slice(h * LANE, (h + 1) * LANE)
        q = _dot(cq, wuq_ref[:, sl]) * cos + _dot(cq, wuqs_ref[:, sl]) * sin
        qm_ref[:, sl] = (q * (MLA_SCALE * LOG2E)).astype(BF16)
        km_ref[:, sl] = (_dot(ckv, wuk_ref[:, sl]) + k_rot).astype(BF16)
        vm_ref[:, sl] = (_dot(ckv, wuv_ref[:, sl]) + ones_lane).astype(BF16)
    qn = xdot(4) * (NSA_SCALE * LOG2E) + slope_ref[...]
    for h in range(NSA_HEADS):
        qn_ref[h] = qn[:, h * LANE:(h + 1) * LANE].astype(BF16)
    kvc_ref[...] = xdot(5).astype(BF16)
    kv8 = xdot(6)
    blk = lambda i: kv8[:, i * LANE:(i + 1) * LANE]
    for g in range(NSA_GROUPS):
        ks_ref[:, 2 * g * LANE:(2 * g + 1) * LANE] = (blk(g) + pos_term).astype(BF16)
        ks_ref[:, (2 * g + 1) * LANE:(2 * g + 2) * LANE] = block_onehot
        kw_ref[:, g * LANE:(g + 1) * LANE] = (blk(2 + g) + pos_term).astype(BF16)
        vs_ref[:, g * LANE:(g + 1) * LANE] = (blk(4 + g) + ones_lane).astype(BF16)
        vw_ref[:, g * LANE:(g + 1) * LANE] = (blk(6 + g) + ones_lane).astype(BF16)
    gate_ref[...] = jax.nn.sigmoid(xdot(7))


def _head_blocks(w, n_heads, width):
    k = w.shape[0]
    w = jnp.pad(w.reshape(k, n_heads, width), ((0, 0), (0, 0), (0, LANE - width)))
    return w.reshape(k, n_heads * LANE)


def even_proj(x2, cos_t, sin_t, dpos, seq, w_in, q_norm_g, kv_norm_g, w_uq, w_uk, w_uv, tm=256):
    m, d = x2.shape
    tm = min(tm, seq)
    half = MLA_ROPE // 2
    gw = NSA_GROUPS * NSA_DH
    cuts = np.cumsum((MLA_Q_RANK, MLA_KV_RANK, MLA_ROPE, NSA_HEADS * NSA_DH) + (gw,) * 6)
    cuts = [0] + [int(c) for c in cuts]
    w_cq, w_ckv, w_kpe, w_q = (w_in[:, cuts[i]:cuts[i + 1]] for i in range(4))
    w_kc, w_vc, w_ks, w_vs, w_kw, w_vw = (w_in[:, cuts[4 + i]:cuts[5 + i]] for i in range(6))
    w_gate = w_in[:, cuts[10]:]
    w_kpe_sw = jnp.concatenate([-w_kpe[:, half:], w_kpe[:, :half]], axis=1)
    rope_pad = ((0, 0), (MLA_NOPE, LANE - MLA_NOPE - MLA_ROPE))
    w_gate_blk = jnp.pad(w_gate, ((0, 0), (0, LANE - w_gate.shape[1])))
    wx = jnp.concatenate(
        [w_cq, w_ckv, jnp.pad(w_kpe, rope_pad), jnp.pad(w_kpe_sw, rope_pad), _head_blocks(w_q, NSA_HEADS, NSA_DH),
         w_kc, w_vc] + [_head_blocks(w, NSA_GROUPS, NSA_DH) for w in (w_ks, w_kw, w_vs, w_vw)] + [w_gate_blk],
        axis=1).astype(BF16)

    qd = MLA_NOPE + MLA_ROPE
    uq = w_uq.reshape(MLA_Q_RANK, MLA_HEADS, qd)
    uq_sw = jnp.concatenate([jnp.zeros_like(uq[..., :MLA_NOPE]), -uq[..., MLA_NOPE + half:],
                             uq[..., MLA_NOPE:MLA_NOPE + half]], axis=-1)
    wuq = _head_blocks(uq.reshape(MLA_Q_RANK, -1), MLA_HEADS, qd).astype(BF16)
    wuqs = _head_blocks(uq_sw.reshape(MLA_Q_RANK, -1), MLA_HEADS, qd).astype(BF16)
    wuk = _head_blocks(w_uk, MLA_HEADS, MLA_NOPE).astype(BF16)
    wuv = _head_blocks(w_uv, MLA_HEADS, MLA_V).astype(BF16)
    slope_row = np.zeros((1, HW), np.float32)
    for h in range(NSA_HEADS):
        for ln in BIAS_LANES:
            slope_row[0, h * LANE + ln] = NSA_SLOPES[h]

    row = lambda w: pl.BlockSpec((tm, w), lambda i: (i, 0))
    sds = jax.ShapeDtypeStruct
    return pl.pallas_call(
        functools.partial(_even_proj_kernel, tiles_per_seq=seq // tm),
        grid=(m // tm,),
        in_specs=[row(d), row(LANE), row(LANE), row(1), _const_spec(wx.shape), _const_spec((1, MLA_Q_RANK)),
                  _const_spec((1, MLA_KV_RANK)), _const_spec(wuq.shape), _const_spec(wuqs.shape),
                  _const_spec(wuk.shape), _const_spec(wuv.shape), _const_spec((1, HW))],
        out_specs=[row(HW), row(HW), row(HW), pl.BlockSpec((NSA_HEADS, tm, LANE), lambda i: (0, i, 0)),
                   row(2 * LANE), row(4 * LANE), row(2 * LANE), row(2 * LANE), row(2 * LANE), row(LANE)],
        out_shape=[sds((m, HW), BF16)] * 3 + [sds((NSA_HEADS, m, LANE), BF16), sds((m, 2 * LANE), BF16),
                                              sds((m, 4 * LANE), BF16), sds((m, 2 * LANE), BF16),
                                              sds((m, 2 * LANE), BF16), sds((m, 2 * LANE), BF16), sds((m, LANE), F32)],
        compiler_params=_params(("parallel",)),
        name="even_proj",
    )(x2, cos_t, sin_t, dpos, wx, q_norm_g.reshape(1, -1), kv_norm_g.reshape(1, -1), wuq, wuqs, wuk, wuv,
      jnp.asarray(slope_row))


def _causal_pairs(nq, tq, tk):
    qi, ki = [], []
    for i in range(nq):
        for j in range(((i + 1) * tq - 1) // tk + 1):
            qi.append(i)
            ki.append(j)
    return jnp.asarray(qi, jnp.int32), jnp.asarray(ki, jnp.int32)


def _mla_kernel(qi_tab, ki_tab, q_ref, k_ref, v_ref, o_ref, m_ref, acc_ref, *, tq, tk):
    p = pl.program_id(1)
    qi = qi_tab[p]
    ki = ki_tab[p]

    @pl.when(ki == 0)
    def _():
        _init_state(m_ref, acc_ref)

    def run(masked):
        if masked:
            mask = (qi * tq + _iota((tq, tk), 0)) >= (ki * tk + _iota((tq, tk), 1))
        scores = lambda h: _dot_nt(q_ref[0, :, h * LANE:(h + 1) * LANE], k_ref[0, :, h * LANE:(h + 1) * LANE])
        s_next = scores(0)
        for h in range(MLA_HEADS):
            s = s_next
            if h + 1 < MLA_HEADS:
                s_next = scores(h + 1)
            if masked:
                s = jnp.where(mask, s, NEG)
            _flash_update(s, v_ref[0, :, h * LANE:(h + 1) * LANE], m_ref, acc_ref, h)

    crosses = (ki + 1) * tk - 1 > qi * tq

    @pl.when(crosses)
    def _():
        run(True)

    @pl.when(jnp.logical_not(crosses))
    def _():
        run(False)

    @pl.when(ki == ((qi + 1) * tq - 1) // tk)
    def _():
        for h in range(MLA_HEADS):
            o_ref[0, :, h * LANE:(h + 1) * LANE] = _normalized(acc_ref[h]).astype(BF16)


def mla_attention(q, k, v, tq=512, tk=512):
    b, s, hw = q.shape
    tq, tk = min(tq, s), min(tk, s)
    qi_tab, ki_tab = _causal_pairs(s // tq, tq, tk)
    qspec = pl.BlockSpec((1, tq, hw), lambda bb, p, qt, kt: (bb, qt[p], 0))
    kspec = pl.BlockSpec((1, tk, hw), lambda bb, p, qt, kt: (bb, kt[p], 0))
    return pl.pallas_call(
        functools.partial(_mla_kernel, tq=tq, tk=tk),
        grid_spec=pltpu.PrefetchScalarGridSpec(
            num_scalar_prefetch=2,
            grid=(b, int(qi_tab.shape[0])),
            in_specs=[qspec, kspec, kspec],
            out_specs=qspec,
            scratch_shapes=[pltpu.VMEM((MLA_HEADS, tq, LANE), F32), pltpu.VMEM((MLA_HEADS, tq, LANE), F32)],
        ),
        out_shape=jax.ShapeDtypeStruct((b, s, hw), BF16),
        compiler_params=_params(("parallel", "arbitrary")),
        name="mla_attention",
    )(qi_tab, ki_tab, q, k, v)


def _compress_kernel(h_ref, pos_ref, w1_ref, w2_ref, ext_ref, o_ref, *, is_value):
    n16 = h_ref.shape[2]
    half = w1_ref.shape[0] // 2
    lane = _iota((n16, LANE), 1)
    if is_value:
        extra = jnp.where(lane == ONES_LANE, 1.0, 0.0)
    else:
        extra = _bias_pieces(ext_ref[0], lane)
    bias = _dot(pos_ref[...], w1_ref[...])[0:1]
    for g in range(NSA_GROUPS):
        hg = h_ref[0, g]
        first = _dot(hg, w1_ref[:half, :])
        second = _dot(hg, w1_ref[half:, :])
        hid = first + pltpu.roll(second, n16 - 1, 0) + bias
        act = jax.nn.gelu(hid).astype(BF16)
        o_ref[0, g] = (_dot(act, w2_ref[...]) + extra).astype(BF16)


def nsa_compress(kv, cmp_pos, w1, w2, dpos_cend, is_value):
    b, s, _ = kv.shape
    n16 = s // CMP_STRIDE
    h = kv.reshape(b, n16, CMP_STRIDE, NSA_GROUPS, NSA_DH).transpose(0, 3, 1, 2, 4)
    h = h.reshape(b, NSA_GROUPS, n16, CMP_STRIDE * NSA_DH)
    pos = jnp.broadcast_to(cmp_pos.reshape(1, CMP_LEN * NSA_DH), (8, CMP_LEN * NSA_DH)).astype(BF16)
    w2p = jnp.pad(w2, ((0, 0), (0, LANE - NSA_DH))).astype(BF16)
    w1 = w1.astype(BF16)
    return pl.pallas_call(
        functools.partial(_compress_kernel, is_value=is_value),
        grid=(b,),
        in_specs=[pl.BlockSpec((1,) + h.shape[1:], lambda i: (i, 0, 0, 0)), _const_spec(pos.shape),
                  _const_spec(w1.shape), _const_spec(w2p.shape), pl.BlockSpec((1, n16, 1), lambda i: (i, 0, 0))],
        out_specs=pl.BlockSpec((1, NSA_GROUPS, n16, LANE), lambda i: (i, 0, 0, 0)),
        out_shape=jax.ShapeDtypeStruct((b, NSA_GROUPS, n16, LANE), BF16),
        compiler_params=_params(("parallel",)),
        name="nsa_compress",
    )(h, pos, w1, w2p, dpos_cend)


def _topk_mask_t(x, k):
    n = x.shape[0]
    ridx = _iota(x.shape, 0).astype(F32)
    sel = jnp.zeros(x.shape, F32)
    for _ in range(k):
        m = jnp.max(x, axis=0, keepdims=True)
        first = jnp.min(jnp.where(x == m, ridx, float(n)), axis=0, keepdims=True)
        hit = ridx == first
        sel = jnp.where(hit, 1.0, sel)
        x = jnp.where(hit, REMOVED, x)
    return sel


def _cmp_topk_kernel(q_ref, kc_ref, vc_ref, gate_ref, oc_ref, selm1_ref, *, tq, ncp, topk):
    qi = pl.program_id(1)
    t = qi * tq + _iota((tq, 1), 0)
    cend = _iota((1, ncp), 1) * CMP_STRIDE + (CMP_LEN - 1)
    mask = (cend <= t)[None]
    gates = gate_ref[0]
    cstart = _iota((ncp, LANE), 0) * CMP_STRIDE
    sstart = _iota((ncp, LANE), 1) * SEL_BLOCK
    overlap = jnp.where((cstart < sstart + SEL_BLOCK) & (cstart + CMP_LEN > sstart)
                        & (cstart < (ncp - 1) * CMP_STRIDE), 1.0, 0.0).astype(BF16)
    blk = _iota((tq, LANE), 1)
    chunk = _shr(t, SEL_BLOCK)
    for g in range(NSA_GROUPS):
        q = q_ref[g * NSA_HPG:(g + 1) * NSA_HPG].reshape(NSA_HPG * tq, LANE)
        s = _dot_nt(q, kc_ref[0, g]).reshape(NSA_HPG, tq, ncp)
        s = jnp.where(mask, s, NEG)
        e = jnp.where(mask, jnp.exp2(s - jnp.max(s, axis=-1, keepdims=True)), 0.0)
        p = e / jnp.maximum(jnp.sum(e, axis=-1, keepdims=True), 1e-30)
        psum = jnp.sum(p, axis=0)
        o = _dot(p.reshape(NSA_HPG * tq, ncp).astype(BF16), vc_ref[0, g])
        for n in range(NSA_HPG):
            h = g * NSA_HPG + n
            oh = jnp.where(blk < NSA_DH, o[n * tq:(n + 1) * tq] * gates[:, h:h + 1], 0.0)
            oc_ref[0, :, h * LANE:(h + 1) * LANE] = oh.astype(BF16)
        hi = psum.astype(BF16)
        r1 = psum - hi.astype(F32)
        mid = r1.astype(BF16)
        lo = (r1 - mid.astype(F32)).astype(BF16)
        imp = _dot(hi, overlap) + _dot(mid, overlap) + _dot(lo, overlap)
        imp = jnp.where((blk == 0) | (blk == chunk), BIG, imp)
        imp = jnp.where(blk <= chunk, imp, NEG)
        sel = _topk_mask_t(imp.T, topk).T
        selm1_ref[0, g] = (jnp.where(blk <= chunk, sel, 0.0) - 1.0).astype(BF16)


def nsa_cmp_topk(qn, kc, vc, gates, tq=128):
    _, b, s, _ = qn.shape
    tq = min(tq, s)
    ncp = kc.shape[2]
    assert s // SEL_BLOCK <= LANE
    topk = min(SEL_TOPK, s // SEL_BLOCK)
    kspec = pl.BlockSpec((1, NSA_GROUPS, ncp, LANE), lambda i, j: (i, 0, 0, 0))
    return pl.pallas_call(
        functools.partial(_cmp_topk_kernel, tq=tq, ncp=ncp, topk=topk),
        grid=(b, s // tq),
        in_specs=[pl.BlockSpec((NSA_HEADS, None, tq, LANE), lambda i, j: (0, i, j, 0)), kspec, kspec,
                  pl.BlockSpec((1, tq, LANE), lambda i, j: (i, j, 0))],
        out_specs=[pl.BlockSpec((1, tq, HW), lambda i, j: (i, j, 0)),
                   pl.BlockSpec((1, NSA_GROUPS, tq, LANE), lambda i, j: (i, 0, j, 0))],
        out_shape=[jax.ShapeDtypeStruct((b, s, HW), BF16), jax.ShapeDtypeStruct((b, NSA_GROUPS, s, LANE), BF16)],
        compiler_params=_params(("parallel", "parallel")),
        name="nsa_cmp_topk",
    )(qn, kc, vc, gates)


def _nsa_write(o_ref, gate_ref, acc_ref, tq, branch):
    gates = gate_ref[0]
    for h in range(NSA_HEADS):
        g, n = divmod(h, NSA_HPG)
        col = branch * NSA_HEADS + h
        o = _normalized(acc_ref[g, n * tq:(n + 1) * tq]) * gates[:, col:col + 1]
        o_ref[0, :, h * LANE:(h + 1) * LANE] = o.astype(BF16)


def _sel_attn_kernel(qi_tab, ki_tab, flags, q_ref, k_ref, v_ref, selm1_ref, gate_ref, o_ref,
                     lhs_ref, m_ref, acc_ref, *, tq, tk, nq, nk):
    b = pl.program_id(0)
    p = pl.program_id(1)
    qi = qi_tab[p]
    ki = ki_tab[p]
    last = ((qi + 1) * tq - 1) // tk

    @pl.when(ki == 0)
    def _():
        _init_state(m_ref, acc_ref)
        for g in range(NSA_GROUPS):
            lhs_ref[g, :, :LANE] = q_ref[g * NSA_HPG:(g + 1) * NSA_HPG].reshape(NSA_HPG * tq, LANE)
            lhs_ref[g, :, LANE:] = jnp.concatenate([selm1_ref[0, g]] * NSA_HPG, axis=0)

    def run(g, masked):
        s = _dot_nt(lhs_ref[g], k_ref[0, :, 2 * g * LANE:(2 * g + 2) * LANE])
        if masked:
            causal = (ki * tk + _iota((tq, tk), 1)) <= (qi * tq + _iota((tq, tk), 0))
            s = jnp.where(causal[None], s.reshape(NSA_HPG, tq, tk), NEG).reshape(NSA_HPG * tq, tk)
        _flash_update(s, v_ref[0, :, g * LANE:(g + 1) * LANE], m_ref, acc_ref, g)

    for g in range(NSA_GROUPS):
        active = flags[((b * NSA_GROUPS + g) * nq + qi) * nk + ki] > 0

        @pl.when(active & (ki == last))
        def _(g=g):
            run(g, True)

        @pl.when(active & (ki != last))
        def _(g=g):
            run(g, False)

    @pl.when(ki == last)
    def _():
        _nsa_write(o_ref, gate_ref, acc_ref, tq, branch=1)


def nsa_selected(qn, ks, vs, selm1, gates, tq=256, tk=512):
    _, b, s, _ = qn.shape
    tq, tk = min(tq, s), min(tk, s)
    nq, nk = s // tq, s // tk
    qi_tab, ki_tab = _causal_pairs(nq, tq, tk)
    bpt = tk // SEL_BLOCK
    flags = (selm1.reshape(b, NSA_GROUPS, nq, tq, LANE // bpt, bpt) > -0.5).any(axis=(3, 5))[..., :nk]
    flags = flags.astype(jnp.int32).reshape(-1)
    imap_q = lambda bb, p, qt, kt, fl: (bb, qt[p], 0)
    imap_k = lambda bb, p, qt, kt, fl: (bb, kt[p], 0)
    rows = NSA_HPG * tq
    return pl.pallas_call(
        functools.partial(_sel_attn_kernel, tq=tq, tk=tk, nq=nq, nk=nk),
        grid_spec=pltpu.PrefetchScalarGridSpec(
            num_scalar_prefetch=3,
            grid=(b, int(qi_tab.shape[0])),
            in_specs=[pl.BlockSpec((NSA_HEADS, None, tq, LANE), lambda bb, p, qt, kt, fl: (0, bb, qt[p], 0)),
                      pl.BlockSpec((1, tk, 4 * LANE), imap_k), pl.BlockSpec((1, tk, 2 * LANE), imap_k),
                      pl.BlockSpec((1, NSA_GROUPS, tq, LANE), lambda bb, p, qt, kt, fl: (bb, 0, qt[p], 0)),
                      pl.BlockSpec((1, tq, LANE), imap_q)],
            out_specs=pl.BlockSpec((1, tq, HW), imap_q),
            scratch_shapes=[pltpu.VMEM((NSA_GROUPS, rows, 2 * LANE), BF16), pltpu.VMEM((NSA_GROUPS, rows, LANE), F32),
                            pltpu.VMEM((NSA_GROUPS, rows, LANE), F32)],
        ),
        out_shape=jax.ShapeDtypeStruct((b, s, HW), BF16),
        compiler_params=_params(("parallel", "arbitrary")),
        name="nsa_selected",
    )(qi_tab, ki_tab, flags, qn, ks, vs, selm1, gates)


def _win_attn_kernel(q_ref, k_ref, v_ref, gate_ref, o_ref, m_ref, acc_ref, *, tq, nkt):
    qi = pl.program_id(1)
    j = pl.program_id(2)
    kt = qi - j

    @pl.when(j == 0)
    def _():
        _init_state(m_ref, acc_ref)

    @pl.when(kt >= 0)
    def _():
        diff = (qi * tq + _iota((tq, tq), 0)) - (kt * tq + _iota((tq, tq), 1))
        mask = ((diff >= 0) & (diff < WIN))[None]
        for g in range(NSA_GROUPS):
            q = q_ref[g * NSA_HPG:(g + 1) * NSA_HPG].reshape(NSA_HPG * tq, LANE)
            s = _dot_nt(q, k_ref[0, :, g * LANE:(g + 1) * LANE]).reshape(NSA_HPG, tq, tq)
            s = jnp.where(mask, s, NEG).reshape(NSA_HPG * tq, tq)
            _flash_update(s, v_ref[0, :, g * LANE:(g + 1) * LANE], m_ref, acc_ref, g)

    @pl.when(j == nkt - 1)
    def _():
        _nsa_write(o_ref, gate_ref, acc_ref, tq, branch=2)


def nsa_window(qn, kw, vw, gates, tq=512):
    _, b, s, _ = qn.shape
    tq = min(tq, s, WIN)
    nkt = WIN // tq + 1
    kmap = lambda bb, i, j: (bb, jnp.maximum(i - j, 0), 0)
    rows = NSA_HPG * tq
    return pl.pallas_call(
        functools.partial(_win_attn_kernel, tq=tq, nkt=nkt),
        grid=(b, s // tq, nkt),
        in_specs=[pl.BlockSpec((NSA_HEADS, None, tq, LANE), lambda bb, i, j: (0, bb, i, 0)),
                  pl.BlockSpec((1, tq, 2 * LANE), kmap), pl.BlockSpec((1, tq, 2 * LANE), kmap),
                  pl.BlockSpec((1, tq, LANE), lambda bb, i, j: (bb, i, 0))],
        out_specs=pl.BlockSpec((1, tq, HW), lambda bb, i, j: (bb, i, 0)),
        out_shape=jax.ShapeDtypeStruct((b, s, HW), BF16),
        scratch_shapes=[pltpu.VMEM((NSA_GROUPS, rows, LANE), F32), pltpu.VMEM((NSA_GROUPS, rows, LANE), F32)],
        compiler_params=_params(("parallel", "parallel", "arbitrary")),
        name="nsa_window",
    )(qn, kw, vw, gates)


def _even_out_ln_kernel(x_ref, om_ref, oc_ref, os_ref, ow_ref, wm_ref, wn_ref, g_ref, b_ref, o_ref):
    nsa = (oc_ref[...].astype(F32) + os_ref[...].astype(F32) + ow_ref[...].astype(F32)).astype(BF16)
    mix = _dot(om_ref[...], wm_ref[...]) + _dot(nsa, wn_ref[...])
    o_ref[...] = _layer_norm(ALPHA * x_ref[...] + mix, g_ref[...], b_ref[...])


def even_out_ln(x2, o_mla, o_c, o_s, o_w, w_mla, w_nsa, g, b, tm=512):
    m, d = x2.shape
    tm = min(tm, m)
    row = lambda w: pl.BlockSpec((tm, w), lambda i: (i, 0))
    return pl.pallas_call(
        _even_out_ln_kernel,
        grid=(m // tm,),
        in_specs=[row(d)] + [row(HW)] * 4 + [_const_spec(w_mla.shape), _const_spec(w_nsa.shape),
                                             _const_spec((1, d)), _const_spec((1, d))],
        out_specs=row(d),
        out_shape=jax.ShapeDtypeStruct((m, d), F32),
        compiler_params=_params(("parallel",)),
        name="even_out_ln",
    )(x2, o_mla, o_c, o_s, o_w, w_mla, w_nsa, g.reshape(1, d), b.reshape(1, d))


def _odd_out_ln_kernel(x_ref, o0_ref, o1_ref, o2_ref, l0_ref, l1_ref, l2_ref, w_ref, g_ref, b_ref, o_ref,
                       o_scr, l_scr):
    tm = x_ref.shape[0]
    for gi, (src, lsrc) in enumerate(((o0_ref, l0_ref), (o1_ref, l1_ref), (o2_ref, l2_ref))):
        dil = DIL_PATTERNS[gi][1]
        for r in range(dil):
            rows = pl.ds(r, tm // dil, stride=dil) if dil > 1 else slice(None)
            for h in range(DIL_GH):
                o_scr[gi, h, rows, :] = src[r, :, h * DIL_DH:(h + 1) * DIL_DH].astype(F32)
            l_scr[gi, rows, :] = lsrc[r]
    lses = [l_scr[gi] for gi in range(DIL_GROUPS)]
    top = jnp.maximum(jnp.maximum(lses[0], lses[1]), lses[2])
    es = [jnp.exp(l - top) for l in lses]
    den = es[0] + es[1] + es[2]
    wts = [e / den for e in es]
    cols = []
    for h in range(DIL_GH):
        merged = None
        for gi in range(DIL_GROUPS):
            term = wts[gi][:, h:h + 1] * o_scr[gi, h]
            merged = term if merged is None else merged + term
        cols.append(merged.astype(BF16))
    mix = _dot(jnp.concatenate(cols, axis=1), w_ref[...])
    o_ref[...] = _layer_norm(ALPHA * x_ref[...] + mix, g_ref[...], b_ref[...])


def odd_out_ln(x2, outs, lses, w, g, b, tm=512):
    m, d = x2.shape
    b_, _, s = outs[0].shape[0], None, outs[0].shape[1] * outs[0].shape[2]
    tm = min(tm, s)
    width = DIL_GH * DIL_DH
    row = pl.BlockSpec((None, tm, d), lambda bb, i: (bb, i, 0))
    cls = lambda gi, wd: pl.BlockSpec((None, DIL_PATTERNS[gi][1], tm // DIL_PATTERNS[gi][1], wd),
                                      lambda bb, i: (bb, 0, i, 0))
    out = pl.pallas_call(
        _odd_out_ln_kernel,
        grid=(b_, s // tm),
        in_specs=[row] + [cls(gi, width) for gi in range(DIL_GROUPS)] + [cls(gi, LANE) for gi in range(DIL_GROUPS)]
        + [_const_spec(w.shape), _const_spec((1, d)), _const_spec((1, d))],
        out_specs=row,
        out_shape=jax.ShapeDtypeStruct((b_, s, d), F32),
        scratch_shapes=[pltpu.VMEM((DIL_GROUPS, DIL_GH, tm, DIL_DH), F32), pltpu.VMEM((DIL_GROUPS, tm, LANE), F32)],
        compiler_params=_params(("parallel", "parallel")),
        name="odd_out_ln",
    )(x2.reshape(b_, s, d), *outs, *lses, w, g.reshape(1, d), b.reshape(1, d))
    return out.reshape(m, d)


def _odd_proj_kernel(x_ref, w_ref, o0_ref, o1_ref, o2_ref, xs_ref):
    tm, d_model = x_ref.shape
    width = DIL_GH * DIL_DH
    n_chunks = d_model // LANE
    for c in range(n_chunks):
        xs_ref[c] = x_ref[:, c * LANE:(c + 1) * LANE]
    for gi, o_ref in enumerate((o0_ref, o1_ref, o2_ref)):
        dil = DIL_PATTERNS[gi][1]
        if dil == 1:
            xg = x_ref[...]
        else:
            xg = jnp.concatenate(
                [jnp.concatenate([xs_ref[c, pl.ds(r, tm // dil, stride=dil), :] for r in range(dil)], axis=0)
                 for c in range(n_chunks)], axis=1)
        xg = xg.astype(BF16)
        for c in range(3):
            col = (gi * 3 + c) * width
            y = _dot(xg, w_ref[:, col:col + width])
            if c == 0:
                y = y * (DIL_SCALE * LOG2E)
            o_ref[:, :, c * width:(c + 1) * width] = y.astype(BF16).reshape(dil, tm // dil, width)


def odd_proj(x2, b, s, w_in, tm=512):
    m, d = x2.shape
    tm = min(tm, s)
    width = DIL_GH * DIL_DH
    w = w_in.reshape(d, 3, DIL_GROUPS, width).transpose(0, 2, 1, 3).reshape(d, 3 * DIL_GROUPS * width).astype(BF16)
    dils = [dil for _, dil in DIL_PATTERNS]
    return pl.pallas_call(
        _odd_proj_kernel,
        grid=(b, s // tm),
        in_specs=[pl.BlockSpec((None, tm, d), lambda bb, i: (bb, i, 0)), _const_spec(w.shape)],
        out_specs=[pl.BlockSpec((None, dil, tm // dil, 3 * width), lambda bb, i: (bb, 0, i, 0)) for dil in dils],
        out_shape=[jax.ShapeDtypeStruct((b, dil, s // dil, 3 * width), BF16) for dil in dils],
        scratch_shapes=[pltpu.VMEM((d // LANE, tm, LANE), F32)],
        compiler_params=_params(("parallel", "parallel")),
        name="odd_proj",
    )(x2.reshape(b, s, d), w)


def _dil_attn_kernel(slope_tab, pos0_tab, q_ref, kc_ref, kp_ref, vc_ref, vp_ref, pc_ref, pp_ref, pq_ref, o_ref,
                     lse_ref, *, group, tq):
    sub = DIL_SPAN
    b = pl.program_id(0)
    jt = pl.program_id(2)
    row = _iota((sub, 2 * sub), 0)
    col = _iota((sub, 2 * sub), 1)
    band = (col >= row) & (col <= row + sub)
    band_bias = jnp.where(band, 0.0, NEG)
    start_bias = jnp.where(band & (col >= sub), 0.0, NEG)
    pos0 = pos0_tab[b]
    lane = _iota((sub, LANE), 1)
    for i in range(tq // sub):
        cur = slice(i * sub, (i + 1) * sub)
        if i == 0:
            mask_bias = jnp.where(jt == 0, start_bias, band_bias)
            pk = jnp.concatenate([pp_ref[...], pc_ref[:, :sub]], axis=1)
        else:
            mask_bias = band_bias
            pk = pc_ref[:, (i - 1) * sub:(i + 1) * sub]
        dpos = pk - pos0
        dq = (pq_ref[cur, :] - pos0) * LN2
        lse_tile = jnp.zeros((sub, LANE), F32)
        for h in range(DIL_GH):
            hs = slice(h * DIL_DH, (h + 1) * DIL_DH)
            slope = slope_tab[group * DIL_GH + h]
            if i == 0:
                k = jnp.concatenate([kp_ref[:, hs], kc_ref[:sub, hs]], axis=0)
                v = jnp.concatenate([vp_ref[:, hs], vc_ref[:sub, hs]], axis=0)
            else:
                k = kc_ref[(i - 1) * sub:(i + 1) * sub, hs]
                v = vc_ref[(i - 1) * sub:(i + 1) * sub, hs]
            s = _dot_nt(q_ref[cur, hs], k) + (mask_bias + slope * dpos)
            m = jnp.max(s, axis=-1, keepdims=True)
            p = jnp.exp2(s - m)
            den = jnp.sum(p, axis=-1, keepdims=True)
            o_ref[cur, hs] = (_dot(p.astype(BF16), v) / den).astype(BF16)
            lse_tile = jnp.where(lane == h, m * LN2 + jnp.log(den) - slope * dq, lse_tile)
        lse_ref[cur, :] = lse_tile


def dilated_group_attention(qkv, pos_f, group, tq=512):
    b, d, sd, _ = qkv.shape
    w, dil = DIL_PATTERNS[group]
    assert w // dil == DIL_SPAN and dil == d
    tq = min(tq, sd)
    sub = DIL_SPAN
    width = DIL_GH * DIL_DH
    n_slopes = DIL_GROUPS * DIL_GH
    slopes = 2.0 ** (-ALIBI_MAX_BIAS * jnp.arange(1, n_slopes + 1, dtype=F32) / n_slopes) * LOG2E
    pos0 = pos_f[:, 0]
    posc = pos_f.reshape(b, sd, d).transpose(0, 2, 1).reshape(b, d, 1, sd)
    r = tq // sub
    cur = lambda c: pl.BlockSpec((None, None, tq, width), lambda bb, rr, j, *_: (bb, rr, j, c))
    prev = lambda c: pl.BlockSpec((None, None, sub, width),
                                  lambda bb, rr, j, *_: (bb, rr, jnp.maximum(j * r - 1, 0), c))
    return pl.pallas_call(
        functools.partial(_dil_attn_kernel, group=group, tq=tq),
        grid_spec=pltpu.PrefetchScalarGridSpec(
            num_scalar_prefetch=2,
            grid=(b, d, sd // tq),
            in_specs=[cur(0), cur(1), prev(1), cur(2), prev(2),
                      pl.BlockSpec((None, None, 1, tq), lambda bb, rr, j, *_: (bb, rr, 0, j)),
                      pl.BlockSpec((None, None, 1, sub), lambda bb, rr, j, *_: (bb, rr, 0, jnp.maximum(j * r - 1, 0))),
                      pl.BlockSpec((None, None, tq, 1), lambda bb, rr, j, *_: (bb, rr, j, 0))],
            out_specs=[pl.BlockSpec((None, None, tq, width), lambda bb, rr, j, *_: (bb, rr, j, 0)),
                       pl.BlockSpec((None, None, tq, LANE), lambda bb, rr, j, *_: (bb, rr, j, 0))],
        ),
        out_shape=[jax.ShapeDtypeStruct((b, d, sd, width), BF16), jax.ShapeDtypeStruct((b, d, sd, LANE), F32)],
        compiler_params=_params(("parallel", "parallel", "parallel")),
        name=f"dilated_attention_g{group}",
    )(slopes, pos0, qkv, qkv, qkv, qkv, qkv, posc, posc, posc.reshape(b, d, sd, 1))


def even_mixer_ln(x2, b, s, pos_f, w_in, q_norm_g, kv_norm_g, w_uq, w_uk, w_uv,
                  cmp_pos, cmp_k_w1, cmp_k_w2, cmp_v_w1, cmp_v_w2, w_out, ln_g, ln_b):
    m = b * s
    d = x2.shape[1]
    half = MLA_ROPE // 2
    inv = ROPE_THETA ** (-jnp.arange(half, dtype=F32) / half)
    ang = (pos_f[..., None] * inv).reshape(m, half)
    ones = jnp.ones((m, MLA_NOPE), F32)
    zeros = jnp.zeros((m, LANE - MLA_NOPE - MLA_ROPE), F32)
    cos_t = jnp.concatenate([ones, jnp.cos(ang), jnp.cos(ang), zeros], axis=1)
    sin_t = jnp.concatenate([0.0 * ones, jnp.sin(ang), jnp.sin(ang), zeros], axis=1)
    dpos = (pos_f - pos_f[:, :1]) * LOG2E
    ncp = s // CMP_STRIDE
    dpos_cend = jnp.pad(dpos[:, CMP_LEN - 1::CMP_STRIDE], ((0, 0), (0, 1))).reshape(b, ncp, 1)

    qm, km, vm, qn, kvc, ks, kw, vs, vw, gates = even_proj(
        x2, cos_t, sin_t, dpos.reshape(m, 1), s, w_in, q_norm_g, kv_norm_g, w_uq, w_uk, w_uv)
    o_mla = mla_attention(qm.reshape(b, s, HW), km.reshape(b, s, HW), vm.reshape(b, s, HW))

    kvc = kvc.reshape(b, s, 2, LANE)
    kc = nsa_compress(kvc[:, :, 0], cmp_pos, cmp_k_w1, cmp_k_w2, dpos_cend, is_value=False)
    vc = nsa_compress(kvc[:, :, 1], cmp_pos, cmp_v_w1, cmp_v_w2, dpos_cend, is_value=True)
    qn = qn.reshape(NSA_HEADS, b, s, LANE)
    gates = gates.reshape(b, s, LANE)
    o_c, selm1 = nsa_cmp_topk(qn, kc, vc, gates)
    o_s = nsa_selected(qn, ks.reshape(b, s, 4 * LANE), vs.reshape(b, s, 2 * LANE), selm1, gates)
    o_w = nsa_window(qn, kw.reshape(b, s, 2 * LANE), vw.reshape(b, s, 2 * LANE), gates)

    pad_rows = lambda w, n, width: jnp.pad(w.reshape(n, width, d), ((0, 0), (0, LANE - width), (0, 0))).reshape(n * LANE, d)
    w_mla = pad_rows(w_out[:MLA_HEADS * MLA_V], MLA_HEADS, MLA_V).astype(BF16)
    w_nsa = pad_rows(w_out[MLA_HEADS * MLA_V:], NSA_HEADS, NSA_DH).astype(BF16)
    return even_out_ln(x2, o_mla.reshape(m, HW), o_c.reshape(m, HW), o_s.reshape(m, HW), o_w.reshape(m, HW),
                       w_mla, w_nsa, ln_g, ln_b)


def odd_mixer_ln(x2, b, s, pos_f, w_in, w_out, ln_g, ln_b):
    qkvs = odd_proj(x2, b, s, w_in)
    parts = [dilated_group_attention(qkvs[g], pos_f, g) for g in range(DIL_GROUPS)]
    return odd_out_ln(x2, [p[0] for p in parts], [p[1] for p in parts], w_out.astype(BF16), ln_g, ln_b)


def kernel(x, positions, ln1_g, ln1_b, ffn1_w_gate, ffn1_w_up, ffn1_w_down, mix_in_even, mla_q_norm, mla_kv_norm, mla_w_uq, mla_w_uk, mla_w_uv, nsa_cmp_pos, nsa_cmp_k_w1, nsa_cmp_k_w2, nsa_cmp_v_w1, nsa_cmp_v_w2, mix_out_even, mix_in_odd, mix_out_odd, ln2_g, ln2_b, ffn2_w_gate, ffn2_w_up, ffn2_w_down, ln3_g, ln3_b):
    b, s, d = x.shape
    x2 = x.reshape(b * s, d)
    pos_f = positions.astype(F32)
    for i in range(DEPTH):
        j = i // 2
        x2 = ffn_ln(x2, ffn1_w_gate[i].astype(BF16), ffn1_w_up[i].astype(BF16), ffn1_w_down[i].astype(BF16),
                    ln1_g[i], ln1_b[i])
        if i % 2 == 0:
            x2 = even_mixer_ln(x2, b, s, pos_f, mix_in_even[j], mla_q_norm[j], mla_kv_norm[j], mla_w_uq[j],
                               mla_w_uk[j], mla_w_uv[j], nsa_cmp_pos[j], nsa_cmp_k_w1[j], nsa_cmp_k_w2[j],
                               nsa_cmp_v_w1[j], nsa_cmp_v_w2[j], mix_out_even[j], ln2_g[i], ln2_b[i])
        else:
            x2 = odd_mixer_ln(x2, b, s, pos_f, mix_in_odd[j], mix_out_odd[j], ln2_g[i], ln2_b[i])
        x2 = ffn_ln(x2, ffn2_w_gate[i].astype(BF16), ffn2_w_up[i].astype(BF16), ffn2_w_down[i].astype(BF16),
                    ln3_g[i], ln3_b[i])
    return x2.reshape(b, s, d)
```

```python
import functools
import math

import numpy as np
import jax
import jax.numpy as jnp
from jax import lax
from jax.experimental import pallas as pl
from jax.experimental.pallas import tpu as pltpu

F32 = jnp.float32
BF16 = jnp.bfloat16

DEPTH = 2
LN_EPS = 1e-5
RMS_EPS = 1e-6
ALPHA = (2 * DEPTH) ** 0.25
HALF_STEP = 0.5
NEG = -1e30
BIG = 1e9
MASK_BIG = 1e30
REMOVED = -3.0e38
ALIBI_MAX_BIAS = 8.0
LOG2E = math.log2(math.e)
LN2 = math.log(2.0)
LANE = 128

MLA_HEADS = 8
MLA_Q_RANK = 384
MLA_KV_RANK = 256
MLA_NOPE = 64
MLA_ROPE = 32
MLA_V = 64
ROPE_THETA = 10000.0
MLA_SCALE = (MLA_NOPE + MLA_ROPE) ** -0.5

NSA_HEADS = 8
NSA_GROUPS = 2
NSA_HPG = 4
NSA_DH = 64
CMP_LEN = 32
CMP_STRIDE = 16
CMP_HIDDEN = 256
SEL_BLOCK = 64
SEL_TOPK = 16
WIN = 512
NSA_SCALE = NSA_DH ** -0.5
NSA_SLOPES = tuple(2.0 ** (-ALIBI_MAX_BIAS * (i + 1) / NSA_HEADS) for i in range(NSA_HEADS))
ONES_LANE = 64
BIAS_LANES = (64, 65, 66)

DIL_PATTERNS = ((128, 1), (512, 4), (2048, 16))
DIL_GROUPS = 3
DIL_GH = 4
DIL_DH = 128
DIL_SCALE = DIL_DH ** -0.5
DIL_SPAN = 128

VMEM_LIMIT = 48 * 1024 * 1024


def _iota(shape, dim):
    return lax.broadcasted_iota(jnp.int32, shape, dim)


def _shr(x, pow2):
    return jnp.right_shift(x, int(pow2).bit_length() - 1)


def _dot(a, b):
    return jnp.dot(a, b, preferred_element_type=F32)


def _dot_nt(a, b):
    return lax.dot_general(a, b, (((1,), (1,)), ((), ())), preferred_element_type=F32)


def _const_spec(shape):
    zeros = (0,) * len(shape)
    return pl.BlockSpec(shape, lambda *_: zeros, pipeline_mode=pl.Buffered(1))


def _params(sem):
    return pltpu.CompilerParams(dimension_semantics=sem, vmem_limit_bytes=VMEM_LIMIT)


def _layer_norm(z, g, b):
    mu = jnp.mean(z, axis=-1, keepdims=True)
    zc = z - mu
    var = jnp.mean(zc * zc, axis=-1, keepdims=True)
    return zc * lax.rsqrt(var + LN_EPS) * g + b


def _rms_norm(z, g):
    return z * lax.rsqrt(jnp.mean(z * z, axis=-1, keepdims=True) + RMS_EPS) * g


def _bias_pieces(d, lane):
    x = jnp.broadcast_to(d, lane.shape)
    hi = x.astype(BF16).astype(F32)
    r = x - hi
    mid = r.astype(BF16).astype(F32)
    lo = (r - mid).astype(BF16).astype(F32)
    return jnp.where(lane == BIAS_LANES[0], hi,
                     jnp.where(lane == BIAS_LANES[1], mid, jnp.where(lane == BIAS_LANES[2], lo, 0.0)))


def _flash_update(s, v, m_ref, acc_ref, idx):
    m_old = m_ref[idx]
    m_new = jnp.maximum(m_old, jnp.max(s, axis=-1, keepdims=True))
    p = jnp.exp2(s - jnp.tile(m_new, (1, s.shape[1] // LANE))).astype(BF16)
    acc_ref[idx] = jnp.exp2(m_old - m_new) * acc_ref[idx] + _dot(p, v)
    m_ref[idx] = m_new


def _init_state(m_ref, acc_ref):
    m_ref[...] = jnp.full(m_ref.shape, NEG, F32)
    acc_ref[...] = jnp.zeros(acc_ref.shape, F32)


def _normalized(acc):
    lane = _iota(acc.shape, 1)
    o = acc / jnp.maximum(acc[:, ONES_LANE:ONES_LANE + 1], 1e-30)
    return jnp.where(lane < ONES_LANE, o, 0.0)


def _ffn_ln_kernel(x_ref, wg_ref, wu_ref, wd_ref, g_ref, b_ref, o_ref, *, n_chunks):
    x = x_ref[...]
    xb = x.astype(BF16)
    c = wg_ref.shape[1] // n_chunks
    y = None
    for i in range(n_chunks):
        gt = _dot(xb, wg_ref[:, i * c:(i + 1) * c])
        up = _dot(xb, wu_ref[:, i * c:(i + 1) * c])
        h = (gt * jax.nn.sigmoid(gt) * up).astype(BF16)
        part = _dot(h, wd_ref[i * c:(i + 1) * c, :])
        y = part if y is None else y + part
    o_ref[...] = _layer_norm(ALPHA * x + HALF_STEP * y, g_ref[...], b_ref[...])


def ffn_ln(x2, wg, wu, wd, g, b, tm=512):
    m, d = x2.shape
    tm = min(tm, m)
    row = pl.BlockSpec((tm, d), lambda i: (i, 0))
    return pl.pallas_call(
        functools.partial(_ffn_ln_kernel, n_chunks=2),
        grid=(m // tm,),
        in_specs=[row, _const_spec(wg.shape), _const_spec(wu.shape), _const_spec(wd.shape),
                  _const_spec((1, d)), _const_spec((1, d))],
        out_specs=row,
        out_shape=jax.ShapeDtypeStruct((m, d), F32),
        compiler_params=_params(("parallel",)),
        name="ffn_ln",
    )(x2, wg, wu, wd, g.reshape(1, d), b.reshape(1, d))


HW = MLA_HEADS * LANE
EVEN_X_COLS = (MLA_Q_RANK, MLA_KV_RANK, LANE, LANE, HW, 2 * LANE, 8 * LANE, LANE)
EVEN_X_OFFS = tuple(int(v) for v in np.cumsum((0,) + EVEN_X_COLS))


def _even_proj_kernel(x_ref, cos_ref, sin_ref, dpos_ref, wx_ref, qg_ref, kvg_ref, wuq_ref, wuqs_ref, wuk_ref,
                      wuv_ref, slope_ref, qm_ref, km_ref, vm_ref, qn_ref, kvc_ref, ks_ref, kw_ref, vs_ref, vw_ref,
                      gate_ref, *, tiles_per_seq):
    tm = x_ref.shape[0]
    xb = x_ref[...].astype(BF16)
    cos = cos_ref[...]
    sin = sin_ref[...]
    lane = _iota((tm, LANE), 1)
    ones_lane = jnp.where(lane == ONES_LANE, 1.0, 0.0)
    pos_term = _bias_pieces(dpos_ref[...], lane)
    tok = (pl.program_id(0) % tiles_per_seq) * tm + _iota((tm, LANE), 0)
    block_onehot = jnp.where(lane == _shr(tok, SEL_BLOCK), MASK_BIG, 0.0).astype(BF16)

    def xdot(i):
        return _dot(xb, wx_ref[:, EVEN_X_OFFS[i]:EVEN_X_OFFS[i + 1]])

    cq = _rms_norm(xdot(0), qg_ref[...]).astype(BF16)
    ckv = _rms_norm(xdot(1), kvg_ref[...]).astype(BF16)
    k_rot = xdot(2) * cos + xdot(3) * sin
    for h in range(MLA_HEADS):
        sl = slice(h * LANE, (h + 1) * LANE)
        q = _dot(cq, wuq_ref[:, sl]) * cos + _dot(cq, wuqs_ref[:, sl]) * sin
        qm_ref[:, sl] = (q * (MLA_SCALE * LOG2E)).astype(BF16)
        km_ref[:, sl] = (_dot(ckv, wuk_ref[:, sl]) + k_rot).astype(BF16)
        vm_ref[:, sl] = (_dot(ckv, wuv_ref[:, sl]) + ones_lane).astype(BF16)
    qn = xdot(4) * (NSA_SCALE * LOG2E) + slope_ref[...]
    for h in range(NSA_HEADS):
        qn_ref[h] = qn[:, h * LANE:(h + 1) * LANE].astype(BF16)
    kvc_ref[...] = xdot(5).astype(BF16)
    kv8 = xdot(6)
    blk = lambda i: kv8[:, i * LANE:(i + 1) * LANE]
    for g in range(NSA_GROUPS):
        ks_ref[:, 2 * g * LANE:(2 * g + 1) * LANE] = (blk(g) + pos_term).astype(BF16)
        ks_ref[:, (2 * g + 1) * LANE:(2 * g + 2) * LANE] = block_onehot
        kw_ref[:, g * LANE:(g + 1) * LANE] = (blk(2 + g) + pos_term).astype(BF16)
        vs_ref[:, g * LANE:(g + 1) * LANE] = (blk(4 + g) + ones_lane).astype(BF16)
        vw_ref[:, g * LANE:(g + 1) * LANE] = (blk(6 + g) + ones_lane).astype(BF16)
    gate_ref[...] = jax.nn.sigmoid(xdot(7))


def _head_blocks(w, n_heads, width):
    k = w.shape[0]
    w = jnp.pad(w.reshape(k, n_heads, width), ((0, 0), (0, 0), (0, LANE - width)))
    return w.reshape(k, n_heads * LANE)


def even_proj(x2, cos_t, sin_t, dpos, seq, w_in, q_norm_g, kv_norm_g, w_uq, w_uk, w_uv, tm=256):
    m, d = x2.shape
    tm = min(tm, seq)
    half = MLA_ROPE // 2
    gw = NSA_GROUPS * NSA_DH
    cuts = np.cumsum((MLA_Q_RANK, MLA_KV_RANK, MLA_ROPE, NSA_HEADS * NSA_DH) + (gw,) * 6)
    cuts = [0] + [int(c) for c in cuts]
    w_cq, w_ckv, w_kpe, w_q = (w_in[:, cuts[i]:cuts[i + 1]] for i in range(4))
    w_kc, w_vc, w_ks, w_vs, w_kw, w_vw = (w_in[:, cuts[4 + i]:cuts[5 + i]] for i in range(6))
    w_gate = w_in[:, cuts[10]:]
    w_kpe_sw = jnp.concatenate([-w_kpe[:, half:], w_kpe[:, :half]], axis=1)
    rope_pad = ((0, 0), (MLA_NOPE, LANE - MLA_NOPE - MLA_ROPE))
    w_gate_blk = jnp.pad(w_gate, ((0, 0), (0, LANE - w_gate.shape[1])))
    wx = jnp.concatenate(
        [w_cq, w_ckv, jnp.pad(w_kpe, rope_pad), jnp.pad(w_kpe_sw, rope_pad), _head_blocks(w_q, NSA_HEADS, NSA_DH),
         w_kc, w_vc] + [_head_blocks(w, NSA_GROUPS, NSA_DH) for w in (w_ks, w_kw, w_vs, w_vw)] + [w_gate_blk],
        axis=1).astype(BF16)

    qd = MLA_NOPE + MLA_ROPE
    uq = w_uq.reshape(MLA_Q_RANK, MLA_HEADS, qd)
    uq_sw = jnp.concatenate([jnp.zeros_like(uq[..., :MLA_NOPE]), -uq[..., MLA_NOPE + half:],
                             uq[..., MLA_NOPE:MLA_NOPE + half]], axis=-1)
    wuq = _head_blocks(uq.reshape(MLA_Q_RANK, -1), MLA_HEADS, qd).astype(BF16)
    wuqs = _head_blocks(uq_sw.reshape(MLA_Q_RANK, -1), MLA_HEADS, qd).astype(BF16)
    wuk = _head_blocks(w_uk, MLA_HEADS, MLA_NOPE).astype(BF16)
    wuv = _head_blocks(w_uv, MLA_HEADS, MLA_V).astype(BF16)
    slope_row = np.zeros((1, HW), np.float32)
    for h in range(NSA_HEADS):
        for ln in BIAS_LANES:
            slope_row[0, h * LANE + ln] = NSA_SLOPES[h]

    row = lambda w: pl.BlockSpec((tm, w), lambda i: (i, 0))
    sds = jax.ShapeDtypeStruct
    return pl.pallas_call(
        functools.partial(_even_proj_kernel, tiles_per_seq=seq // tm),
        grid=(m // tm,),
        in_specs=[row(d), row(LANE), row(LANE), row(1), _const_spec(wx.shape), _const_spec((1, MLA_Q_RANK)),
                  _const_spec((1, MLA_KV_RANK)), _const_spec(wuq.shape), _const_spec(wuqs.shape),
                  _const_spec(wuk.shape), _const_spec(wuv.shape), _const_spec((1, HW))],
        out_specs=[row(HW), row(HW), row(HW), pl.BlockSpec((NSA_HEADS, tm, LANE), lambda i: (0, i, 0)),
                   row(2 * LANE), row(4 * LANE), row(2 * LANE), row(2 * LANE), row(2 * LANE), row(LANE)],
        out_shape=[sds((m, HW), BF16)] * 3 + [sds((NSA_HEADS, m, LANE), BF16), sds((m, 2 * LANE), BF16),
                                              sds((m, 4 * LANE), BF16), sds((m, 2 * LANE), BF16),
                                              sds((m, 2 * LANE), BF16), sds((m, 2 * LANE), BF16), sds((m, LANE), F32)],
        compiler_params=_params(("parallel",)),
        name="even_proj",
    )(x2, cos_t, sin_t, dpos, wx, q_norm_g.reshape(1, -1), kv_norm_g.reshape(1, -1), wuq, wuqs, wuk, wuv,
      jnp.asarray(slope_row))


def _causal_pairs(nq, tq, tk):
    qi, ki = [], []
    for i in range(nq):
        for j in range(((i + 1) * tq - 1) // tk + 1):
            qi.append(i)
            ki.append(j)
    return jnp.asarray(qi, jnp.int32), jnp.asarray(ki, jnp.int32)


def _mla_kernel(qi_tab, ki_tab, q_ref, k_ref, v_ref, o_ref, m_ref, acc_ref, *, tq, tk):
    p = pl.program_id(1)
    qi = qi_tab[p]
    ki = ki_tab[p]

    @pl.when(ki == 0)
    def _():
        _init_state(m_ref, acc_ref)

    def run(masked):
        if masked:
            mask = (qi * tq + _iota((tq, tk), 0)) >= (ki * tk + _iota((tq, tk), 1))
        scores = lambda h: _dot_nt(q_ref[0, :, h * LANE:(h + 1) * LANE], k_ref[0, :, h * LANE:(h + 1) * LANE])
        s_next = scores(0)
        for h in range(MLA_HEADS):
            s = s_next
            if h + 1 < MLA_HEADS:
                s_next = scores(h + 1)
            if masked:
                s = jnp.where(mask, s, NEG)
            _flash_update(s, v_ref[0, :, h * LANE:(h + 1) * LANE], m_ref, acc_ref, h)

    crosses = (ki + 1) * tk - 1 > qi * tq

    @pl.when(crosses)
    def _():
        run(True)

    @pl.when(jnp.logical_not(crosses))
    def _():
        run(False)

    @pl.when(ki == ((qi + 1) * tq - 1) // tk)
    def _():
        for h in range(MLA_HEADS):
            o_ref[0, :, h * LANE:(h + 1) * LANE] = _normalized(acc_ref[h]).astype(BF16)


def mla_attention(q, k, v, tq=512, tk=512):
    b, s, hw = q.shape
    tq, tk = min(tq, s), min(tk, s)
    qi_tab, ki_tab = _causal_pairs(s // tq, tq, tk)
    qspec = pl.BlockSpec((1, tq, hw), lambda bb, p, qt, kt: (bb, qt[p], 0))
    kspec = pl.BlockSpec((1, tk, hw), lambda bb, p, qt, kt: (bb, kt[p], 0))
    return pl.pallas_call(
        functools.partial(_mla_kernel, tq=tq, tk=tk),
        grid_spec=pltpu.PrefetchScalarGridSpec(
            num_scalar_prefetch=2,
            grid=(b, int(qi_tab.shape[0])),
            in_specs=[qspec, kspec, kspec],
            out_specs=qspec,
            scratch_shapes=[pltpu.VMEM((MLA_HEADS, tq, LANE), F32), pltpu.VMEM((MLA_HEADS, tq, LANE), F32)],
        ),
        out_shape=jax.ShapeDtypeStruct((b, s, hw), BF16),
        compiler_params=_params(("parallel", "arbitrary")),
        name="mla_attention",
    )(qi_tab, ki_tab, q, k, v)


def _compress_kernel(h_ref, pos_ref, w1_ref, w2_ref, ext_ref, o_ref, *, is_value):
    n16 = h_ref.shape[2]
    half = w1_ref.shape[0] // 2
    lane = _iota((n16, LANE), 1)
    if is_value:
        extra = jnp.where(lane == ONES_LANE, 1.0, 0.0)
    else:
        extra = _bias_pieces(ext_ref[0], lane)
    bias = _dot(pos_ref[...], w1_ref[...])[0:1]
    for g in range(NSA_GROUPS):
        hg = h_ref[0, g]
        first = _dot(hg, w1_ref[:half, :])
        second = _dot(hg, w1_ref[half:, :])
        hid = first + pltpu.roll(second, n16 - 1, 0) + bias
        act = jax.nn.gelu(hid).astype(BF16)
        o_ref[0, g] = (_dot(act, w2_ref[...]) + extra).astype(BF16)


def nsa_compress(kv, cmp_pos, w1, w2, dpos_cend, is_value):
    b, s, _ = kv.shape
    n16 = s // CMP_STRIDE
    h = kv.reshape(b, n16, CMP_STRIDE, NSA_GROUPS, NSA_DH).transpose(0, 3, 1, 2, 4)
    h = h.reshape(b, NSA_GROUPS, n16, CMP_STRIDE * NSA_DH)
    pos = jnp.broadcast_to(cmp_pos.reshape(1, CMP_LEN * NSA_DH), (8, CMP_LEN * NSA_DH)).astype(BF16)
    w2p = jnp.pad(w2, ((0, 0), (0, LANE - NSA_DH))).astype(BF16)
    w1 = w1.astype(BF16)
    return pl.pallas_call(
        functools.partial(_compress_kernel, is_value=is_value),
        grid=(b,),
        in_specs=[pl.BlockSpec((1,) + h.shape[1:], lambda i: (i, 0, 0, 0)), _const_spec(pos.shape),
                  _const_spec(w1.shape), _const_spec(w2p.shape), pl.BlockSpec((1, n16, 1), lambda i: (i, 0, 0))],
        out_specs=pl.BlockSpec((1, NSA_GROUPS, n16, LANE), lambda i: (i, 0, 0, 0)),
        out_shape=jax.ShapeDtypeStruct((b, NSA_GROUPS, n16, LANE), BF16),
        compiler_params=_params(("parallel",)),
        name="nsa_compress",
    )(h, pos, w1, w2p, dpos_cend)


def _topk_mask_t(x, k):
    n = x.shape[0]
    ridx = _iota(x.shape, 0).astype(F32)
    sel = jnp.zeros(x.shape, F32)
    for _ in range(k):
        m = jnp.max(x, axis=0, keepdims=True)
        first = jnp.min(jnp.where(x == m, ridx, float(n)), axis=0, keepdims=True)
        hit = ridx == first
        sel = jnp.where(hit, 1.0, sel)
        x = jnp.where(hit, REMOVED, x)
    return sel


def _cmp_topk_kernel(q_ref, kc_ref, vc_ref, gate_ref, oc_ref, selm1_ref, *, tq, ncp, topk):
    qi = pl.program_id(1)
    t = qi * tq + _iota((tq, 1), 0)
    cend = _iota((1, ncp), 1) * CMP_STRIDE + (CMP_LEN - 1)
    mask = (cend <= t)[None]
    gates = gate_ref[0]
    cstart = _iota((ncp, LANE), 0) * CMP_STRIDE
    sstart = _iota((ncp, LANE), 1) * SEL_BLOCK
    overlap = jnp.where((cstart < sstart + SEL_BLOCK) & (cstart + CMP_LEN > sstart)
                        & (cstart < (ncp - 1) * CMP_STRIDE), 1.0, 0.0).astype(BF16)
    blk = _iota((tq, LANE), 1)
    chunk = _shr(t, SEL_BLOCK)
    for g in range(NSA_GROUPS):
        q = q_ref[g * NSA_HPG:(g + 1) * NSA_HPG].reshape(NSA_HPG * tq, LANE)
        s = _dot_nt(q, kc_ref[0, g]).reshape(NSA_HPG, tq, ncp)
        s = jnp.where(mask, s, NEG)
        e = jnp.where(mask, jnp.exp2(s - jnp.max(s, axis=-1, keepdims=True)), 0.0)
        p = e / jnp.maximum(jnp.sum(e, axis=-1, keepdims=True), 1e-30)
        psum = jnp.sum(p, axis=0)
        o = _dot(p.reshape(NSA_HPG * tq, ncp).astype(BF16), vc_ref[0, g])
        for n in range(NSA_HPG):
            h = g * NSA_HPG + n
            oh = jnp.where(blk < NSA_DH, o[n * tq:(n + 1) * tq] * gates[:, h:h + 1], 0.0)
            oc_ref[0, :, h * LANE:(h + 1) * LANE] = oh.astype(BF16)
        hi = psum.astype(BF16)
        r1 = psum - hi.astype(F32)
        mid = r1.astype(BF16)
        lo = (r1 - mid.astype(F32)).astype(BF16)
        imp = _dot(hi, overlap) + _dot(mid, overlap) + _dot(lo, overlap)
        imp = jnp.where((blk == 0) | (blk == chunk), BIG, imp)
        imp = jnp.where(blk <= chunk, imp, NEG)
        sel = _topk_mask_t(imp.T, topk).T
        selm1_ref[0, g] = (jnp.where(blk <= chunk, sel, 0.0) - 1.0).astype(BF16)


def nsa_cmp_topk(qn, kc, vc, gates, tq=128):
    _, b, s, _ = qn.shape
    tq = min(tq, s)
    ncp = kc.shape[2]
    assert s // SEL_BLOCK <= LANE
    topk = min(SEL_TOPK, s // SEL_BLOCK)
    kspec = pl.BlockSpec((1, NSA_GROUPS, ncp, LANE), lambda i, j: (i, 0, 0, 0))
    return pl.pallas_call(
        functools.partial(_cmp_topk_kernel, tq=tq, ncp=ncp, topk=topk),
        grid=(b, s // tq),
        in_specs=[pl.BlockSpec((NSA_HEADS, None, tq, LANE), lambda i, j: (0, i, j, 0)), kspec, kspec,
                  pl.BlockSpec((1, tq, LANE), lambda i, j: (i, j, 0))],
        out_specs=[pl.BlockSpec((1, tq, HW), lambda i, j: (i, j, 0)),
                   pl.BlockSpec((1, NSA_GROUPS, tq, LANE), lambda i, j: (i, 0, j, 0))],
        out_shape=[jax.ShapeDtypeStruct((b, s, HW), BF16), jax.ShapeDtypeStruct((b, NSA_GROUPS, s, LANE), BF16)],
        compiler_params=_params(("parallel", "parallel")),
        name="nsa_cmp_topk",
    )(qn, kc, vc, gates)


def _nsa_write(o_ref, gate_ref, acc_ref, tq, branch):
    gates = gate_ref[0]
    for h in range(NSA_HEADS):
        g, n = divmod(h, NSA_HPG)
        col = branch * NSA_HEADS + h
        o = _normalized(acc_ref[g, n * tq:(n + 1) * tq]) * gates[:, col:col + 1]
        o_ref[0, :, h * LANE:(h + 1) * LANE] = o.astype(BF16)


def _sel_attn_kernel(flags, q_ref, k_ref, v_ref, selm1_ref, gate_ref, o_ref, lhs_ref, m_ref, acc_ref, *, t, nq):
    b = pl.program_id(0)
    qi = pl.program_id(1)
    _init_state(m_ref, acc_ref)
    for g in range(NSA_GROUPS):
        lhs_ref[g, :, :LANE] = q_ref[g * NSA_HPG:(g + 1) * NSA_HPG].reshape(NSA_HPG * t, LANE)
        lhs_ref[g, :, LANE:] = jnp.concatenate([selm1_ref[0, g]] * NSA_HPG, axis=0)

    half = NSA_HPG * t // 2
    chains = [(g, i) for g in range(NSA_GROUPS) for i in range(2)]

    def update(ki, diagonal):
        rows = pl.ds(pl.multiple_of(ki * t, t), t)
        scores = [_dot_nt(lhs_ref[g, i * half:(i + 1) * half], k_ref[rows, 2 * g * LANE:(2 * g + 2) * LANE])
                  for g, i in chains]
        for (g, i), s in zip(chains, scores):
            if diagonal:
                causal = _iota((t, t), 1) <= _iota((t, t), 0)
                s = jnp.where(causal[None], s.reshape(NSA_HPG // 2, t, t), NEG).reshape(half, t)
            _flash_update(s, v_ref[rows, g * LANE:(g + 1) * LANE], m_ref, acc_ref,
                          (g, slice(i * half, (i + 1) * half)))

    base = (b * nq + qi) * nq

    def body(ki, carry):
        @pl.when(flags[base + ki] > 0)
        def _():
            update(ki, False)
        return carry

    lax.fori_loop(0, qi, body, 0)
    update(qi, True)
    _nsa_write(o_ref, gate_ref, acc_ref, t, branch=1)


def nsa_selected(qn, ks, vs, selm1, gates, t=256):
    _, b, s, _ = qn.shape
    t = min(t, s)
    nq = s // t
    bpt = t // SEL_BLOCK
    flags = (selm1.reshape(b, NSA_GROUPS, nq, t, LANE // bpt, bpt) > -0.5).any(axis=(1, 3, 5))[..., :nq]
    flags = flags.astype(jnp.int32).reshape(-1)
    imap_q = lambda bb, i, fl: (bb, i, 0)
    rows = NSA_HPG * t
    return pl.pallas_call(
        functools.partial(_sel_attn_kernel, t=t, nq=nq),
        grid_spec=pltpu.PrefetchScalarGridSpec(
            num_scalar_prefetch=1,
            grid=(b, nq),
            in_specs=[pl.BlockSpec((NSA_HEADS, None, t, LANE), lambda bb, i, fl: (0, bb, i, 0)),
                      pl.BlockSpec((None, s, 4 * LANE), lambda bb, i, fl: (bb, 0, 0)),
                      pl.BlockSpec((None, s, 2 * LANE), lambda bb, i, fl: (bb, 0, 0)),
                      pl.BlockSpec((1, NSA_GROUPS, t, LANE), lambda bb, i, fl: (bb, 0, i, 0)),
                      pl.BlockSpec((1, t, LANE), imap_q)],
            out_specs=pl.BlockSpec((1, t, HW), imap_q),
            scratch_shapes=[pltpu.VMEM((NSA_GROUPS, rows, 2 * LANE), BF16), pltpu.VMEM((NSA_GROUPS, rows, LANE), F32),
                            pltpu.VMEM((NSA_GROUPS, rows, LANE), F32)],
        ),
        out_shape=jax.ShapeDtypeStruct((b, s, HW), BF16),
        compiler_params=_params(("parallel", "arbitrary")),
        name="nsa_selected",
    )(flags, qn, ks, vs, selm1, gates)


def _win_attn_kernel(q_ref, k_ref, v_ref, gate_ref, o_ref, m_ref, acc_ref, *, tq, nkt):
    qi = pl.program_id(1)
    j = pl.program_id(2)
    kt = qi - j

    @pl.when(j == 0)
    def _():
        _init_state(m_ref, acc_ref)

    @pl.when(kt >= 0)
    def _():
        diff = (qi * tq + _iota((tq, tq), 0)) - (kt * tq + _iota((tq, tq), 1))
        mask = ((diff >= 0) & (diff < WIN))[None]
        half = NSA_HPG // 2
        chains = [(g, i) for g in range(NSA_GROUPS) for i in range(2)]
        scores = [_dot_nt(q_ref[g * NSA_HPG + i * half:g * NSA_HPG + (i + 1) * half].reshape(half * tq, LANE),
                          k_ref[0, :, g * LANE:(g + 1) * LANE]) for g, i in chains]
        for (g, i), s in zip(chains, scores):
            s = jnp.where(mask, s.reshape(half, tq, tq), NEG).reshape(half * tq, tq)
            _flash_update(s, v_ref[0, :, g * LANE:(g + 1) * LANE], m_ref, acc_ref,
                          (g, slice(i * half * tq, (i + 1) * half * tq)))

    @pl.when(j == nkt - 1)
    def _():
        _nsa_write(o_ref, gate_ref, acc_ref, tq, branch=2)


def nsa_window(qn, kw, vw, gates, tq=512):
    _, b, s, _ = qn.shape
    tq = min(tq, s, WIN)
    nkt = WIN // tq + 1
    kmap = lambda bb, i, j: (bb, jnp.maximum(i - j, 0), 0)
    rows = NSA_HPG * tq
    return pl.pallas_call(
        functools.partial(_win_attn_kernel, tq=tq, nkt=nkt),
        grid=(b, s // tq, nkt),
        in_specs=[pl.BlockSpec((NSA_HEADS, None, tq, LANE), lambda bb, i, j: (0, bb, i, 0)),
                  pl.BlockSpec((1, tq, 2 * LANE), kmap), pl.BlockSpec((1, tq, 2 * LANE), kmap),
                  pl.BlockSpec((1, tq, LANE), lambda bb, i, j: (bb, i, 0))],
        out_specs=pl.BlockSpec((1, tq, HW), lambda bb, i, j: (bb, i, 0)),
        out_shape=jax.ShapeDtypeStruct((b, s, HW), BF16),
        scratch_shapes=[pltpu.VMEM((NSA_GROUPS, rows, LANE), F32), pltpu.VMEM((NSA_GROUPS, rows, LANE), F32)],
        compiler_params=_params(("parallel", "parallel", "arbitrary")),
        name="nsa_window",
    )(qn, kw, vw, gates)


def _even_out_ln_kernel(x_ref, om_ref, oc_ref, os_ref, ow_ref, wm_ref, wn_ref, g_ref, b_ref, o_ref):
    nsa = (oc_ref[...].astype(F32) + os_ref[...].astype(F32) + ow_ref[...].astype(F32)).astype(BF16)
    mix = _dot(om_ref[...], wm_ref[...]) + _dot(nsa, wn_ref[...])
    o_ref[...] = _layer_norm(ALPHA * x_ref[...] + mix, g_ref[...], b_ref[...])


def even_out_ln(x2, o_mla, o_c, o_s, o_w, w_mla, w_nsa, g, b, tm=512):
    m, d = x2.shape
    tm = min(tm, m)
    row = lambda w: pl.BlockSpec((tm, w), lambda i: (i, 0))
    return pl.pallas_call(
        _even_out_ln_kernel,
        grid=(m // tm,),
        in_specs=[row(d)] + [row(HW)] * 4 + [_const_spec(w_mla.shape), _const_spec(w_nsa.shape),
                                             _const_spec((1, d)), _const_spec((1, d))],
        out_specs=row(d),
        out_shape=jax.ShapeDtypeStruct((m, d), F32),
        compiler_params=_params(("parallel",)),
        name="even_out_ln",
    )(x2, o_mla, o_c, o_s, o_w, w_mla, w_nsa, g.reshape(1, d), b.reshape(1, d))


def _odd_out_ln_kernel(x_ref, o0_ref, o1_ref, o2_ref, l0_ref, l1_ref, l2_ref, w_ref, g_ref, b_ref, o_ref,
                       o_scr, l_scr):
    tm = x_ref.shape[0]
    for gi, (src, lsrc) in enumerate(((o0_ref, l0_ref), (o1_ref, l1_ref), (o2_ref, l2_ref))):
        dil = DIL_PATTERNS[gi][1]
        for r in range(dil):
            rows = pl.ds(r, tm // dil, stride=dil) if dil > 1 else slice(None)
            for h in range(DIL_GH):
                o_scr[gi, h, rows, :] = src[r, :, h * DIL_DH:(h + 1) * DIL_DH].astype(F32)
            l_scr[gi, rows, :] = lsrc[r]
    lses = [l_scr[gi] for gi in range(DIL_GROUPS)]
    top = jnp.maximum(jnp.maximum(lses[0], lses[1]), lses[2])
    es = [jnp.exp(l - top) for l in lses]
    den = es[0] + es[1] + es[2]
    wts = [e / den for e in es]
    cols = []
    for h in range(DIL_GH):
        merged = None
        for gi in range(DIL_GROUPS):
            term = wts[gi][:, h:h + 1] * o_scr[gi, h]
            merged = term if merged is None else merged + term
        cols.append(merged.astype(BF16))
    mix = _dot(jnp.concatenate(cols, axis=1), w_ref[...])
    o_ref[...] = _layer_norm(ALPHA * x_ref[...] + mix, g_ref[...], b_ref[...])


def odd_out_ln(x2, outs, lses, w, g, b, tm=512):
    m, d = x2.shape
    b_, _, s = outs[0].shape[0], None, outs[0].shape[1] * outs[0].shape[2]
    tm = min(tm, s)
    width = DIL_GH * DIL_DH
    row = pl.BlockSpec((None, tm, d), lambda bb, i: (bb, i, 0))
    cls = lambda gi, wd: pl.BlockSpec((None, DIL_PATTERNS[gi][1], tm // DIL_PATTERNS[gi][1], wd),
                                      lambda bb, i: (bb, 0, i, 0))
    out = pl.pallas_call(
        _odd_out_ln_kernel,
        grid=(b_, s // tm),
        in_specs=[row] + [cls(gi, width) for gi in range(DIL_GROUPS)] + [cls(gi, LANE) for gi in range(DIL_GROUPS)]
        + [_const_spec(w.shape), _const_spec((1, d)), _const_spec((1, d))],
        out_specs=row,
        out_shape=jax.ShapeDtypeStruct((b_, s, d), F32),
        scratch_shapes=[pltpu.VMEM((DIL_GROUPS, DIL_GH, tm, DIL_DH), F32), pltpu.VMEM((DIL_GROUPS, tm, LANE), F32)],
        compiler_params=_params(("parallel", "parallel")),
        name="odd_out_ln",
    )(x2.reshape(b_, s, d), *outs, *lses, w, g.reshape(1, d), b.reshape(1, d))
    return out.reshape(m, d)


def _odd_proj_kernel(x_ref, w_ref, o0_ref, o1_ref, o2_ref, xs_ref):
    tm, d_model = x_ref.shape
    width = DIL_GH * DIL_DH
    n_chunks = d_model // LANE
    for c in range(n_chunks):
        xs_ref[c] = x_ref[:, c * LANE:(c + 1) * LANE]
    for gi, o_ref in enumerate((o0_ref, o1_ref, o2_ref)):
        dil = DIL_PATTERNS[gi][1]
        if dil == 1:
            xg = x_ref[...]
        else:
            xg = jnp.concatenate(
                [jnp.concatenate([xs_ref[c, pl.ds(r, tm // dil, stride=dil), :] for r in range(dil)], axis=0)
                 for c in range(n_chunks)], axis=1)
        xg = xg.astype(BF16)
        for c in range(3):
            col = (gi * 3 + c) * width
            y = _dot(xg, w_ref[:, col:col + width])
            if c == 0:
                y = y * (DIL_SCALE * LOG2E)
            o_ref[:, :, c * width:(c + 1) * width] = y.astype(BF16).reshape(dil, tm // dil, width)


def odd_proj(x2, b, s, w_in, tm=512):
    m, d = x2.shape
    tm = min(tm, s)
    width = DIL_GH * DIL_DH
    w = w_in.reshape(d, 3, DIL_GROUPS, width).transpose(0, 2, 1, 3).reshape(d, 3 * DIL_GROUPS * width).astype(BF16)
    dils = [dil for _, dil in DIL_PATTERNS]
    return pl.pallas_call(
        _odd_proj_kernel,
        grid=(b, s // tm),
        in_specs=[pl.BlockSpec((None, tm, d), lambda bb, i: (bb, i, 0)), _const_spec(w.shape)],
        out_specs=[pl.BlockSpec((None, dil, tm // dil, 3 * width), lambda bb, i: (bb, 0, i, 0)) for dil in dils],
        out_shape=[jax.ShapeDtypeStruct((b, dil, s // dil, 3 * width), BF16) for dil in dils],
        scratch_shapes=[pltpu.VMEM((d // LANE, tm, LANE), F32)],
        compiler_params=_params(("parallel", "parallel")),
        name="odd_proj",
    )(x2.reshape(b, s, d), w)


def _dil_attn_kernel(slope_tab, pos0_tab, q_ref, kc_ref, kp_ref, vc_ref, vp_ref, pc_ref, pp_ref, pq_ref, o_ref,
                     lse_ref, *, group, tq):
    sub = DIL_SPAN
    b = pl.program_id(0)
    jt = pl.program_id(2)
    row = _iota((sub, 2 * sub), 0)
    col = _iota((sub, 2 * sub), 1)
    band = (col >= row) & (col <= row + sub)
    band_bias = jnp.where(band, 0.0, NEG)
    start_bias = jnp.where(band & (col >= sub), 0.0, NEG)
    pos0 = pos0_tab[b]
    lane = _iota((sub, LANE), 1)
    for i in range(tq // sub):
        cur = slice(i * sub, (i + 1) * sub)
        if i == 0:
            mask_bias = jnp.where(jt == 0, start_bias, band_bias)
            pk = jnp.concatenate([pp_ref[...], pc_ref[:, :sub]], axis=1)
        else:
            mask_bias = band_bias
            pk = pc_ref[:, (i - 1) * sub:(i + 1) * sub]
        dpos = pk - pos0
        dq = (pq_ref[cur, :] - pos0) * LN2
        lse_tile = jnp.zeros((sub, LANE), F32)
        for h in range(DIL_GH):
            hs = slice(h * DIL_DH, (h + 1) * DIL_DH)
            slope = slope_tab[group * DIL_GH + h]
            if i == 0:
                k = jnp.concatenate([kp_ref[:, hs], kc_ref[:sub, hs]], axis=0)
                v = jnp.concatenate([vp_ref[:, hs], vc_ref[:sub, hs]], axis=0)
            else:
                k = kc_ref[(i - 1) * sub:(i + 1) * sub, hs]
                v = vc_ref[(i - 1) * sub:(i + 1) * sub, hs]
            s = _dot_nt(q_ref[cur, hs], k) + (mask_bias + slope * dpos)
            m = jnp.max(s, axis=-1, keepdims=True)
            p = jnp.exp2(s - m)
            den = jnp.sum(p, axis=-1, keepdims=True)
            o_ref[cur, hs] = (_dot(p.astype(BF16), v) / den).astype(BF16)
            lse_tile = jnp.where(lane == h, m * LN2 + jnp.log(den) - slope * dq, lse_tile)
        lse_ref[cur, :] = lse_tile


def dilated_group_attention(qkv, pos_f, group, tq=512):
    b, d, sd, _ = qkv.shape
    w, dil = DIL_PATTERNS[group]
    assert w // dil == DIL_SPAN and dil == d
    tq = min(tq, sd)
    sub = DIL_SPAN
    width = DIL_GH * DIL_DH
    n_slopes = DIL_GROUPS * DIL_GH
    slopes = 2.0 ** (-ALIBI_MAX_BIAS * jnp.arange(1, n_slopes + 1, dtype=F32) / n_slopes) * LOG2E
    pos0 = pos_f[:, 0]
    posc = pos_f.reshape(b, sd, d).transpose(0, 2, 1).reshape(b, d, 1, sd)
    r = tq // sub
    cur = lambda c: pl.BlockSpec((None, None, tq, width), lambda bb, rr, j, *_: (bb, rr, j, c))
    prev = lambda c: pl.BlockSpec((None, None, sub, width),
                                  lambda bb, rr, j, *_: (bb, rr, jnp.maximum(j * r - 1, 0), c))
    return pl.pallas_call(
        functools.partial(_dil_attn_kernel, group=group, tq=tq),
        grid_spec=pltpu.PrefetchScalarGridSpec(
            num_scalar_prefetch=2,
            grid=(b, d, sd // tq),
            in_specs=[cur(0), cur(1), prev(1), cur(2), prev(2),
                      pl.BlockSpec((None, None, 1, tq), lambda bb, rr, j, *_: (bb, rr, 0, j)),
                      pl.BlockSpec((None, None, 1, sub), lambda bb, rr, j, *_: (bb, rr, 0, jnp.maximum(j * r - 1, 0))),
                      pl.BlockSpec((None, None, tq, 1), lambda bb, rr, j, *_: (bb, rr, j, 0))],
            out_specs=[pl.BlockSpec((None, None, tq, width), lambda bb, rr, j, *_: (bb, rr, j, 0)),
                       pl.BlockSpec((None, None, tq, LANE), lambda bb, rr, j, *_: (bb, rr, j, 0))],
        ),
        out_shape=[jax.ShapeDtypeStruct((b, d, sd, width), BF16), jax.ShapeDtypeStruct((b, d, sd, LANE), F32)],
        compiler_params=_params(("parallel", "parallel", "parallel")),
        name=f"dilated_attention_g{group}",
    )(slopes, pos0, qkv, qkv, qkv, qkv, qkv, posc, posc, posc.reshape(b, d, sd, 1))


def even_mixer_ln(x2, b, s, pos_f, w_in, q_norm_g, kv_norm_g, w_uq, w_uk, w_uv,
                  cmp_pos, cmp_k_w1, cmp_k_w2, cmp_v_w1, cmp_v_w2, w_out, ln_g, ln_b):
    m = b * s
    d = x2.shape[1]
    half = MLA_ROPE // 2
    inv = ROPE_THETA ** (-jnp.arange(half, dtype=F32) / half)
    ang = (pos_f[..., None] * inv).reshape(m, half)
    ones = jnp.ones((m, MLA_NOPE), F32)
    zeros = jnp.zeros((m, LANE - MLA_NOPE - MLA_ROPE), F32)
    cos_t = jnp.concatenate([ones, jnp.cos(ang), jnp.cos(ang), zeros], axis=1)
    sin_t = jnp.concatenate([0.0 * ones, jnp.sin(ang), jnp.sin(ang), zeros], axis=1)
    dpos = (pos_f - pos_f[:, :1]) * LOG2E
    ncp = s // CMP_STRIDE
    dpos_cend = jnp.pad(dpos[:, CMP_LEN - 1::CMP_STRIDE], ((0, 0), (0, 1))).reshape(b, ncp, 1)

    qm, km, vm, qn, kvc, ks, kw, vs, vw, gates = even_proj(
        x2, cos_t, sin_t, dpos.reshape(m, 1), s, w_in, q_norm_g, kv_norm_g, w_uq, w_uk, w_uv)
    o_mla = mla_attention(qm.reshape(b, s, HW), km.reshape(b, s, HW), vm.reshape(b, s, HW))

    kvc = kvc.reshape(b, s, 2, LANE)
    kc = nsa_compress(kvc[:, :, 0], cmp_pos, cmp_k_w1, cmp_k_w2, dpos_cend, is_value=False)
    vc = nsa_compress(kvc[:, :, 1], cmp_pos, cmp_v_w1, cmp_v_w2, dpos_cend, is_value=True)
    qn = qn.reshape(NSA_HEADS, b, s, LANE)
    gates = gates.reshape(b, s, LANE)
    o_c, selm1 = nsa_cmp_topk(qn, kc, vc, gates)
    o_s = nsa_selected(qn, ks.reshape(b, s, 4 * LANE), vs.reshape(b, s, 2 * LANE), selm1, gates)
    o_w = nsa_window(qn, kw.reshape(b, s, 2 * LANE), vw.reshape(b, s, 2 * LANE), gates)

    pad_rows = lambda w, n, width: jnp.pad(w.reshape(n, width, d), ((0, 0), (0, LANE - width), (0, 0))).reshape(n * LANE, d)
    w_mla = pad_rows(w_out[:MLA_HEADS * MLA_V], MLA_HEADS, MLA_V).astype(BF16)
    w_nsa = pad_rows(w_out[MLA_HEADS * MLA_V:], NSA_HEADS, NSA_DH).astype(BF16)
    return even_out_ln(x2, o_mla.reshape(m, HW), o_c.reshape(m, HW), o_s.reshape(m, HW), o_w.reshape(m, HW),
                       w_mla, w_nsa, ln_g, ln_b)


def odd_mixer_ln(x2, b, s, pos_f, w_in, w_out, ln_g, ln_b):
    qkvs = odd_proj(x2, b, s, w_in)
    parts = [dilated_group_attention(qkvs[g], pos_f, g) for g in range(DIL_GROUPS)]
    return odd_out_ln(x2, [p[0] for p in parts], [p[1] for p in parts], w_out.astype(BF16), ln_g, ln_b)


def kernel(x, positions, ln1_g, ln1_b, ffn1_w_gate, ffn1_w_up, ffn1_w_down, mix_in_even, mla_q_norm, mla_kv_norm, mla_w_uq, mla_w_uk, mla_w_uv, nsa_cmp_pos, nsa_cmp_k_w1, nsa_cmp_k_w2, nsa_cmp_v_w1, nsa_cmp_v_w2, mix_out_even, mix_in_odd, mix_out_odd, ln2_g, ln2_b, ffn2_w_gate, ffn2_w_up, ffn2_w_down, ln3_g, ln3_b):
    b, s, d = x.shape
    x2 = x.reshape(b * s, d)
    pos_f = positions.astype(F32)
    for i in range(DEPTH):
        j = i // 2
        x2 = ffn_ln(x2, ffn1_w_gate[i].astype(BF16), ffn1_w_up[i].astype(BF16), ffn1_w_down[i].astype(BF16),
                    ln1_g[i], ln1_b[i])
        if i % 2 == 0:
            x2 = even_mixer_ln(x2, b, s, pos_f, mix_in_even[j], mla_q_norm[j], mla_kv_norm[j], mla_w_uq[j],
                               mla_w_uk[j], mla_w_uv[j], nsa_cmp_pos[j], nsa_cmp_k_w1[j], nsa_cmp_k_w2[j],
                               nsa_cmp_v_w1[j], nsa_cmp_v_w2[j], mix_out_even[j], ln2_g[i], ln2_b[i])
        else:
            x2 = odd_mixer_ln(x2, b, s, pos_f, mix_in_odd[j], mix_out_odd[j], ln2_g[i], ln2_b[i])
        x2 = ffn_ln(x2, ffn2_w_gate[i].astype(BF16), ffn2_w_up[i].astype(BF16), ffn2_w_down[i].astype(BF16),
                    ln3_g[i], ln3_b[i])
    return x2.reshape(b, s, d)
```

```python
import functools
import math

import numpy as np
import jax
import jax.numpy as jnp
from jax import lax
from jax.experimental import pallas as pl
from jax.experimental.pallas import tpu as pltpu

F32 = jnp.float32
BF16 = jnp.bfloat16

DEPTH = 2
LN_EPS = 1e-5
RMS_EPS = 1e-6
ALPHA = (2 * DEPTH) ** 0.25
HALF_STEP = 0.5
NEG = -1e30
BIG = 1e9
MASK_BIG = 1e30
REMOVED = -3.0e38
ALIBI_MAX_BIAS = 8.0
LOG2E = math.log2(math.e)
LN2 = math.log(2.0)
LANE = 128

MLA_HEADS = 8
MLA_Q_RANK = 384
MLA_KV_RANK = 256
MLA_NOPE = 64
MLA_ROPE = 32
MLA_V = 64
ROPE_THETA = 10000.0
MLA_SCALE = (MLA_NOPE + MLA_ROPE) ** -0.5

NSA_HEADS = 8
NSA_GROUPS = 2
NSA_HPG = 4
NSA_DH = 64
CMP_LEN = 32
CMP_STRIDE = 16
CMP_HIDDEN = 256
SEL_BLOCK = 64
SEL_TOPK = 16
WIN = 512
NSA_SCALE = NSA_DH ** -0.5
NSA_SLOPES = tuple(2.0 ** (-ALIBI_MAX_BIAS * (i + 1) / NSA_HEADS) for i in range(NSA_HEADS))
ONES_LANE = 64
BIAS_LANES = (64, 65, 66)

DIL_PATTERNS = ((128, 1), (512, 4), (2048, 16))
DIL_GROUPS = 3
DIL_GH = 4
DIL_DH = 128
DIL_SCALE = DIL_DH ** -0.5
DIL_SPAN = 128

VMEM_LIMIT = 48 * 1024 * 1024


def _iota(shape, dim):
    return lax.broadcasted_iota(jnp.int32, shape, dim)


def _shr(x, pow2):
    return jnp.right_shift(x, int(pow2).bit_length() - 1)


def _dot(a, b):
    return jnp.dot(a, b, preferred_element_type=F32)


def _dot_nt(a, b):
    return lax.dot_general(a, b, (((1,), (1,)), ((), ())), preferred_element_type=F32)


def _const_spec(shape):
    zeros = (0,) * len(shape)
    return pl.BlockSpec(shape, lambda *_: zeros, pipeline_mode=pl.Buffered(1))


def _params(sem):
    return pltpu.CompilerParams(dimension_semantics=sem, vmem_limit_bytes=VMEM_LIMIT)


def _layer_norm(z, g, b):
    mu = jnp.mean(z, axis=-1, keepdims=True)
    zc = z - mu
    var = jnp.mean(zc * zc, axis=-1, keepdims=True)
    return zc * lax.rsqrt(var + LN_EPS) * g + b


def _rms_norm(z, g):
    return z * lax.rsqrt(jnp.mean(z * z, axis=-1, keepdims=True) + RMS_EPS) * g


def _bias_pieces(d, lane):
    x = jnp.broadcast_to(d, lane.shape)
    hi = x.astype(BF16).astype(F32)
    r = x - hi
    mid = r.astype(BF16).astype(F32)
    lo = (r - mid).astype(BF16).astype(F32)
    return jnp.where(lane == BIAS_LANES[0], hi,
                     jnp.where(lane == BIAS_LANES[1], mid, jnp.where(lane == BIAS_LANES[2], lo, 0.0)))


def _flash_update(s, v, m_ref, acc_ref, idx):
    m_old = m_ref[idx]
    m_new = jnp.maximum(m_old, jnp.max(s, axis=-1, keepdims=True))
    p = jnp.exp2(s - jnp.tile(m_new, (1, s.shape[1] // LANE))).astype(BF16)
    acc_ref[idx] = jnp.exp2(m_old - m_new) * acc_ref[idx] + _dot(p, v)
    m_ref[idx] = m_new


def _init_state(m_ref, acc_ref):
    m_ref[...] = jnp.full(m_ref.shape, NEG, F32)
    acc_ref[...] = jnp.zeros(acc_ref.shape, F32)


def _normalized(acc):
    lane = _iota(acc.shape, 1)
    o = acc / jnp.maximum(acc[:, ONES_LANE:ONES_LANE + 1], 1e-30)
    return jnp.where(lane < ONES_LANE, o, 0.0)


def _ffn_ln_kernel(x_ref, wg_ref, wu_ref, wd_ref, g_ref, b_ref, o_ref, *, n_chunks):
    x = x_ref[...]
    xb = x.astype(BF16)
    c = wg_ref.shape[1] // n_chunks
    y = None
    for i in range(n_chunks):
        gt = _dot(xb, wg_ref[:, i * c:(i + 1) * c])
        up = _dot(xb, wu_ref[:, i * c:(i + 1) * c])
        h = (gt * jax.nn.sigmoid(gt) * up).astype(BF16)
        part = _dot(h, wd_ref[i * c:(i + 1) * c, :])
        y = part if y is None else y + part
    o_ref[...] = _layer_norm(ALPHA * x + HALF_STEP * y, g_ref[...], b_ref[...])


def ffn_ln(x2, wg, wu, wd, g, b, tm=512):
    m, d = x2.shape
    tm = min(tm, m)
    row = pl.BlockSpec((tm, d), lambda i: (i, 0))
    return pl.pallas_call(
        functools.partial(_ffn_ln_kernel, n_chunks=2),
        grid=(m // tm,),
        in_specs=[row, _const_spec(wg.shape), _const_spec(wu.shape), _const_spec(wd.shape),
                  _const_spec((1, d)), _const_spec((1, d))],
        out_specs=row,
        out_shape=jax.ShapeDtypeStruct((m, d), F32),
        compiler_params=_params(("parallel",)),
        name="ffn_ln",
    )(x2, wg, wu, wd, g.reshape(1, d), b.reshape(1, d))


HW = MLA_HEADS * LANE
EVEN_X_COLS = (MLA_Q_RANK, MLA_KV_RANK, LANE, LANE, HW, 2 * LANE, 8 * LANE, LANE)
EVEN_X_OFFS = tuple(int(v) for v in np.cumsum((0,) + EVEN_X_COLS))


def _even_proj_kernel(x_ref, cos_ref, sin_ref, dpos_ref, wx_ref, qg_ref, kvg_ref, wuq_ref, wuqs_ref, wuk_ref,
                      wuv_ref, slope_ref, qm_ref, km_ref, vm_ref, qn_ref, kvc_ref, ks_ref, kw_ref, vs_ref, vw_ref,
                      gate_ref, *, tiles_per_seq):
    tm = x_ref.shape[0]
    xb = x_ref[...].astype(BF16)
    cos = cos_ref[...]
    sin = sin_ref[...]
    lane = _iota((tm, LANE), 1)
    ones_lane = jnp.where(lane == ONES_LANE, 1.0, 0.0)
    pos_term = _bias_pieces(dpos_ref[...], lane)
    tok = (pl.program_id(0) % tiles_per_seq) * tm + _iota((tm, LANE), 0)
    block_onehot = jnp.where(lane == _shr(tok, SEL_BLOCK), MASK_BIG, 0.0).astype(BF16)

    def xdot(i):
        return _dot(xb, wx_ref[:, EVEN_X_OFFS[i]:EVEN_X_OFFS[i + 1]])

    cq = _rms_norm(xdot(0), qg_ref[...]).astype(BF16)
    ckv = _rms_norm(xdot(1), kvg_ref[...]).astype(BF16)
    k_rot = xdot(2) * cos + xdot(3) * sin
    for h in range(MLA_HEADS):
        sl = slice(h * LANE, (h + 1) * LANE)
        q = _dot(cq, wuq_ref[:, sl]) * cos + _dot(cq, wuqs_ref[:, sl]) * sin
        qm_ref[:, sl] = (q * (MLA_SCALE * LOG2E)).astype(BF16)
        km_ref[:, sl] = (_dot(ckv, wuk_ref[:, sl]) + k_rot).astype(BF16)
        vm_ref[:, sl] = (_dot(ckv, wuv_ref[:, sl]) + ones_lane).astype(BF16)
    qn = xdot(4) * (NSA_SCALE * LOG2E) + slope_ref[...]
    for h in range(NSA_HEADS):
        qn_ref[h] = qn[:, h * LANE:(h + 1) * LANE].astype(BF16)
    kvc = xdot(5).astype(BF16)
    kvc_ref[0] = kvc[:, :LANE]
    kvc_ref[1] = kvc[:, LANE:]
    kv8 = xdot(6)
    blk = lambda i: kv8[:, i * LANE:(i + 1) * LANE]
    for g in range(NSA_GROUPS):
        ks_ref[:, 2 * g * LANE:(2 * g + 1) * LANE] = (blk(g) + pos_term).astype(BF16)
        ks_ref[:, (2 * g + 1) * LANE:(2 * g + 2) * LANE] = block_onehot
        kw_ref[:, g * LANE:(g + 1) * LANE] = (blk(2 + g) + pos_term).astype(BF16)
        vs_ref[:, g * LANE:(g + 1) * LANE] = (blk(4 + g) + ones_lane).astype(BF16)
        vw_ref[:, g * LANE:(g + 1) * LANE] = (blk(6 + g) + ones_lane).astype(BF16)
    gate_ref[...] = jax.nn.sigmoid(xdot(7))


def _head_blocks(w, n_heads, width):
    k = w.shape[0]
    w = jnp.pad(w.reshape(k, n_heads, width), ((0, 0), (0, 0), (0, LANE - width)))
    return w.reshape(k, n_heads * LANE)


def even_proj(x2, cos_t, sin_t, dpos, seq, w_in, q_norm_g, kv_norm_g, w_uq, w_uk, w_uv, tm=256):
    m, d = x2.shape
    tm = min(tm, seq)
    half = MLA_ROPE // 2
    gw = NSA_GROUPS * NSA_DH
    cuts = np.cumsum((MLA_Q_RANK, MLA_KV_RANK, MLA_ROPE, NSA_HEADS * NSA_DH) + (gw,) * 6)
    cuts = [0] + [int(c) for c in cuts]
    w_cq, w_ckv, w_kpe, w_q = (w_in[:, cuts[i]:cuts[i + 1]] for i in range(4))
    w_kc, w_vc, w_ks, w_vs, w_kw, w_vw = (w_in[:, cuts[4 + i]:cuts[5 + i]] for i in range(6))
    w_gate = w_in[:, cuts[10]:]
    w_kpe_sw = jnp.concatenate([-w_kpe[:, half:], w_kpe[:, :half]], axis=1)
    rope_pad = ((0, 0), (MLA_NOPE, LANE - MLA_NOPE - MLA_ROPE))
    w_gate_blk = jnp.pad(w_gate, ((0, 0), (0, LANE - w_gate.shape[1])))
    wx = jnp.concatenate(
        [w_cq, w_ckv, jnp.pad(w_kpe, rope_pad), jnp.pad(w_kpe_sw, rope_pad), _head_blocks(w_q, NSA_HEADS, NSA_DH),
         w_kc, w_vc] + [_head_blocks(w, NSA_GROUPS, NSA_DH) for w in (w_ks, w_kw, w_vs, w_vw)] + [w_gate_blk],
        axis=1).astype(BF16)

    qd = MLA_NOPE + MLA_ROPE
    uq = w_uq.reshape(MLA_Q_RANK, MLA_HEADS, qd)
    uq_sw = jnp.concatenate([jnp.zeros_like(uq[..., :MLA_NOPE]), -uq[..., MLA_NOPE + half:],
                             uq[..., MLA_NOPE:MLA_NOPE + half]], axis=-1)
    wuq = _head_blocks(uq.reshape(MLA_Q_RANK, -1), MLA_HEADS, qd).astype(BF16)
    wuqs = _head_blocks(uq_sw.reshape(MLA_Q_RANK, -1), MLA_HEADS, qd).astype(BF16)
    wuk = _head_blocks(w_uk, MLA_HEADS, MLA_NOPE).astype(BF16)
    wuv = _head_blocks(w_uv, MLA_HEADS, MLA_V).astype(BF16)
    slope_row = np.zeros((1, HW), np.float32)
    for h in range(NSA_HEADS):
        for ln in BIAS_LANES:
            slope_row[0, h * LANE + ln] = NSA_SLOPES[h]

    row = lambda w: pl.BlockSpec((tm, w), lambda i: (i, 0))
    sds = jax.ShapeDtypeStruct
    return pl.pallas_call(
        functools.partial(_even_proj_kernel, tiles_per_seq=seq // tm),
        grid=(m // tm,),
        in_specs=[row(d), row(LANE), row(LANE), row(1), _const_spec(wx.shape), _const_spec((1, MLA_Q_RANK)),
                  _const_spec((1, MLA_KV_RANK)), _const_spec(wuq.shape), _const_spec(wuqs.shape),
                  _const_spec(wuk.shape), _const_spec(wuv.shape), _const_spec((1, HW))],
        out_specs=[row(HW), row(HW), row(HW), pl.BlockSpec((NSA_HEADS, tm, LANE), lambda i: (0, i, 0)),
                   pl.BlockSpec((2, tm, LANE), lambda i: (0, i, 0)), row(4 * LANE), row(2 * LANE), row(2 * LANE),
                   row(2 * LANE), row(LANE)],
        out_shape=[sds((m, HW), BF16)] * 3 + [sds((NSA_HEADS, m, LANE), BF16), sds((2, m, LANE), BF16),
                                              sds((m, 4 * LANE), BF16), sds((m, 2 * LANE), BF16),
                                              sds((m, 2 * LANE), BF16), sds((m, 2 * LANE), BF16), sds((m, LANE), F32)],
        compiler_params=_params(("parallel",)),
        name="even_proj",
    )(x2, cos_t, sin_t, dpos, wx, q_norm_g.reshape(1, -1), kv_norm_g.reshape(1, -1), wuq, wuqs, wuk, wuv,
      jnp.asarray(slope_row))


def _causal_pairs(nq, tq, tk):
    qi, ki = [], []
    for i in range(nq):
        for j in range(((i + 1) * tq - 1) // tk + 1):
            qi.append(i)
            ki.append(j)
    return jnp.asarray(qi, jnp.int32), jnp.asarray(ki, jnp.int32)


def _mla_kernel(qi_tab, ki_tab, q_ref, k_ref, v_ref, o_ref, m_ref, acc_ref, *, tq, tk):
    p = pl.program_id(1)
    qi = qi_tab[p]
    ki = ki_tab[p]

    @pl.when(ki == 0)
    def _():
        _init_state(m_ref, acc_ref)

    def run(masked):
        if masked:
            mask = (qi * tq + _iota((tq, tk), 0)) >= (ki * tk + _iota((tq, tk), 1))
        scores = lambda h: _dot_nt(q_ref[0, :, h * LANE:(h + 1) * LANE], k_ref[0, :, h * LANE:(h + 1) * LANE])
        s_next = scores(0)
        for h in range(MLA_HEADS):
            s = s_next
            if h + 1 < MLA_HEADS:
                s_next = scores(h + 1)
            if masked:
                s = jnp.where(mask, s, NEG)
            _flash_update(s, v_ref[0, :, h * LANE:(h + 1) * LANE], m_ref, acc_ref, h)

    crosses = (ki + 1) * tk - 1 > qi * tq

    @pl.when(crosses)
    def _():
        run(True)

    @pl.when(jnp.logical_not(crosses))
    def _():
        run(False)

    @pl.when(ki == ((qi + 1) * tq - 1) // tk)
    def _():
        for h in range(MLA_HEADS):
            o_ref[0, :, h * LANE:(h + 1) * LANE] = _normalized(acc_ref[h]).astype(BF16)


def mla_attention(q, k, v, tq=512, tk=1024):
    b, s, hw = q.shape
    tq, tk = min(tq, s), min(tk, s)
    qi_tab, ki_tab = _causal_pairs(s // tq, tq, tk)
    qspec = pl.BlockSpec((1, tq, hw), lambda bb, p, qt, kt: (bb, qt[p], 0))
    kspec = pl.BlockSpec((1, tk, hw), lambda bb, p, qt, kt: (bb, kt[p], 0))
    return pl.pallas_call(
        functools.partial(_mla_kernel, tq=tq, tk=tk),
        grid_spec=pltpu.PrefetchScalarGridSpec(
            num_scalar_prefetch=2,
            grid=(b, int(qi_tab.shape[0])),
            in_specs=[qspec, kspec, kspec],
            out_specs=qspec,
            scratch_shapes=[pltpu.VMEM((MLA_HEADS, tq, LANE), F32), pltpu.VMEM((MLA_HEADS, tq, LANE), F32)],
        ),
        out_shape=jax.ShapeDtypeStruct((b, s, hw), BF16),
        compiler_params=_params(("parallel", "arbitrary")),
        name="mla_attention",
    )(qi_tab, ki_tab, q, k, v)


def _compress_kernel(h_ref, pos_ref, w1_ref, w2_ref, ext_ref, o_ref, *, is_value):
    n16 = h_ref.shape[2]
    half = w1_ref.shape[0] // 2
    lane = _iota((n16, LANE), 1)
    if is_value:
        extra = jnp.where(lane == ONES_LANE, 1.0, 0.0)
    else:
        extra = _bias_pieces(ext_ref[0], lane)
    bias = _dot(pos_ref[...], w1_ref[...])[0:1]
    for g in range(NSA_GROUPS):
        hg = h_ref[0, g]
        first = _dot(hg, w1_ref[:half, :])
        second = _dot(hg, w1_ref[half:, :])
        hid = first + pltpu.roll(second, n16 - 1, 0) + bias
        act = jax.nn.gelu(hid).astype(BF16)
        o_ref[0, g] = (_dot(act, w2_ref[...]) + extra).astype(BF16)


def nsa_compress(kv, cmp_pos, w1, w2, dpos_cend, is_value):
    b, s, _ = kv.shape
    n16 = s // CMP_STRIDE
    h = kv.reshape(b, n16, CMP_STRIDE, NSA_GROUPS, NSA_DH).transpose(0, 3, 1, 2, 4)
    h = h.reshape(b, NSA_GROUPS, n16, CMP_STRIDE * NSA_DH)
    pos = jnp.broadcast_to(cmp_pos.reshape(1, CMP_LEN * NSA_DH), (8, CMP_LEN * NSA_DH)).astype(BF16)
    w2p = jnp.pad(w2, ((0, 0), (0, LANE - NSA_DH))).astype(BF16)
    w1 = w1.astype(BF16)
    return pl.pallas_call(
        functools.partial(_compress_kernel, is_value=is_value),
        grid=(b,),
        in_specs=[pl.BlockSpec((1,) + h.shape[1:], lambda i: (i, 0, 0, 0)), _const_spec(pos.shape),
                  _const_spec(w1.shape), _const_spec(w2p.shape), pl.BlockSpec((1, n16, 1), lambda i: (i, 0, 0))],
        out_specs=pl.BlockSpec((1, NSA_GROUPS, n16, LANE), lambda i: (i, 0, 0, 0)),
        out_shape=jax.ShapeDtypeStruct((b, NSA_GROUPS, n16, LANE), BF16),
        compiler_params=_params(("parallel",)),
        name="nsa_compress",
    )(h, pos, w1, w2p, dpos_cend)


def _topk_mask_t(x, k):
    n = x.shape[0]
    ridx = _iota(x.shape, 0).astype(F32)
    sel = jnp.zeros(x.shape, F32)
    for _ in range(k):
        m = jnp.max(x, axis=0, keepdims=True)
        first = jnp.min(jnp.where(x == m, ridx, float(n)), axis=0, keepdims=True)
        hit = ridx == first
        sel = jnp.where(hit, 1.0, sel)
        x = jnp.where(hit, REMOVED, x)
    return sel


def _cmp_topk_kernel(q_ref, kc_ref, vc_ref, gate_ref, oc_ref, selm1_ref, imp_ref, *, tq, ncp, topk):
    qi = pl.program_id(1)
    t = qi * tq + _iota((tq, 1), 0)
    gates = gate_ref[0]
    blk = _iota((tq, LANE), 1)
    chunk = _shr(t, SEL_BLOCK)

    def attend(nc):
        cend = _iota((1, nc), 1) * CMP_STRIDE + (CMP_LEN - 1)
        mask = (cend <= t)[None]
        cstart = _iota((nc, LANE), 0) * CMP_STRIDE
        sstart = _iota((nc, LANE), 1) * SEL_BLOCK
        overlap = jnp.where((cstart < sstart + SEL_BLOCK) & (cstart + CMP_LEN > sstart)
                            & (cstart < (ncp - 1) * CMP_STRIDE), 1.0, 0.0).astype(BF16)
        for g in range(NSA_GROUPS):
            q = q_ref[g * NSA_HPG:(g + 1) * NSA_HPG].reshape(NSA_HPG * tq, LANE)
            s = _dot_nt(q, kc_ref[0, g, :nc]).reshape(NSA_HPG, tq, nc)
            s = jnp.where(mask, s, NEG)
            e = jnp.where(mask, jnp.exp2(s - jnp.max(s, axis=-1, keepdims=True)), 0.0)
            p = e / jnp.maximum(jnp.sum(e, axis=-1, keepdims=True), 1e-30)
            psum = jnp.sum(p, axis=0)
            o = _dot(p.reshape(NSA_HPG * tq, nc).astype(BF16), vc_ref[0, g, :nc])
            for n in range(NSA_HPG):
                h = g * NSA_HPG + n
                oh = jnp.where(blk < NSA_DH, o[n * tq:(n + 1) * tq] * gates[:, h:h + 1], 0.0)
                oc_ref[0, :, h * LANE:(h + 1) * LANE] = oh.astype(BF16)
            hi = psum.astype(BF16)
            r1 = psum - hi.astype(F32)
            mid = r1.astype(BF16)
            lo = (r1 - mid.astype(F32)).astype(BF16)
            imp_ref[g] = _dot(hi, overlap) + _dot(mid, overlap) + _dot(lo, overlap)

    lane_tiles = ((qi + 1) * tq // CMP_STRIDE + LANE - 1) // LANE
    for v in range(1, ncp // LANE + 1):
        @pl.when(jnp.minimum(lane_tiles, ncp // LANE) == v)
        def _(v=v):
            attend(v * LANE)

    for g in range(NSA_GROUPS):
        imp = jnp.where((blk == 0) | (blk == chunk), BIG, imp_ref[g])
        imp = jnp.where(blk <= chunk, imp, NEG)
        sel = _topk_mask_t(imp.T, topk).T
        selm1_ref[0, g] = (jnp.where(blk <= chunk, sel, 0.0) - 1.0).astype(BF16)


def nsa_cmp_topk(qn, kc, vc, gates, tq=128):
    _, b, s, _ = qn.shape
    tq = min(tq, s)
    ncp = kc.shape[2]
    assert s // SEL_BLOCK <= LANE
    topk = min(SEL_TOPK, s // SEL_BLOCK)
    kspec = pl.BlockSpec((1, NSA_GROUPS, ncp, LANE), lambda i, j: (i, 0, 0, 0))
    return pl.pallas_call(
        functools.partial(_cmp_topk_kernel, tq=tq, ncp=ncp, topk=topk),
        grid=(b, s // tq),
        in_specs=[pl.BlockSpec((NSA_HEADS, None, tq, LANE), lambda i, j: (0, i, j, 0)), kspec, kspec,
                  pl.BlockSpec((1, tq, LANE), lambda i, j: (i, j, 0))],
        out_specs=[pl.BlockSpec((1, tq, HW), lambda i, j: (i, j, 0)),
                   pl.BlockSpec((1, NSA_GROUPS, tq, LANE), lambda i, j: (i, 0, j, 0))],
        out_shape=[jax.ShapeDtypeStruct((b, s, HW), BF16), jax.ShapeDtypeStruct((b, NSA_GROUPS, s, LANE), BF16)],
        scratch_shapes=[pltpu.VMEM((NSA_GROUPS, tq, LANE), F32)],
        compiler_params=_params(("parallel", "parallel")),
        name="nsa_cmp_topk",
    )(qn, kc, vc, gates)


def _nsa_write(o_ref, gate_ref, acc_ref, tq, branch):
    gates = gate_ref[0]
    for h in range(NSA_HEADS):
        g, n = divmod(h, NSA_HPG)
        col = branch * NSA_HEADS + h
        o = _normalized(acc_ref[g, n * tq:(n + 1) * tq]) * gates[:, col:col + 1]
        o_ref[0, :, h * LANE:(h + 1) * LANE] = o.astype(BF16)


def _sel_attn_kernel(flags, q_ref, k_ref, v_ref, selm1_ref, gate_ref, o_ref, lhs_ref, m_ref, acc_ref, *, t, nq):
    b = pl.program_id(0)
    qi = pl.program_id(1)
    _init_state(m_ref, acc_ref)
    for g in range(NSA_GROUPS):
        lhs_ref[g, :, :LANE] = q_ref[g * NSA_HPG:(g + 1) * NSA_HPG].reshape(NSA_HPG * t, LANE)
        lhs_ref[g, :, LANE:] = jnp.concatenate([selm1_ref[0, g]] * NSA_HPG, axis=0)

    half = NSA_HPG * t // 2
    chains = [(g, i) for g in range(NSA_GROUPS) for i in range(2)]

    def update(ki, diagonal):
        rows = pl.ds(pl.multiple_of(ki * t, t), t)
        scores = [_dot_nt(lhs_ref[g, i * half:(i + 1) * half], k_ref[rows, 2 * g * LANE:(2 * g + 2) * LANE])
                  for g, i in chains]
        for (g, i), s in zip(chains, scores):
            if diagonal:
                causal = _iota((t, t), 1) <= _iota((t, t), 0)
                s = jnp.where(causal[None], s.reshape(NSA_HPG // 2, t, t), NEG).reshape(half, t)
            _flash_update(s, v_ref[rows, g * LANE:(g + 1) * LANE], m_ref, acc_ref,
                          (g, slice(i * half, (i + 1) * half)))

    base = (b * nq + qi) * nq

    def body(ki, carry):
        @pl.when(flags[base + ki] > 0)
        def _():
            update(ki, False)
        return carry

    lax.fori_loop(0, qi, body, 0)
    update(qi, True)
    _nsa_write(o_ref, gate_ref, acc_ref, t, branch=1)


def nsa_selected(qn, ks, vs, selm1, gates, t=256):
    _, b, s, _ = qn.shape
    t = min(t, s)
    nq = s // t
    bpt = t // SEL_BLOCK
    flags = (selm1.reshape(b, NSA_GROUPS, nq, t, LANE // bpt, bpt) > -0.5).any(axis=(1, 3, 5))[..., :nq]
    flags = flags.astype(jnp.int32).reshape(-1)
    imap_q = lambda bb, i, fl: (bb, i, 0)
    rows = NSA_HPG * t
    return pl.pallas_call(
        functools.partial(_sel_attn_kernel, t=t, nq=nq),
        grid_spec=pltpu.PrefetchScalarGridSpec(
            num_scalar_prefetch=1,
            grid=(b, nq),
            in_specs=[pl.BlockSpec((NSA_HEADS, None, t, LANE), lambda bb, i, fl: (0, bb, i, 0)),
                      pl.BlockSpec((None, s, 4 * LANE), lambda bb, i, fl: (bb, 0, 0)),
                      pl.BlockSpec((None, s, 2 * LANE), lambda bb, i, fl: (bb, 0, 0)),
                      pl.BlockSpec((1, NSA_GROUPS, t, LANE), lambda bb, i, fl: (bb, 0, i, 0)),
                      pl.BlockSpec((1, t, LANE), imap_q)],
            out_specs=pl.BlockSpec((1, t, HW), imap_q),
            scratch_shapes=[pltpu.VMEM((NSA_GROUPS, rows, 2 * LANE), BF16), pltpu.VMEM((NSA_GROUPS, rows, LANE), F32),
                            pltpu.VMEM((NSA_GROUPS, rows, LANE), F32)],
        ),
        out_shape=jax.ShapeDtypeStruct((b, s, HW), BF16),
        compiler_params=_params(("parallel", "arbitrary")),
        name="nsa_selected",
    )(flags, qn, ks, vs, selm1, gates)


def _win_attn_kernel(q_ref, k_ref, v_ref, gate_ref, o_ref, m_ref, acc_ref, *, tq, nkt):
    qi = pl.program_id(1)
    j = pl.program_id(2)
    kt = qi - j

    @pl.when(j == 0)
    def _():
        _init_state(m_ref, acc_ref)

    @pl.when(kt >= 0)
    def _():
        diff = (qi * tq + _iota((tq, tq), 0)) - (kt * tq + _iota((tq, tq), 1))
        mask = ((diff >= 0) & (diff < WIN))[None]
        half = NSA_HPG // 2
        chains = [(g, i) for g in range(NSA_GROUPS) for i in range(2)]
        scores = [_dot_nt(q_ref[g * NSA_HPG + i * half:g * NSA_HPG + (i + 1) * half].reshape(half * tq, LANE),
                          k_ref[0, :, g * LANE:(g + 1) * LANE]) for g, i in chains]
        for (g, i), s in zip(chains, scores):
            s = jnp.where(mask, s.reshape(half, tq, tq), NEG).reshape(half * tq, tq)
            _flash_update(s, v_ref[0, :, g * LANE:(g + 1) * LANE], m_ref, acc_ref,
                          (g, slice(i * half * tq, (i + 1) * half * tq)))

    @pl.when(j == nkt - 1)
    def _():
        _nsa_write(o_ref, gate_ref, acc_ref, tq, branch=2)


def nsa_window(qn, kw, vw, gates, tq=512):
    _, b, s, _ = qn.shape
    tq = min(tq, s, WIN)
    nkt = WIN // tq + 1
    kmap = lambda bb, i, j: (bb, jnp.maximum(i - j, 0), 0)
    rows = NSA_HPG * tq
    return pl.pallas_call(
        functools.partial(_win_attn_kernel, tq=tq, nkt=nkt),
        grid=(b, s // tq, nkt),
        in_specs=[pl.BlockSpec((NSA_HEADS, None, tq, LANE), lambda bb, i, j: (0, bb, i, 0)),
                  pl.BlockSpec((1, tq, 2 * LANE), kmap), pl.BlockSpec((1, tq, 2 * LANE), kmap),
                  pl.BlockSpec((1, tq, LANE), lambda bb, i, j: (bb, i, 0))],
        out_specs=pl.BlockSpec((1, tq, HW), lambda bb, i, j: (bb, i, 0)),
        out_shape=jax.ShapeDtypeStruct((b, s, HW), BF16),
        scratch_shapes=[pltpu.VMEM((NSA_GROUPS, rows, LANE), F32), pltpu.VMEM((NSA_GROUPS, rows, LANE), F32)],
        compiler_params=_params(("parallel", "parallel", "arbitrary")),
        name="nsa_window",
    )(qn, kw, vw, gates)


def _even_out_ln_kernel(x_ref, om_ref, oc_ref, os_ref, ow_ref, wm_ref, wn_ref, g_ref, b_ref, o_ref):
    nsa = (oc_ref[...].astype(F32) + os_ref[...].astype(F32) + ow_ref[...].astype(F32)).astype(BF16)
    mix = _dot(om_ref[...], wm_ref[...]) + _dot(nsa, wn_ref[...])
    o_ref[...] = _layer_norm(ALPHA * x_ref[...] + mix, g_ref[...], b_ref[...])


def even_out_ln(x2, o_mla, o_c, o_s, o_w, w_mla, w_nsa, g, b, tm=512):
    m, d = x2.shape
    tm = min(tm, m)
    row = lambda w: pl.BlockSpec((tm, w), lambda i: (i, 0))
    return pl.pallas_call(
        _even_out_ln_kernel,
        grid=(m // tm,),
        in_specs=[row(d)] + [row(HW)] * 4 + [_const_spec(w_mla.shape), _const_spec(w_nsa.shape),
                                             _const_spec((1, d)), _const_spec((1, d))],
        out_specs=row(d),
        out_shape=jax.ShapeDtypeStruct((m, d), F32),
        compiler_params=_params(("parallel",)),
        name="even_out_ln",
    )(x2, o_mla, o_c, o_s, o_w, w_mla, w_nsa, g.reshape(1, d), b.reshape(1, d))


def _odd_out_ln_kernel(x_ref, o0_ref, o1_ref, o2_ref, l0_ref, l1_ref, l2_ref, w_ref, g_ref, b_ref, o_ref,
                       o_scr, l_scr):
    tm = x_ref.shape[0]
    for gi, (src, lsrc) in enumerate(((o0_ref, l0_ref), (o1_ref, l1_ref), (o2_ref, l2_ref))):
        dil = DIL_PATTERNS[gi][1]
        for r in range(dil):
            rows = pl.ds(r, tm // dil, stride=dil) if dil > 1 else slice(None)
            for h in range(DIL_GH):
                o_scr[gi, h, rows, :] = src[r, :, h * DIL_DH:(h + 1) * DIL_DH].astype(F32)
            l_scr[gi, rows, :] = lsrc[r]
    lses = [l_scr[gi] for gi in range(DIL_GROUPS)]
    top = jnp.maximum(jnp.maximum(lses[0], lses[1]), lses[2])
    es = [jnp.exp(l - top) for l in lses]
    den = es[0] + es[1] + es[2]
    wts = [e / den for e in es]
    cols = []
    for h in range(DIL_GH):
        merged = None
        for gi in range(DIL_GROUPS):
            term = wts[gi][:, h:h + 1] * o_scr[gi, h]
            merged = term if merged is None else merged + term
        cols.append(merged.astype(BF16))
    mix = _dot(jnp.concatenate(cols, axis=1), w_ref[...])
    o_ref[...] = _layer_norm(ALPHA * x_ref[...] + mix, g_ref[...], b_ref[...])


def odd_out_ln(x2, outs, lses, w, g, b, tm=512):
    m, d = x2.shape
    b_, _, s = outs[0].shape[0], None, outs[0].shape[1] * outs[0].shape[2]
    tm = min(tm, s)
    width = DIL_GH * DIL_DH
    row = pl.BlockSpec((None, tm, d), lambda bb, i: (bb, i, 0))
    cls = lambda gi, wd: pl.BlockSpec((None, DIL_PATTERNS[gi][1], tm // DIL_PATTERNS[gi][1], wd),
                                      lambda bb, i: (bb, 0, i, 0))
    out = pl.pallas_call(
        _odd_out_ln_kernel,
        grid=(b_, s // tm),
        in_specs=[row] + [cls(gi, width) for gi in range(DIL_GROUPS)] + [cls(gi, LANE) for gi in range(DIL_GROUPS)]
        + [_const_spec(w.shape), _const_spec((1, d)), _const_spec((1, d))],
        out_specs=row,
        out_shape=jax.ShapeDtypeStruct((b_, s, d), F32),
        scratch_shapes=[pltpu.VMEM((DIL_GROUPS, DIL_GH, tm, DIL_DH), F32), pltpu.VMEM((DIL_GROUPS, tm, LANE), F32)],
        compiler_params=_params(("parallel", "parallel")),
        name="odd_out_ln",
    )(x2.reshape(b_, s, d), *outs, *lses, w, g.reshape(1, d), b.reshape(1, d))
    return out.reshape(m, d)


def _odd_proj_kernel(x_ref, w_ref, o0_ref, o1_ref, o2_ref, xs_ref):
    tm, d_model = x_ref.shape
    width = DIL_GH * DIL_DH
    n_chunks = d_model // LANE
    for c in range(n_chunks):
        xs_ref[c] = x_ref[:, c * LANE:(c + 1) * LANE]
    for gi, o_ref in enumerate((o0_ref, o1_ref, o2_ref)):
        dil = DIL_PATTERNS[gi][1]
        if dil == 1:
            xg = x_ref[...]
        else:
            xg = jnp.concatenate(
                [jnp.concatenate([xs_ref[c, pl.ds(r, tm // dil, stride=dil), :] for r in range(dil)], axis=0)
                 for c in range(n_chunks)], axis=1)
        xg = xg.astype(BF16)
        for c in range(3):
            col = (gi * 3 + c) * width
            y = _dot(xg, w_ref[:, col:col + width])
            if c == 0:
                y = y * (DIL_SCALE * LOG2E)
            o_ref[:, :, c * width:(c + 1) * width] = y.astype(BF16).reshape(dil, tm // dil, width)


def odd_proj(x2, b, s, w_in, tm=512):
    m, d = x2.shape
    tm = min(tm, s)
    width = DIL_GH * DIL_DH
    w = w_in.reshape(d, 3, DIL_GROUPS, width).transpose(0, 2, 1, 3).reshape(d, 3 * DIL_GROUPS * width).astype(BF16)
    dils = [dil for _, dil in DIL_PATTERNS]
    return pl.pallas_call(
        _odd_proj_kernel,
        grid=(b, s // tm),
        in_specs=[pl.BlockSpec((None, tm, d), lambda bb, i: (bb, i, 0)), _const_spec(w.shape)],
        out_specs=[pl.BlockSpec((None, dil, tm // dil, 3 * width), lambda bb, i: (bb, 0, i, 0)) for dil in dils],
        out_shape=[jax.ShapeDtypeStruct((b, dil, s // dil, 3 * width), BF16) for dil in dils],
        scratch_shapes=[pltpu.VMEM((d // LANE, tm, LANE), F32)],
        compiler_params=_params(("parallel", "parallel")),
        name="odd_proj",
    )(x2.reshape(b, s, d), w)


def _dil_attn_kernel(slope_tab, pos0_tab, q_ref, kc_ref, kp_ref, vc_ref, vp_ref, pc_ref, pp_ref, pq_ref, o_ref,
                     lse_ref, *, group, tq):
    sub = DIL_SPAN
    b = pl.program_id(0)
    jt = pl.program_id(2)
    row = _iota((sub, 2 * sub), 0)
    col = _iota((sub, 2 * sub), 1)
    band = (col >= row) & (col <= row + sub)
    band_bias = jnp.where(band, 0.0, NEG)
    start_bias = jnp.where(band & (col >= sub), 0.0, NEG)
    pos0 = pos0_tab[b]
    lane = _iota((sub, LANE), 1)
    for i in range(tq // sub):
        cur = slice(i * sub, (i + 1) * sub)
        if i == 0:
            mask_bias = jnp.where(jt == 0, start_bias, band_bias)
            pk = jnp.concatenate([pp_ref[...], pc_ref[:, :sub]], axis=1)
        else:
            mask_bias = band_bias
            pk = pc_ref[:, (i - 1) * sub:(i + 1) * sub]
        dpos = pk - pos0
        dq = (pq_ref[cur, :] - pos0) * LN2
        lse_tile = jnp.zeros((sub, LANE), F32)
        for h in range(DIL_GH):
            hs = slice(h * DIL_DH, (h + 1) * DIL_DH)
            slope = slope_tab[group * DIL_GH + h]
            if i == 0:
                k = jnp.concatenate([kp_ref[:, hs], kc_ref[:sub, hs]], axis=0)
                v = jnp.concatenate([vp_ref[:, hs], vc_ref[:sub, hs]], axis=0)
            else:
                k = kc_ref[(i - 1) * sub:(i + 1) * sub, hs]
                v = vc_ref[(i - 1) * sub:(i + 1) * sub, hs]
            s = _dot_nt(q_ref[cur, hs], k) + (mask_bias + slope * dpos)
            m = jnp.max(s, axis=-1, keepdims=True)
            p = jnp.exp2(s - m)
            den = jnp.sum(p, axis=-1, keepdims=True)
            o_ref[cur, hs] = (_dot(p.astype(BF16), v) / den).astype(BF16)
            lse_tile = jnp.where(lane == h, m * LN2 + jnp.log(den) - slope * dq, lse_tile)
        lse_ref[cur, :] = lse_tile


def dilated_group_attention(qkv, pos_f, group, tq=512):
    b, d, sd, _ = qkv.shape
    w, dil = DIL_PATTERNS[group]
    assert w // dil == DIL_SPAN and dil == d
    tq = min(tq, sd)
    sub = DIL_SPAN
    width = DIL_GH * DIL_DH
    n_slopes = DIL_GROUPS * DIL_GH
    slopes = 2.0 ** (-ALIBI_MAX_BIAS * jnp.arange(1, n_slopes + 1, dtype=F32) / n_slopes) * LOG2E
    pos0 = pos_f[:, 0]
    posc = pos_f.reshape(b, sd, d).transpose(0, 2, 1).reshape(b, d, 1, sd)
    r = tq // sub
    cur = lambda c: pl.BlockSpec((None, None, tq, width), lambda bb, rr, j, *_: (bb, rr, j, c))
    prev = lambda c: pl.BlockSpec((None, None, sub, width),
                                  lambda bb, rr, j, *_: (bb, rr, jnp.maximum(j * r - 1, 0), c))
    return pl.pallas_call(
        functools.partial(_dil_attn_kernel, group=group, tq=tq),
        grid_spec=pltpu.PrefetchScalarGridSpec(
            num_scalar_prefetch=2,
            grid=(b, d, sd // tq),
            in_specs=[cur(0), cur(1), prev(1), cur(2), prev(2),
                      pl.BlockSpec((None, None, 1, tq), lambda bb, rr, j, *_: (bb, rr, 0, j)),
                      pl.BlockSpec((None, None, 1, sub), lambda bb, rr, j, *_: (bb, rr, 0, jnp.maximum(j * r - 1, 0))),
                      pl.BlockSpec((None, None, tq, 1), lambda bb, rr, j, *_: (bb, rr, j, 0))],
            out_specs=[pl.BlockSpec((None, None, tq, width), lambda bb, rr, j, *_: (bb, rr, j, 0)),
                       pl.BlockSpec((None, None, tq, LANE), lambda bb, rr, j, *_: (bb, rr, j, 0))],
        ),
        out_shape=[jax.ShapeDtypeStruct((b, d, sd, width), BF16), jax.ShapeDtypeStruct((b, d, sd, LANE), F32)],
        compiler_params=_params(("parallel", "parallel", "parallel")),
        name=f"dilated_attention_g{group}",
    )(slopes, pos0, qkv, qkv, qkv, qkv, qkv, posc, posc, posc.reshape(b, d, sd, 1))


def even_mixer_ln(x2, b, s, pos_f, w_in, q_norm_g, kv_norm_g, w_uq, w_uk, w_uv,
                  cmp_pos, cmp_k_w1, cmp_k_w2, cmp_v_w1, cmp_v_w2, w_out, ln_g, ln_b):
    m = b * s
    d = x2.shape[1]
    half = MLA_ROPE // 2
    inv = ROPE_THETA ** (-jnp.arange(half, dtype=F32) / half)
    ang = (pos_f[..., None] * inv).reshape(m, half)
    ones = jnp.ones((m, MLA_NOPE), F32)
    zeros = jnp.zeros((m, LANE - MLA_NOPE - MLA_ROPE), F32)
    cos_t = jnp.concatenate([ones, jnp.cos(ang), jnp.cos(ang), zeros], axis=1)
    sin_t = jnp.concatenate([0.0 * ones, jnp.sin(ang), jnp.sin(ang), zeros], axis=1)
    dpos = (pos_f - pos_f[:, :1]) * LOG2E
    ncp = s // CMP_STRIDE
    dpos_cend = jnp.pad(dpos[:, CMP_LEN - 1::CMP_STRIDE], ((0, 0), (0, 1))).reshape(b, ncp, 1)

    qm, km, vm, qn, kvc, ks, kw, vs, vw, gates = even_proj(
        x2, cos_t, sin_t, dpos.reshape(m, 1), s, w_in, q_norm_g, kv_norm_g, w_uq, w_uk, w_uv)
    o_mla = mla_attention(qm.reshape(b, s, HW), km.reshape(b, s, HW), vm.reshape(b, s, HW))

    kvc = kvc.reshape(2, b, s, LANE)
    kc = nsa_compress(kvc[0], cmp_pos, cmp_k_w1, cmp_k_w2, dpos_cend, is_value=False)
    vc = nsa_compress(kvc[1], cmp_pos, cmp_v_w1, cmp_v_w2, dpos_cend, is_value=True)
    qn = qn.reshape(NSA_HEADS, b, s, LANE)
    gates = gates.reshape(b, s, LANE)
    o_c, selm1 = nsa_cmp_topk(qn, kc, vc, gates)
    o_s = nsa_selected(qn, ks.reshape(b, s, 4 * LANE), vs.reshape(b, s, 2 * LANE), selm1, gates)
    o_w = nsa_window(qn, kw.reshape(b, s, 2 * LANE), vw.reshape(b, s, 2 * LANE), gates)

    pad_rows = lambda w, n, width: jnp.pad(w.reshape(n, width, d), ((0, 0), (0, LANE - width), (0, 0))).reshape(n * LANE, d)
    w_mla = pad_rows(w_out[:MLA_HEADS * MLA_V], MLA_HEADS, MLA_V).astype(BF16)
    w_nsa = pad_rows(w_out[MLA_HEADS * MLA_V:], NSA_HEADS, NSA_DH).astype(BF16)
    return even_out_ln(x2, o_mla.reshape(m, HW), o_c.reshape(m, HW), o_s.reshape(m, HW), o_w.reshape(m, HW),
                       w_mla, w_nsa, ln_g, ln_b)


def odd_mixer_ln(x2, b, s, pos_f, w_in, w_out, ln_g, ln_b):
    qkvs = odd_proj(x2, b, s, w_in)
    parts = [dilated_group_attention(qkvs[g], pos_f, g) for g in range(DIL_GROUPS)]
    return odd_out_ln(x2, [p[0] for p in parts], [p[1] for p in parts], w_out.astype(BF16), ln_g, ln_b)


def kernel(x, positions, ln1_g, ln1_b, ffn1_w_gate, ffn1_w_up, ffn1_w_down, mix_in_even, mla_q_norm, mla_kv_norm, mla_w_uq, mla_w_uk, mla_w_uv, nsa_cmp_pos, nsa_cmp_k_w1, nsa_cmp_k_w2, nsa_cmp_v_w1, nsa_cmp_v_w2, mix_out_even, mix_in_odd, mix_out_odd, ln2_g, ln2_b, ffn2_w_gate, ffn2_w_up, ffn2_w_down, ln3_g, ln3_b):
    b, s, d = x.shape
    x2 = x.reshape(b * s, d)
    pos_f = positions.astype(F32)
    for i in range(DEPTH):
        j = i // 2
        x2 = ffn_ln(x2, ffn1_w_gate[i].astype(BF16), ffn1_w_up[i].astype(BF16), ffn1_w_down[i].astype(BF16),
                    ln1_g[i], ln1_b[i])
        if i % 2 == 0:
            x2 = even_mixer_ln(x2, b, s, pos_f, mix_in_even[j], mla_q_norm[j], mla_kv_norm[j], mla_w_uq[j],
                               mla_w_uk[j], mla_w_uv[j], nsa_cmp_pos[j], nsa_cmp_k_w1[j], nsa_cmp_k_w2[j],
                               nsa_cmp_v_w1[j], nsa_cmp_v_w2[j], mix_out_even[j], ln2_g[i], ln2_b[i])
        else:
            x2 = odd_mixer_ln(x2, b, s, pos_f, mix_in_odd[j], mix_out_odd[j], ln2_g[i], ln2_b[i])
        x2 = ffn_ln(x2, ffn2_w_gate[i].astype(BF16), ffn2_w_up[i].astype(BF16), ffn2_w_down[i].astype(BF16),
                    ln3_g[i], ln3_b[i])
    return x2.reshape(b, s, d)
```

```python
import functools
import math

import numpy as np
import jax
import jax.numpy as jnp
from jax import lax
from jax.experimental import pallas as pl
from jax.experimental.pallas import tpu as pltpu

F32 = jnp.float32
BF16 = jnp.bfloat16

DEPTH = 2
LN_EPS = 1e-5
RMS_EPS = 1e-6
ALPHA = (2 * DEPTH) ** 0.25
HALF_STEP = 0.5
NEG = -1e30
BIG = 1e9
MASK_BIG = 1e30
REMOVED = -3.0e38
ALIBI_MAX_BIAS = 8.0
LOG2E = math.log2(math.e)
LN2 = math.log(2.0)
LANE = 128

MLA_HEADS = 8
MLA_Q_RANK = 384
MLA_KV_RANK = 256
MLA_NOPE = 64
MLA_ROPE = 32
MLA_V = 64
ROPE_THETA = 10000.0
MLA_SCALE = (MLA_NOPE + MLA_ROPE) ** -0.5
MLA_CHAINS = 4

NSA_HEADS = 8
NSA_GROUPS = 2
NSA_HPG = 4
NSA_DH = 64
CMP_LEN = 32
CMP_STRIDE = 16
CMP_HIDDEN = 256
SEL_BLOCK = 64
SEL_TOPK = 16
WIN = 512
NSA_SCALE = NSA_DH ** -0.5
NSA_SLOPES = tuple(2.0 ** (-ALIBI_MAX_BIAS * (i + 1) / NSA_HEADS) for i in range(NSA_HEADS))
ONES_LANE = 64
BIAS_LANES = (64, 65, 66)

DIL_PATTERNS = ((128, 1), (512, 4), (2048, 16))
DIL_GROUPS = 3
DIL_GH = 4
DIL_DH = 128
DIL_SCALE = DIL_DH ** -0.5
DIL_SPAN = 128

VMEM_LIMIT = 48 * 1024 * 1024


def _iota(shape, dim):
    return lax.broadcasted_iota(jnp.int32, shape, dim)


def _shr(x, pow2):
    return jnp.right_shift(x, int(pow2).bit_length() - 1)


def _dot(a, b):
    return jnp.dot(a, b, preferred_element_type=F32)


def _dot_nt(a, b):
    return lax.dot_general(a, b, (((1,), (1,)), ((), ())), preferred_element_type=F32)


def _const_spec(shape):
    zeros = (0,) * len(shape)
    return pl.BlockSpec(shape, lambda *_: zeros, pipeline_mode=pl.Buffered(1))


def _params(sem):
    return pltpu.CompilerParams(dimension_semantics=sem, vmem_limit_bytes=VMEM_LIMIT)


def _layer_norm(z, g, b):
    mu = jnp.mean(z, axis=-1, keepdims=True)
    zc = z - mu
    var = jnp.mean(zc * zc, axis=-1, keepdims=True)
    return zc * lax.rsqrt(var + LN_EPS) * g + b


def _rms_norm(z, g):
    return z * lax.rsqrt(jnp.mean(z * z, axis=-1, keepdims=True) + RMS_EPS) * g


def _bias_pieces(d, lane):
    x = jnp.broadcast_to(d, lane.shape)
    hi = x.astype(BF16).astype(F32)
    r = x - hi
    mid = r.astype(BF16).astype(F32)
    lo = (r - mid).astype(BF16).astype(F32)
    return jnp.where(lane == BIAS_LANES[0], hi,
                     jnp.where(lane == BIAS_LANES[1], mid, jnp.where(lane == BIAS_LANES[2], lo, 0.0)))


def _flash_update(s, v, m_ref, acc_ref, idx):
    m_old = m_ref[idx]
    m_new = jnp.maximum(m_old, jnp.max(s, axis=-1, keepdims=True))
    p = jnp.exp2(s - jnp.tile(m_new, (1, s.shape[1] // LANE))).astype(BF16)
    acc_ref[idx] = jnp.exp2(m_old - m_new) * acc_ref[idx] + _dot(p, v)
    m_ref[idx] = m_new


def _flash_update_staged(scores, values, m_ref, acc_ref, idxs):
    m_old = [m_ref[i] for i in idxs]
    m_new = [jnp.maximum(mo, jnp.max(s, axis=-1, keepdims=True)) for mo, s in zip(m_old, scores)]
    ps = [jnp.exp2(s - jnp.tile(mn, (1, s.shape[1] // LANE))).astype(BF16) for s, mn in zip(scores, m_new)]
    for i, mo, mn, p, v in zip(idxs, m_old, m_new, ps, values):
        acc_ref[i] = jnp.exp2(mo - mn) * acc_ref[i] + _dot(p, v)
        m_ref[i] = mn


def _init_state(m_ref, acc_ref):
    m_ref[...] = jnp.full(m_ref.shape, NEG, F32)
    acc_ref[...] = jnp.zeros(acc_ref.shape, F32)


def _normalized(acc):
    lane = _iota(acc.shape, 1)
    o = acc / jnp.maximum(acc[:, ONES_LANE:ONES_LANE + 1], 1e-30)
    return jnp.where(lane < ONES_LANE, o, 0.0)


def _ffn_ln_kernel(x_ref, wg_ref, wu_ref, wd_ref, g_ref, b_ref, o_ref, *, n_chunks):
    x = x_ref[...]
    xb = x.astype(BF16)
    c = wg_ref.shape[1] // n_chunks
    y = None
    for i in range(n_chunks):
        gt = _dot(xb, wg_ref[:, i * c:(i + 1) * c])
        up = _dot(xb, wu_ref[:, i * c:(i + 1) * c])
        h = (gt * jax.nn.sigmoid(gt) * up).astype(BF16)
        part = _dot(h, wd_ref[i * c:(i + 1) * c, :])
        y = part if y is None else y + part
    o_ref[...] = _layer_norm(ALPHA * x + HALF_STEP * y, g_ref[...], b_ref[...])


def ffn_ln(x2, wg, wu, wd, g, b, tm=1024, n_chunks=11):
    m, d = x2.shape
    tm = min(tm, m)
    row = pl.BlockSpec((tm, d), lambda i: (i, 0))
    return pl.pallas_call(
        functools.partial(_ffn_ln_kernel, n_chunks=n_chunks),
        grid=(m // tm,),
        in_specs=[row, _const_spec(wg.shape), _const_spec(wu.shape), _const_spec(wd.shape),
                  _const_spec((1, d)), _const_spec((1, d))],
        out_specs=row,
        out_shape=jax.ShapeDtypeStruct((m, d), F32),
        compiler_params=_params(("parallel",)),
        name="ffn_ln",
    )(x2, wg, wu, wd, g.reshape(1, d), b.reshape(1, d))


HW = MLA_HEADS * LANE
EVEN_X_COLS = (MLA_Q_RANK, MLA_KV_RANK, LANE, LANE, HW, 2 * LANE, 8 * LANE, LANE)
EVEN_X_OFFS = tuple(int(v) for v in np.cumsum((0,) + EVEN_X_COLS))


def _even_proj_kernel(x_ref, cos_ref, sin_ref, dpos_ref, wx_ref, qg_ref, kvg_ref, wuq_ref, wuqs_ref, wuk_ref,
                      wuv_ref, slope_ref, qm_ref, km_ref, vm_ref, qn_ref, kvc_ref, ks_ref, kw_ref, vs_ref, vw_ref,
                      gate_ref, *, tiles_per_seq):
    tm = x_ref.shape[0]
    xb = x_ref[...].astype(BF16)
    cos = cos_ref[...]
    sin = sin_ref[...]
    lane = _iota((tm, LANE), 1)
    ones_lane = jnp.where(lane == ONES_LANE, 1.0, 0.0)
    pos_term = _bias_pieces(dpos_ref[...], lane)
    tok = (pl.program_id(0) % tiles_per_seq) * tm + _iota((tm, LANE), 0)
    block_onehot = jnp.where(lane == _shr(tok, SEL_BLOCK), MASK_BIG, 0.0).astype(BF16)

    def xdot(i):
        return _dot(xb, wx_ref[:, EVEN_X_OFFS[i]:EVEN_X_OFFS[i + 1]])

    cq = _rms_norm(xdot(0), qg_ref[...]).astype(BF16)
    ckv = _rms_norm(xdot(1), kvg_ref[...]).astype(BF16)
    k_rot = xdot(2) * cos + xdot(3) * sin
    for h in range(MLA_HEADS):
        sl = slice(h * LANE, (h + 1) * LANE)
        q = _dot(cq, wuq_ref[:, sl]) * cos + _dot(cq, wuqs_ref[:, sl]) * sin
        qm_ref[:, sl] = (q * (MLA_SCALE * LOG2E)).astype(BF16)
        km_ref[:, sl] = (_dot(ckv, wuk_ref[:, sl]) + k_rot).astype(BF16)
        vm_ref[:, sl] = (_dot(ckv, wuv_ref[:, sl]) + ones_lane).astype(BF16)
    qn = xdot(4) * (NSA_SCALE * LOG2E) + slope_ref[...]
    for h in range(NSA_HEADS):
        qn_ref[h] = qn[:, h * LANE:(h + 1) * LANE].astype(BF16)
    kvc = xdot(5).astype(BF16)
    kvc_ref[0] = kvc[:, :LANE]
    kvc_ref[1] = kvc[:, LANE:]
    kv8 = xdot(6)
    blk = lambda i: kv8[:, i * LANE:(i + 1) * LANE]
    for g in range(NSA_GROUPS):
        ks_ref[:, 2 * g * LANE:(2 * g + 1) * LANE] = (blk(g) + pos_term).astype(BF16)
        ks_ref[:, (2 * g + 1) * LANE:(2 * g + 2) * LANE] = block_onehot
        kw_ref[:, g * LANE:(g + 1) * LANE] = (blk(2 + g) + pos_term).astype(BF16)
        vs_ref[:, g * LANE:(g + 1) * LANE] = (blk(4 + g) + ones_lane).astype(BF16)
        vw_ref[:, g * LANE:(g + 1) * LANE] = (blk(6 + g) + ones_lane).astype(BF16)
    gate_ref[...] = jax.nn.sigmoid(xdot(7))


def _head_blocks(w, n_heads, width):
    k = w.shape[0]
    w = jnp.pad(w.reshape(k, n_heads, width), ((0, 0), (0, 0), (0, LANE - width)))
    return w.reshape(k, n_heads * LANE)


def even_proj(x2, cos_t, sin_t, dpos, seq, w_in, q_norm_g, kv_norm_g, w_uq, w_uk, w_uv, tm=256):
    m, d = x2.shape
    tm = min(tm, seq)
    half = MLA_ROPE // 2
    gw = NSA_GROUPS * NSA_DH
    cuts = np.cumsum((MLA_Q_RANK, MLA_KV_RANK, MLA_ROPE, NSA_HEADS * NSA_DH) + (gw,) * 6)
    cuts = [0] + [int(c) for c in cuts]
    w_cq, w_ckv, w_kpe, w_q = (w_in[:, cuts[i]:cuts[i + 1]] for i in range(4))
    w_kc, w_vc, w_ks, w_vs, w_kw, w_vw = (w_in[:, cuts[4 + i]:cuts[5 + i]] for i in range(6))
    w_gate = w_in[:, cuts[10]:]
    w_kpe_sw = jnp.concatenate([-w_kpe[:, half:], w_kpe[:, :half]], axis=1)
    rope_pad = ((0, 0), (MLA_NOPE, LANE - MLA_NOPE - MLA_ROPE))
    w_gate_blk = jnp.pad(w_gate, ((0, 0), (0, LANE - w_gate.shape[1])))
    wx = jnp.concatenate(
        [w_cq, w_ckv, jnp.pad(w_kpe, rope_pad), jnp.pad(w_kpe_sw, rope_pad), _head_blocks(w_q, NSA_HEADS, NSA_DH),
         w_kc, w_vc] + [_head_blocks(w, NSA_GROUPS, NSA_DH) for w in (w_ks, w_kw, w_vs, w_vw)] + [w_gate_blk],
        axis=1).astype(BF16)

    qd = MLA_NOPE + MLA_ROPE
    uq = w_uq.reshape(MLA_Q_RANK, MLA_HEADS, qd)
    uq_sw = jnp.concatenate([jnp.zeros_like(uq[..., :MLA_NOPE]), -uq[..., MLA_NOPE + half:],
                             uq[..., MLA_NOPE:MLA_NOPE + half]], axis=-1)
    wuq = _head_blocks(uq.reshape(MLA_Q_RANK, -1), MLA_HEADS, qd).astype(BF16)
    wuqs = _head_blocks(uq_sw.reshape(MLA_Q_RANK, -1), MLA_HEADS, qd).astype(BF16)
    wuk = _head_blocks(w_uk, MLA_HEADS, MLA_NOPE).astype(BF16)
    wuv = _head_blocks(w_uv, MLA_HEADS, MLA_V).astype(BF16)
    slope_row = np.zeros((1, HW), np.float32)
    for h in range(NSA_HEADS):
        for ln in BIAS_LANES:
            slope_row[0, h * LANE + ln] = NSA_SLOPES[h]

    row = lambda w: pl.BlockSpec((tm, w), lambda i: (i, 0))
    sds = jax.ShapeDtypeStruct
    return pl.pallas_call(
        functools.partial(_even_proj_kernel, tiles_per_seq=seq // tm),
        grid=(m // tm,),
        in_specs=[row(d), row(LANE), row(LANE), row(1), _const_spec(wx.shape), _const_spec((1, MLA_Q_RANK)),
                  _const_spec((1, MLA_KV_RANK)), _const_spec(wuq.shape), _const_spec(wuqs.shape),
                  _const_spec(wuk.shape), _const_spec(wuv.shape), _const_spec((1, HW))],
        out_specs=[row(HW), row(HW), row(HW), pl.BlockSpec((NSA_HEADS, tm, LANE), lambda i: (0, i, 0)),
                   pl.BlockSpec((2, tm, LANE), lambda i: (0, i, 0)), row(4 * LANE), row(2 * LANE), row(2 * LANE),
                   row(2 * LANE), row(LANE)],
        out_shape=[sds((m, HW), BF16)] * 3 + [sds((NSA_HEADS, m, LANE), BF16), sds((2, m, LANE), BF16),
                                              sds((m, 4 * LANE), BF16), sds((m, 2 * LANE), BF16),
                                              sds((m, 2 * LANE), BF16), sds((m, 2 * LANE), BF16), sds((m, LANE), F32)],
        compiler_params=_params(("parallel",)),
        name="even_proj",
    )(x2, cos_t, sin_t, dpos, wx, q_norm_g.reshape(1, -1), kv_norm_g.reshape(1, -1), wuq, wuqs, wuk, wuv,
      jnp.asarray(slope_row))


def _causal_pairs(nq, tq, tk):
    qi, ki = [], []
    for i in range(nq):
        for j in range(((i + 1) * tq - 1) // tk + 1):
            qi.append(i)
            ki.append(j)
    return jnp.asarray(qi, jnp.int32), jnp.asarray(ki, jnp.int32)


def _mla_kernel(qi_tab, ki_tab, q_ref, k_ref, v_ref, o_ref, m_ref, acc_ref, *, tq, tk):
    p = pl.program_id(1)
    qi = qi_tab[p]
    ki = ki_tab[p]

    @pl.when(ki == 0)
    def _():
        _init_state(m_ref, acc_ref)

    def run(masked):
        if masked:
            mask = (qi * tq + _iota((tq, tk), 0)) >= (ki * tk + _iota((tq, tk), 1))
        for h0 in range(0, MLA_HEADS, MLA_CHAINS):
            heads = range(h0, h0 + MLA_CHAINS)
            scores = [_dot_nt(q_ref[0, :, h * LANE:(h + 1) * LANE], k_ref[0, :, h * LANE:(h + 1) * LANE])
                      for h in heads]
            if masked:
                scores = [jnp.where(mask, s, NEG) for s in scores]
            _flash_update_staged(scores, [v_ref[0, :, h * LANE:(h + 1) * LANE] for h in heads], m_ref, acc_ref,
                                 list(heads))

    crosses = (ki + 1) * tk - 1 > qi * tq

    @pl.when(crosses)
    def _():
        run(True)

    @pl.when(jnp.logical_not(crosses))
    def _():
        run(False)

    @pl.when(ki == ((qi + 1) * tq - 1) // tk)
    def _():
        for h in range(MLA_HEADS):
            o_ref[0, :, h * LANE:(h + 1) * LANE] = _normalized(acc_ref[h]).astype(BF16)


def mla_attention(q, k, v, tq=512, tk=512):
    b, s, hw = q.shape
    tq, tk = min(tq, s), min(tk, s)
    qi_tab, ki_tab = _causal_pairs(s // tq, tq, tk)
    qspec = pl.BlockSpec((1, tq, hw), lambda bb, p, qt, kt: (bb, qt[p], 0))
    kspec = pl.BlockSpec((1, tk, hw), lambda bb, p, qt, kt: (bb, kt[p], 0))
    return pl.pallas_call(
        functools.partial(_mla_kernel, tq=tq, tk=tk),
        grid_spec=pltpu.PrefetchScalarGridSpec(
            num_scalar_prefetch=2,
            grid=(b, int(qi_tab.shape[0])),
            in_specs=[qspec, kspec, kspec],
            out_specs=qspec,
            scratch_shapes=[pltpu.VMEM((MLA_HEADS, tq, LANE), F32), pltpu.VMEM((MLA_HEADS, tq, LANE), F32)],
        ),
        out_shape=jax.ShapeDtypeStruct((b, s, hw), BF16),
        compiler_params=_params(("parallel", "arbitrary")),
        name="mla_attention",
    )(qi_tab, ki_tab, q, k, v)


def _compress_kernel(h_ref, pos_ref, w1_ref, w2_ref, ext_ref, o_ref, *, is_value):
    n16 = h_ref.shape[2]
    half = w1_ref.shape[0] // 2
    lane = _iota((n16, LANE), 1)
    if is_value:
        extra = jnp.where(lane == ONES_LANE, 1.0, 0.0)
    else:
        extra = _bias_pieces(ext_ref[0], lane)
    bias = _dot(pos_ref[...], w1_ref[...])[0:1]
    for g in range(NSA_GROUPS):
        hg = h_ref[0, g]
        first = _dot(hg, w1_ref[:half, :])
        second = _dot(hg, w1_ref[half:, :])
        hid = first + pltpu.roll(second, n16 - 1, 0) + bias
        act = jax.nn.gelu(hid).astype(BF16)
        o_ref[0, g] = (_dot(act, w2_ref[...]) + extra).astype(BF16)


def nsa_compress(kv, cmp_pos, w1, w2, dpos_cend, is_value):
    b, s, _ = kv.shape
    n16 = s // CMP_STRIDE
    h = kv.reshape(b, n16, CMP_STRIDE, NSA_GROUPS, NSA_DH).transpose(0, 3, 1, 2, 4)
    h = h.reshape(b, NSA_GROUPS, n16, CMP_STRIDE * NSA_DH)
    pos = jnp.broadcast_to(cmp_pos.reshape(1, CMP_LEN * NSA_DH), (8, CMP_LEN * NSA_DH)).astype(BF16)
    w2p = jnp.pad(w2, ((0, 0), (0, LANE - NSA_DH))).astype(BF16)
    w1 = w1.astype(BF16)
    return pl.pallas_call(
        functools.partial(_compress_kernel, is_value=is_value),
        grid=(b,),
        in_specs=[pl.BlockSpec((1,) + h.shape[1:], lambda i: (i, 0, 0, 0)), _const_spec(pos.shape),
                  _const_spec(w1.shape), _const_spec(w2p.shape), pl.BlockSpec((1, n16, 1), lambda i: (i, 0, 0))],
        out_specs=pl.BlockSpec((1, NSA_GROUPS, n16, LANE), lambda i: (i, 0, 0, 0)),
        out_shape=jax.ShapeDtypeStruct((b, NSA_GROUPS, n16, LANE), BF16),
        compiler_params=_params(("parallel",)),
        name="nsa_compress",
    )(h, pos, w1, w2p, dpos_cend)


def _topk_mask_t(x, k):
    n = x.shape[0]
    ridx = _iota(x.shape, 0).astype(F32)
    sel = jnp.zeros(x.shape, F32)
    for _ in range(k):
        m = jnp.max(x, axis=0, keepdims=True)
        first = jnp.min(jnp.where(x == m, ridx, float(n)), axis=0, keepdims=True)
        hit = ridx == first
        sel = jnp.where(hit, 1.0, sel)
        x = jnp.where(hit, REMOVED, x)
    return sel


def _cmp_topk_kernel(q_ref, kc_ref, vc_ref, gate_ref, oc_ref, selm1_ref, imp_ref, *, tq, ncp, topk):
    qi = pl.program_id(1)
    t = qi * tq + _iota((tq, 1), 0)
    gates = gate_ref[0]
    blk = _iota((tq, LANE), 1)
    chunk = _shr(t, SEL_BLOCK)

    def attend(nc):
        cend = _iota((1, nc), 1) * CMP_STRIDE + (CMP_LEN - 1)
        mask = (cend <= t)[None]
        cstart = _iota((nc, LANE), 0) * CMP_STRIDE
        sstart = _iota((nc, LANE), 1) * SEL_BLOCK
        overlap = jnp.where((cstart < sstart + SEL_BLOCK) & (cstart + CMP_LEN > sstart)
                            & (cstart < (ncp - 1) * CMP_STRIDE), 1.0, 0.0).astype(BF16)
        for g in range(NSA_GROUPS):
            q = q_ref[g * NSA_HPG:(g + 1) * NSA_HPG].reshape(NSA_HPG * tq, LANE)
            s = _dot_nt(q, kc_ref[0, g, :nc]).reshape(NSA_HPG, tq, nc)
            s = jnp.where(mask, s, NEG)
            e = jnp.where(mask, jnp.exp2(s - jnp.max(s, axis=-1, keepdims=True)), 0.0)
            p = e / jnp.maximum(jnp.sum(e, axis=-1, keepdims=True), 1e-30)
            psum = jnp.sum(p, axis=0)
            o = _dot(p.reshape(NSA_HPG * tq, nc).astype(BF16), vc_ref[0, g, :nc])
            for n in range(NSA_HPG):
                h = g * NSA_HPG + n
                oh = jnp.where(blk < NSA_DH, o[n * tq:(n + 1) * tq] * gates[:, h:h + 1], 0.0)
                oc_ref[0, :, h * LANE:(h + 1) * LANE] = oh.astype(BF16)
            hi = psum.astype(BF16)
            r1 = psum - hi.astype(F32)
            mid = r1.astype(BF16)
            lo = (r1 - mid.astype(F32)).astype(BF16)
            imp_ref[g] = _dot(hi, overlap) + _dot(mid, overlap) + _dot(lo, overlap)

    lane_tiles = ((qi + 1) * tq // CMP_STRIDE + LANE - 1) // LANE
    for v in range(1, ncp // LANE + 1):
        @pl.when(jnp.minimum(lane_tiles, ncp // LANE) == v)
        def _(v=v):
            attend(v * LANE)

    for g in range(NSA_GROUPS):
        forced = (blk == 0) | (blk == chunk)
        imp = jnp.where(forced, REMOVED, jnp.where(blk <= chunk, imp_ref[g], NEG))
        sel = jnp.where(forced, 1.0, _topk_mask_t(imp.T, topk - 2).T)
        selm1_ref[0, g] = (jnp.where(blk <= chunk, sel, 0.0) - 1.0).astype(BF16)


def nsa_cmp_topk(qn, kc, vc, gates, tq=128):
    _, b, s, _ = qn.shape
    tq = min(tq, s)
    ncp = kc.shape[2]
    assert s // SEL_BLOCK <= LANE
    topk = min(SEL_TOPK, s // SEL_BLOCK)
    kspec = pl.BlockSpec((1, NSA_GROUPS, ncp, LANE), lambda i, j: (i, 0, 0, 0))
    return pl.pallas_call(
        functools.partial(_cmp_topk_kernel, tq=tq, ncp=ncp, topk=topk),
        grid=(b, s // tq),
        in_specs=[pl.BlockSpec((NSA_HEADS, None, tq, LANE), lambda i, j: (0, i, j, 0)), kspec, kspec,
                  pl.BlockSpec((1, tq, LANE), lambda i, j: (i, j, 0))],
        out_specs=[pl.BlockSpec((1, tq, HW), lambda i, j: (i, j, 0)),
                   pl.BlockSpec((1, NSA_GROUPS, tq, LANE), lambda i, j: (i, 0, j, 0))],
        out_shape=[jax.ShapeDtypeStruct((b, s, HW), BF16), jax.ShapeDtypeStruct((b, NSA_GROUPS, s, LANE), BF16)],
        scratch_shapes=[pltpu.VMEM((NSA_GROUPS, tq, LANE), F32)],
        compiler_params=_params(("parallel", "parallel")),
        name="nsa_cmp_topk",
    )(qn, kc, vc, gates)


def _nsa_write(o_ref, gate_ref, acc_ref, tq, branch):
    gates = gate_ref[0]
    for h in range(NSA_HEADS):
        g, n = divmod(h, NSA_HPG)
        col = branch * NSA_HEADS + h
        o = _normalized(acc_ref[g, n * tq:(n + 1) * tq]) * gates[:, col:col + 1]
        o_ref[0, :, h * LANE:(h + 1) * LANE] = o.astype(BF16)


def _sel_attn_kernel(flags, q_ref, k_ref, v_ref, selm1_ref, gate_ref, o_ref, lhs_ref, m_ref, acc_ref, *, t, nq):
    b = pl.program_id(0)
    qi = pl.program_id(1)
    _init_state(m_ref, acc_ref)
    for g in range(NSA_GROUPS):
        lhs_ref[g, :, :LANE] = q_ref[g * NSA_HPG:(g + 1) * NSA_HPG].reshape(NSA_HPG * t, LANE)
        lhs_ref[g, :, LANE:] = jnp.concatenate([selm1_ref[0, g]] * NSA_HPG, axis=0)

    half = NSA_HPG * t // 2
    chains = [(g, i) for g in range(NSA_GROUPS) for i in range(2)]

    def update(ki, diagonal):
        rows = pl.ds(pl.multiple_of(ki * t, t), t)
        scores = [_dot_nt(lhs_ref[g, i * half:(i + 1) * half], k_ref[rows, 2 * g * LANE:(2 * g + 2) * LANE])
                  for g, i in chains]
        if diagonal:
            causal = (_iota((t, t), 1) <= _iota((t, t), 0))[None]
            scores = [jnp.where(causal, s.reshape(NSA_HPG // 2, t, t), NEG).reshape(half, t) for s in scores]
        _flash_update_staged(scores, [v_ref[rows, g * LANE:(g + 1) * LANE] for g, _ in chains], m_ref, acc_ref,
                             [(g, slice(i * half, (i + 1) * half)) for g, i in chains])

    base = (b * nq + qi) * nq

    def body(ki, carry):
        @pl.when(flags[base + ki] > 0)
        def _():
            update(ki, False)
        return carry

    lax.fori_loop(0, qi, body, 0)
    update(qi, True)
    _nsa_write(o_ref, gate_ref, acc_ref, t, branch=1)


def nsa_selected(qn, ks, vs, selm1, gates, t=256):
    _, b, s, _ = qn.shape
    t = min(t, s)
    nq = s // t
    bpt = t // SEL_BLOCK
    flags = (selm1.reshape(b, NSA_GROUPS, nq, t, LANE // bpt, bpt) > -0.5).any(axis=(1, 3, 5))[..., :nq]
    flags = flags.astype(jnp.int32).reshape(-1)
    imap_q = lambda bb, i, fl: (bb, i, 0)
    rows = NSA_HPG * t
    return pl.pallas_call(
        functools.partial(_sel_attn_kernel, t=t, nq=nq),
        grid_spec=pltpu.PrefetchScalarGridSpec(
            num_scalar_prefetch=1,
            grid=(b, nq),
            in_specs=[pl.BlockSpec((NSA_HEADS, None, t, LANE), lambda bb, i, fl: (0, bb, i, 0)),
                      pl.BlockSpec((None, s, 4 * LANE), lambda bb, i, fl: (bb, 0, 0)),
                      pl.BlockSpec((None, s, 2 * LANE), lambda bb, i, fl: (bb, 0, 0)),
                      pl.BlockSpec((1, NSA_GROUPS, t, LANE), lambda bb, i, fl: (bb, 0, i, 0)),
                      pl.BlockSpec((1, t, LANE), imap_q)],
            out_specs=pl.BlockSpec((1, t, HW), imap_q),
            scratch_shapes=[pltpu.VMEM((NSA_GROUPS, rows, 2 * LANE), BF16), pltpu.VMEM((NSA_GROUPS, rows, LANE), F32),
                            pltpu.VMEM((NSA_GROUPS, rows, LANE), F32)],
        ),
        out_shape=jax.ShapeDtypeStruct((b, s, HW), BF16),
        compiler_params=_params(("parallel", "arbitrary")),
        name="nsa_selected",
    )(flags, qn, ks, vs, selm1, gates)


def _win_attn_kernel(q_ref, k_ref, v_ref, gate_ref, o_ref, m_ref, acc_ref, *, tq, nkt):
    qi = pl.program_id(1)
    j = pl.program_id(2)
    kt = qi - j

    @pl.when(j == 0)
    def _():
        _init_state(m_ref, acc_ref)

    @pl.when(kt >= 0)
    def _():
        diff = (qi * tq + _iota((tq, tq), 0)) - (kt * tq + _iota((tq, tq), 1))
        mask = ((diff >= 0) & (diff < WIN))[None]
        half = NSA_HPG // 2
        chains = [(g, i) for g in range(NSA_GROUPS) for i in range(2)]
        scores = [_dot_nt(q_ref[g * NSA_HPG + i * half:g * NSA_HPG + (i + 1) * half].reshape(half * tq, LANE),
                          k_ref[0, :, g * LANE:(g + 1) * LANE]) for g, i in chains]
        scores = [jnp.where(mask, s.reshape(half, tq, tq), NEG).reshape(half * tq, tq) for s in scores]
        _flash_update_staged(scores, [v_ref[0, :, g * LANE:(g + 1) * LANE] for g, _ in chains], m_ref, acc_ref,
                             [(g, slice(i * half * tq, (i + 1) * half * tq)) for g, i in chains])

    @pl.when(j == nkt - 1)
    def _():
        _nsa_write(o_ref, gate_ref, acc_ref, tq, branch=2)


def nsa_window(qn, kw, vw, gates, tq=512):
    _, b, s, _ = qn.shape
    tq = min(tq, s, WIN)
    nkt = WIN // tq + 1
    kmap = lambda bb, i, j: (bb, jnp.maximum(i - j, 0), 0)
    rows = NSA_HPG * tq
    return pl.pallas_call(
        functools.partial(_win_attn_kernel, tq=tq, nkt=nkt),
        grid=(b, s // tq, nkt),
        in_specs=[pl.BlockSpec((NSA_HEADS, None, tq, LANE), lambda bb, i, j: (0, bb, i, 0)),
                  pl.BlockSpec((1, tq, 2 * LANE), kmap), pl.BlockSpec((1, tq, 2 * LANE), kmap),
                  pl.BlockSpec((1, tq, LANE), lambda bb, i, j: (bb, i, 0))],
        out_specs=pl.BlockSpec((1, tq, HW), lambda bb, i, j: (bb, i, 0)),
        out_shape=jax.ShapeDtypeStruct((b, s, HW), BF16),
        scratch_shapes=[pltpu.VMEM((NSA_GROUPS, rows, LANE), F32), pltpu.VMEM((NSA_GROUPS, rows, LANE), F32)],
        compiler_params=_params(("parallel", "parallel", "arbitrary")),
        name="nsa_window",
    )(qn, kw, vw, gates)


def _even_out_ln_kernel(x_ref, om_ref, oc_ref, os_ref, ow_ref, wm_ref, wn_ref, g_ref, b_ref, o_ref):
    nsa = (oc_ref[...].astype(F32) + os_ref[...].astype(F32) + ow_ref[...].astype(F32)).astype(BF16)
    mix = _dot(om_ref[...], wm_ref[...]) + _dot(nsa, wn_ref[...])
    o_ref[...] = _layer_norm(ALPHA * x_ref[...] + mix, g_ref[...], b_ref[...])


def even_out_ln(x2, o_mla, o_c, o_s, o_w, w_mla, w_nsa, g, b, tm=512):
    m, d = x2.shape
    tm = min(tm, m)
    row = lambda w: pl.BlockSpec((tm, w), lambda i: (i, 0))
    return pl.pallas_call(
        _even_out_ln_kernel,
        grid=(m // tm,),
        in_specs=[row(d)] + [row(HW)] * 4 + [_const_spec(w_mla.shape), _const_spec(w_nsa.shape),
                                             _const_spec((1, d)), _const_spec((1, d))],
        out_specs=row(d),
        out_shape=jax.ShapeDtypeStruct((m, d), F32),
        compiler_params=_params(("parallel",)),
        name="even_out_ln",
    )(x2, o_mla, o_c, o_s, o_w, w_mla, w_nsa, g.reshape(1, d), b.reshape(1, d))


def _odd_out_ln_kernel(x_ref, o0_ref, o1_ref, o2_ref, l0_ref, l1_ref, l2_ref, w_ref, g_ref, b_ref, o_ref,
                       o_scr, l_scr):
    tm = x_ref.shape[0]
    for gi, (src, lsrc) in enumerate(((o0_ref, l0_ref), (o1_ref, l1_ref), (o2_ref, l2_ref))):
        dil = DIL_PATTERNS[gi][1]
        for r in range(dil):
            rows = pl.ds(r, tm // dil, stride=dil) if dil > 1 else slice(None)
            for h in range(DIL_GH):
                o_scr[gi, h, rows, :] = src[r, :, h * DIL_DH:(h + 1) * DIL_DH].astype(F32)
            l_scr[gi, rows, :] = lsrc[r]
    lses = [l_scr[gi] for gi in range(DIL_GROUPS)]
    top = jnp.maximum(jnp.maximum(lses[0], lses[1]), lses[2])
    es = [jnp.exp(l - top) for l in lses]
    den = es[0] + es[1] + es[2]
    wts = [e / den for e in es]
    cols = []
    for h in range(DIL_GH):
        merged = None
        for gi in range(DIL_GROUPS):
            term = wts[gi][:, h:h + 1] * o_scr[gi, h]
            merged = term if merged is None else merged + term
        cols.append(merged.astype(BF16))
    mix = _dot(jnp.concatenate(cols, axis=1), w_ref[...])
    o_ref[...] = _layer_norm(ALPHA * x_ref[...] + mix, g_ref[...], b_ref[...])


def odd_out_ln(x2, outs, lses, w, g, b, tm=512):
    m, d = x2.shape
    b_, _, s = outs[0].shape[0], None, outs[0].shape[1] * outs[0].shape[2]
    tm = min(tm, s)
    width = DIL_GH * DIL_DH
    row = pl.BlockSpec((None, tm, d), lambda bb, i: (bb, i, 0))
    cls = lambda gi, wd: pl.BlockSpec((None, DIL_PATTERNS[gi][1], tm // DIL_PATTERNS[gi][1], wd),
                                      lambda bb, i: (bb, 0, i, 0))
    out = pl.pallas_call(
        _odd_out_ln_kernel,
        grid=(b_, s // tm),
        in_specs=[row] + [cls(gi, width) for gi in range(DIL_GROUPS)] + [cls(gi, LANE) for gi in range(DIL_GROUPS)]
        + [_const_spec(w.shape), _const_spec((1, d)), _const_spec((1, d))],
        out_specs=row,
        out_shape=jax.ShapeDtypeStruct((b_, s, d), F32),
        scratch_shapes=[pltpu.VMEM((DIL_GROUPS, DIL_GH, tm, DIL_DH), F32), pltpu.VMEM((DIL_GROUPS, tm, LANE), F32)],
        compiler_params=_params(("parallel", "parallel")),
        name="odd_out_ln",
    )(x2.reshape(b_, s, d), *outs, *lses, w, g.reshape(1, d), b.reshape(1, d))
    return out.reshape(m, d)


def _odd_proj_kernel(x_ref, w_ref, o0_ref, o1_ref, o2_ref, xs_ref):
    tm, d_model = x_ref.shape
    width = DIL_GH * DIL_DH
    n_chunks = d_model // LANE
    for c in range(n_chunks):
        xs_ref[c] = x_ref[:, c * LANE:(c + 1) * LANE]
    for gi, o_ref in enumerate((o0_ref, o1_ref, o2_ref)):
        dil = DIL_PATTERNS[gi][1]
        if dil == 1:
            xg = x_ref[...]
        else:
            xg = jnp.concatenate(
                [jnp.concatenate([xs_ref[c, pl.ds(r, tm // dil, stride=dil), :] for r in range(dil)], axis=0)
                 for c in range(n_chunks)], axis=1)
        xg = xg.astype(BF16)
        for c in range(3):
            col = (gi * 3 + c) * width
            y = _dot(xg, w_ref[:, col:col + width])
            if c == 0:
                y = y * (DIL_SCALE * LOG2E)
            o_ref[:, :, c * width:(c + 1) * width] = y.astype(BF16).reshape(dil, tm // dil, width)


def odd_proj(x2, b, s, w_in, tm=512):
    m, d = x2.shape
    tm = min(tm, s)
    width = DIL_GH * DIL_DH
    w = w_in.reshape(d, 3, DIL_GROUPS, width).transpose(0, 2, 1, 3).reshape(d, 3 * DIL_GROUPS * width).astype(BF16)
    dils = [dil for _, dil in DIL_PATTERNS]
    return pl.pallas_call(
        _odd_proj_kernel,
        grid=(b, s // tm),
        in_specs=[pl.BlockSpec((None, tm, d), lambda bb, i: (bb, i, 0)), _const_spec(w.shape)],
        out_specs=[pl.BlockSpec((None, dil, tm // dil, 3 * width), lambda bb, i: (bb, 0, i, 0)) for dil in dils],
        out_shape=[jax.ShapeDtypeStruct((b, dil, s // dil, 3 * width), BF16) for dil in dils],
        scratch_shapes=[pltpu.VMEM((d // LANE, tm, LANE), F32)],
        compiler_params=_params(("parallel", "parallel")),
        name="odd_proj",
    )(x2.reshape(b, s, d), w)


def _dil_attn_kernel(slope_tab, pos0_tab, q_ref, kc_ref, kp_ref, vc_ref, vp_ref, pc_ref, pp_ref, pq_ref, o_ref,
                     lse_ref, *, group, tq):
    sub = DIL_SPAN
    b = pl.program_id(0)
    jt = pl.program_id(2)
    row = _iota((sub, 2 * sub), 0)
    col = _iota((sub, 2 * sub), 1)
    band = (col >= row) & (col <= row + sub)
    band_bias = jnp.where(band, 0.0, NEG)
    start_bias = jnp.where(band & (col >= sub), 0.0, NEG)
    pos0 = pos0_tab[b]
    lane = _iota((sub, LANE), 1)
    for i in range(tq // sub):
        cur = slice(i * sub, (i + 1) * sub)
        if i == 0:
            mask_bias = jnp.where(jt == 0, start_bias, band_bias)
            pk = jnp.concatenate([pp_ref[...], pc_ref[:, :sub]], axis=1)
        else:
            mask_bias = band_bias
            pk = pc_ref[:, (i - 1) * sub:(i + 1) * sub]
        dpos = pk - pos0
        dq = (pq_ref[cur, :] - pos0) * LN2
        heads = range(DIL_GH)
        hs = [slice(h * DIL_DH, (h + 1) * DIL_DH) for h in heads]
        slopes = [slope_tab[group * DIL_GH + h] for h in heads]
        if i == 0:
            ks = [jnp.concatenate([kp_ref[:, hs[h]], kc_ref[:sub, hs[h]]], axis=0) for h in heads]
            vs = [jnp.concatenate([vp_ref[:, hs[h]], vc_ref[:sub, hs[h]]], axis=0) for h in heads]
        else:
            ks = [kc_ref[(i - 1) * sub:(i + 1) * sub, hs[h]] for h in heads]
            vs = [vc_ref[(i - 1) * sub:(i + 1) * sub, hs[h]] for h in heads]
        ss = [_dot_nt(q_ref[cur, hs[h]], ks[h]) + (mask_bias + slopes[h] * dpos) for h in heads]
        ms = [jnp.max(s, axis=-1, keepdims=True) for s in ss]
        ps = [jnp.exp2(s - m) for s, m in zip(ss, ms)]
        dens = [jnp.sum(p, axis=-1, keepdims=True) for p in ps]
        lse_tile = jnp.zeros((sub, LANE), F32)
        for h in heads:
            o_ref[cur, hs[h]] = (_dot(ps[h].astype(BF16), vs[h]) / dens[h]).astype(BF16)
            lse_tile = jnp.where(lane == h, ms[h] * LN2 + jnp.log(dens[h]) - slopes[h] * dq, lse_tile)
        lse_ref[cur, :] = lse_tile


def dilated_group_attention(qkv, pos_f, group, tq=512):
    b, d, sd, _ = qkv.shape
    w, dil = DIL_PATTERNS[group]
    assert w // dil == DIL_SPAN and dil == d
    tq = min(tq, sd)
    sub = DIL_SPAN
    width = DIL_GH * DIL_DH
    n_slopes = DIL_GROUPS * DIL_GH
    slopes = 2.0 ** (-ALIBI_MAX_BIAS * jnp.arange(1, n_slopes + 1, dtype=F32) / n_slopes) * LOG2E
    pos0 = pos_f[:, 0]
    posc = pos_f.reshape(b, sd, d).transpose(0, 2, 1).reshape(b, d, 1, sd)
    r = tq // sub
    cur = lambda c: pl.BlockSpec((None, None, tq, width), lambda bb, rr, j, *_: (bb, rr, j, c))
    prev = lambda c: pl.BlockSpec((None, None, sub, width),
                                  lambda bb, rr, j, *_: (bb, rr, jnp.maximum(j * r - 1, 0), c))
    return pl.pallas_call(
        functools.partial(_dil_attn_kernel, group=group, tq=tq),
        grid_spec=pltpu.PrefetchScalarGridSpec(
            num_scalar_prefetch=2,
            grid=(b, d, sd // tq),
            in_specs=[cur(0), cur(1), prev(1), cur(2), prev(2),
                      pl.BlockSpec((None, None, 1, tq), lambda bb, rr, j, *_: (bb, rr, 0, j)),
                      pl.BlockSpec((None, None, 1, sub), lambda bb, rr, j, *_: (bb, rr, 0, jnp.maximum(j * r - 1, 0))),
                      pl.BlockSpec((None, None, tq, 1), lambda bb, rr, j, *_: (bb, rr, j, 0))],
            out_specs=[pl.BlockSpec((None, None, tq, width), lambda bb, rr, j, *_: (bb, rr, j, 0)),
                       pl.BlockSpec((None, None, tq, LANE), lambda bb, rr, j, *_: (bb, rr, j, 0))],
        ),
        out_shape=[jax.ShapeDtypeStruct((b, d, sd, width), BF16), jax.ShapeDtypeStruct((b, d, sd, LANE), F32)],
        compiler_params=_params(("parallel", "parallel", "parallel")),
        name=f"dilated_attention_g{group}",
    )(slopes, pos0, qkv, qkv, qkv, qkv, qkv, posc, posc, posc.reshape(b, d, sd, 1))


def even_mixer_ln(x2, b, s, pos_f, w_in, q_norm_g, kv_norm_g, w_uq, w_uk, w_uv,
                  cmp_pos, cmp_k_w1, cmp_k_w2, cmp_v_w1, cmp_v_w2, w_out, ln_g, ln_b):
    m = b * s
    d = x2.shape[1]
    half = MLA_ROPE // 2
    inv = ROPE_THETA ** (-jnp.arange(half, dtype=F32) / half)
    ang = (pos_f[..., None] * inv).reshape(m, half)
    ones = jnp.ones((m, MLA_NOPE), F32)
    zeros = jnp.zeros((m, LANE - MLA_NOPE - MLA_ROPE), F32)
    cos_t = jnp.concatenate([ones, jnp.cos(ang), jnp.cos(ang), zeros], axis=1)
    sin_t = jnp.concatenate([0.0 * ones, jnp.sin(ang), jnp.sin(ang), zeros], axis=1)
    dpos = (pos_f - pos_f[:, :1]) * LOG2E
    ncp = s // CMP_STRIDE
    dpos_cend = jnp.pad(dpos[:, CMP_LEN - 1::CMP_STRIDE], ((0, 0), (0, 1))).reshape(b, ncp, 1)

    qm, km, vm, qn, kvc, ks, kw, vs, vw, gates = even_proj(
        x2, cos_t, sin_t, dpos.reshape(m, 1), s, w_in, q_norm_g, kv_norm_g, w_uq, w_uk, w_uv)
    o_mla = mla_attention(qm.reshape(b, s, HW), km.reshape(b, s, HW), vm.reshape(b, s, HW))

    kvc = kvc.reshape(2, b, s, LANE)
    kc = nsa_compress(kvc[0], cmp_pos, cmp_k_w1, cmp_k_w2, dpos_cend, is_value=False)
    vc = nsa_compress(kvc[1], cmp_pos, cmp_v_w1, cmp_v_w2, dpos_cend, is_value=True)
    qn = qn.reshape(NSA_HEADS, b, s, LANE)
    gates = gates.reshape(b, s, LANE)
    o_c, selm1 = nsa_cmp_topk(qn, kc, vc, gates)
    o_s = nsa_selected(qn, ks.reshape(b, s, 4 * LANE), vs.reshape(b, s, 2 * LANE), selm1, gates)
    o_w = nsa_window(qn, kw.reshape(b, s, 2 * LANE), vw.reshape(b, s, 2 * LANE), gates)

    pad_rows = lambda w, n, width: jnp.pad(w.reshape(n, width, d), ((0, 0), (0, LANE - width), (0, 0))).reshape(n * LANE, d)
    w_mla = pad_rows(w_out[:MLA_HEADS * MLA_V], MLA_HEADS, MLA_V).astype(BF16)
    w_nsa = pad_rows(w_out[MLA_HEADS * MLA_V:], NSA_HEADS, NSA_DH).astype(BF16)
    return even_out_ln(x2, o_mla.reshape(m, HW), o_c.reshape(m, HW), o_s.reshape(m, HW), o_w.reshape(m, HW),
                       w_mla, w_nsa, ln_g, ln_b)


def odd_mixer_ln(x2, b, s, pos_f, w_in, w_out, ln_g, ln_b):
    qkvs = odd_proj(x2, b, s, w_in)
    parts = [dilated_group_attention(qkvs[g], pos_f, g) for g in range(DIL_GROUPS)]
    return odd_out_ln(x2, [p[0] for p in parts], [p[1] for p in parts], w_out.astype(BF16), ln_g, ln_b)


def kernel(x, positions, ln1_g, ln1_b, ffn1_w_gate, ffn1_w_up, ffn1_w_down, mix_in_even, mla_q_norm, mla_kv_norm, mla_w_uq, mla_w_uk, mla_w_uv, nsa_cmp_pos, nsa_cmp_k_w1, nsa_cmp_k_w2, nsa_cmp_v_w1, nsa_cmp_v_w2, mix_out_even, mix_in_odd, mix_out_odd, ln2_g, ln2_b, ffn2_w_gate, ffn2_w_up, ffn2_w_down, ln3_g, ln3_b):
    b, s, d = x.shape
    x2 = x.reshape(b * s, d)
    pos_f = positions.astype(F32)
    for i in range(DEPTH):
        j = i // 2
        x2 = ffn_ln(x2, ffn1_w_gate[i].astype(BF16), ffn1_w_up[i].astype(BF16), ffn1_w_down[i].astype(BF16),
                    ln1_g[i], ln1_b[i])
        if i % 2 == 0:
            x2 = even_mixer_ln(x2, b, s, pos_f, mix_in_even[j], mla_q_norm[j], mla_kv_norm[j], mla_w_uq[j],
                               mla_w_uk[j], mla_w_uv[j], nsa_cmp_pos[j], nsa_cmp_k_w1[j], nsa_cmp_k_w2[j],
                               nsa_cmp_v_w1[j], nsa_cmp_v_w2[j], mix_out_even[j], ln2_g[i], ln2_b[i])
        else:
            x2 = odd_mixer_ln(x2, b, s, pos_f, mix_in_odd[j], mix_out_odd[j], ln2_g[i], ln2_b[i])
        x2 = ffn_ln(x2, ffn2_w_gate[i].astype(BF16), ffn2_w_up[i].astype(BF16), ffn2_w_down[i].astype(BF16),
                    ln3_g[i], ln3_b[i])
    return x2.reshape(b, s, d)
```

```python
import functools
import math

import numpy as np
import jax
import jax.numpy as jnp
from jax import lax
from jax.experimental import pallas as pl
from jax.experimental.pallas import tpu as pltpu

F32 = jnp.float32
BF16 = jnp.bfloat16

DEPTH = 2
LN_EPS = 1e-5
RMS_EPS = 1e-6
ALPHA = (2 * DEPTH) ** 0.25
HALF_STEP = 0.5
NEG = -1e30
BIG = 1e9
MASK_BIG = 1e30
REMOVED = -3.0e38
ALIBI_MAX_BIAS = 8.0
LOG2E = math.log2(math.e)
LN2 = math.log(2.0)
LANE = 128

MLA_HEADS = 8
MLA_Q_RANK = 384
MLA_KV_RANK = 256
MLA_NOPE = 64
MLA_ROPE = 32
MLA_V = 64
ROPE_THETA = 10000.0
MLA_SCALE = (MLA_NOPE + MLA_ROPE) ** -0.5
MLA_CHAINS = 4

NSA_HEADS = 8
NSA_GROUPS = 2
NSA_HPG = 4
NSA_DH = 64
CMP_LEN = 32
CMP_STRIDE = 16
CMP_HIDDEN = 256
SEL_BLOCK = 64
SEL_TOPK = 16
WIN = 512
NSA_SCALE = NSA_DH ** -0.5
NSA_SLOPES = tuple(2.0 ** (-ALIBI_MAX_BIAS * (i + 1) / NSA_HEADS) for i in range(NSA_HEADS))
ONES_LANE = 64
BIAS_LANES = (64, 65, 66)

DIL_PATTERNS = ((128, 1), (512, 4), (2048, 16))
DIL_GROUPS = 3
DIL_GH = 4
DIL_DH = 128
DIL_SCALE = DIL_DH ** -0.5
DIL_SPAN = 128

VMEM_LIMIT = 48 * 1024 * 1024


def _iota(shape, dim):
    return lax.broadcasted_iota(jnp.int32, shape, dim)


def _shr(x, pow2):
    return jnp.right_shift(x, int(pow2).bit_length() - 1)


def _dot(a, b):
    return jnp.dot(a, b, preferred_element_type=F32)


def _dot_nt(a, b):
    return lax.dot_general(a, b, (((1,), (1,)), ((), ())), preferred_element_type=F32)


def _const_spec(shape):
    zeros = (0,) * len(shape)
    return pl.BlockSpec(shape, lambda *_: zeros, pipeline_mode=pl.Buffered(1))


def _params(sem):
    return pltpu.CompilerParams(dimension_semantics=sem, vmem_limit_bytes=VMEM_LIMIT)


def _layer_norm(z, g, b):
    mu = jnp.mean(z, axis=-1, keepdims=True)
    zc = z - mu
    var = jnp.mean(zc * zc, axis=-1, keepdims=True)
    return zc * lax.rsqrt(var + LN_EPS) * g + b


def _rms_norm(z, g):
    return z * lax.rsqrt(jnp.mean(z * z, axis=-1, keepdims=True) + RMS_EPS) * g


def _bias_pieces(d, lane):
    x = jnp.broadcast_to(d, lane.shape)
    hi = x.astype(BF16).astype(F32)
    r = x - hi
    mid = r.astype(BF16).astype(F32)
    lo = (r - mid).astype(BF16).astype(F32)
    return jnp.where(lane == BIAS_LANES[0], hi,
                     jnp.where(lane == BIAS_LANES[1], mid, jnp.where(lane == BIAS_LANES[2], lo, 0.0)))


def _flash_update(s, v, m_ref, acc_ref, idx):
    m_old = m_ref[idx]
    m_new = jnp.maximum(m_old, jnp.max(s, axis=-1, keepdims=True))
    p = jnp.exp2(s - jnp.tile(m_new, (1, s.shape[1] // LANE))).astype(BF16)
    acc_ref[idx] = jnp.exp2(m_old - m_new) * acc_ref[idx] + _dot(p, v)
    m_ref[idx] = m_new


def _flash_update_staged(scores, values, m_ref, acc_ref, idxs):
    m_old = [m_ref[i] for i in idxs]
    m_new = [jnp.maximum(mo, jnp.max(s, axis=-1, keepdims=True)) for mo, s in zip(m_old, scores)]
    ps = [jnp.exp2(s - jnp.tile(mn, (1, s.shape[1] // LANE))).astype(BF16) for s, mn in zip(scores, m_new)]
    for i, mo, mn, p, v in zip(idxs, m_old, m_new, ps, values):
        acc_ref[i] = jnp.exp2(mo - mn) * acc_ref[i] + _dot(p, v)
        m_ref[i] = mn


def _init_state(m_ref, acc_ref):
    m_ref[...] = jnp.full(m_ref.shape, NEG, F32)
    acc_ref[...] = jnp.zeros(acc_ref.shape, F32)


def _normalized(acc):
    lane = _iota(acc.shape, 1)
    o = acc / jnp.maximum(acc[:, ONES_LANE:ONES_LANE + 1], 1e-30)
    return jnp.where(lane < ONES_LANE, o, 0.0)


def _store_head_pairs(o_ref, rows, heads, first_pair=0):
    for pr in range(len(heads) // 2):
        packed = heads[2 * pr] + pltpu.roll(heads[2 * pr + 1], LANE // 2, 1)
        o_ref[0, rows, (first_pair + pr) * LANE:(first_pair + pr + 1) * LANE] = packed.astype(BF16)


def _ffn_ln_kernel(x_ref, wg_ref, wu_ref, wd_ref, g_ref, b_ref, o_ref, *, n_chunks):
    x = x_ref[...]
    xb = x.astype(BF16)
    c = wg_ref.shape[1] // n_chunks
    y = None
    for i in range(n_chunks):
        gt = _dot(xb, wg_ref[:, i * c:(i + 1) * c])
        up = _dot(xb, wu_ref[:, i * c:(i + 1) * c])
        h = (gt * jax.nn.sigmoid(gt) * up).astype(BF16)
        part = _dot(h, wd_ref[i * c:(i + 1) * c, :])
        y = part if y is None else y + part
    o_ref[...] = _layer_norm(ALPHA * x + HALF_STEP * y, g_ref[...], b_ref[...])


def ffn_ln(x2, wg, wu, wd, g, b, tm=1024, n_chunks=11):
    m, d = x2.shape
    tm = min(tm, m)
    row = pl.BlockSpec((tm, d), lambda i: (i, 0))
    return pl.pallas_call(
        functools.partial(_ffn_ln_kernel, n_chunks=n_chunks),
        grid=(m // tm,),
        in_specs=[row, _const_spec(wg.shape), _const_spec(wu.shape), _const_spec(wd.shape),
                  _const_spec((1, d)), _const_spec((1, d))],
        out_specs=row,
        out_shape=jax.ShapeDtypeStruct((m, d), F32),
        compiler_params=_params(("parallel",)),
        name="ffn_ln",
    )(x2, wg, wu, wd, g.reshape(1, d), b.reshape(1, d))


HW = MLA_HEADS * LANE
OUT_W = MLA_HEADS * MLA_V
EVEN_X_COLS = (MLA_Q_RANK, MLA_KV_RANK, LANE, LANE, HW, 2 * LANE, 8 * LANE, LANE)
EVEN_X_OFFS = tuple(int(v) for v in np.cumsum((0,) + EVEN_X_COLS))


def _even_proj_kernel(x_ref, cos_ref, sin_ref, dpos_ref, wx_ref, qg_ref, kvg_ref, wuq_ref, wuqs_ref, wuk_ref,
                      wuv_ref, slope_ref, qm_ref, km_ref, vm_ref, qn_ref, kvc_ref, ks_ref, kw_ref, vs_ref, vw_ref,
                      gate_ref, kvc_scr, *, tiles_per_seq):
    tm = x_ref.shape[0]
    xb = x_ref[...].astype(BF16)
    cos = cos_ref[...]
    sin = sin_ref[...]
    lane = _iota((tm, LANE), 1)
    ones_lane = jnp.where(lane == ONES_LANE, 1.0, 0.0)
    pos_term = _bias_pieces(dpos_ref[...], lane)
    tok = (pl.program_id(0) % tiles_per_seq) * tm + _iota((tm, LANE), 0)
    block_onehot = jnp.where(lane == _shr(tok, SEL_BLOCK), MASK_BIG, 0.0).astype(BF16)

    def xdot(i):
        return _dot(xb, wx_ref[:, EVEN_X_OFFS[i]:EVEN_X_OFFS[i + 1]])

    cq = _rms_norm(xdot(0), qg_ref[...]).astype(BF16)
    ckv = _rms_norm(xdot(1), kvg_ref[...]).astype(BF16)
    k_rot = xdot(2) * cos + xdot(3) * sin
    for h in range(MLA_HEADS):
        sl = slice(h * LANE, (h + 1) * LANE)
        q = _dot(cq, wuq_ref[:, sl]) * cos + _dot(cq, wuqs_ref[:, sl]) * sin
        qm_ref[:, sl] = (q * (MLA_SCALE * LOG2E)).astype(BF16)
        km_ref[:, sl] = (_dot(ckv, wuk_ref[:, sl]) + k_rot).astype(BF16)
        vm_ref[:, sl] = (_dot(ckv, wuv_ref[:, sl]) + ones_lane).astype(BF16)
    qn = xdot(4) * (NSA_SCALE * LOG2E) + slope_ref[...]
    for h in range(NSA_HEADS):
        qn_ref[h] = qn[:, h * LANE:(h + 1) * LANE].astype(BF16)
    kvc = xdot(5)
    for j in range(2):
        kvc_scr[j] = kvc[:, j * LANE:(j + 1) * LANE]
        for l in range(CMP_STRIDE):
            kvc_ref[j, :, l * LANE:(l + 1) * LANE] = kvc_scr[j, pl.ds(l, tm // CMP_STRIDE, stride=CMP_STRIDE), :].astype(BF16)
    kv8 = xdot(6)
    blk = lambda i: kv8[:, i * LANE:(i + 1) * LANE]
    for g in range(NSA_GROUPS):
        ks_ref[:, 2 * g * LANE:(2 * g + 1) * LANE] = (blk(g) + pos_term).astype(BF16)
        ks_ref[:, (2 * g + 1) * LANE:(2 * g + 2) * LANE] = block_onehot
        kw_ref[:, g * LANE:(g + 1) * LANE] = (blk(2 + g) + pos_term).astype(BF16)
        vs_ref[:, g * LANE:(g + 1) * LANE] = (blk(4 + g) + ones_lane).astype(BF16)
        vw_ref[:, g * LANE:(g + 1) * LANE] = (blk(6 + g) + ones_lane).astype(BF16)
    gate_ref[...] = jax.nn.sigmoid(xdot(7))


def _head_blocks(w, n_heads, width):
    k = w.shape[0]
    w = jnp.pad(w.reshape(k, n_heads, width), ((0, 0), (0, 0), (0, LANE - width)))
    return w.reshape(k, n_heads * LANE)


def even_proj(x2, cos_t, sin_t, dpos, seq, w_in, q_norm_g, kv_norm_g, w_uq, w_uk, w_uv, tm=256):
    m, d = x2.shape
    tm = min(tm, seq)
    half = MLA_ROPE // 2
    gw = NSA_GROUPS * NSA_DH
    cuts = np.cumsum((MLA_Q_RANK, MLA_KV_RANK, MLA_ROPE, NSA_HEADS * NSA_DH) + (gw,) * 6)
    cuts = [0] + [int(c) for c in cuts]
    w_cq, w_ckv, w_kpe, w_q = (w_in[:, cuts[i]:cuts[i + 1]] for i in range(4))
    w_kc, w_vc, w_ks, w_vs, w_kw, w_vw = (w_in[:, cuts[4 + i]:cuts[5 + i]] for i in range(6))
    w_gate = w_in[:, cuts[10]:]
    w_kpe_sw = jnp.concatenate([-w_kpe[:, half:], w_kpe[:, :half]], axis=1)
    rope_pad = ((0, 0), (MLA_NOPE, LANE - MLA_NOPE - MLA_ROPE))
    w_gate_blk = jnp.pad(w_gate, ((0, 0), (0, LANE - w_gate.shape[1])))
    wx = jnp.concatenate(
        [w_cq, w_ckv, jnp.pad(w_kpe, rope_pad), jnp.pad(w_kpe_sw, rope_pad), _head_blocks(w_q, NSA_HEADS, NSA_DH),
         w_kc, w_vc] + [_head_blocks(w, NSA_GROUPS, NSA_DH) for w in (w_ks, w_kw, w_vs, w_vw)] + [w_gate_blk],
        axis=1).astype(BF16)

    qd = MLA_NOPE + MLA_ROPE
    uq = w_uq.reshape(MLA_Q_RANK, MLA_HEADS, qd)
    uq_sw = jnp.concatenate([jnp.zeros_like(uq[..., :MLA_NOPE]), -uq[..., MLA_NOPE + half:],
                             uq[..., MLA_NOPE:MLA_NOPE + half]], axis=-1)
    wuq = _head_blocks(uq.reshape(MLA_Q_RANK, -1), MLA_HEADS, qd).astype(BF16)
    wuqs = _head_blocks(uq_sw.reshape(MLA_Q_RANK, -1), MLA_HEADS, qd).astype(BF16)
    wuk = _head_blocks(w_uk, MLA_HEADS, MLA_NOPE).astype(BF16)
    wuv = _head_blocks(w_uv, MLA_HEADS, MLA_V).astype(BF16)
    slope_row = np.zeros((1, HW), np.float32)
    for h in range(NSA_HEADS):
        for ln in BIAS_LANES:
            slope_row[0, h * LANE + ln] = NSA_SLOPES[h]

    row = lambda w: pl.BlockSpec((tm, w), lambda i: (i, 0))
    sds = jax.ShapeDtypeStruct
    return pl.pallas_call(
        functools.partial(_even_proj_kernel, tiles_per_seq=seq // tm),
        grid=(m // tm,),
        in_specs=[row(d), row(LANE), row(LANE), row(1), _const_spec(wx.shape), _const_spec((1, MLA_Q_RANK)),
                  _const_spec((1, MLA_KV_RANK)), _const_spec(wuq.shape), _const_spec(wuqs.shape),
                  _const_spec(wuk.shape), _const_spec(wuv.shape), _const_spec((1, HW))],
        out_specs=[row(HW), row(HW), row(HW), pl.BlockSpec((NSA_HEADS, tm, LANE), lambda i: (0, i, 0)),
                   pl.BlockSpec((2, tm // CMP_STRIDE, CMP_STRIDE * LANE), lambda i: (0, i, 0)), row(4 * LANE),
                   row(2 * LANE), row(2 * LANE), row(2 * LANE), row(LANE)],
        out_shape=[sds((m, HW), BF16)] * 3 + [sds((NSA_HEADS, m, LANE), BF16),
                                              sds((2, m // CMP_STRIDE, CMP_STRIDE * LANE), BF16),
                                              sds((m, 4 * LANE), BF16), sds((m, 2 * LANE), BF16),
                                              sds((m, 2 * LANE), BF16), sds((m, 2 * LANE), BF16), sds((m, LANE), F32)],
        scratch_shapes=[pltpu.VMEM((2, tm, LANE), F32)],
        compiler_params=_params(("parallel",)),
        name="even_proj",
    )(x2, cos_t, sin_t, dpos, wx, q_norm_g.reshape(1, -1), kv_norm_g.reshape(1, -1), wuq, wuqs, wuk, wuv,
      jnp.asarray(slope_row))


def _causal_pairs(nq, tq, tk):
    qi, ki = [], []
    for i in range(nq):
        for j in range(((i + 1) * tq - 1) // tk + 1):
            qi.append(i)
            ki.append(j)
    return jnp.asarray(qi, jnp.int32), jnp.asarray(ki, jnp.int32)


def _mla_kernel(qi_tab, ki_tab, q_ref, k_ref, v_ref, o_ref, m_ref, acc_ref, *, tq, tk):
    p = pl.program_id(1)
    qi = qi_tab[p]
    ki = ki_tab[p]

    @pl.when(ki == 0)
    def _():
        _init_state(m_ref, acc_ref)

    def run(masked):
        if masked:
            mask = (qi * tq + _iota((tq, tk), 0)) >= (ki * tk + _iota((tq, tk), 1))
        for h0 in range(0, MLA_HEADS, MLA_CHAINS):
            heads = range(h0, h0 + MLA_CHAINS)
            scores = [_dot_nt(q_ref[0, :, h * LANE:(h + 1) * LANE], k_ref[0, :, h * LANE:(h + 1) * LANE])
                      for h in heads]
            if masked:
                scores = [jnp.where(mask, s, NEG) for s in scores]
            _flash_update_staged(scores, [v_ref[0, :, h * LANE:(h + 1) * LANE] for h in heads], m_ref, acc_ref,
                                 list(heads))

    crosses = (ki + 1) * tk - 1 > qi * tq

    @pl.when(crosses)
    def _():
        run(True)

    @pl.when(jnp.logical_not(crosses))
    def _():
        run(False)

    @pl.when(ki == ((qi + 1) * tq - 1) // tk)
    def _():
        _store_head_pairs(o_ref, slice(None), [_normalized(acc_ref[h]) for h in range(MLA_HEADS)])


def mla_attention(q, k, v, tq=512, tk=512):
    b, s, hw = q.shape
    tq, tk = min(tq, s), min(tk, s)
    qi_tab, ki_tab = _causal_pairs(s // tq, tq, tk)
    qspec = pl.BlockSpec((1, tq, hw), lambda bb, p, qt, kt: (bb, qt[p], 0))
    kspec = pl.BlockSpec((1, tk, hw), lambda bb, p, qt, kt: (bb, kt[p], 0))
    return pl.pallas_call(
        functools.partial(_mla_kernel, tq=tq, tk=tk),
        grid_spec=pltpu.PrefetchScalarGridSpec(
            num_scalar_prefetch=2,
            grid=(b, int(qi_tab.shape[0])),
            in_specs=[qspec, kspec, kspec],
            out_specs=pl.BlockSpec((1, tq, OUT_W), lambda bb, p, qt, kt: (bb, qt[p], 0)),
            scratch_shapes=[pltpu.VMEM((MLA_HEADS, tq, LANE), F32), pltpu.VMEM((MLA_HEADS, tq, LANE), F32)],
        ),
        out_shape=jax.ShapeDtypeStruct((b, s, OUT_W), BF16),
        compiler_params=_params(("parallel", "arbitrary")),
        name="mla_attention",
    )(qi_tab, ki_tab, q, k, v)


def _compress_kernel(h_ref, pos_ref, w1_ref, w1g_ref, w2_ref, ext_ref, o_ref, *, is_value):
    n16 = h_ref.shape[1]
    lane = _iota((n16, LANE), 1)
    if is_value:
        extra = jnp.where(lane == ONES_LANE, 1.0, 0.0)
    else:
        extra = _bias_pieces(ext_ref[0], lane)
    bias = _dot(pos_ref[...], w1_ref[...])[0:1]
    h = h_ref[0]
    for g in range(NSA_GROUPS):
        first = _dot(h, w1g_ref[g, 0])
        second = _dot(h, w1g_ref[g, 1])
        hid = first + pltpu.roll(second, n16 - 1, 0) + bias
        act = jax.nn.gelu(hid).astype(BF16)
        o_ref[0, g] = (_dot(act, w2_ref[...]) + extra).astype(BF16)


def nsa_compress(h, cmp_pos, w1, w2, dpos_cend, is_value):
    b, n16, hw = h.shape
    pos = jnp.broadcast_to(cmp_pos.reshape(1, CMP_LEN * NSA_DH), (8, CMP_LEN * NSA_DH)).astype(BF16)
    w2p = jnp.pad(w2, ((0, 0), (0, LANE - NSA_DH))).astype(BF16)
    halves = w1.reshape(2, CMP_STRIDE, 1, NSA_DH, CMP_HIDDEN)
    w1g = jnp.stack([jnp.pad(halves, ((0, 0), (0, 0), (g, NSA_GROUPS - 1 - g), (0, 0), (0, 0))).reshape(2, hw, CMP_HIDDEN)
                     for g in range(NSA_GROUPS)]).astype(BF16)
    w1 = w1.astype(BF16)
    return pl.pallas_call(
        functools.partial(_compress_kernel, is_value=is_value),
        grid=(b,),
        in_specs=[pl.BlockSpec((1, n16, hw), lambda i: (i, 0, 0)), _const_spec(pos.shape),
                  _const_spec(w1.shape), _const_spec(w1g.shape), _const_spec(w2p.shape),
                  pl.BlockSpec((1, n16, 1), lambda i: (i, 0, 0))],
        out_specs=pl.BlockSpec((1, NSA_GROUPS, n16, LANE), lambda i: (i, 0, 0, 0)),
        out_shape=jax.ShapeDtypeStruct((b, NSA_GROUPS, n16, LANE), BF16),
        compiler_params=_params(("parallel",)),
        name="nsa_compress",
    )(h, pos, w1, w1g, w2p, dpos_cend)


def _topk_mask_t(x, k):
    n = x.shape[0]
    ridx = _iota(x.shape, 0).astype(F32)
    sel = jnp.zeros(x.shape, F32)
    for _ in range(k):
        m = jnp.max(x, axis=0, keepdims=True)
        first = jnp.min(jnp.where(x == m, ridx, float(n)), axis=0, keepdims=True)
        hit = ridx == first
        sel = jnp.where(hit, 1.0, sel)
        x = jnp.where(hit, REMOVED, x)
    return sel


def _cmp_topk_kernel(q_ref, kc_ref, vc_ref, gate_ref, oc_ref, selm1_ref, imp_ref, *, tq, ncp, topk):
    qi = pl.program_id(1)
    t = qi * tq + _iota((tq, 1), 0)
    gates = gate_ref[0]
    blk = _iota((tq, LANE), 1)
    chunk = _shr(t, SEL_BLOCK)

    def attend(nc):
        cend = _iota((1, nc), 1) * CMP_STRIDE + (CMP_LEN - 1)
        mask = (cend <= t)[None]
        cstart = _iota((nc, LANE), 0) * CMP_STRIDE
        sstart = _iota((nc, LANE), 1) * SEL_BLOCK
        overlap = jnp.where((cstart < sstart + SEL_BLOCK) & (cstart + CMP_LEN > sstart)
                            & (cstart < (ncp - 1) * CMP_STRIDE), 1.0, 0.0).astype(BF16)
        for g in range(NSA_GROUPS):
            q = q_ref[g * NSA_HPG:(g + 1) * NSA_HPG].reshape(NSA_HPG * tq, LANE)
            s = _dot_nt(q, kc_ref[0, g, :nc]).reshape(NSA_HPG, tq, nc)
            s = jnp.where(mask, s, NEG)
            e = jnp.where(mask, jnp.exp2(s - jnp.max(s, axis=-1, keepdims=True)), 0.0)
            p = e / jnp.maximum(jnp.sum(e, axis=-1, keepdims=True), 1e-30)
            psum = jnp.sum(p, axis=0)
            o = _dot(p.reshape(NSA_HPG * tq, nc).astype(BF16), vc_ref[0, g, :nc])
            heads = [jnp.where(blk < NSA_DH, o[n * tq:(n + 1) * tq] * gates[:, g * NSA_HPG + n:g * NSA_HPG + n + 1], 0.0)
                     for n in range(NSA_HPG)]
            _store_head_pairs(oc_ref, slice(None), heads, first_pair=g * NSA_HPG // 2)
            hi = psum.astype(BF16)
            r1 = psum - hi.astype(F32)
            mid = r1.astype(BF16)
            lo = (r1 - mid.astype(F32)).astype(BF16)
            imp_ref[g] = _dot(hi, overlap) + _dot(mid, overlap) + _dot(lo, overlap)

    lane_tiles = ((qi + 1) * tq // CMP_STRIDE + LANE - 1) // LANE
    for v in range(1, ncp // LANE + 1):
        @pl.when(jnp.minimum(lane_tiles, ncp // LANE) == v)
        def _(v=v):
            attend(v * LANE)

    for g in range(NSA_GROUPS):
        forced = (blk == 0) | (blk == chunk)
        imp = jnp.where(forced, REMOVED, jnp.where(blk <= chunk, imp_ref[g], NEG))
        sel = jnp.where(forced, 1.0, _topk_mask_t(imp.T, topk - 2).T)
        selm1_ref[0, g] = (jnp.where(blk <= chunk, sel, 0.0) - 1.0).astype(BF16)


def nsa_cmp_topk(qn, kc, vc, gates, tq=128):
    _, b, s, _ = qn.shape
    tq = min(tq, s)
    ncp = kc.shape[2]
    assert s // SEL_BLOCK <= LANE
    topk = min(SEL_TOPK, s // SEL_BLOCK)
    kspec = pl.BlockSpec((1, NSA_GROUPS, ncp, LANE), lambda i, j: (i, 0, 0, 0))
    return pl.pallas_call(
        functools.partial(_cmp_topk_kernel, tq=tq, ncp=ncp, topk=topk),
        grid=(b, s // tq),
        in_specs=[pl.BlockSpec((NSA_HEADS, None, tq, LANE), lambda i, j: (0, i, j, 0)), kspec, kspec,
                  pl.BlockSpec((1, tq, LANE), lambda i, j: (i, j, 0))],
        out_specs=[pl.BlockSpec((1, tq, OUT_W), lambda i, j: (i, j, 0)),
                   pl.BlockSpec((1, NSA_GROUPS, tq, LANE), lambda i, j: (i, 0, j, 0))],
        out_shape=[jax.ShapeDtypeStruct((b, s, OUT_W), BF16), jax.ShapeDtypeStruct((b, NSA_GROUPS, s, LANE), BF16)],
        scratch_shapes=[pltpu.VMEM((NSA_GROUPS, tq, LANE), F32)],
        compiler_params=_params(("parallel", "parallel")),
        name="nsa_cmp_topk",
    )(qn, kc, vc, gates)


def _nsa_write(o_ref, gate_ref, acc_ref, tq, branch):
    gates = gate_ref[0]
    heads = []
    for h in range(NSA_HEADS):
        g, n = divmod(h, NSA_HPG)
        col = branch * NSA_HEADS + h
        heads.append(_normalized(acc_ref[g, n * tq:(n + 1) * tq]) * gates[:, col:col + 1])
    _store_head_pairs(o_ref, slice(None), heads)


def _sel_attn_kernel(flags, q_ref, k_ref, v_ref, selm1_ref, gate_ref, o_ref, lhs_ref, m_ref, acc_ref, *, t, nq):
    b = pl.program_id(0)
    qi = pl.program_id(1)
    _init_state(m_ref, acc_ref)
    for g in range(NSA_GROUPS):
        lhs_ref[g, :, :LANE] = q_ref[g * NSA_HPG:(g + 1) * NSA_HPG].reshape(NSA_HPG * t, LANE)
        lhs_ref[g, :, LANE:] = jnp.concatenate([selm1_ref[0, g]] * NSA_HPG, axis=0)

    half = NSA_HPG * t // 2
    chains = [(g, i) for g in range(NSA_GROUPS) for i in range(2)]

    def update(ki, diagonal):
        rows = pl.ds(pl.multiple_of(ki * t, t), t)
        scores = [_dot_nt(lhs_ref[g, i * half:(i + 1) * half], k_ref[rows, 2 * g * LANE:(2 * g + 2) * LANE])
                  for g, i in chains]
        if diagonal:
            causal = (_iota((t, t), 1) <= _iota((t, t), 0))[None]
            scores = [jnp.where(causal, s.reshape(NSA_HPG // 2, t, t), NEG).reshape(half, t) for s in scores]
        _flash_update_staged(scores, [v_ref[rows, g * LANE:(g + 1) * LANE] for g, _ in chains], m_ref, acc_ref,
                             [(g, slice(i * half, (i + 1) * half)) for g, i in chains])

    base = (b * nq + qi) * nq

    def body(ki, carry):
        @pl.when(flags[base + ki] > 0)
        def _():
            update(ki, False)
        return carry

    lax.fori_loop(0, qi, body, 0)
    update(qi, True)
    _nsa_write(o_ref, gate_ref, acc_ref, t, branch=1)


def nsa_selected(qn, ks, vs, selm1, gates, t=256):
    _, b, s, _ = qn.shape
    t = min(t, s)
    nq = s // t
    bpt = t // SEL_BLOCK
    flags = (selm1.reshape(b, NSA_GROUPS, nq, t, LANE // bpt, bpt) > -0.5).any(axis=(1, 3, 5))[..., :nq]
    flags = flags.astype(jnp.int32).reshape(-1)
    imap_q = lambda bb, i, fl: (bb, i, 0)
    rows = NSA_HPG * t
    return pl.pallas_call(
        functools.partial(_sel_attn_kernel, t=t, nq=nq),
        grid_spec=pltpu.PrefetchScalarGridSpec(
            num_scalar_prefetch=1,
            grid=(b, nq),
            in_specs=[pl.BlockSpec((NSA_HEADS, None, t, LANE), lambda bb, i, fl: (0, bb, i, 0)),
                      pl.BlockSpec((None, s, 4 * LANE), lambda bb, i, fl: (bb, 0, 0)),
                      pl.BlockSpec((None, s, 2 * LANE), lambda bb, i, fl: (bb, 0, 0)),
                      pl.BlockSpec((1, NSA_GROUPS, t, LANE), lambda bb, i, fl: (bb, 0, i, 0)),
                      pl.BlockSpec((1, t, LANE), imap_q)],
            out_specs=pl.BlockSpec((1, t, OUT_W), imap_q),
            scratch_shapes=[pltpu.VMEM((NSA_GROUPS, rows, 2 * LANE), BF16), pltpu.VMEM((NSA_GROUPS, rows, LANE), F32),
                            pltpu.VMEM((NSA_GROUPS, rows, LANE), F32)],
        ),
        out_shape=jax.ShapeDtypeStruct((b, s, OUT_W), BF16),
        compiler_params=_params(("parallel", "arbitrary")),
        name="nsa_selected",
    )(flags, qn, ks, vs, selm1, gates)


def _win_attn_kernel(q_ref, kp_ref, kc_ref, vp_ref, vc_ref, gate_ref, o_ref, *, tq):
    qi = pl.program_id(1)
    hq = tq // 2
    span = WIN + hq
    row = _iota((hq, span), 0)
    col = _iota((hq, span), 1)
    diff = WIN + row - col
    band = (diff >= 0) & (diff < WIN)
    gates = gate_ref[0]
    chains = [(g, rh) for g in range(NSA_GROUPS) for rh in range(2)]
    scores = []
    for g, rh in chains:
        gl = slice(g * LANE, (g + 1) * LANE)
        q = q_ref[g * NSA_HPG:(g + 1) * NSA_HPG, rh * hq:(rh + 1) * hq].reshape(NSA_HPG * hq, LANE)
        s = jnp.concatenate([_dot_nt(q, kp_ref[0, rh * hq:, gl]), _dot_nt(q, kc_ref[0, :(rh + 1) * hq, gl])], axis=1)
        valid = (band & (col >= WIN - qi * tq - rh * hq))[None]
        scores.append(jnp.where(valid, s.reshape(NSA_HPG, hq, span), NEG).reshape(NSA_HPG * hq, span))
    tops = [jnp.max(s, axis=-1, keepdims=True) for s in scores]
    probs = [jnp.exp2(s - m).astype(BF16) for s, m in zip(scores, tops)]
    for (g, rh), p in zip(chains, probs):
        gl = slice(g * LANE, (g + 1) * LANE)
        n_prev = tq - rh * hq
        acc = _dot(p[:, :n_prev], vp_ref[0, rh * hq:, gl]) + _dot(p[:, n_prev:], vc_ref[0, :(rh + 1) * hq, gl])
        rows = slice(rh * hq, (rh + 1) * hq)
        cols = [2 * NSA_HEADS + g * NSA_HPG + n for n in range(NSA_HPG)]
        heads = [_normalized(acc[n * hq:(n + 1) * hq]) * gates[rows, cols[n]:cols[n] + 1] for n in range(NSA_HPG)]
        _store_head_pairs(o_ref, rows, heads, first_pair=g * NSA_HPG // 2)


def nsa_window(qn, kw, vw, gates, tq=512):
    _, b, s, _ = qn.shape
    assert tq == WIN and s % tq == 0
    prev = pl.BlockSpec((1, tq, 2 * LANE), lambda bb, i: (bb, jnp.maximum(i - 1, 0), 0))
    cur = pl.BlockSpec((1, tq, 2 * LANE), lambda bb, i: (bb, i, 0))
    return pl.pallas_call(
        functools.partial(_win_attn_kernel, tq=tq),
        grid=(b, s // tq),
        in_specs=[pl.BlockSpec((NSA_HEADS, None, tq, LANE), lambda bb, i: (0, bb, i, 0)), prev, cur, prev, cur,
                  pl.BlockSpec((1, tq, LANE), lambda bb, i: (bb, i, 0))],
        out_specs=pl.BlockSpec((1, tq, OUT_W), lambda bb, i: (bb, i, 0)),
        out_shape=jax.ShapeDtypeStruct((b, s, OUT_W), BF16),
        compiler_params=_params(("parallel", "parallel")),
        name="nsa_window",
    )(qn, kw, kw, vw, vw, gates)


def _even_out_ln_kernel(x_ref, om_ref, oc_ref, os_ref, ow_ref, wm_ref, wn_ref, g_ref, b_ref, o_ref):
    nsa = (oc_ref[...].astype(F32) + os_ref[...].astype(F32) + ow_ref[...].astype(F32)).astype(BF16)
    mix = _dot(om_ref[...], wm_ref[...]) + _dot(nsa, wn_ref[...])
    o_ref[...] = _layer_norm(ALPHA * x_ref[...] + mix, g_ref[...], b_ref[...])


def even_out_ln(x2, o_mla, o_c, o_s, o_w, w_mla, w_nsa, g, b, tm=512):
    m, d = x2.shape
    tm = min(tm, m)
    row = lambda w: pl.BlockSpec((tm, w), lambda i: (i, 0))
    return pl.pallas_call(
        _even_out_ln_kernel,
        grid=(m // tm,),
        in_specs=[row(d)] + [row(OUT_W)] * 4 + [_const_spec(w_mla.shape), _const_spec(w_nsa.shape),
                                             _const_spec((1, d)), _const_spec((1, d))],
        out_specs=row(d),
        out_shape=jax.ShapeDtypeStruct((m, d), F32),
        compiler_params=_params(("parallel",)),
        name="even_out_ln",
    )(x2, o_mla, o_c, o_s, o_w, w_mla, w_nsa, g.reshape(1, d), b.reshape(1, d))


def _odd_out_ln_kernel(x_ref, o0_ref, o1_ref, o2_ref, l0_ref, l1_ref, l2_ref, w_ref, g_ref, b_ref, o_ref,
                       o_scr, l_scr):
    tm = x_ref.shape[0]
    for gi, (src, lsrc) in enumerate(((o0_ref, l0_ref), (o1_ref, l1_ref), (o2_ref, l2_ref))):
        dil = DIL_PATTERNS[gi][1]
        for r in range(dil):
            rows = pl.ds(r, tm // dil, stride=dil) if dil > 1 else slice(None)
            for h in range(DIL_GH):
                o_scr[gi, h, rows, :] = src[r, :, h * DIL_DH:(h + 1) * DIL_DH].astype(F32)
            l_scr[gi, rows, :] = lsrc[r]
    lses = [l_scr[gi] for gi in range(DIL_GROUPS)]
    top = jnp.maximum(jnp.maximum(lses[0], lses[1]), lses[2])
    es = [jnp.exp(l - top) for l in lses]
    den = es[0] + es[1] + es[2]
    wts = [e / den for e in es]
    cols = []
    for h in range(DIL_GH):
        merged = None
        for gi in range(DIL_GROUPS):
            term = wts[gi][:, h:h + 1] * o_scr[gi, h]
            merged = term if merged is None else merged + term
        cols.append(merged.astype(BF16))
    mix = _dot(jnp.concatenate(cols, axis=1), w_ref[...])
    o_ref[...] = _layer_norm(ALPHA * x_ref[...] + mix, g_ref[...], b_ref[...])


def odd_out_ln(x2, outs, lses, w, g, b, tm=512):
    m, d = x2.shape
    b_, _, s = outs[0].shape[0], None, outs[0].shape[1] * outs[0].shape[2]
    tm = min(tm, s)
    width = DIL_GH * DIL_DH
    row = pl.BlockSpec((None, tm, d), lambda bb, i: (bb, i, 0))
    cls = lambda gi, wd: pl.BlockSpec((None, DIL_PATTERNS[gi][1], tm // DIL_PATTERNS[gi][1], wd),
                                      lambda bb, i: (bb, 0, i, 0))
    out = pl.pallas_call(
        _odd_out_ln_kernel,
        grid=(b_, s // tm),
        in_specs=[row] + [cls(gi, width) for gi in range(DIL_GROUPS)] + [cls(gi, LANE) for gi in range(DIL_GROUPS)]
        + [_const_spec(w.shape), _const_spec((1, d)), _const_spec((1, d))],
        out_specs=row,
        out_shape=jax.ShapeDtypeStruct((b_, s, d), F32),
        scratch_shapes=[pltpu.VMEM((DIL_GROUPS, DIL_GH, tm, DIL_DH), F32), pltpu.VMEM((DIL_GROUPS, tm, LANE), F32)],
        compiler_params=_params(("parallel", "parallel")),
        name="odd_out_ln",
    )(x2.reshape(b_, s, d), *outs, *lses, w, g.reshape(1, d), b.reshape(1, d))
    return out.reshape(m, d)


def _odd_proj_kernel(x_ref, w_ref, o0_ref, o1_ref, o2_ref, xs_ref):
    tm, d_model = x_ref.shape
    width = DIL_GH * DIL_DH
    n_chunks = d_model // LANE
    for c in range(n_chunks):
        xs_ref[c] = x_ref[:, c * LANE:(c + 1) * LANE]
    for gi, o_ref in enumerate((o0_ref, o1_ref, o2_ref)):
        dil = DIL_PATTERNS[gi][1]
        if dil == 1:
            xg = x_ref[...]
        else:
            xg = jnp.concatenate(
                [jnp.concatenate([xs_ref[c, pl.ds(r, tm // dil, stride=dil), :] for r in range(dil)], axis=0)
                 for c in range(n_chunks)], axis=1)
        xg = xg.astype(BF16)
        for c in range(3):
            col = (gi * 3 + c) * width
            y = _dot(xg, w_ref[:, col:col + width])
            if c == 0:
                y = y * (DIL_SCALE * LOG2E)
            o_ref[:, :, c * width:(c + 1) * width] = y.astype(BF16).reshape(dil, tm // dil, width)


def odd_proj(x2, b, s, w_in, tm=512):
    m, d = x2.shape
    tm = min(tm, s)
    width = DIL_GH * DIL_DH
    w = w_in.reshape(d, 3, DIL_GROUPS, width).transpose(0, 2, 1, 3).reshape(d, 3 * DIL_GROUPS * width).astype(BF16)
    dils = [dil for _, dil in DIL_PATTERNS]
    return pl.pallas_call(
        _odd_proj_kernel,
        grid=(b, s // tm),
        in_specs=[pl.BlockSpec((None, tm, d), lambda bb, i: (bb, i, 0)), _const_spec(w.shape)],
        out_specs=[pl.BlockSpec((None, dil, tm // dil, 3 * width), lambda bb, i: (bb, 0, i, 0)) for dil in dils],
        out_shape=[jax.ShapeDtypeStruct((b, dil, s // dil, 3 * width), BF16) for dil in dils],
        scratch_shapes=[pltpu.VMEM((d // LANE, tm, LANE), F32)],
        compiler_params=_params(("parallel", "parallel")),
        name="odd_proj",
    )(x2.reshape(b, s, d), w)


def _dil_attn_kernel(slope_tab, pos0_tab, q_ref, kc_ref, kp_ref, vc_ref, vp_ref, pc_ref, pp_ref, pq_ref, o_ref,
                     lse_ref, *, group, tq):
    sub = DIL_SPAN
    b = pl.program_id(0)
    jt = pl.program_id(2)
    row = _iota((sub, 2 * sub), 0)
    col = _iota((sub, 2 * sub), 1)
    band = (col >= row) & (col <= row + sub)
    band_bias = jnp.where(band, 0.0, NEG)
    start_bias = jnp.where(band & (col >= sub), 0.0, NEG)
    pos0 = pos0_tab[b]
    lane = _iota((sub, LANE), 1)
    for i in range(tq // sub):
        cur = slice(i * sub, (i + 1) * sub)
        if i == 0:
            mask_bias = jnp.where(jt == 0, start_bias, band_bias)
            pk = jnp.concatenate([pp_ref[...], pc_ref[:, :sub]], axis=1)
        else:
            mask_bias = band_bias
            pk = pc_ref[:, (i - 1) * sub:(i + 1) * sub]
        dpos = pk - pos0
        dq = (pq_ref[cur, :] - pos0) * LN2
        heads = range(DIL_GH)
        hs = [slice(h * DIL_DH, (h + 1) * DIL_DH) for h in heads]
        slopes = [slope_tab[group * DIL_GH + h] for h in heads]
        if i == 0:
            ks = [jnp.concatenate([kp_ref[:, hs[h]], kc_ref[:sub, hs[h]]], axis=0) for h in heads]
            vs = [jnp.concatenate([vp_ref[:, hs[h]], vc_ref[:sub, hs[h]]], axis=0) for h in heads]
        else:
            ks = [kc_ref[(i - 1) * sub:(i + 1) * sub, hs[h]] for h in heads]
            vs = [vc_ref[(i - 1) * sub:(i + 1) * sub, hs[h]] for h in heads]
        ss = [_dot_nt(q_ref[cur, hs[h]], ks[h]) + (mask_bias + slopes[h] * dpos) for h in heads]
        ms = [jnp.max(s, axis=-1, keepdims=True) for s in ss]
        ps = [jnp.exp2(s - m) for s, m in zip(ss, ms)]
        dens = [jnp.sum(p, axis=-1, keepdims=True) for p in ps]
        lse_tile = jnp.zeros((sub, LANE), F32)
        for h in heads:
            o_ref[cur, hs[h]] = (_dot(ps[h].astype(BF16), vs[h]) / dens[h]).astype(BF16)
            lse_tile = jnp.where(lane == h, ms[h] * LN2 + jnp.log(dens[h]) - slopes[h] * dq, lse_tile)
        lse_ref[cur, :] = lse_tile


def dilated_group_attention(qkv, pos_f, group, tq=512):
    b, d, sd, _ = qkv.shape
    w, dil = DIL_PATTERNS[group]
    assert w // dil == DIL_SPAN and dil == d
    tq = min(tq, sd)
    sub = DIL_SPAN
    width = DIL_GH * DIL_DH
    n_slopes = DIL_GROUPS * DIL_GH
    slopes = 2.0 ** (-ALIBI_MAX_BIAS * jnp.arange(1, n_slopes + 1, dtype=F32) / n_slopes) * LOG2E
    pos0 = pos_f[:, 0]
    posc = pos_f.reshape(b, sd, d).transpose(0, 2, 1).reshape(b, d, 1, sd)
    r = tq // sub
    cur = lambda c: pl.BlockSpec((None, None, tq, width), lambda bb, rr, j, *_: (bb, rr, j, c))
    prev = lambda c: pl.BlockSpec((None, None, sub, width),
                                  lambda bb, rr, j, *_: (bb, rr, jnp.maximum(j * r - 1, 0), c))
    return pl.pallas_call(
        functools.partial(_dil_attn_kernel, group=group, tq=tq),
        grid_spec=pltpu.PrefetchScalarGridSpec(
            num_scalar_prefetch=2,
            grid=(b, d, sd // tq),
            in_specs=[cur(0), cur(1), prev(1), cur(2), prev(2),
                      pl.BlockSpec((None, None, 1, tq), lambda bb, rr, j, *_: (bb, rr, 0, j)),
                      pl.BlockSpec((None, None, 1, sub), lambda bb, rr, j, *_: (bb, rr, 0, jnp.maximum(j * r - 1, 0))),
                      pl.BlockSpec((None, None, tq, 1), lambda bb, rr, j, *_: (bb, rr, j, 0))],
            out_specs=[pl.BlockSpec((None, None, tq, width), lambda bb, rr, j, *_: (bb, rr, j, 0)),
                       pl.BlockSpec((None, None, tq, LANE), lambda bb, rr, j, *_: (bb, rr, j, 0))],
        ),
        out_shape=[jax.ShapeDtypeStruct((b, d, sd, width), BF16), jax.ShapeDtypeStruct((b, d, sd, LANE), F32)],
        compiler_params=_params(("parallel", "parallel", "parallel")),
        name=f"dilated_attention_g{group}",
    )(slopes, pos0, qkv, qkv, qkv, qkv, qkv, posc, posc, posc.reshape(b, d, sd, 1))


def even_mixer_ln(x2, b, s, pos_f, w_in, q_norm_g, kv_norm_g, w_uq, w_uk, w_uv,
                  cmp_pos, cmp_k_w1, cmp_k_w2, cmp_v_w1, cmp_v_w2, w_out, ln_g, ln_b):
    m = b * s
    d = x2.shape[1]
    half = MLA_ROPE // 2
    inv = ROPE_THETA ** (-jnp.arange(half, dtype=F32) / half)
    ang = (pos_f[..., None] * inv).reshape(m, half)
    ones = jnp.ones((m, MLA_NOPE), F32)
    zeros = jnp.zeros((m, LANE - MLA_NOPE - MLA_ROPE), F32)
    cos_t = jnp.concatenate([ones, jnp.cos(ang), jnp.cos(ang), zeros], axis=1)
    sin_t = jnp.concatenate([0.0 * ones, jnp.sin(ang), jnp.sin(ang), zeros], axis=1)
    dpos = (pos_f - pos_f[:, :1]) * LOG2E
    ncp = s // CMP_STRIDE
    dpos_cend = jnp.pad(dpos[:, CMP_LEN - 1::CMP_STRIDE], ((0, 0), (0, 1))).reshape(b, ncp, 1)

    qm, km, vm, qn, kvc, ks, kw, vs, vw, gates = even_proj(
        x2, cos_t, sin_t, dpos.reshape(m, 1), s, w_in, q_norm_g, kv_norm_g, w_uq, w_uk, w_uv)
    o_mla = mla_attention(qm.reshape(b, s, HW), km.reshape(b, s, HW), vm.reshape(b, s, HW))

    kvc = kvc.reshape(2, b, ncp, CMP_STRIDE * LANE)
    kc = nsa_compress(kvc[0], cmp_pos, cmp_k_w1, cmp_k_w2, dpos_cend, is_value=False)
    vc = nsa_compress(kvc[1], cmp_pos, cmp_v_w1, cmp_v_w2, dpos_cend, is_value=True)
    qn = qn.reshape(NSA_HEADS, b, s, LANE)
    gates = gates.reshape(b, s, LANE)
    o_c, selm1 = nsa_cmp_topk(qn, kc, vc, gates)
    o_s = nsa_selected(qn, ks.reshape(b, s, 4 * LANE), vs.reshape(b, s, 2 * LANE), selm1, gates)
    o_w = nsa_window(qn, kw.reshape(b, s, 2 * LANE), vw.reshape(b, s, 2 * LANE), gates)

    w_mla = w_out[:OUT_W].astype(BF16)
    w_nsa = w_out[OUT_W:].astype(BF16)
    return even_out_ln(x2, o_mla.reshape(m, OUT_W), o_c.reshape(m, OUT_W), o_s.reshape(m, OUT_W),
                       o_w.reshape(m, OUT_W), w_mla, w_nsa, ln_g, ln_b)


def odd_mixer_ln(x2, b, s, pos_f, w_in, w_out, ln_g, ln_b):
    qkvs = odd_proj(x2, b, s, w_in)
    parts = [dilated_group_attention(qkvs[g], pos_f, g) for g in range(DIL_GROUPS)]
    return odd_out_ln(x2, [p[0] for p in parts], [p[1] for p in parts], w_out.astype(BF16), ln_g, ln_b)


def kernel(x, positions, ln1_g, ln1_b, ffn1_w_gate, ffn1_w_up, ffn1_w_down, mix_in_even, mla_q_norm, mla_kv_norm, mla_w_uq, mla_w_uk, mla_w_uv, nsa_cmp_pos, nsa_cmp_k_w1, nsa_cmp_k_w2, nsa_cmp_v_w1, nsa_cmp_v_w2, mix_out_even, mix_in_odd, mix_out_odd, ln2_g, ln2_b, ffn2_w_gate, ffn2_w_up, ffn2_w_down, ln3_g, ln3_b):
    b, s, d = x.shape
    x2 = x.reshape(b * s, d)
    pos_f = positions.astype(F32)
    for i in range(DEPTH):
        j = i // 2
        x2 = ffn_ln(x2, ffn1_w_gate[i].astype(BF16), ffn1_w_up[i].astype(BF16), ffn1_w_down[i].astype(BF16),
                    ln1_g[i], ln1_b[i])
        if i % 2 == 0:
            x2 = even_mixer_ln(x2, b, s, pos_f, mix_in_even[j], mla_q_norm[j], mla_kv_norm[j], mla_w_uq[j],
                               mla_w_uk[j], mla_w_uv[j], nsa_cmp_pos[j], nsa_cmp_k_w1[j], nsa_cmp_k_w2[j],
                               nsa_cmp_v_w1[j], nsa_cmp_v_w2[j], mix_out_even[j], ln2_g[i], ln2_b[i])
        else:
            x2 = odd_mixer_ln(x2, b, s, pos_f, mix_in_odd[j], mix_out_odd[j], ln2_g[i], ln2_b[i])
        x2 = ffn_ln(x2, ffn2_w_gate[i].astype(BF16), ffn2_w_up[i].astype(BF16), ffn2_w_down[i].astype(BF16),
                    ln3_g[i], ln3_b[i])
    return x2.reshape(b, s, d)
```

```python
import functools
import math

import numpy as np
import jax
import jax.numpy as jnp
from jax import lax
from jax.experimental import pallas as pl
from jax.experimental.pallas import tpu as pltpu

F32 = jnp.float32
BF16 = jnp.bfloat16

DEPTH = 2
LN_EPS = 1e-5
RMS_EPS = 1e-6
ALPHA = (2 * DEPTH) ** 0.25
HALF_STEP = 0.5
NEG = -1e30
BIG = 1e9
MASK_BIG = 1e30
REMOVED = -3.0e38
ALIBI_MAX_BIAS = 8.0
LOG2E = math.log2(math.e)
LN2 = math.log(2.0)
LANE = 128

MLA_HEADS = 8
MLA_Q_RANK = 384
MLA_KV_RANK = 256
MLA_NOPE = 64
MLA_ROPE = 32
MLA_V = 64
ROPE_THETA = 10000.0
MLA_SCALE = (MLA_NOPE + MLA_ROPE) ** -0.5
MLA_CHAINS = 4

NSA_HEADS = 8
NSA_GROUPS = 2
NSA_HPG = 4
NSA_DH = 64
CMP_LEN = 32
CMP_STRIDE = 16
CMP_HIDDEN = 256
SEL_BLOCK = 64
SEL_TOPK = 16
WIN = 512
NSA_SCALE = NSA_DH ** -0.5
NSA_SLOPES = tuple(2.0 ** (-ALIBI_MAX_BIAS * (i + 1) / NSA_HEADS) for i in range(NSA_HEADS))
ONES_LANE = 64
BIAS_LANES = (64, 65, 66)

DIL_PATTERNS = ((128, 1), (512, 4), (2048, 16))
DIL_GROUPS = 3
DIL_GH = 4
DIL_DH = 128
DIL_SCALE = DIL_DH ** -0.5
DIL_SPAN = 128

VMEM_LIMIT = 48 * 1024 * 1024


def _iota(shape, dim):
    return lax.broadcasted_iota(jnp.int32, shape, dim)


def _shr(x, pow2):
    return jnp.right_shift(x, int(pow2).bit_length() - 1)


def _dot(a, b):
    return jnp.dot(a, b, preferred_element_type=F32)


def _dot_nt(a, b):
    return lax.dot_general(a, b, (((1,), (1,)), ((), ())), preferred_element_type=F32)


def _const_spec(shape):
    zeros = (0,) * len(shape)
    return pl.BlockSpec(shape, lambda *_: zeros, pipeline_mode=pl.Buffered(1))


def _params(sem):
    return pltpu.CompilerParams(dimension_semantics=sem, vmem_limit_bytes=VMEM_LIMIT)


def _layer_norm(z, g, b):
    mu = jnp.mean(z, axis=-1, keepdims=True)
    zc = z - mu
    var = jnp.mean(zc * zc, axis=-1, keepdims=True)
    return zc * lax.rsqrt(var + LN_EPS) * g + b


def _rms_norm(z, g):
    return z * lax.rsqrt(jnp.mean(z * z, axis=-1, keepdims=True) + RMS_EPS) * g


def _bias_pieces(d, lane):
    x = jnp.broadcast_to(d, lane.shape)
    hi = x.astype(BF16).astype(F32)
    r = x - hi
    mid = r.astype(BF16).astype(F32)
    lo = (r - mid).astype(BF16).astype(F32)
    return jnp.where(lane == BIAS_LANES[0], hi,
                     jnp.where(lane == BIAS_LANES[1], mid, jnp.where(lane == BIAS_LANES[2], lo, 0.0)))


def _flash_update(s, v, m_ref, acc_ref, idx):
    m_old = m_ref[idx]
    m_new = jnp.maximum(m_old, jnp.max(s, axis=-1, keepdims=True))
    p = jnp.exp2(s - jnp.tile(m_new, (1, s.shape[1] // LANE))).astype(BF16)
    acc_ref[idx] = jnp.exp2(m_old - m_new) * acc_ref[idx] + _dot(p, v)
    m_ref[idx] = m_new


def _flash_update_staged(scores, values, m_ref, acc_ref, idxs):
    m_old = [m_ref[i] for i in idxs]
    m_new = [jnp.maximum(mo, jnp.max(s, axis=-1, keepdims=True)) for mo, s in zip(m_old, scores)]
    ps = [jnp.exp2(s - jnp.tile(mn, (1, s.shape[1] // LANE))).astype(BF16) for s, mn in zip(scores, m_new)]
    for i, mo, mn, p, v in zip(idxs, m_old, m_new, ps, values):
        acc_ref[i] = jnp.exp2(mo - mn) * acc_ref[i] + _dot(p, v)
        m_ref[i] = mn


def _init_state(m_ref, acc_ref):
    m_ref[...] = jnp.full(m_ref.shape, NEG, F32)
    acc_ref[...] = jnp.zeros(acc_ref.shape, F32)


def _normalized(acc):
    lane = _iota(acc.shape, 1)
    o = acc / jnp.maximum(acc[:, ONES_LANE:ONES_LANE + 1], 1e-30)
    return jnp.where(lane < ONES_LANE, o, 0.0)


def _store_head_pairs(o_ref, rows, heads, first_pair=0):
    for pr in range(len(heads) // 2):
        packed = heads[2 * pr] + pltpu.roll(heads[2 * pr + 1], LANE // 2, 1)
        o_ref[0, rows, (first_pair + pr) * LANE:(first_pair + pr + 1) * LANE] = packed.astype(BF16)


def _ffn_ln_kernel(x_ref, wg_ref, wu_ref, wd_ref, g_ref, b_ref, o_ref, *, n_chunks):
    x = x_ref[...]
    xb = x.astype(BF16)
    c = wg_ref.shape[1] // n_chunks
    y = None
    for i in range(n_chunks):
        gt = _dot(xb, wg_ref[:, i * c:(i + 1) * c])
        up = _dot(xb, wu_ref[:, i * c:(i + 1) * c])
        h = (gt * jax.nn.sigmoid(gt) * up).astype(BF16)
        part = _dot(h, wd_ref[i * c:(i + 1) * c, :])
        y = part if y is None else y + part
    o_ref[...] = _layer_norm(ALPHA * x + HALF_STEP * y, g_ref[...], b_ref[...])


def ffn_ln(x2, wg, wu, wd, g, b, tm=1024, n_chunks=11):
    m, d = x2.shape
    tm = min(tm, m)
    row = pl.BlockSpec((tm, d), lambda i: (i, 0))
    return pl.pallas_call(
        functools.partial(_ffn_ln_kernel, n_chunks=n_chunks),
        grid=(m // tm,),
        in_specs=[row, _const_spec(wg.shape), _const_spec(wu.shape), _const_spec(wd.shape),
                  _const_spec((1, d)), _const_spec((1, d))],
        out_specs=row,
        out_shape=jax.ShapeDtypeStruct((m, d), F32),
        compiler_params=_params(("parallel",)),
        name="ffn_ln",
    )(x2, wg, wu, wd, g.reshape(1, d), b.reshape(1, d))


HW = MLA_HEADS * LANE
OUT_W = MLA_HEADS * MLA_V
EVEN_X_COLS = (MLA_Q_RANK, MLA_KV_RANK, LANE, LANE, HW, 2 * LANE, 8 * LANE, LANE)
EVEN_X_OFFS = tuple(int(v) for v in np.cumsum((0,) + EVEN_X_COLS))


def _even_proj_kernel(x_ref, cos_ref, sin_ref, dpos_ref, wx_ref, qg_ref, kvg_ref, wuq_ref, wuqs_ref, wuk_ref,
                      wuv_ref, slope_ref, qm_ref, km_ref, vm_ref, qn_ref, kvc_ref, ks_ref, kw_ref, vs_ref, vw_ref,
                      gate_ref, kvc_scr, *, tiles_per_seq):
    tm = x_ref.shape[0]
    xb = x_ref[...].astype(BF16)
    cos = cos_ref[...]
    sin = sin_ref[...]
    lane = _iota((tm, LANE), 1)
    ones_lane = jnp.where(lane == ONES_LANE, 1.0, 0.0)
    pos_term = _bias_pieces(dpos_ref[...], lane)
    tok = (pl.program_id(0) % tiles_per_seq) * tm + _iota((tm, LANE), 0)
    block_onehot = jnp.where(lane == _shr(tok, SEL_BLOCK), MASK_BIG, 0.0).astype(BF16)

    def xdot(i):
        return _dot(xb, wx_ref[:, EVEN_X_OFFS[i]:EVEN_X_OFFS[i + 1]])

    cq = _rms_norm(xdot(0), qg_ref[...]).astype(BF16)
    ckv = _rms_norm(xdot(1), kvg_ref[...]).astype(BF16)
    k_rot = xdot(2) * cos + xdot(3) * sin
    for h in range(MLA_HEADS):
        sl = slice(h * LANE, (h + 1) * LANE)
        q = _dot(cq, wuq_ref[:, sl]) * cos + _dot(cq, wuqs_ref[:, sl]) * sin
        qm_ref[:, sl] = (q * (MLA_SCALE * LOG2E)).astype(BF16)
        km_ref[:, sl] = (_dot(ckv, wuk_ref[:, sl]) + k_rot).astype(BF16)
        vm_ref[:, sl] = (_dot(ckv, wuv_ref[:, sl]) + ones_lane).astype(BF16)
    qn = xdot(4) * (NSA_SCALE * LOG2E) + slope_ref[...]
    for h in range(NSA_HEADS):
        qn_ref[h] = qn[:, h * LANE:(h + 1) * LANE].astype(BF16)
    kvc = xdot(5)
    for j in range(2):
        kvc_scr[j] = kvc[:, j * LANE:(j + 1) * LANE]
        for l in range(CMP_STRIDE):
            kvc_ref[j, :, l * LANE:(l + 1) * LANE] = kvc_scr[j, pl.ds(l, tm // CMP_STRIDE, stride=CMP_STRIDE), :].astype(BF16)
    kv8 = xdot(6)
    blk = lambda i: kv8[:, i * LANE:(i + 1) * LANE]
    for g in range(NSA_GROUPS):
        ks_ref[:, 2 * g * LANE:(2 * g + 1) * LANE] = (blk(g) + pos_term).astype(BF16)
        ks_ref[:, (2 * g + 1) * LANE:(2 * g + 2) * LANE] = block_onehot
        kw_ref[:, g * LANE:(g + 1) * LANE] = (blk(2 + g) + pos_term).astype(BF16)
        vs_ref[:, g * LANE:(g + 1) * LANE] = (blk(4 + g) + ones_lane).astype(BF16)
        vw_ref[:, g * LANE:(g + 1) * LANE] = (blk(6 + g) + ones_lane).astype(BF16)
    gate_ref[...] = jax.nn.sigmoid(xdot(7))


def _head_blocks(w, n_heads, width):
    k = w.shape[0]
    w = jnp.pad(w.reshape(k, n_heads, width), ((0, 0), (0, 0), (0, LANE - width)))
    return w.reshape(k, n_heads * LANE)


def even_proj(x2, cos_t, sin_t, dpos, seq, w_in, q_norm_g, kv_norm_g, w_uq, w_uk, w_uv, tm=256):
    m, d = x2.shape
    tm = min(tm, seq)
    half = MLA_ROPE // 2
    gw = NSA_GROUPS * NSA_DH
    cuts = np.cumsum((MLA_Q_RANK, MLA_KV_RANK, MLA_ROPE, NSA_HEADS * NSA_DH) + (gw,) * 6)
    cuts = [0] + [int(c) for c in cuts]
    w_cq, w_ckv, w_kpe, w_q = (w_in[:, cuts[i]:cuts[i + 1]] for i in range(4))
    w_kc, w_vc, w_ks, w_vs, w_kw, w_vw = (w_in[:, cuts[4 + i]:cuts[5 + i]] for i in range(6))
    w_gate = w_in[:, cuts[10]:]
    w_kpe_sw = jnp.concatenate([-w_kpe[:, half:], w_kpe[:, :half]], axis=1)
    rope_pad = ((0, 0), (MLA_NOPE, LANE - MLA_NOPE - MLA_ROPE))
    w_gate_blk = jnp.pad(w_gate, ((0, 0), (0, LANE - w_gate.shape[1])))
    wx = jnp.concatenate(
        [w_cq, w_ckv, jnp.pad(w_kpe, rope_pad), jnp.pad(w_kpe_sw, rope_pad), _head_blocks(w_q, NSA_HEADS, NSA_DH),
         w_kc, w_vc] + [_head_blocks(w, NSA_GROUPS, NSA_DH) for w in (w_ks, w_kw, w_vs, w_vw)] + [w_gate_blk],
        axis=1).astype(BF16)

    qd = MLA_NOPE + MLA_ROPE
    uq = w_uq.reshape(MLA_Q_RANK, MLA_HEADS, qd)
    uq_sw = jnp.concatenate([jnp.zeros_like(uq[..., :MLA_NOPE]), -uq[..., MLA_NOPE + half:],
                             uq[..., MLA_NOPE:MLA_NOPE + half]], axis=-1)
    wuq = _head_blocks(uq.reshape(MLA_Q_RANK, -1), MLA_HEADS, qd).astype(BF16)
    wuqs = _head_blocks(uq_sw.reshape(MLA_Q_RANK, -1), MLA_HEADS, qd).astype(BF16)
    wuk = _head_blocks(w_uk, MLA_HEADS, MLA_NOPE).astype(BF16)
    wuv = _head_blocks(w_uv, MLA_HEADS, MLA_V).astype(BF16)
    slope_row = np.zeros((1, HW), np.float32)
    for h in range(NSA_HEADS):
        for ln in BIAS_LANES:
            slope_row[0, h * LANE + ln] = NSA_SLOPES[h]

    row = lambda w: pl.BlockSpec((tm, w), lambda i: (i, 0))
    sds = jax.ShapeDtypeStruct
    return pl.pallas_call(
        functools.partial(_even_proj_kernel, tiles_per_seq=seq // tm),
        grid=(m // tm,),
        in_specs=[row(d), row(LANE), row(LANE), row(1), _const_spec(wx.shape), _const_spec((1, MLA_Q_RANK)),
                  _const_spec((1, MLA_KV_RANK)), _const_spec(wuq.shape), _const_spec(wuqs.shape),
                  _const_spec(wuk.shape), _const_spec(wuv.shape), _const_spec((1, HW))],
        out_specs=[row(HW), row(HW), row(HW), pl.BlockSpec((NSA_HEADS, tm, LANE), lambda i: (0, i, 0)),
                   pl.BlockSpec((2, tm // CMP_STRIDE, CMP_STRIDE * LANE), lambda i: (0, i, 0)), row(4 * LANE),
                   row(2 * LANE), row(2 * LANE), row(2 * LANE), row(LANE)],
        out_shape=[sds((m, HW), BF16)] * 3 + [sds((NSA_HEADS, m, LANE), BF16),
                                              sds((2, m // CMP_STRIDE, CMP_STRIDE * LANE), BF16),
                                              sds((m, 4 * LANE), BF16), sds((m, 2 * LANE), BF16),
                                              sds((m, 2 * LANE), BF16), sds((m, 2 * LANE), BF16), sds((m, LANE), F32)],
        scratch_shapes=[pltpu.VMEM((2, tm, LANE), F32)],
        compiler_params=_params(("parallel",)),
        name="even_proj",
    )(x2, cos_t, sin_t, dpos, wx, q_norm_g.reshape(1, -1), kv_norm_g.reshape(1, -1), wuq, wuqs, wuk, wuv,
      jnp.asarray(slope_row))


def _causal_pairs(nq, tq, tk):
    qi, ki = [], []
    for i in range(nq):
        for j in range(((i + 1) * tq - 1) // tk + 1):
            qi.append(i)
            ki.append(j)
    return jnp.asarray(qi, jnp.int32), jnp.asarray(ki, jnp.int32)


def _mla_kernel(qi_tab, ki_tab, q_ref, k_ref, v_ref, o_ref, m_ref, acc_ref, *, tq, tk):
    p = pl.program_id(1)
    qi = qi_tab[p]
    ki = ki_tab[p]

    @pl.when(ki == 0)
    def _():
        _init_state(m_ref, acc_ref)

    def run(masked):
        if masked:
            mask = (qi * tq + _iota((tq, tk), 0)) >= (ki * tk + _iota((tq, tk), 1))
        for h0 in range(0, MLA_HEADS, MLA_CHAINS):
            heads = range(h0, h0 + MLA_CHAINS)
            scores = [_dot_nt(q_ref[0, :, h * LANE:(h + 1) * LANE], k_ref[0, :, h * LANE:(h + 1) * LANE])
                      for h in heads]
            if masked:
                scores = [jnp.where(mask, s, NEG) for s in scores]
            _flash_update_staged(scores, [v_ref[0, :, h * LANE:(h + 1) * LANE] for h in heads], m_ref, acc_ref,
                                 list(heads))

    crosses = (ki + 1) * tk - 1 > qi * tq

    @pl.when(crosses)
    def _():
        run(True)

    @pl.when(jnp.logical_not(crosses))
    def _():
        run(False)

    @pl.when(ki == ((qi + 1) * tq - 1) // tk)
    def _():
        _store_head_pairs(o_ref, slice(None), [_normalized(acc_ref[h]) for h in range(MLA_HEADS)])


def mla_attention(q, k, v, tq=1024, tk=512):
    b, s, hw = q.shape
    tq, tk = min(tq, s), min(tk, s)
    qi_tab, ki_tab = _causal_pairs(s // tq, tq, tk)
    qspec = pl.BlockSpec((1, tq, hw), lambda bb, p, qt, kt: (bb, qt[p], 0))
    kspec = pl.BlockSpec((1, tk, hw), lambda bb, p, qt, kt: (bb, kt[p], 0))
    return pl.pallas_call(
        functools.partial(_mla_kernel, tq=tq, tk=tk),
        grid_spec=pltpu.PrefetchScalarGridSpec(
            num_scalar_prefetch=2,
            grid=(b, int(qi_tab.shape[0])),
            in_specs=[qspec, kspec, kspec],
            out_specs=pl.BlockSpec((1, tq, OUT_W), lambda bb, p, qt, kt: (bb, qt[p], 0)),
            scratch_shapes=[pltpu.VMEM((MLA_HEADS, tq, LANE), F32), pltpu.VMEM((MLA_HEADS, tq, LANE), F32)],
        ),
        out_shape=jax.ShapeDtypeStruct((b, s, OUT_W), BF16),
        compiler_params=_params(("parallel", "arbitrary")),
        name="mla_attention",
    )(qi_tab, ki_tab, q, k, v)


def _compress_kernel(h_ref, pos_ref, w1_ref, w1g_ref, w2_ref, ext_ref, o_ref, *, is_value):
    n16 = h_ref.shape[1]
    lane = _iota((n16, LANE), 1)
    if is_value:
        extra = jnp.where(lane == ONES_LANE, 1.0, 0.0)
    else:
        extra = _bias_pieces(ext_ref[0], lane)
    bias = _dot(pos_ref[...], w1_ref[...])[0:1]
    h = h_ref[0]
    for g in range(NSA_GROUPS):
        first = _dot(h, w1g_ref[g, 0])
        second = _dot(h, w1g_ref[g, 1])
        hid = first + pltpu.roll(second, n16 - 1, 0) + bias
        act = jax.nn.gelu(hid).astype(BF16)
        o_ref[0, g] = (_dot(act, w2_ref[...]) + extra).astype(BF16)


def nsa_compress(h, cmp_pos, w1, w2, dpos_cend, is_value):
    b, n16, hw = h.shape
    pos = jnp.broadcast_to(cmp_pos.reshape(1, CMP_LEN * NSA_DH), (8, CMP_LEN * NSA_DH)).astype(BF16)
    w2p = jnp.pad(w2, ((0, 0), (0, LANE - NSA_DH))).astype(BF16)
    halves = w1.reshape(2, CMP_STRIDE, 1, NSA_DH, CMP_HIDDEN)
    w1g = jnp.stack([jnp.pad(halves, ((0, 0), (0, 0), (g, NSA_GROUPS - 1 - g), (0, 0), (0, 0))).reshape(2, hw, CMP_HIDDEN)
                     for g in range(NSA_GROUPS)]).astype(BF16)
    w1 = w1.astype(BF16)
    return pl.pallas_call(
        functools.partial(_compress_kernel, is_value=is_value),
        grid=(b,),
        in_specs=[pl.BlockSpec((1, n16, hw), lambda i: (i, 0, 0)), _const_spec(pos.shape),
                  _const_spec(w1.shape), _const_spec(w1g.shape), _const_spec(w2p.shape),
                  pl.BlockSpec((1, n16, 1), lambda i: (i, 0, 0))],
        out_specs=pl.BlockSpec((1, NSA_GROUPS, n16, LANE), lambda i: (i, 0, 0, 0)),
        out_shape=jax.ShapeDtypeStruct((b, NSA_GROUPS, n16, LANE), BF16),
        compiler_params=_params(("parallel",)),
        name="nsa_compress",
    )(h, pos, w1, w1g, w2p, dpos_cend)


def _topk_mask_t(x, k):
    n = x.shape[0]
    ridx = _iota(x.shape, 0).astype(F32)
    sel = jnp.zeros(x.shape, F32)
    for _ in range(k):
        m = jnp.max(x, axis=0, keepdims=True)
        first = jnp.min(jnp.where(x == m, ridx, float(n)), axis=0, keepdims=True)
        hit = ridx == first
        sel = jnp.where(hit, 1.0, sel)
        x = jnp.where(hit, REMOVED, x)
    return sel


def _cmp_topk_kernel(q_ref, kc_ref, vc_ref, gate_ref, oc_ref, selm1_ref, imp_ref, *, tq, ncp, topk):
    qi = pl.program_id(1)
    t = qi * tq + _iota((tq, 1), 0)
    gates = gate_ref[0]
    blk = _iota((tq, LANE), 1)
    chunk = _shr(t, SEL_BLOCK)

    def attend(nc):
        cend = _iota((1, nc), 1) * CMP_STRIDE + (CMP_LEN - 1)
        mask = (cend <= t)[None]
        cstart = _iota((nc, LANE), 0) * CMP_STRIDE
        sstart = _iota((nc, LANE), 1) * SEL_BLOCK
        overlap = jnp.where((cstart < sstart + SEL_BLOCK) & (cstart + CMP_LEN > sstart)
                            & (cstart < (ncp - 1) * CMP_STRIDE), 1.0, 0.0).astype(BF16)
        for g in range(NSA_GROUPS):
            q = q_ref[g * NSA_HPG:(g + 1) * NSA_HPG].reshape(NSA_HPG * tq, LANE)
            s = _dot_nt(q, kc_ref[0, g, :nc]).reshape(NSA_HPG, tq, nc)
            s = jnp.where(mask, s, NEG)
            e = jnp.where(mask, jnp.exp2(s - jnp.max(s, axis=-1, keepdims=True)), 0.0)
            p = e / jnp.maximum(jnp.sum(e, axis=-1, keepdims=True), 1e-30)
            psum = jnp.sum(p, axis=0)
            o = _dot(p.reshape(NSA_HPG * tq, nc).astype(BF16), vc_ref[0, g, :nc])
            heads = [jnp.where(blk < NSA_DH, o[n * tq:(n + 1) * tq] * gates[:, g * NSA_HPG + n:g * NSA_HPG + n + 1], 0.0)
                     for n in range(NSA_HPG)]
            _store_head_pairs(oc_ref, slice(None), heads, first_pair=g * NSA_HPG // 2)
            hi = psum.astype(BF16)
            r1 = psum - hi.astype(F32)
            mid = r1.astype(BF16)
            lo = (r1 - mid.astype(F32)).astype(BF16)
            imp_ref[g] = _dot(hi, overlap) + _dot(mid, overlap) + _dot(lo, overlap)

    lane_tiles = ((qi + 1) * tq // CMP_STRIDE + LANE - 1) // LANE
    for v in range(1, ncp // LANE + 1):
        @pl.when(jnp.minimum(lane_tiles, ncp // LANE) == v)
        def _(v=v):
            attend(v * LANE)

    forced = (blk == 0) | (blk == chunk)
    imps = [jnp.where(forced, REMOVED, jnp.where(blk <= chunk, imp_ref[g], NEG)).T for g in range(NSA_GROUPS)]
    picked = _topk_mask_t(jnp.concatenate(imps, axis=1), topk - 2)
    for g in range(NSA_GROUPS):
        sel = jnp.where(forced, 1.0, picked[:, g * tq:(g + 1) * tq].T)
        selm1_ref[0, g] = (jnp.where(blk <= chunk, sel, 0.0) - 1.0).astype(BF16)


def nsa_cmp_topk(qn, kc, vc, gates, tq=256):
    _, b, s, _ = qn.shape
    tq = min(tq, s)
    ncp = kc.shape[2]
    assert s // SEL_BLOCK <= LANE
    topk = min(SEL_TOPK, s // SEL_BLOCK)
    kspec = pl.BlockSpec((1, NSA_GROUPS, ncp, LANE), lambda i, j: (i, 0, 0, 0))
    return pl.pallas_call(
        functools.partial(_cmp_topk_kernel, tq=tq, ncp=ncp, topk=topk),
        grid=(b, s // tq),
        in_specs=[pl.BlockSpec((NSA_HEADS, None, tq, LANE), lambda i, j: (0, i, j, 0)), kspec, kspec,
                  pl.BlockSpec((1, tq, LANE), lambda i, j: (i, j, 0))],
        out_specs=[pl.BlockSpec((1, tq, OUT_W), lambda i, j: (i, j, 0)),
                   pl.BlockSpec((1, NSA_GROUPS, tq, LANE), lambda i, j: (i, 0, j, 0))],
        out_shape=[jax.ShapeDtypeStruct((b, s, OUT_W), BF16), jax.ShapeDtypeStruct((b, NSA_GROUPS, s, LANE), BF16)],
        scratch_shapes=[pltpu.VMEM((NSA_GROUPS, tq, LANE), F32)],
        compiler_params=_params(("parallel", "parallel")),
        name="nsa_cmp_topk",
    )(qn, kc, vc, gates)


def _nsa_write(o_ref, gate_ref, acc_ref, tq, branch):
    gates = gate_ref[0]
    heads = []
    for h in range(NSA_HEADS):
        g, n = divmod(h, NSA_HPG)
        col = branch * NSA_HEADS + h
        heads.append(_normalized(acc_ref[g, n * tq:(n + 1) * tq]) * gates[:, col:col + 1])
    _store_head_pairs(o_ref, slice(None), heads)


def _sel_attn_kernel(flags, q_ref, k_ref, v_ref, selm1_ref, gate_ref, o_ref, lhs_ref, m_ref, acc_ref, *, t, nq):
    b = pl.program_id(0)
    qi = pl.program_id(1)
    _init_state(m_ref, acc_ref)
    for g in range(NSA_GROUPS):
        lhs_ref[g, :, :LANE] = q_ref[g * NSA_HPG:(g + 1) * NSA_HPG].reshape(NSA_HPG * t, LANE)
        lhs_ref[g, :, LANE:] = jnp.concatenate([selm1_ref[0, g]] * NSA_HPG, axis=0)

    half = NSA_HPG * t // 2
    chains = [(g, i) for g in range(NSA_GROUPS) for i in range(2)]

    def update(ki, diagonal):
        rows = pl.ds(pl.multiple_of(ki * t, t), t)
        scores = [_dot_nt(lhs_ref[g, i * half:(i + 1) * half], k_ref[rows, 2 * g * LANE:(2 * g + 2) * LANE])
                  for g, i in chains]
        if diagonal:
            causal = (_iota((t, t), 1) <= _iota((t, t), 0))[None]
            scores = [jnp.where(causal, s.reshape(NSA_HPG // 2, t, t), NEG).reshape(half, t) for s in scores]
        _flash_update_staged(scores, [v_ref[rows, g * LANE:(g + 1) * LANE] for g, _ in chains], m_ref, acc_ref,
                             [(g, slice(i * half, (i + 1) * half)) for g, i in chains])

    base = (b * nq + qi) * nq

    def body(ki, carry):
        @pl.when(flags[base + ki] > 0)
        def _():
            update(ki, False)
        return carry

    lax.fori_loop(0, qi, body, 0)
    update(qi, True)
    _nsa_write(o_ref, gate_ref, acc_ref, t, branch=1)


def nsa_selected(qn, ks, vs, selm1, gates, t=256):
    _, b, s, _ = qn.shape
    t = min(t, s)
    nq = s // t
    bpt = t // SEL_BLOCK
    flags = (selm1.reshape(b, NSA_GROUPS, nq, t, LANE // bpt, bpt) > -0.5).any(axis=(1, 3, 5))[..., :nq]
    flags = flags.astype(jnp.int32).reshape(-1)
    imap_q = lambda bb, i, fl: (bb, i, 0)
    rows = NSA_HPG * t
    return pl.pallas_call(
        functools.partial(_sel_attn_kernel, t=t, nq=nq),
        grid_spec=pltpu.PrefetchScalarGridSpec(
            num_scalar_prefetch=1,
            grid=(b, nq),
            in_specs=[pl.BlockSpec((NSA_HEADS, None, t, LANE), lambda bb, i, fl: (0, bb, i, 0)),
                      pl.BlockSpec((None, s, 4 * LANE), lambda bb, i, fl: (bb, 0, 0)),
                      pl.BlockSpec((None, s, 2 * LANE), lambda bb, i, fl: (bb, 0, 0)),
                      pl.BlockSpec((1, NSA_GROUPS, t, LANE), lambda bb, i, fl: (bb, 0, i, 0)),
                      pl.BlockSpec((1, t, LANE), imap_q)],
            out_specs=pl.BlockSpec((1, t, OUT_W), imap_q),
            scratch_shapes=[pltpu.VMEM((NSA_GROUPS, rows, 2 * LANE), BF16), pltpu.VMEM((NSA_GROUPS, rows, LANE), F32),
                            pltpu.VMEM((NSA_GROUPS, rows, LANE), F32)],
        ),
        out_shape=jax.ShapeDtypeStruct((b, s, OUT_W), BF16),
        compiler_params=_params(("parallel", "arbitrary")),
        name="nsa_selected",
    )(flags, qn, ks, vs, selm1, gates)


def _win_attn_kernel(q_ref, kp_ref, kc_ref, vp_ref, vc_ref, gate_ref, o_ref, *, tq):
    qi = pl.program_id(1)
    hq = tq // 2
    span = WIN + hq
    row = _iota((hq, span), 0)
    col = _iota((hq, span), 1)
    diff = WIN + row - col
    band = (diff >= 0) & (diff < WIN)
    gates = gate_ref[0]
    chains = [(g, rh) for g in range(NSA_GROUPS) for rh in range(2)]
    scores = []
    for g, rh in chains:
        gl = slice(g * LANE, (g + 1) * LANE)
        q = q_ref[g * NSA_HPG:(g + 1) * NSA_HPG, rh * hq:(rh + 1) * hq].reshape(NSA_HPG * hq, LANE)
        s = jnp.concatenate([_dot_nt(q, kp_ref[0, rh * hq:, gl]), _dot_nt(q, kc_ref[0, :(rh + 1) * hq, gl])], axis=1)
        valid = (band & (col >= WIN - qi * tq - rh * hq))[None]
        scores.append(jnp.where(valid, s.reshape(NSA_HPG, hq, span), NEG).reshape(NSA_HPG * hq, span))
    tops = [jnp.max(s, axis=-1, keepdims=True) for s in scores]
    probs = [jnp.exp2(s - m).astype(BF16) for s, m in zip(scores, tops)]
    for (g, rh), p in zip(chains, probs):
        gl = slice(g * LANE, (g + 1) * LANE)
        n_prev = tq - rh * hq
        acc = _dot(p[:, :n_prev], vp_ref[0, rh * hq:, gl]) + _dot(p[:, n_prev:], vc_ref[0, :(rh + 1) * hq, gl])
        rows = slice(rh * hq, (rh + 1) * hq)
        cols = [2 * NSA_HEADS + g * NSA_HPG + n for n in range(NSA_HPG)]
        heads = [_normalized(acc[n * hq:(n + 1) * hq]) * gates[rows, cols[n]:cols[n] + 1] for n in range(NSA_HPG)]
        _store_head_pairs(o_ref, rows, heads, first_pair=g * NSA_HPG // 2)


def nsa_window(qn, kw, vw, gates, tq=512):
    _, b, s, _ = qn.shape
    assert tq == WIN and s % tq == 0
    prev = pl.BlockSpec((1, tq, 2 * LANE), lambda bb, i: (bb, jnp.maximum(i - 1, 0), 0))
    cur = pl.BlockSpec((1, tq, 2 * LANE), lambda bb, i: (bb, i, 0))
    return pl.pallas_call(
        functools.partial(_win_attn_kernel, tq=tq),
        grid=(b, s // tq),
        in_specs=[pl.BlockSpec((NSA_HEADS, None, tq, LANE), lambda bb, i: (0, bb, i, 0)), prev, cur, prev, cur,
                  pl.BlockSpec((1, tq, LANE), lambda bb, i: (bb, i, 0))],
        out_specs=pl.BlockSpec((1, tq, OUT_W), lambda bb, i: (bb, i, 0)),
        out_shape=jax.ShapeDtypeStruct((b, s, OUT_W), BF16),
        compiler_params=_params(("parallel", "parallel")),
        name="nsa_window",
    )(qn, kw, kw, vw, vw, gates)


def _even_out_ln_kernel(x_ref, om_ref, oc_ref, os_ref, ow_ref, wm_ref, wn_ref, g_ref, b_ref, o_ref):
    nsa = (oc_ref[...].astype(F32) + os_ref[...].astype(F32) + ow_ref[...].astype(F32)).astype(BF16)
    mix = _dot(om_ref[...], wm_ref[...]) + _dot(nsa, wn_ref[...])
    o_ref[...] = _layer_norm(ALPHA * x_ref[...] + mix, g_ref[...], b_ref[...])


def even_out_ln(x2, o_mla, o_c, o_s, o_w, w_mla, w_nsa, g, b, tm=512):
    m, d = x2.shape
    tm = min(tm, m)
    row = lambda w: pl.BlockSpec((tm, w), lambda i: (i, 0))
    return pl.pallas_call(
        _even_out_ln_kernel,
        grid=(m // tm,),
        in_specs=[row(d)] + [row(OUT_W)] * 4 + [_const_spec(w_mla.shape), _const_spec(w_nsa.shape),
                                             _const_spec((1, d)), _const_spec((1, d))],
        out_specs=row(d),
        out_shape=jax.ShapeDtypeStruct((m, d), F32),
        compiler_params=_params(("parallel",)),
        name="even_out_ln",
    )(x2, o_mla, o_c, o_s, o_w, w_mla, w_nsa, g.reshape(1, d), b.reshape(1, d))


def _odd_out_ln_kernel(x_ref, o0_ref, o1_ref, o2_ref, l0_ref, l1_ref, l2_ref, w_ref, g_ref, b_ref, o_ref,
                       o_scr, l_scr):
    tm = x_ref.shape[0]
    for gi, (src, lsrc) in enumerate(((o0_ref, l0_ref), (o1_ref, l1_ref), (o2_ref, l2_ref))):
        dil = DIL_PATTERNS[gi][1]
        for r in range(dil):
            rows = pl.ds(r, tm // dil, stride=dil) if dil > 1 else slice(None)
            for h in range(DIL_GH):
                o_scr[gi, h, rows, :] = src[r, :, h * DIL_DH:(h + 1) * DIL_DH].astype(F32)
            l_scr[gi, rows, :] = lsrc[r]
    lses = [l_scr[gi] for gi in range(DIL_GROUPS)]
    top = jnp.maximum(jnp.maximum(lses[0], lses[1]), lses[2])
    es = [jnp.exp(l - top) for l in lses]
    den = es[0] + es[1] + es[2]
    wts = [e / den for e in es]
    cols = []
    for h in range(DIL_GH):
        merged = None
        for gi in range(DIL_GROUPS):
            term = wts[gi][:, h:h + 1] * o_scr[gi, h]
            merged = term if merged is None else merged + term
        cols.append(merged.astype(BF16))
    mix = _dot(jnp.concatenate(cols, axis=1), w_ref[...])
    o_ref[...] = _layer_norm(ALPHA * x_ref[...] + mix, g_ref[...], b_ref[...])


def odd_out_ln(x2, outs, lses, w, g, b, tm=512):
    m, d = x2.shape
    b_, _, s = outs[0].shape[0], None, outs[0].shape[1] * outs[0].shape[2]
    tm = min(tm, s)
    width = DIL_GH * DIL_DH
    row = pl.BlockSpec((None, tm, d), lambda bb, i: (bb, i, 0))
    cls = lambda gi, wd: pl.BlockSpec((None, DIL_PATTERNS[gi][1], tm // DIL_PATTERNS[gi][1], wd),
                                      lambda bb, i: (bb, 0, i, 0))
    out = pl.pallas_call(
        _odd_out_ln_kernel,
        grid=(b_, s // tm),
        in_specs=[row] + [cls(gi, width) for gi in range(DIL_GROUPS)] + [cls(gi, LANE) for gi in range(DIL_GROUPS)]
        + [_const_spec(w.shape), _const_spec((1, d)), _const_spec((1, d))],
        out_specs=row,
        out_shape=jax.ShapeDtypeStruct((b_, s, d), F32),
        scratch_shapes=[pltpu.VMEM((DIL_GROUPS, DIL_GH, tm, DIL_DH), F32), pltpu.VMEM((DIL_GROUPS, tm, LANE), F32)],
        compiler_params=_params(("parallel", "parallel")),
        name="odd_out_ln",
    )(x2.reshape(b_, s, d), *outs, *lses, w, g.reshape(1, d), b.reshape(1, d))
    return out.reshape(m, d)


def _odd_proj_kernel(x_ref, w_ref, o0_ref, o1_ref, o2_ref, xs_ref):
    tm, d_model = x_ref.shape
    width = DIL_GH * DIL_DH
    n_chunks = d_model // LANE
    for c in range(n_chunks):
        xs_ref[c] = x_ref[:, c * LANE:(c + 1) * LANE]
    for gi, o_ref in enumerate((o0_ref, o1_ref, o2_ref)):
        dil = DIL_PATTERNS[gi][1]
        if dil == 1:
            xg = x_ref[...]
        else:
            xg = jnp.concatenate(
                [jnp.concatenate([xs_ref[c, pl.ds(r, tm // dil, stride=dil), :] for r in range(dil)], axis=0)
                 for c in range(n_chunks)], axis=1)
        xg = xg.astype(BF16)
        for c in range(3):
            col = (gi * 3 + c) * width
            y = _dot(xg, w_ref[:, col:col + width])
            if c == 0:
                y = y * (DIL_SCALE * LOG2E)
            o_ref[:, :, c * width:(c + 1) * width] = y.astype(BF16).reshape(dil, tm // dil, width)


def odd_proj(x2, b, s, w_in, tm=512):
    m, d = x2.shape
    tm = min(tm, s)
    width = DIL_GH * DIL_DH
    w = w_in.reshape(d, 3, DIL_GROUPS, width).transpose(0, 2, 1, 3).reshape(d, 3 * DIL_GROUPS * width).astype(BF16)
    dils = [dil for _, dil in DIL_PATTERNS]
    return pl.pallas_call(
        _odd_proj_kernel,
        grid=(b, s // tm),
        in_specs=[pl.BlockSpec((None, tm, d), lambda bb, i: (bb, i, 0)), _const_spec(w.shape)],
        out_specs=[pl.BlockSpec((None, dil, tm // dil, 3 * width), lambda bb, i: (bb, 0, i, 0)) for dil in dils],
        out_shape=[jax.ShapeDtypeStruct((b, dil, s // dil, 3 * width), BF16) for dil in dils],
        scratch_shapes=[pltpu.VMEM((d // LANE, tm, LANE), F32)],
        compiler_params=_params(("parallel", "parallel")),
        name="odd_proj",
    )(x2.reshape(b, s, d), w)


def _dil_attn_kernel(slope_tab, pos0_tab, q_ref, kc_ref, kp_ref, vc_ref, vp_ref, pc_ref, pp_ref, pq_ref, o_ref,
                     lse_ref, *, group, tq):
    sub = DIL_SPAN
    b = pl.program_id(0)
    jt = pl.program_id(2)
    row = _iota((sub, 2 * sub), 0)
    col = _iota((sub, 2 * sub), 1)
    band = (col >= row) & (col <= row + sub)
    band_bias = jnp.where(band, 0.0, NEG)
    start_bias = jnp.where(band & (col >= sub), 0.0, NEG)
    pos0 = pos0_tab[b]
    lane = _iota((sub, LANE), 1)
    for i in range(tq // sub):
        cur = slice(i * sub, (i + 1) * sub)
        if i == 0:
            mask_bias = jnp.where(jt == 0, start_bias, band_bias)
            pk = jnp.concatenate([pp_ref[...], pc_ref[:, :sub]], axis=1)
        else:
            mask_bias = band_bias
            pk = pc_ref[:, (i - 1) * sub:(i + 1) * sub]
        dpos = pk - pos0
        dq = (pq_ref[cur, :] - pos0) * LN2
        heads = range(DIL_GH)
        hs = [slice(h * DIL_DH, (h + 1) * DIL_DH) for h in heads]
        slopes = [slope_tab[group * DIL_GH + h] for h in heads]
        if i == 0:
            ks = [jnp.concatenate([kp_ref[:, hs[h]], kc_ref[:sub, hs[h]]], axis=0) for h in heads]
            vs = [jnp.concatenate([vp_ref[:, hs[h]], vc_ref[:sub, hs[h]]], axis=0) for h in heads]
        else:
            ks = [kc_ref[(i - 1) * sub:(i + 1) * sub, hs[h]] for h in heads]
            vs = [vc_ref[(i - 1) * sub:(i + 1) * sub, hs[h]] for h in heads]
        ss = [_dot_nt(q_ref[cur, hs[h]], ks[h]) + (mask_bias + slopes[h] * dpos) for h in heads]
        ms = [jnp.max(s, axis=-1, keepdims=True) for s in ss]
        ps = [jnp.exp2(s - m) for s, m in zip(ss, ms)]
        dens = [jnp.sum(p, axis=-1, keepdims=True) for p in ps]
        lse_tile = jnp.zeros((sub, LANE), F32)
        for h in heads:
            o_ref[cur, hs[h]] = (_dot(ps[h].astype(BF16), vs[h]) / dens[h]).astype(BF16)
            lse_tile = jnp.where(lane == h, ms[h] * LN2 + jnp.log(dens[h]) - slopes[h] * dq, lse_tile)
        lse_ref[cur, :] = lse_tile


def dilated_group_attention(qkv, pos_f, group, tq=512):
    b, d, sd, _ = qkv.shape
    w, dil = DIL_PATTERNS[group]
    assert w // dil == DIL_SPAN and dil == d
    tq = min(tq, sd)
    sub = DIL_SPAN
    width = DIL_GH * DIL_DH
    n_slopes = DIL_GROUPS * DIL_GH
    slopes = 2.0 ** (-ALIBI_MAX_BIAS * jnp.arange(1, n_slopes + 1, dtype=F32) / n_slopes) * LOG2E
    pos0 = pos_f[:, 0]
    posc = pos_f.reshape(b, sd, d).transpose(0, 2, 1).reshape(b, d, 1, sd)
    r = tq // sub
    cur = lambda c: pl.BlockSpec((None, None, tq, width), lambda bb, rr, j, *_: (bb, rr, j, c))
    prev = lambda c: pl.BlockSpec((None, None, sub, width),
                                  lambda bb, rr, j, *_: (bb, rr, jnp.maximum(j * r - 1, 0), c))
    return pl.pallas_call(
        functools.partial(_dil_attn_kernel, group=group, tq=tq),
        grid_spec=pltpu.PrefetchScalarGridSpec(
            num_scalar_prefetch=2,
            grid=(b, d, sd // tq),
            in_specs=[cur(0), cur(1), prev(1), cur(2), prev(2),
                      pl.BlockSpec((None, None, 1, tq), lambda bb, rr, j, *_: (bb, rr, 0, j)),
                      pl.BlockSpec((None, None, 1, sub), lambda bb, rr, j, *_: (bb, rr, 0, jnp.maximum(j * r - 1, 0))),
                      pl.BlockSpec((None, None, tq, 1), lambda bb, rr, j, *_: (bb, rr, j, 0))],
            out_specs=[pl.BlockSpec((None, None, tq, width), lambda bb, rr, j, *_: (bb, rr, j, 0)),
                       pl.BlockSpec((None, None, tq, LANE), lambda bb, rr, j, *_: (bb, rr, j, 0))],
        ),
        out_shape=[jax.ShapeDtypeStruct((b, d, sd, width), BF16), jax.ShapeDtypeStruct((b, d, sd, LANE), F32)],
        compiler_params=_params(("parallel", "parallel", "parallel")),
        name=f"dilated_attention_g{group}",
    )(slopes, pos0, qkv, qkv, qkv, qkv, qkv, posc, posc, posc.reshape(b, d, sd, 1))


def even_mixer_ln(x2, b, s, pos_f, w_in, q_norm_g, kv_norm_g, w_uq, w_uk, w_uv,
                  cmp_pos, cmp_k_w1, cmp_k_w2, cmp_v_w1, cmp_v_w2, w_out, ln_g, ln_b):
    m = b * s
    d = x2.shape[1]
    half = MLA_ROPE // 2
    inv = ROPE_THETA ** (-jnp.arange(half, dtype=F32) / half)
    ang = (pos_f[..., None] * inv).reshape(m, half)
    ones = jnp.ones((m, MLA_NOPE), F32)
    zeros = jnp.zeros((m, LANE - MLA_NOPE - MLA_ROPE), F32)
    cos_t = jnp.concatenate([ones, jnp.cos(ang), jnp.cos(ang), zeros], axis=1)
    sin_t = jnp.concatenate([0.0 * ones, jnp.sin(ang), jnp.sin(ang), zeros], axis=1)
    dpos = (pos_f - pos_f[:, :1]) * LOG2E
    ncp = s // CMP_STRIDE
    dpos_cend = jnp.pad(dpos[:, CMP_LEN - 1::CMP_STRIDE], ((0, 0), (0, 1))).reshape(b, ncp, 1)

    qm, km, vm, qn, kvc, ks, kw, vs, vw, gates = even_proj(
        x2, cos_t, sin_t, dpos.reshape(m, 1), s, w_in, q_norm_g, kv_norm_g, w_uq, w_uk, w_uv)
    o_mla = mla_attention(qm.reshape(b, s, HW), km.reshape(b, s, HW), vm.reshape(b, s, HW))

    kvc = kvc.reshape(2, b, ncp, CMP_STRIDE * LANE)
    kc = nsa_compress(kvc[0], cmp_pos, cmp_k_w1, cmp_k_w2, dpos_cend, is_value=False)
    vc = nsa_compress(kvc[1], cmp_pos, cmp_v_w1, cmp_v_w2, dpos_cend, is_value=True)
    qn = qn.reshape(NSA_HEADS, b, s, LANE)
    gates = gates.reshape(b, s, LANE)
    o_c, selm1 = nsa_cmp_topk(qn, kc, vc, gates)
    o_s = nsa_selected(qn, ks.reshape(b, s, 4 * LANE), vs.reshape(b, s, 2 * LANE), selm1, gates)
    o_w = nsa_window(qn, kw.reshape(b, s, 2 * LANE), vw.reshape(b, s, 2 * LANE), gates)

    w_mla = w_out[:OUT_W].astype(BF16)
    w_nsa = w_out[OUT_W:].astype(BF16)
    return even_out_ln(x2, o_mla.reshape(m, OUT_W), o_c.reshape(m, OUT_W), o_s.reshape(m, OUT_W),
                       o_w.reshape(m, OUT_W), w_mla, w_nsa, ln_g, ln_b)


def odd_mixer_ln(x2, b, s, pos_f, w_in, w_out, ln_g, ln_b):
    qkvs = odd_proj(x2, b, s, w_in)
    parts = [dilated_group_attention(qkvs[g], pos_f, g) for g in range(DIL_GROUPS)]
    return odd_out_ln(x2, [p[0] for p in parts], [p[1] for p in parts], w_out.astype(BF16), ln_g, ln_b)


def kernel(x, positions, ln1_g, ln1_b, ffn1_w_gate, ffn1_w_up, ffn1_w_down, mix_in_even, mla_q_norm, mla_kv_norm, mla_w_uq, mla_w_uk, mla_w_uv, nsa_cmp_pos, nsa_cmp_k_w1, nsa_cmp_k_w2, nsa_cmp_v_w1, nsa_cmp_v_w2, mix_out_even, mix_in_odd, mix_out_odd, ln2_g, ln2_b, ffn2_w_gate, ffn2_w_up, ffn2_w_down, ln3_g, ln3_b):
    b, s, d = x.shape
    x2 = x.reshape(b * s, d)
    pos_f = positions.astype(F32)
    for i in range(DEPTH):
        j = i // 2
        x2 = ffn_ln(x2, ffn1_w_gate[i].astype(BF16), ffn1_w_up[i].astype(BF16), ffn1_w_down[i].astype(BF16),
                    ln1_g[i], ln1_b[i])
        if i % 2 == 0:
            x2 = even_mixer_ln(x2, b, s, pos_f, mix_in_even[j], mla_q_norm[j], mla_kv_norm[j], mla_w_uq[j],
                               mla_w_uk[j], mla_w_uv[j], nsa_cmp_pos[j], nsa_cmp_k_w1[j], nsa_cmp_k_w2[j],
                               nsa_cmp_v_w1[j], nsa_cmp_v_w2[j], mix_out_even[j], ln2_g[i], ln2_b[i])
        else:
            x2 = odd_mixer_ln(x2, b, s, pos_f, mix_in_odd[j], mix_out_odd[j], ln2_g[i], ln2_b[i])
        x2 = ffn_ln(x2, ffn2_w_gate[i].astype(BF16), ffn2_w_up[i].astype(BF16), ffn2_w_down[i].astype(BF16),
                    ln3_g[i], ln3_b[i])
    return x2.reshape(b, s, d)
```

```python
import functools
import math

import numpy as np
import jax
import jax.numpy as jnp
from jax import lax
from jax.experimental import pallas as pl
from jax.experimental.pallas import tpu as pltpu

F32 = jnp.float32
BF16 = jnp.bfloat16

DEPTH = 2
LN_EPS = 1e-5
RMS_EPS = 1e-6
ALPHA = (2 * DEPTH) ** 0.25
HALF_STEP = 0.5
NEG = -1e30
BIG = 1e9
MASK_BIG = 1e30
REMOVED = -3.0e38
ALIBI_MAX_BIAS = 8.0
LOG2E = math.log2(math.e)
LN2 = math.log(2.0)
LANE = 128

MLA_HEADS = 8
MLA_Q_RANK = 384
MLA_KV_RANK = 256
MLA_NOPE = 64
MLA_ROPE = 32
MLA_V = 64
ROPE_THETA = 10000.0
MLA_SCALE = (MLA_NOPE + MLA_ROPE) ** -0.5
MLA_CHAINS = 4

NSA_HEADS = 8
NSA_GROUPS = 2
NSA_HPG = 4
NSA_DH = 64
CMP_LEN = 32
CMP_STRIDE = 16
CMP_HIDDEN = 256
SEL_BLOCK = 64
SEL_TOPK = 16
WIN = 512
NSA_SCALE = NSA_DH ** -0.5
NSA_SLOPES = tuple(2.0 ** (-ALIBI_MAX_BIAS * (i + 1) / NSA_HEADS) for i in range(NSA_HEADS))
ONES_LANE = 64
BIAS_LANES = (64, 65, 66)

DIL_PATTERNS = ((128, 1), (512, 4), (2048, 16))
DIL_GROUPS = 3
DIL_GH = 4
DIL_DH = 128
DIL_SCALE = DIL_DH ** -0.5
DIL_SPAN = 128

VMEM_LIMIT = 48 * 1024 * 1024


def _iota(shape, dim):
    return lax.broadcasted_iota(jnp.int32, shape, dim)


def _shr(x, pow2):
    return jnp.right_shift(x, int(pow2).bit_length() - 1)


def _dot(a, b):
    return jnp.dot(a, b, preferred_element_type=F32)


def _dot_nt(a, b):
    return lax.dot_general(a, b, (((1,), (1,)), ((), ())), preferred_element_type=F32)


def _const_spec(shape):
    zeros = (0,) * len(shape)
    return pl.BlockSpec(shape, lambda *_: zeros, pipeline_mode=pl.Buffered(1))


def _params(sem):
    return pltpu.CompilerParams(dimension_semantics=sem, vmem_limit_bytes=VMEM_LIMIT)


def _layer_norm(z, g, b):
    mu = jnp.mean(z, axis=-1, keepdims=True)
    zc = z - mu
    var = jnp.mean(zc * zc, axis=-1, keepdims=True)
    return zc * lax.rsqrt(var + LN_EPS) * g + b


def _rms_norm(z, g):
    return z * lax.rsqrt(jnp.mean(z * z, axis=-1, keepdims=True) + RMS_EPS) * g


def _bias_pieces(d, lane):
    x = jnp.broadcast_to(d, lane.shape)
    hi = x.astype(BF16).astype(F32)
    r = x - hi
    mid = r.astype(BF16).astype(F32)
    lo = (r - mid).astype(BF16).astype(F32)
    return jnp.where(lane == BIAS_LANES[0], hi,
                     jnp.where(lane == BIAS_LANES[1], mid, jnp.where(lane == BIAS_LANES[2], lo, 0.0)))


def _flash_update(s, v, m_ref, acc_ref, idx):
    m_old = m_ref[idx]
    m_new = jnp.maximum(m_old, jnp.max(s, axis=-1, keepdims=True))
    p = jnp.exp2(s - jnp.tile(m_new, (1, s.shape[1] // LANE))).astype(BF16)
    acc_ref[idx] = jnp.exp2(m_old - m_new) * acc_ref[idx] + _dot(p, v)
    m_ref[idx] = m_new


def _flash_update_staged(scores, values, m_ref, acc_ref, idxs):
    m_old = [m_ref[i] for i in idxs]
    m_new = [jnp.maximum(mo, jnp.max(s, axis=-1, keepdims=True)) for mo, s in zip(m_old, scores)]
    ps = [jnp.exp2(s - jnp.tile(mn, (1, s.shape[1] // LANE))).astype(BF16) for s, mn in zip(scores, m_new)]
    for i, mo, mn, p, v in zip(idxs, m_old, m_new, ps, values):
        acc_ref[i] = jnp.exp2(mo - mn) * acc_ref[i] + _dot(p, v)
        m_ref[i] = mn


def _init_state(m_ref, acc_ref):
    m_ref[...] = jnp.full(m_ref.shape, NEG, F32)
    acc_ref[...] = jnp.zeros(acc_ref.shape, F32)


def _normalized(acc):
    lane = _iota(acc.shape, 1)
    o = acc / jnp.maximum(acc[:, ONES_LANE:ONES_LANE + 1], 1e-30)
    return jnp.where(lane < ONES_LANE, o, 0.0)


def _store_head_pairs(o_ref, rows, heads, first_pair=0):
    for pr in range(len(heads) // 2):
        packed = heads[2 * pr] + pltpu.roll(heads[2 * pr + 1], LANE // 2, 1)
        o_ref[0, rows, (first_pair + pr) * LANE:(first_pair + pr + 1) * LANE] = packed.astype(BF16)


def _ffn_ln_kernel(x_ref, wg_ref, wu_ref, wd_ref, g_ref, b_ref, o_ref, *, n_chunks):
    x = x_ref[...]
    xb = x.astype(BF16)
    c = wg_ref.shape[1] // n_chunks
    y = None
    for i in range(n_chunks):
        gt = _dot(xb, wg_ref[:, i * c:(i + 1) * c])
        up = _dot(xb, wu_ref[:, i * c:(i + 1) * c])
        h = (gt * jax.nn.sigmoid(gt) * up).astype(BF16)
        part = _dot(h, wd_ref[i * c:(i + 1) * c, :])
        y = part if y is None else y + part
    o_ref[...] = _layer_norm(ALPHA * x + HALF_STEP * y, g_ref[...], b_ref[...])


def ffn_ln(x2, wg, wu, wd, g, b, tm=1024, n_chunks=11):
    m, d = x2.shape
    tm = min(tm, m)
    row = pl.BlockSpec((tm, d), lambda i: (i, 0))
    return pl.pallas_call(
        functools.partial(_ffn_ln_kernel, n_chunks=n_chunks),
        grid=(m // tm,),
        in_specs=[row, _const_spec(wg.shape), _const_spec(wu.shape), _const_spec(wd.shape),
                  _const_spec((1, d)), _const_spec((1, d))],
        out_specs=row,
        out_shape=jax.ShapeDtypeStruct((m, d), F32),
        compiler_params=_params(("parallel",)),
        name="ffn_ln",
    )(x2, wg, wu, wd, g.reshape(1, d), b.reshape(1, d))


HW = MLA_HEADS * LANE
OUT_W = MLA_HEADS * MLA_V
EVEN_X_COLS = (MLA_Q_RANK, MLA_KV_RANK, LANE, LANE, HW, 2 * LANE, 8 * LANE, LANE)
EVEN_X_OFFS = tuple(int(v) for v in np.cumsum((0,) + EVEN_X_COLS))


def _even_proj_kernel(x_ref, cos_ref, sin_ref, dpos_ref, wx_ref, qg_ref, kvg_ref, wuq_ref, wuqs_ref, wuk_ref,
                      wuv_ref, slope_ref, qm_ref, km_ref, vm_ref, qn_ref, kvc_ref, ks_ref, kw_ref, vs_ref, vw_ref,
                      gate_ref, kvc_scr, *, tiles_per_seq):
    tm = x_ref.shape[0]
    xb = x_ref[...].astype(BF16)
    cos = cos_ref[...]
    sin = sin_ref[...]
    lane = _iota((tm, LANE), 1)
    ones_lane = jnp.where(lane == ONES_LANE, 1.0, 0.0)
    pos_term = _bias_pieces(dpos_ref[...], lane)
    tok = (pl.program_id(0) % tiles_per_seq) * tm + _iota((tm, LANE), 0)
    block_onehot = jnp.where(lane == _shr(tok, SEL_BLOCK), MASK_BIG, 0.0).astype(BF16)

    def xdot(i):
        return _dot(xb, wx_ref[:, EVEN_X_OFFS[i]:EVEN_X_OFFS[i + 1]])

    cq = _rms_norm(xdot(0), qg_ref[...]).astype(BF16)
    ckv = _rms_norm(xdot(1), kvg_ref[...]).astype(BF16)
    k_rot = xdot(2) * cos + xdot(3) * sin
    for h in range(MLA_HEADS):
        sl = slice(h * LANE, (h + 1) * LANE)
        q = _dot(cq, wuq_ref[:, sl]) * cos + _dot(cq, wuqs_ref[:, sl]) * sin
        qm_ref[:, sl] = (q * (MLA_SCALE * LOG2E)).astype(BF16)
        km_ref[:, sl] = (_dot(ckv, wuk_ref[:, sl]) + k_rot).astype(BF16)
        vm_ref[:, sl] = (_dot(ckv, wuv_ref[:, sl]) + ones_lane).astype(BF16)
    qn = xdot(4) * (NSA_SCALE * LOG2E) + slope_ref[...]
    for h in range(NSA_HEADS):
        qn_ref[h] = qn[:, h * LANE:(h + 1) * LANE].astype(BF16)
    kvc = xdot(5)
    for j in range(2):
        kvc_scr[j] = kvc[:, j * LANE:(j + 1) * LANE]
        for l in range(CMP_STRIDE):
            kvc_ref[j, :, l * LANE:(l + 1) * LANE] = kvc_scr[j, pl.ds(l, tm // CMP_STRIDE, stride=CMP_STRIDE), :].astype(BF16)
    kv8 = xdot(6)
    blk = lambda i: kv8[:, i * LANE:(i + 1) * LANE]
    for g in range(NSA_GROUPS):
        ks_ref[:, 2 * g * LANE:(2 * g + 1) * LANE] = (blk(g) + pos_term).astype(BF16)
        ks_ref[:, (2 * g + 1) * LANE:(2 * g + 2) * LANE] = block_onehot
        kw_ref[:, g * LANE:(g + 1) * LANE] = (blk(2 + g) + pos_term).astype(BF16)
        vs_ref[:, g * LANE:(g + 1) * LANE] = (blk(4 + g) + ones_lane).astype(BF16)
        vw_ref[:, g * LANE:(g + 1) * LANE] = (blk(6 + g) + ones_lane).astype(BF16)
    gate_ref[...] = jax.nn.sigmoid(xdot(7))


def _head_blocks(w, n_heads, width):
    k = w.shape[0]
    w = jnp.pad(w.reshape(k, n_heads, width), ((0, 0), (0, 0), (0, LANE - width)))
    return w.reshape(k, n_heads * LANE)


def even_proj(x2, cos_t, sin_t, dpos, seq, w_in, q_norm_g, kv_norm_g, w_uq, w_uk, w_uv, tm=256):
    m, d = x2.shape
    tm = min(tm, seq)
    half = MLA_ROPE // 2
    gw = NSA_GROUPS * NSA_DH
    cuts = np.cumsum((MLA_Q_RANK, MLA_KV_RANK, MLA_ROPE, NSA_HEADS * NSA_DH) + (gw,) * 6)
    cuts = [0] + [int(c) for c in cuts]
    w_cq, w_ckv, w_kpe, w_q = (w_in[:, cuts[i]:cuts[i + 1]] for i in range(4))
    w_kc, w_vc, w_ks, w_vs, w_kw, w_vw = (w_in[:, cuts[4 + i]:cuts[5 + i]] for i in range(6))
    w_gate = w_in[:, cuts[10]:]
    w_kpe_sw = jnp.concatenate([-w_kpe[:, half:], w_kpe[:, :half]], axis=1)
    rope_pad = ((0, 0), (MLA_NOPE, LANE - MLA_NOPE - MLA_ROPE))
    w_gate_blk = jnp.pad(w_gate, ((0, 0), (0, LANE - w_gate.shape[1])))
    wx = jnp.concatenate(
        [w_cq, w_ckv, jnp.pad(w_kpe, rope_pad), jnp.pad(w_kpe_sw, rope_pad), _head_blocks(w_q, NSA_HEADS, NSA_DH),
         w_kc, w_vc] + [_head_blocks(w, NSA_GROUPS, NSA_DH) for w in (w_ks, w_kw, w_vs, w_vw)] + [w_gate_blk],
        axis=1).astype(BF16)

    qd = MLA_NOPE + MLA_ROPE
    uq = w_uq.reshape(MLA_Q_RANK, MLA_HEADS, qd)
    uq_sw = jnp.concatenate([jnp.zeros_like(uq[..., :MLA_NOPE]), -uq[..., MLA_NOPE + half:],
                             uq[..., MLA_NOPE:MLA_NOPE + half]], axis=-1)
    wuq = _head_blocks(uq.reshape(MLA_Q_RANK, -1), MLA_HEADS, qd).astype(BF16)
    wuqs = _head_blocks(uq_sw.reshape(MLA_Q_RANK, -1), MLA_HEADS, qd).astype(BF16)
    wuk = _head_blocks(w_uk, MLA_HEADS, MLA_NOPE).astype(BF16)
    wuv = _head_blocks(w_uv, MLA_HEADS, MLA_V).astype(BF16)
    slope_row = np.zeros((1, HW), np.float32)
    for h in range(NSA_HEADS):
        for ln in BIAS_LANES:
            slope_row[0, h * LANE + ln] = NSA_SLOPES[h]

    row = lambda w: pl.BlockSpec((tm, w), lambda i: (i, 0))
    sds = jax.ShapeDtypeStruct
    return pl.pallas_call(
        functools.partial(_even_proj_kernel, tiles_per_seq=seq // tm),
        grid=(m // tm,),
        in_specs=[row(d), row(LANE), row(LANE), row(1), _const_spec(wx.shape), _const_spec((1, MLA_Q_RANK)),
                  _const_spec((1, MLA_KV_RANK)), _const_spec(wuq.shape), _const_spec(wuqs.shape),
                  _const_spec(wuk.shape), _const_spec(wuv.shape), _const_spec((1, HW))],
        out_specs=[row(HW), row(HW), row(HW), pl.BlockSpec((NSA_HEADS, tm, LANE), lambda i: (0, i, 0)),
                   pl.BlockSpec((2, tm // CMP_STRIDE, CMP_STRIDE * LANE), lambda i: (0, i, 0)), row(4 * LANE),
                   row(2 * LANE), row(2 * LANE), row(2 * LANE), row(LANE)],
        out_shape=[sds((m, HW), BF16)] * 3 + [sds((NSA_HEADS, m, LANE), BF16),
                                              sds((2, m // CMP_STRIDE, CMP_STRIDE * LANE), BF16),
                                              sds((m, 4 * LANE), BF16), sds((m, 2 * LANE), BF16),
                                              sds((m, 2 * LANE), BF16), sds((m, 2 * LANE), BF16), sds((m, LANE), F32)],
        scratch_shapes=[pltpu.VMEM((2, tm, LANE), F32)],
        compiler_params=_params(("parallel",)),
        name="even_proj",
    )(x2, cos_t, sin_t, dpos, wx, q_norm_g.reshape(1, -1), kv_norm_g.reshape(1, -1), wuq, wuqs, wuk, wuv,
      jnp.asarray(slope_row))


def _causal_pairs(nq, tq, tk):
    qi, ki = [], []
    for i in range(nq):
        for j in range(((i + 1) * tq - 1) // tk + 1):
            qi.append(i)
            ki.append(j)
    return jnp.asarray(qi, jnp.int32), jnp.asarray(ki, jnp.int32)


def _mla_kernel(qi_tab, ki_tab, q_ref, k_ref, v_ref, o_ref, m_ref, acc_ref, *, tq, tk):
    p = pl.program_id(1)
    qi = qi_tab[p]
    ki = ki_tab[p]

    @pl.when(ki == 0)
    def _():
        _init_state(m_ref, acc_ref)

    def run(rows, diagonal):
        if diagonal:
            mask = _iota((tk, tk), 0) >= _iota((tk, tk), 1)
        for h0 in range(0, MLA_HEADS, MLA_CHAINS):
            heads = range(h0, h0 + MLA_CHAINS)
            scores = [_dot_nt(q_ref[0, rows, h * LANE:(h + 1) * LANE], k_ref[0, :, h * LANE:(h + 1) * LANE])
                      for h in heads]
            if diagonal:
                scores = [jnp.where(mask, s, NEG) for s in scores]
            _flash_update_staged(scores, [v_ref[0, :, h * LANE:(h + 1) * LANE] for h in heads], m_ref, acc_ref,
                                 [(h, rows) for h in heads])

    blocks = tq // tk
    for r in range(blocks):
        rows = slice(r * tk, (r + 1) * tk)
        own = qi * blocks + r

        @pl.when(ki < own)
        def _(rows=rows):
            run(rows, False)

        @pl.when(ki == own)
        def _(rows=rows):
            run(rows, True)

    @pl.when(ki == ((qi + 1) * tq - 1) // tk)
    def _():
        _store_head_pairs(o_ref, slice(None), [_normalized(acc_ref[h]) for h in range(MLA_HEADS)])


def mla_attention(q, k, v, tq=1024, tk=512):
    b, s, hw = q.shape
    tq, tk = min(tq, s), min(tk, s)
    qi_tab, ki_tab = _causal_pairs(s // tq, tq, tk)
    qspec = pl.BlockSpec((1, tq, hw), lambda bb, p, qt, kt: (bb, qt[p], 0))
    kspec = pl.BlockSpec((1, tk, hw), lambda bb, p, qt, kt: (bb, kt[p], 0))
    return pl.pallas_call(
        functools.partial(_mla_kernel, tq=tq, tk=tk),
        grid_spec=pltpu.PrefetchScalarGridSpec(
            num_scalar_prefetch=2,
            grid=(b, int(qi_tab.shape[0])),
            in_specs=[qspec, kspec, kspec],
            out_specs=pl.BlockSpec((1, tq, OUT_W), lambda bb, p, qt, kt: (bb, qt[p], 0)),
            scratch_shapes=[pltpu.VMEM((MLA_HEADS, tq, LANE), F32), pltpu.VMEM((MLA_HEADS, tq, LANE), F32)],
        ),
        out_shape=jax.ShapeDtypeStruct((b, s, OUT_W), BF16),
        compiler_params=_params(("parallel", "arbitrary")),
        name="mla_attention",
    )(qi_tab, ki_tab, q, k, v)


def _compress_kernel(h_ref, pos_ref, w1_ref, w1g_ref, w2_ref, ext_ref, o_ref, *, is_value):
    n16 = h_ref.shape[1]
    lane = _iota((n16, LANE), 1)
    if is_value:
        extra = jnp.where(lane == ONES_LANE, 1.0, 0.0)
    else:
        extra = _bias_pieces(ext_ref[0], lane)
    bias = _dot(pos_ref[...], w1_ref[...])[0:1]
    h = h_ref[0]
    for g in range(NSA_GROUPS):
        first = _dot(h, w1g_ref[g, 0])
        second = _dot(h, w1g_ref[g, 1])
        hid = first + pltpu.roll(second, n16 - 1, 0) + bias
        act = jax.nn.gelu(hid).astype(BF16)
        o_ref[0, g] = (_dot(act, w2_ref[...]) + extra).astype(BF16)


def nsa_compress(h, cmp_pos, w1, w2, dpos_cend, is_value):
    b, n16, hw = h.shape
    pos = jnp.broadcast_to(cmp_pos.reshape(1, CMP_LEN * NSA_DH), (8, CMP_LEN * NSA_DH)).astype(BF16)
    w2p = jnp.pad(w2, ((0, 0), (0, LANE - NSA_DH))).astype(BF16)
    halves = w1.reshape(2, CMP_STRIDE, 1, NSA_DH, CMP_HIDDEN)
    w1g = jnp.stack([jnp.pad(halves, ((0, 0), (0, 0), (g, NSA_GROUPS - 1 - g), (0, 0), (0, 0))).reshape(2, hw, CMP_HIDDEN)
                     for g in range(NSA_GROUPS)]).astype(BF16)
    w1 = w1.astype(BF16)
    return pl.pallas_call(
        functools.partial(_compress_kernel, is_value=is_value),
        grid=(b,),
        in_specs=[pl.BlockSpec((1, n16, hw), lambda i: (i, 0, 0)), _const_spec(pos.shape),
                  _const_spec(w1.shape), _const_spec(w1g.shape), _const_spec(w2p.shape),
                  pl.BlockSpec((1, n16, 1), lambda i: (i, 0, 0))],
        out_specs=pl.BlockSpec((1, NSA_GROUPS, n16, LANE), lambda i: (i, 0, 0, 0)),
        out_shape=jax.ShapeDtypeStruct((b, NSA_GROUPS, n16, LANE), BF16),
        compiler_params=_params(("parallel",)),
        name="nsa_compress",
    )(h, pos, w1, w1g, w2p, dpos_cend)


def _topk_mask_t(x, k):
    n = x.shape[0]
    ridx = _iota(x.shape, 0).astype(F32)
    sel = jnp.zeros(x.shape, F32)
    for _ in range(k):
        m = jnp.max(x, axis=0, keepdims=True)
        first = jnp.min(jnp.where(x == m, ridx, float(n)), axis=0, keepdims=True)
        hit = ridx == first
        sel = jnp.where(hit, 1.0, sel)
        x = jnp.where(hit, REMOVED, x)
    return sel


def _cmp_topk_kernel(q_ref, kc_ref, vc_ref, gate_ref, oc_ref, selm1_ref, imp_ref, *, tq, ncp, topk):
    qi = pl.program_id(1)
    t = qi * tq + _iota((tq, 1), 0)
    gates = gate_ref[0]
    blk = _iota((tq, LANE), 1)
    chunk = _shr(t, SEL_BLOCK)

    def attend(nc):
        cend = _iota((1, nc), 1) * CMP_STRIDE + (CMP_LEN - 1)
        mask = (cend <= t)[None]
        cstart = _iota((nc, LANE), 0) * CMP_STRIDE
        sstart = _iota((nc, LANE), 1) * SEL_BLOCK
        overlap = jnp.where((cstart < sstart + SEL_BLOCK) & (cstart + CMP_LEN > sstart)
                            & (cstart < (ncp - 1) * CMP_STRIDE), 1.0, 0.0).astype(BF16)
        for g in range(NSA_GROUPS):
            q = q_ref[g * NSA_HPG:(g + 1) * NSA_HPG].reshape(NSA_HPG * tq, LANE)
            s = _dot_nt(q, kc_ref[0, g, :nc]).reshape(NSA_HPG, tq, nc)
            s = jnp.where(mask, s, NEG)
            e = jnp.where(mask, jnp.exp2(s - jnp.max(s, axis=-1, keepdims=True)), 0.0)
            p = e / jnp.maximum(jnp.sum(e, axis=-1, keepdims=True), 1e-30)
            psum = jnp.sum(p, axis=0)
            o = _dot(p.reshape(NSA_HPG * tq, nc).astype(BF16), vc_ref[0, g, :nc])
            heads = [jnp.where(blk < NSA_DH, o[n * tq:(n + 1) * tq] * gates[:, g * NSA_HPG + n:g * NSA_HPG + n + 1], 0.0)
                     for n in range(NSA_HPG)]
            _store_head_pairs(oc_ref, slice(None), heads, first_pair=g * NSA_HPG // 2)
            hi = psum.astype(BF16)
            r1 = psum - hi.astype(F32)
            mid = r1.astype(BF16)
            lo = (r1 - mid.astype(F32)).astype(BF16)
            imp_ref[g] = _dot(hi, overlap) + _dot(mid, overlap) + _dot(lo, overlap)

    lane_tiles = ((qi + 1) * tq // CMP_STRIDE + LANE - 1) // LANE
    for v in range(1, ncp // LANE + 1):
        @pl.when(jnp.minimum(lane_tiles, ncp // LANE) == v)
        def _(v=v):
            attend(v * LANE)

    forced = (blk == 0) | (blk == chunk)
    imps = [jnp.where(forced, REMOVED, jnp.where(blk <= chunk, imp_ref[g], NEG)).T for g in range(NSA_GROUPS)]
    picked = _topk_mask_t(jnp.concatenate(imps, axis=1), topk - 2)
    for g in range(NSA_GROUPS):
        sel = jnp.where(forced, 1.0, picked[:, g * tq:(g + 1) * tq].T)
        selm1_ref[0, g] = (jnp.where(blk <= chunk, sel, 0.0) - 1.0).astype(BF16)


def nsa_cmp_topk(qn, kc, vc, gates, tq=256):
    _, b, s, _ = qn.shape
    tq = min(tq, s)
    ncp = kc.shape[2]
    assert s // SEL_BLOCK <= LANE
    topk = min(SEL_TOPK, s // SEL_BLOCK)
    kspec = pl.BlockSpec((1, NSA_GROUPS, ncp, LANE), lambda i, j: (i, 0, 0, 0))
    return pl.pallas_call(
        functools.partial(_cmp_topk_kernel, tq=tq, ncp=ncp, topk=topk),
        grid=(b, s // tq),
        in_specs=[pl.BlockSpec((NSA_HEADS, None, tq, LANE), lambda i, j: (0, i, j, 0)), kspec, kspec,
                  pl.BlockSpec((1, tq, LANE), lambda i, j: (i, j, 0))],
        out_specs=[pl.BlockSpec((1, tq, OUT_W), lambda i, j: (i, j, 0)),
                   pl.BlockSpec((1, NSA_GROUPS, tq, LANE), lambda i, j: (i, 0, j, 0))],
        out_shape=[jax.ShapeDtypeStruct((b, s, OUT_W), BF16), jax.ShapeDtypeStruct((b, NSA_GROUPS, s, LANE), BF16)],
        scratch_shapes=[pltpu.VMEM((NSA_GROUPS, tq, LANE), F32)],
        compiler_params=_params(("parallel", "parallel")),
        name="nsa_cmp_topk",
    )(qn, kc, vc, gates)


def _nsa_write(o_ref, gate_ref, acc_ref, tq, branch):
    gates = gate_ref[0]
    heads = []
    for h in range(NSA_HEADS):
        g, n = divmod(h, NSA_HPG)
        col = branch * NSA_HEADS + h
        heads.append(_normalized(acc_ref[g, n * tq:(n + 1) * tq]) * gates[:, col:col + 1])
    _store_head_pairs(o_ref, slice(None), heads)


def _sel_attn_kernel(flags, q_ref, k_ref, v_ref, selm1_ref, gate_ref, o_ref, lhs_ref, m_ref, acc_ref, *, t, nq):
    b = pl.program_id(0)
    qi = pl.program_id(1)
    _init_state(m_ref, acc_ref)
    for g in range(NSA_GROUPS):
        lhs_ref[g, :, :LANE] = q_ref[g * NSA_HPG:(g + 1) * NSA_HPG].reshape(NSA_HPG * t, LANE)
        lhs_ref[g, :, LANE:] = jnp.concatenate([selm1_ref[0, g]] * NSA_HPG, axis=0)

    half = NSA_HPG * t // 2
    chains = [(g, i) for g in range(NSA_GROUPS) for i in range(2)]

    def update(ki, diagonal):
        rows = pl.ds(pl.multiple_of(ki * t, t), t)
        scores = [_dot_nt(lhs_ref[g, i * half:(i + 1) * half], k_ref[rows, 2 * g * LANE:(2 * g + 2) * LANE])
                  for g, i in chains]
        if diagonal:
            causal = (_iota((t, t), 1) <= _iota((t, t), 0))[None]
            scores = [jnp.where(causal, s.reshape(NSA_HPG // 2, t, t), NEG).reshape(half, t) for s in scores]
        _flash_update_staged(scores, [v_ref[rows, g * LANE:(g + 1) * LANE] for g, _ in chains], m_ref, acc_ref,
                             [(g, slice(i * half, (i + 1) * half)) for g, i in chains])

    base = (b * nq + qi) * nq

    def body(ki, carry):
        @pl.when(flags[base + ki] > 0)
        def _():
            update(ki, False)
        return carry

    lax.fori_loop(0, qi, body, 0)
    update(qi, True)
    _nsa_write(o_ref, gate_ref, acc_ref, t, branch=1)


def nsa_selected(qn, ks, vs, selm1, gates, t=256):
    _, b, s, _ = qn.shape
    t = min(t, s)
    nq = s // t
    bpt = t // SEL_BLOCK
    flags = (selm1.reshape(b, NSA_GROUPS, nq, t, LANE // bpt, bpt) > -0.5).any(axis=(1, 3, 5))[..., :nq]
    flags = flags.astype(jnp.int32).reshape(-1)
    imap_q = lambda bb, i, fl: (bb, i, 0)
    rows = NSA_HPG * t
    return pl.pallas_call(
        functools.partial(_sel_attn_kernel, t=t, nq=nq),
        grid_spec=pltpu.PrefetchScalarGridSpec(
            num_scalar_prefetch=1,
            grid=(b, nq),
            in_specs=[pl.BlockSpec((NSA_HEADS, None, t, LANE), lambda bb, i, fl: (0, bb, i, 0)),
                      pl.BlockSpec((None, s, 4 * LANE), lambda bb, i, fl: (bb, 0, 0)),
                      pl.BlockSpec((None, s, 2 * LANE), lambda bb, i, fl: (bb, 0, 0)),
                      pl.BlockSpec((1, NSA_GROUPS, t, LANE), lambda bb, i, fl: (bb, 0, i, 0)),
                      pl.BlockSpec((1, t, LANE), imap_q)],
            out_specs=pl.BlockSpec((1, t, OUT_W), imap_q),
            scratch_shapes=[pltpu.VMEM((NSA_GROUPS, rows, 2 * LANE), BF16), pltpu.VMEM((NSA_GROUPS, rows, LANE), F32),
                            pltpu.VMEM((NSA_GROUPS, rows, LANE), F32)],
        ),
        out_shape=jax.ShapeDtypeStruct((b, s, OUT_W), BF16),
        compiler_params=_params(("parallel", "arbitrary")),
        name="nsa_selected",
    )(flags, qn, ks, vs, selm1, gates)


def _win_attn_kernel(q_ref, kp_ref, kc_ref, vp_ref, vc_ref, gate_ref, o_ref, *, tq):
    qi = pl.program_id(1)
    hq = tq // 2
    span = WIN + hq
    row = _iota((hq, span), 0)
    col = _iota((hq, span), 1)
    diff = WIN + row - col
    band = (diff >= 0) & (diff < WIN)
    gates = gate_ref[0]
    chains = [(g, rh) for g in range(NSA_GROUPS) for rh in range(2)]
    scores = []
    for g, rh in chains:
        gl = slice(g * LANE, (g + 1) * LANE)
        q = q_ref[g * NSA_HPG:(g + 1) * NSA_HPG, rh * hq:(rh + 1) * hq].reshape(NSA_HPG * hq, LANE)
        s = jnp.concatenate([_dot_nt(q, kp_ref[0, rh * hq:, gl]), _dot_nt(q, kc_ref[0, :(rh + 1) * hq, gl])], axis=1)
        valid = (band & (col >= WIN - qi * tq - rh * hq))[None]
        scores.append(jnp.where(valid, s.reshape(NSA_HPG, hq, span), NEG).reshape(NSA_HPG * hq, span))
    tops = [jnp.max(s, axis=-1, keepdims=True) for s in scores]
    probs = [jnp.exp2(s - m).astype(BF16) for s, m in zip(scores, tops)]
    for (g, rh), p in zip(chains, probs):
        gl = slice(g * LANE, (g + 1) * LANE)
        n_prev = tq - rh * hq
        acc = _dot(p[:, :n_prev], vp_ref[0, rh * hq:, gl]) + _dot(p[:, n_prev:], vc_ref[0, :(rh + 1) * hq, gl])
        rows = slice(rh * hq, (rh + 1) * hq)
        cols = [2 * NSA_HEADS + g * NSA_HPG + n for n in range(NSA_HPG)]
        heads = [_normalized(acc[n * hq:(n + 1) * hq]) * gates[rows, cols[n]:cols[n] + 1] for n in range(NSA_HPG)]
        _store_head_pairs(o_ref, rows, heads, first_pair=g * NSA_HPG // 2)


def nsa_window(qn, kw, vw, gates, tq=512):
    _, b, s, _ = qn.shape
    assert tq == WIN and s % tq == 0
    prev = pl.BlockSpec((1, tq, 2 * LANE), lambda bb, i: (bb, jnp.maximum(i - 1, 0), 0))
    cur = pl.BlockSpec((1, tq, 2 * LANE), lambda bb, i: (bb, i, 0))
    return pl.pallas_call(
        functools.partial(_win_attn_kernel, tq=tq),
        grid=(b, s // tq),
        in_specs=[pl.BlockSpec((NSA_HEADS, None, tq, LANE), lambda bb, i: (0, bb, i, 0)), prev, cur, prev, cur,
                  pl.BlockSpec((1, tq, LANE), lambda bb, i: (bb, i, 0))],
        out_specs=pl.BlockSpec((1, tq, OUT_W), lambda bb, i: (bb, i, 0)),
        out_shape=jax.ShapeDtypeStruct((b, s, OUT_W), BF16),
        compiler_params=_params(("parallel", "parallel")),
        name="nsa_window",
    )(qn, kw, kw, vw, vw, gates)


def _even_out_ln_kernel(x_ref, om_ref, oc_ref, os_ref, ow_ref, wm_ref, wn_ref, g_ref, b_ref, o_ref):
    nsa = (oc_ref[...].astype(F32) + os_ref[...].astype(F32) + ow_ref[...].astype(F32)).astype(BF16)
    mix = _dot(om_ref[...], wm_ref[...]) + _dot(nsa, wn_ref[...])
    o_ref[...] = _layer_norm(ALPHA * x_ref[...] + mix, g_ref[...], b_ref[...])


def even_out_ln(x2, o_mla, o_c, o_s, o_w, w_mla, w_nsa, g, b, tm=512):
    m, d = x2.shape
    tm = min(tm, m)
    row = lambda w: pl.BlockSpec((tm, w), lambda i: (i, 0))
    return pl.pallas_call(
        _even_out_ln_kernel,
        grid=(m // tm,),
        in_specs=[row(d)] + [row(OUT_W)] * 4 + [_const_spec(w_mla.shape), _const_spec(w_nsa.shape),
                                             _const_spec((1, d)), _const_spec((1, d))],
        out_specs=row(d),
        out_shape=jax.ShapeDtypeStruct((m, d), F32),
        compiler_params=_params(("parallel",)),
        name="even_out_ln",
    )(x2, o_mla, o_c, o_s, o_w, w_mla, w_nsa, g.reshape(1, d), b.reshape(1, d))


def _odd_out_ln_kernel(x_ref, o0_ref, o1_ref, o2_ref, l0_ref, l1_ref, l2_ref, w_ref, g_ref, b_ref, o_ref,
                       o_scr, l_scr):
    tm = x_ref.shape[0]
    for gi, (src, lsrc) in enumerate(((o0_ref, l0_ref), (o1_ref, l1_ref), (o2_ref, l2_ref))):
        dil = DIL_PATTERNS[gi][1]
        for r in range(dil):
            rows = pl.ds(r, tm // dil, stride=dil) if dil > 1 else slice(None)
            for h in range(DIL_GH):
                o_scr[gi, h, rows, :] = src[r, :, h * DIL_DH:(h + 1) * DIL_DH].astype(F32)
            l_scr[gi, rows, :] = lsrc[r]
    lses = [l_scr[gi] for gi in range(DIL_GROUPS)]
    top = jnp.maximum(jnp.maximum(lses[0], lses[1]), lses[2])
    es = [jnp.exp(l - top) for l in lses]
    den = es[0] + es[1] + es[2]
    wts = [e / den for e in es]
    cols = []
    for h in range(DIL_GH):
        merged = None
        for gi in range(DIL_GROUPS):
            term = wts[gi][:, h:h + 1] * o_scr[gi, h]
            merged = term if merged is None else merged + term
        cols.append(merged.astype(BF16))
    mix = _dot(jnp.concatenate(cols, axis=1), w_ref[...])
    o_ref[...] = _layer_norm(ALPHA * x_ref[...] + mix, g_ref[...], b_ref[...])


def odd_out_ln(x2, outs, lses, w, g, b, tm=512):
    m, d = x2.shape
    b_, _, s = outs[0].shape[0], None, outs[0].shape[1] * outs[0].shape[2]
    tm = min(tm, s)
    width = DIL_GH * DIL_DH
    row = pl.BlockSpec((None, tm, d), lambda bb, i: (bb, i, 0))
    cls = lambda gi, wd: pl.BlockSpec((None, DIL_PATTERNS[gi][1], tm // DIL_PATTERNS[gi][1], wd),
                                      lambda bb, i: (bb, 0, i, 0))
    out = pl.pallas_call(
        _odd_out_ln_kernel,
        grid=(b_, s // tm),
        in_specs=[row] + [cls(gi, width) for gi in range(DIL_GROUPS)] + [cls(gi, LANE) for gi in range(DIL_GROUPS)]
        + [_const_spec(w.shape), _const_spec((1, d)), _const_spec((1, d))],
        out_specs=row,
        out_shape=jax.ShapeDtypeStruct((b_, s, d), F32),
        scratch_shapes=[pltpu.VMEM((DIL_GROUPS, DIL_GH, tm, DIL_DH), F32), pltpu.VMEM((DIL_GROUPS, tm, LANE), F32)],
        compiler_params=_params(("parallel", "parallel")),
        name="odd_out_ln",
    )(x2.reshape(b_, s, d), *outs, *lses, w, g.reshape(1, d), b.reshape(1, d))
    return out.reshape(m, d)


def _odd_proj_kernel(x_ref, w_ref, o0_ref, o1_ref, o2_ref, xs_ref):
    tm, d_model = x_ref.shape
    width = DIL_GH * DIL_DH
    n_chunks = d_model // LANE
    for c in range(n_chunks):
        xs_ref[c] = x_ref[:, c * LANE:(c + 1) * LANE]
    for gi, o_ref in enumerate((o0_ref, o1_ref, o2_ref)):
        dil = DIL_PATTERNS[gi][1]
        if dil == 1:
            xg = x_ref[...]
        else:
            xg = jnp.concatenate(
                [jnp.concatenate([xs_ref[c, pl.ds(r, tm // dil, stride=dil), :] for r in range(dil)], axis=0)
                 for c in range(n_chunks)], axis=1)
        xg = xg.astype(BF16)
        for c in range(3):
            col = (gi * 3 + c) * width
            y = _dot(xg, w_ref[:, col:col + width])
            if c == 0:
                y = y * (DIL_SCALE * LOG2E)
            o_ref[:, :, c * width:(c + 1) * width] = y.astype(BF16).reshape(dil, tm // dil, width)


def odd_proj(x2, b, s, w_in, tm=512):
    m, d = x2.shape
    tm = min(tm, s)
    width = DIL_GH * DIL_DH
    w = w_in.reshape(d, 3, DIL_GROUPS, width).transpose(0, 2, 1, 3).reshape(d, 3 * DIL_GROUPS * width).astype(BF16)
    dils = [dil for _, dil in DIL_PATTERNS]
    return pl.pallas_call(
        _odd_proj_kernel,
        grid=(b, s // tm),
        in_specs=[pl.BlockSpec((None, tm, d), lambda bb, i: (bb, i, 0)), _const_spec(w.shape)],
        out_specs=[pl.BlockSpec((None, dil, tm // dil, 3 * width), lambda bb, i: (bb, 0, i, 0)) for dil in dils],
        out_shape=[jax.ShapeDtypeStruct((b, dil, s // dil, 3 * width), BF16) for dil in dils],
        scratch_shapes=[pltpu.VMEM((d // LANE, tm, LANE), F32)],
        compiler_params=_params(("parallel", "parallel")),
        name="odd_proj",
    )(x2.reshape(b, s, d), w)


def _dil_attn_kernel(slope_tab, pos0_tab, q_ref, kc_ref, kp_ref, vc_ref, vp_ref, pc_ref, pp_ref, pq_ref, o_ref,
                     lse_ref, *, group, tq):
    sub = DIL_SPAN
    b = pl.program_id(0)
    jt = pl.program_id(2)
    row = _iota((sub, 2 * sub), 0)
    col = _iota((sub, 2 * sub), 1)
    band = (col >= row) & (col <= row + sub)
    band_bias = jnp.where(band, 0.0, NEG)
    start_bias = jnp.where(band & (col >= sub), 0.0, NEG)
    pos0 = pos0_tab[b]
    lane = _iota((sub, LANE), 1)
    heads = range(DIL_GH)
    hs = [slice(h * DIL_DH, (h + 1) * DIL_DH) for h in heads]
    slopes = [slope_tab[group * DIL_GH + h] for h in heads]
    tiles = range(tq // sub)
    cur = [slice(i * sub, (i + 1) * sub) for i in tiles]
    biases, ks, vs = [], [], []
    for i in tiles:
        if i == 0:
            mask_bias = jnp.where(jt == 0, start_bias, band_bias)
            pk = jnp.concatenate([pp_ref[...], pc_ref[:, :sub]], axis=1)
            ks.append([jnp.concatenate([kp_ref[:, hs[h]], kc_ref[:sub, hs[h]]], axis=0) for h in heads])
            vs.append([jnp.concatenate([vp_ref[:, hs[h]], vc_ref[:sub, hs[h]]], axis=0) for h in heads])
        else:
            mask_bias = band_bias
            pk = pc_ref[:, (i - 1) * sub:(i + 1) * sub]
            ks.append([kc_ref[(i - 1) * sub:(i + 1) * sub, hs[h]] for h in heads])
            vs.append([vc_ref[(i - 1) * sub:(i + 1) * sub, hs[h]] for h in heads])
        dpos = pk - pos0
        biases.append([mask_bias + slopes[h] * dpos for h in heads])
    chains = [(i, h) for i in tiles for h in heads]
    ss = [_dot_nt(q_ref[cur[i], hs[h]], ks[i][h]) + biases[i][h] for i, h in chains]
    ms = [jnp.max(s, axis=-1, keepdims=True) for s in ss]
    ps = [jnp.exp2(s - m) for s, m in zip(ss, ms)]
    dens = [jnp.sum(p, axis=-1, keepdims=True) for p in ps]
    for n, (i, h) in enumerate(chains):
        o_ref[cur[i], hs[h]] = (_dot(ps[n].astype(BF16), vs[i][h]) / dens[n]).astype(BF16)
    for i in tiles:
        dq = (pq_ref[cur[i], :] - pos0) * LN2
        lse_tile = jnp.zeros((sub, LANE), F32)
        for h in heads:
            n = i * DIL_GH + h
            lse_tile = jnp.where(lane == h, ms[n] * LN2 + jnp.log(dens[n]) - slopes[h] * dq, lse_tile)
        lse_ref[cur[i], :] = lse_tile


def dilated_group_attention(qkv, pos_f, group, tq=512):
    b, d, sd, _ = qkv.shape
    w, dil = DIL_PATTERNS[group]
    assert w // dil == DIL_SPAN and dil == d
    tq = min(tq, sd)
    sub = DIL_SPAN
    width = DIL_GH * DIL_DH
    n_slopes = DIL_GROUPS * DIL_GH
    slopes = 2.0 ** (-ALIBI_MAX_BIAS * jnp.arange(1, n_slopes + 1, dtype=F32) / n_slopes) * LOG2E
    pos0 = pos_f[:, 0]
    posc = pos_f.reshape(b, sd, d).transpose(0, 2, 1).reshape(b, d, 1, sd)
    r = tq // sub
    cur = lambda c: pl.BlockSpec((None, None, tq, width), lambda bb, rr, j, *_: (bb, rr, j, c))
    prev = lambda c: pl.BlockSpec((None, None, sub, width),
                                  lambda bb, rr, j, *_: (bb, rr, jnp.maximum(j * r - 1, 0), c))
    return pl.pallas_call(
        functools.partial(_dil_attn_kernel, group=group, tq=tq),
        grid_spec=pltpu.PrefetchScalarGridSpec(
            num_scalar_prefetch=2,
            grid=(b, d, sd // tq),
            in_specs=[cur(0), cur(1), prev(1), cur(2), prev(2),
                      pl.BlockSpec((None, None, 1, tq), lambda bb, rr, j, *_: (bb, rr, 0, j)),
                      pl.BlockSpec((None, None, 1, sub), lambda bb, rr, j, *_: (bb, rr, 0, jnp.maximum(j * r - 1, 0))),
                      pl.BlockSpec((None, None, tq, 1), lambda bb, rr, j, *_: (bb, rr, j, 0))],
            out_specs=[pl.BlockSpec((None, None, tq, width), lambda bb, rr, j, *_: (bb, rr, j, 0)),
                       pl.BlockSpec((None, None, tq, LANE), lambda bb, rr, j, *_: (bb, rr, j, 0))],
        ),
        out_shape=[jax.ShapeDtypeStruct((b, d, sd, width), BF16), jax.ShapeDtypeStruct((b, d, sd, LANE), F32)],
        compiler_params=_params(("parallel", "parallel", "parallel")),
        name=f"dilated_attention_g{group}",
    )(slopes, pos0, qkv, qkv, qkv, qkv, qkv, posc, posc, posc.reshape(b, d, sd, 1))


def even_mixer_ln(x2, b, s, pos_f, w_in, q_norm_g, kv_norm_g, w_uq, w_uk, w_uv,
                  cmp_pos, cmp_k_w1, cmp_k_w2, cmp_v_w1, cmp_v_w2, w_out, ln_g, ln_b):
    m = b * s
    d = x2.shape[1]
    half = MLA_ROPE // 2
    inv = ROPE_THETA ** (-jnp.arange(half, dtype=F32) / half)
    ang = (pos_f[..., None] * inv).reshape(m, half)
    ones = jnp.ones((m, MLA_NOPE), F32)
    zeros = jnp.zeros((m, LANE - MLA_NOPE - MLA_ROPE), F32)
    cos_t = jnp.concatenate([ones, jnp.cos(ang), jnp.cos(ang), zeros], axis=1)
    sin_t = jnp.concatenate([0.0 * ones, jnp.sin(ang), jnp.sin(ang), zeros], axis=1)
    dpos = (pos_f - pos_f[:, :1]) * LOG2E
    ncp = s // CMP_STRIDE
    dpos_cend = jnp.pad(dpos[:, CMP_LEN - 1::CMP_STRIDE], ((0, 0), (0, 1))).reshape(b, ncp, 1)

    qm, km, vm, qn, kvc, ks, kw, vs, vw, gates = even_proj(
        x2, cos_t, sin_t, dpos.reshape(m, 1), s, w_in, q_norm_g, kv_norm_g, w_uq, w_uk, w_uv)
    o_mla = mla_attention(qm.reshape(b, s, HW), km.reshape(b, s, HW), vm.reshape(b, s, HW))

    kvc = kvc.reshape(2, b, ncp, CMP_STRIDE * LANE)
    kc = nsa_compress(kvc[0], cmp_pos, cmp_k_w1, cmp_k_w2, dpos_cend, is_value=False)
    vc = nsa_compress(kvc[1], cmp_pos, cmp_v_w1, cmp_v_w2, dpos_cend, is_value=True)
    qn = qn.reshape(NSA_HEADS, b, s, LANE)
    gates = gates.reshape(b, s, LANE)
    o_c, selm1 = nsa_cmp_topk(qn, kc, vc, gates)
    o_s = nsa_selected(qn, ks.reshape(b, s, 4 * LANE), vs.reshape(b, s, 2 * LANE), selm1, gates)
    o_w = nsa_window(qn, kw.reshape(b, s, 2 * LANE), vw.reshape(b, s, 2 * LANE), gates)

    w_mla = w_out[:OUT_W].astype(BF16)
    w_nsa = w_out[OUT_W:].astype(BF16)
    return even_out_ln(x2, o_mla.reshape(m, OUT_W), o_c.reshape(m, OUT_W), o_s.reshape(m, OUT_W),
                       o_w.reshape(m, OUT_W), w_mla, w_nsa, ln_g, ln_b)


def odd_mixer_ln(x2, b, s, pos_f, w_in, w_out, ln_g, ln_b):
    qkvs = odd_proj(x2, b, s, w_in)
    parts = [dilated_group_attention(qkvs[g], pos_f, g) for g in range(DIL_GROUPS)]
    return odd_out_ln(x2, [p[0] for p in parts], [p[1] for p in parts], w_out.astype(BF16), ln_g, ln_b)


def kernel(x, positions, ln1_g, ln1_b, ffn1_w_gate, ffn1_w_up, ffn1_w_down, mix_in_even, mla_q_norm, mla_kv_norm, mla_w_uq, mla_w_uk, mla_w_uv, nsa_cmp_pos, nsa_cmp_k_w1, nsa_cmp_k_w2, nsa_cmp_v_w1, nsa_cmp_v_w2, mix_out_even, mix_in_odd, mix_out_odd, ln2_g, ln2_b, ffn2_w_gate, ffn2_w_up, ffn2_w_down, ln3_g, ln3_b):
    b, s, d = x.shape
    x2 = x.reshape(b * s, d)
    pos_f = positions.astype(F32)
    for i in range(DEPTH):
        j = i // 2
        x2 = ffn_ln(x2, ffn1_w_gate[i].astype(BF16), ffn1_w_up[i].astype(BF16), ffn1_w_down[i].astype(BF16),
                    ln1_g[i], ln1_b[i])
        if i % 2 == 0:
            x2 = even_mixer_ln(x2, b, s, pos_f, mix_in_even[j], mla_q_norm[j], mla_kv_norm[j], mla_w_uq[j],
                               mla_w_uk[j], mla_w_uv[j], nsa_cmp_pos[j], nsa_cmp_k_w1[j], nsa_cmp_k_w2[j],
                               nsa_cmp_v_w1[j], nsa_cmp_v_w2[j], mix_out_even[j], ln2_g[i], ln2_b[i])
        else:
            x2 = odd_mixer_ln(x2, b, s, pos_f, mix_in_odd[j], mix_out_odd[j], ln2_g[i], ln2_b[i])
        x2 = ffn_ln(x2, ffn2_w_gate[i].astype(BF16), ffn2_w_up[i].astype(BF16), ffn2_w_down[i].astype(BF16),
                    ln3_g[i], ln3_b[i])
    return x2.reshape(b, s, d)
```

```python
import functools
import math

import numpy as np
import jax
import jax.numpy as jnp
from jax import lax
from jax.experimental import pallas as pl
from jax.experimental.pallas import tpu as pltpu

F32 = jnp.float32
BF16 = jnp.bfloat16

DEPTH = 2
LN_EPS = 1e-5
RMS_EPS = 1e-6
ALPHA = (2 * DEPTH) ** 0.25
HALF_STEP = 0.5
NEG = -1e30
BIG = 1e9
MASK_BIG = 1e30
REMOVED = -3.0e38
ALIBI_MAX_BIAS = 8.0
LOG2E = math.log2(math.e)
LN2 = math.log(2.0)
LANE = 128

MLA_HEADS = 8
MLA_Q_RANK = 384
MLA_KV_RANK = 256
MLA_NOPE = 64
MLA_ROPE = 32
MLA_V = 64
ROPE_THETA = 10000.0
MLA_SCALE = (MLA_NOPE + MLA_ROPE) ** -0.5
MLA_CHAINS = 4

NSA_HEADS = 8
NSA_GROUPS = 2
NSA_HPG = 4
NSA_DH = 64
CMP_LEN = 32
CMP_STRIDE = 16
CMP_HIDDEN = 256
SEL_BLOCK = 64
SEL_TOPK = 16
WIN = 512
NSA_SCALE = NSA_DH ** -0.5
NSA_SLOPES = tuple(2.0 ** (-ALIBI_MAX_BIAS * (i + 1) / NSA_HEADS) for i in range(NSA_HEADS))
ONES_LANE = 64
BIAS_LANES = (64, 65, 66)

DIL_PATTERNS = ((128, 1), (512, 4), (2048, 16))
DIL_GROUPS = 3
DIL_GH = 4
DIL_DH = 128
DIL_SCALE = DIL_DH ** -0.5
DIL_SPAN = 128

VMEM_LIMIT = 48 * 1024 * 1024
MIXED_FFN_VMEM_LIMIT = 56 * 1024 * 1024


def _iota(shape, dim):
    return lax.broadcasted_iota(jnp.int32, shape, dim)


def _shr(x, pow2):
    return jnp.right_shift(x, int(pow2).bit_length() - 1)


def _dot(a, b):
    return jnp.dot(a, b, preferred_element_type=F32)


def _dot_nt(a, b):
    return lax.dot_general(a, b, (((1,), (1,)), ((), ())), preferred_element_type=F32)


def _const_spec(shape):
    zeros = (0,) * len(shape)
    return pl.BlockSpec(shape, lambda *_: zeros, pipeline_mode=pl.Buffered(1))


def _params(sem):
    return pltpu.CompilerParams(dimension_semantics=sem, vmem_limit_bytes=VMEM_LIMIT)


def _layer_norm(z, g, b):
    mu = jnp.mean(z, axis=-1, keepdims=True)
    zc = z - mu
    var = jnp.mean(zc * zc, axis=-1, keepdims=True)
    return zc * lax.rsqrt(var + LN_EPS) * g + b


def _rms_norm(z, g):
    return z * lax.rsqrt(jnp.mean(z * z, axis=-1, keepdims=True) + RMS_EPS) * g


def _bias_pieces(d, lane):
    x = jnp.broadcast_to(d, lane.shape)
    hi = x.astype(BF16).astype(F32)
    r = x - hi
    mid = r.astype(BF16).astype(F32)
    lo = (r - mid).astype(BF16).astype(F32)
    return jnp.where(lane == BIAS_LANES[0], hi,
                     jnp.where(lane == BIAS_LANES[1], mid, jnp.where(lane == BIAS_LANES[2], lo, 0.0)))


def _flash_update(s, v, m_ref, acc_ref, idx):
    m_old = m_ref[idx]
    m_new = jnp.maximum(m_old, jnp.max(s, axis=-1, keepdims=True))
    p = jnp.exp2(s - jnp.tile(m_new, (1, s.shape[1] // LANE))).astype(BF16)
    acc_ref[idx] = jnp.exp2(m_old - m_new) * acc_ref[idx] + _dot(p, v)
    m_ref[idx] = m_new


def _flash_update_staged(scores, values, m_ref, acc_ref, idxs):
    m_old = [m_ref[i] for i in idxs]
    m_new = [jnp.maximum(mo, jnp.max(s, axis=-1, keepdims=True)) for mo, s in zip(m_old, scores)]
    ps = [jnp.exp2(s - jnp.tile(mn, (1, s.shape[1] // LANE))).astype(BF16) for s, mn in zip(scores, m_new)]
    for i, mo, mn, p, v in zip(idxs, m_old, m_new, ps, values):
        acc_ref[i] = jnp.exp2(mo - mn) * acc_ref[i] + _dot(p, v)
        m_ref[i] = mn


def _init_state(m_ref, acc_ref):
    m_ref[...] = jnp.full(m_ref.shape, NEG, F32)
    acc_ref[...] = jnp.zeros(acc_ref.shape, F32)


def _normalized(acc):
    lane = _iota(acc.shape, 1)
    o = acc / jnp.maximum(acc[:, ONES_LANE:ONES_LANE + 1], 1e-30)
    return jnp.where(lane < ONES_LANE, o, 0.0)


def _store_head_pairs(o_ref, rows, heads, first_pair=0):
    for pr in range(len(heads) // 2):
        packed = heads[2 * pr] + pltpu.roll(heads[2 * pr + 1], LANE // 2, 1)
        o_ref[0, rows, (first_pair + pr) * LANE:(first_pair + pr + 1) * LANE] = packed.astype(BF16)


def _ffn_ln_kernel(*refs, n_chunks, mixed):
    if mixed:
        x_ref, om_ref, oc_ref, os_ref, ow_ref, wm_ref, wn_ref, g2_ref, b2_ref = refs[:9]
        nsa = (oc_ref[...].astype(F32) + os_ref[...].astype(F32) + ow_ref[...].astype(F32)).astype(BF16)
        mix = _dot(om_ref[...], wm_ref[...]) + _dot(nsa, wn_ref[...])
        x = _layer_norm(ALPHA * x_ref[...] + mix, g2_ref[...], b2_ref[...])
        refs = refs[9:]
    else:
        x = refs[0][...]
        refs = refs[1:]
    wg_ref, wu_ref, wd_ref, g_ref, b_ref, o_ref = refs
    xb = x.astype(BF16)
    c = wg_ref.shape[1] // n_chunks
    y = None
    for i in range(n_chunks):
        gt = _dot(xb, wg_ref[:, i * c:(i + 1) * c])
        up = _dot(xb, wu_ref[:, i * c:(i + 1) * c])
        h = (gt * jax.nn.sigmoid(gt) * up).astype(BF16)
        part = _dot(h, wd_ref[i * c:(i + 1) * c, :])
        y = part if y is None else y + part
    o_ref[...] = _layer_norm(ALPHA * x + HALF_STEP * y, g_ref[...], b_ref[...])


def ffn_ln(x2, wg, wu, wd, g, b, mix=None, tm=1024, n_chunks=11):
    m, d = x2.shape
    tm = min(tm, m)
    row = lambda w: pl.BlockSpec((tm, w), lambda i: (i, 0))
    args, specs = [x2], [row(d)]
    if mix is not None:
        outs, ws, g2, b2 = mix
        args += list(outs) + list(ws) + [g2.reshape(1, d), b2.reshape(1, d)]
        specs += [row(o.shape[1]) for o in outs] + [_const_spec(w.shape) for w in ws] + [_const_spec((1, d))] * 2
    args += [wg, wu, wd, g.reshape(1, d), b.reshape(1, d)]
    specs += [_const_spec(wg.shape), _const_spec(wu.shape), _const_spec(wd.shape), _const_spec((1, d)), _const_spec((1, d))]
    limit = VMEM_LIMIT if mix is None else MIXED_FFN_VMEM_LIMIT
    return pl.pallas_call(
        functools.partial(_ffn_ln_kernel, n_chunks=n_chunks, mixed=mix is not None),
        grid=(m // tm,),
        in_specs=specs,
        out_specs=row(d),
        out_shape=jax.ShapeDtypeStruct((m, d), F32),
        compiler_params=pltpu.CompilerParams(dimension_semantics=("parallel",), vmem_limit_bytes=limit),
        name="ffn_ln" if mix is None else "mix_ffn_ln",
    )(*args)


HW = MLA_HEADS * LANE
OUT_W = MLA_HEADS * MLA_V
EVEN_X_COLS = (MLA_Q_RANK, MLA_KV_RANK, LANE, LANE, HW, 2 * LANE, 8 * LANE, LANE)
EVEN_X_OFFS = tuple(int(v) for v in np.cumsum((0,) + EVEN_X_COLS))


def _even_proj_kernel(x_ref, cos_ref, sin_ref, dpos_ref, wx_ref, qg_ref, kvg_ref, wuq_ref, wuqs_ref, wuk_ref,
                      wuv_ref, slope_ref, qm_ref, km_ref, vm_ref, qn_ref, kvc_ref, ks_ref, kw_ref, vs_ref, vw_ref,
                      gate_ref, kvc_scr, *, tiles_per_seq):
    tm = x_ref.shape[0]
    xb = x_ref[...].astype(BF16)
    cos = cos_ref[...]
    sin = sin_ref[...]
    lane = _iota((tm, LANE), 1)
    ones_lane = jnp.where(lane == ONES_LANE, 1.0, 0.0)
    pos_term = _bias_pieces(dpos_ref[...], lane)
    tok = (pl.program_id(0) % tiles_per_seq) * tm + _iota((tm, LANE), 0)
    block_onehot = jnp.where(lane == _shr(tok, SEL_BLOCK), MASK_BIG, 0.0).astype(BF16)

    def xdot(i):
        return _dot(xb, wx_ref[:, EVEN_X_OFFS[i]:EVEN_X_OFFS[i + 1]])

    cq = _rms_norm(xdot(0), qg_ref[...]).astype(BF16)
    ckv = _rms_norm(xdot(1), kvg_ref[...]).astype(BF16)
    k_rot = xdot(2) * cos + xdot(3) * sin
    for h in range(MLA_HEADS):
        sl = slice(h * LANE, (h + 1) * LANE)
        q = _dot(cq, wuq_ref[:, sl]) * cos + _dot(cq, wuqs_ref[:, sl]) * sin
        qm_ref[:, sl] = (q * (MLA_SCALE * LOG2E)).astype(BF16)
        km_ref[:, sl] = (_dot(ckv, wuk_ref[:, sl]) + k_rot).astype(BF16)
        vm_ref[:, sl] = (_dot(ckv, wuv_ref[:, sl]) + ones_lane).astype(BF16)
    qn = xdot(4) * (NSA_SCALE * LOG2E) + slope_ref[...]
    for h in range(NSA_HEADS):
        qn_ref[h] = qn[:, h * LANE:(h + 1) * LANE].astype(BF16)
    kvc = xdot(5)
    for j in range(2):
        kvc_scr[j] = kvc[:, j * LANE:(j + 1) * LANE]
        for l in range(CMP_STRIDE):
            kvc_ref[j, :, l * LANE:(l + 1) * LANE] = kvc_scr[j, pl.ds(l, tm // CMP_STRIDE, stride=CMP_STRIDE), :].astype(BF16)
    kv8 = xdot(6)
    blk = lambda i: kv8[:, i * LANE:(i + 1) * LANE]
    for g in range(NSA_GROUPS):
        ks_ref[:, 2 * g * LANE:(2 * g + 1) * LANE] = (blk(g) + pos_term).astype(BF16)
        ks_ref[:, (2 * g + 1) * LANE:(2 * g + 2) * LANE] = block_onehot
        kw_ref[:, g * LANE:(g + 1) * LANE] = (blk(2 + g) + pos_term).astype(BF16)
        vs_ref[:, g * LANE:(g + 1) * LANE] = (blk(4 + g) + ones_lane).astype(BF16)
        vw_ref[:, g * LANE:(g + 1) * LANE] = (blk(6 + g) + ones_lane).astype(BF16)
    gate_ref[...] = jax.nn.sigmoid(xdot(7))


def _head_blocks(w, n_heads, width):
    k = w.shape[0]
    w = jnp.pad(w.reshape(k, n_heads, width), ((0, 0), (0, 0), (0, LANE - width)))
    return w.reshape(k, n_heads * LANE)


def even_proj(x2, cos_t, sin_t, dpos, seq, w_in, q_norm_g, kv_norm_g, w_uq, w_uk, w_uv, tm=256):
    m, d = x2.shape
    tm = min(tm, seq)
    half = MLA_ROPE // 2
    gw = NSA_GROUPS * NSA_DH
    cuts = np.cumsum((MLA_Q_RANK, MLA_KV_RANK, MLA_ROPE, NSA_HEADS * NSA_DH) + (gw,) * 6)
    cuts = [0] + [int(c) for c in cuts]
    w_cq, w_ckv, w_kpe, w_q = (w_in[:, cuts[i]:cuts[i + 1]] for i in range(4))
    w_kc, w_vc, w_ks, w_vs, w_kw, w_vw = (w_in[:, cuts[4 + i]:cuts[5 + i]] for i in range(6))
    w_gate = w_in[:, cuts[10]:]
    w_kpe_sw = jnp.concatenate([-w_kpe[:, half:], w_kpe[:, :half]], axis=1)
    rope_pad = ((0, 0), (MLA_NOPE, LANE - MLA_NOPE - MLA_ROPE))
    w_gate_blk = jnp.pad(w_gate, ((0, 0), (0, LANE - w_gate.shape[1])))
    wx = jnp.concatenate(
        [w_cq, w_ckv, jnp.pad(w_kpe, rope_pad), jnp.pad(w_kpe_sw, rope_pad), _head_blocks(w_q, NSA_HEADS, NSA_DH),
         w_kc, w_vc] + [_head_blocks(w, NSA_GROUPS, NSA_DH) for w in (w_ks, w_kw, w_vs, w_vw)] + [w_gate_blk],
        axis=1).astype(BF16)

    qd = MLA_NOPE + MLA_ROPE
    uq = w_uq.reshape(MLA_Q_RANK, MLA_HEADS, qd)
    uq_sw = jnp.concatenate([jnp.zeros_like(uq[..., :MLA_NOPE]), -uq[..., MLA_NOPE + half:],
                             uq[..., MLA_NOPE:MLA_NOPE + half]], axis=-1)
    wuq = _head_blocks(uq.reshape(MLA_Q_RANK, -1), MLA_HEADS, qd).astype(BF16)
    wuqs = _head_blocks(uq_sw.reshape(MLA_Q_RANK, -1), MLA_HEADS, qd).astype(BF16)
    wuk = _head_blocks(w_uk, MLA_HEADS, MLA_NOPE).astype(BF16)
    wuv = _head_blocks(w_uv, MLA_HEADS, MLA_V).astype(BF16)
    slope_row = np.zeros((1, HW), np.float32)
    for h in range(NSA_HEADS):
        for ln in BIAS_LANES:
            slope_row[0, h * LANE + ln] = NSA_SLOPES[h]

    row = lambda w: pl.BlockSpec((tm, w), lambda i: (i, 0))
    sds = jax.ShapeDtypeStruct
    return pl.pallas_call(
        functools.partial(_even_proj_kernel, tiles_per_seq=seq // tm),
        grid=(m // tm,),
        in_specs=[row(d), row(LANE), row(LANE), row(1), _const_spec(wx.shape), _const_spec((1, MLA_Q_RANK)),
                  _const_spec((1, MLA_KV_RANK)), _const_spec(wuq.shape), _const_spec(wuqs.shape),
                  _const_spec(wuk.shape), _const_spec(wuv.shape), _const_spec((1, HW))],
        out_specs=[row(HW), row(HW), row(HW), pl.BlockSpec((NSA_HEADS, tm, LANE), lambda i: (0, i, 0)),
                   pl.BlockSpec((2, tm // CMP_STRIDE, CMP_STRIDE * LANE), lambda i: (0, i, 0)), row(4 * LANE),
                   row(2 * LANE), row(2 * LANE), row(2 * LANE), row(LANE)],
        out_shape=[sds((m, HW), BF16)] * 3 + [sds((NSA_HEADS, m, LANE), BF16),
                                              sds((2, m // CMP_STRIDE, CMP_STRIDE * LANE), BF16),
                                              sds((m, 4 * LANE), BF16), sds((m, 2 * LANE), BF16),
                                              sds((m, 2 * LANE), BF16), sds((m, 2 * LANE), BF16), sds((m, LANE), F32)],
        scratch_shapes=[pltpu.VMEM((2, tm, LANE), F32)],
        compiler_params=_params(("parallel",)),
        name="even_proj",
    )(x2, cos_t, sin_t, dpos, wx, q_norm_g.reshape(1, -1), kv_norm_g.reshape(1, -1), wuq, wuqs, wuk, wuv,
      jnp.asarray(slope_row))


def _causal_pairs(nq, tq, tk):
    qi, ki = [], []
    for i in range(nq):
        for j in range(((i + 1) * tq - 1) // tk + 1):
            qi.append(i)
            ki.append(j)
    return jnp.asarray(qi, jnp.int32), jnp.asarray(ki, jnp.int32)


def _mla_kernel(qi_tab, ki_tab, q_ref, k_ref, v_ref, o_ref, m_ref, acc_ref, *, tq, tk):
    p = pl.program_id(1)
    qi = qi_tab[p]
    ki = ki_tab[p]

    @pl.when(ki == 0)
    def _():
        _init_state(m_ref, acc_ref)

    def run(rows, diagonal):
        if diagonal:
            mask = _iota((tk, tk), 0) >= _iota((tk, tk), 1)
        for h0 in range(0, MLA_HEADS, MLA_CHAINS):
            heads = range(h0, h0 + MLA_CHAINS)
            scores = [_dot_nt(q_ref[0, rows, h * LANE:(h + 1) * LANE], k_ref[0, :, h * LANE:(h + 1) * LANE])
                      for h in heads]
            if diagonal:
                scores = [jnp.where(mask, s, NEG) for s in scores]
            _flash_update_staged(scores, [v_ref[0, :, h * LANE:(h + 1) * LANE] for h in heads], m_ref, acc_ref,
                                 [(h, rows) for h in heads])

    blocks = tq // tk
    for r in range(blocks):
        rows = slice(r * tk, (r + 1) * tk)
        own = qi * blocks + r

        @pl.when(ki < own)
        def _(rows=rows):
            run(rows, False)

        @pl.when(ki == own)
        def _(rows=rows):
            run(rows, True)

    @pl.when(ki == ((qi + 1) * tq - 1) // tk)
    def _():
        _store_head_pairs(o_ref, slice(None), [_normalized(acc_ref[h]) for h in range(MLA_HEADS)])


def mla_attention(q, k, v, tq=1024, tk=512):
    b, s, hw = q.shape
    tq, tk = min(tq, s), min(tk, s)
    qi_tab, ki_tab = _causal_pairs(s // tq, tq, tk)
    qspec = pl.BlockSpec((1, tq, hw), lambda bb, p, qt, kt: (bb, qt[p], 0))
    kspec = pl.BlockSpec((1, tk, hw), lambda bb, p, qt, kt: (bb, kt[p], 0))
    return pl.pallas_call(
        functools.partial(_mla_kernel, tq=tq, tk=tk),
        grid_spec=pltpu.PrefetchScalarGridSpec(
            num_scalar_prefetch=2,
            grid=(b, int(qi_tab.shape[0])),
            in_specs=[qspec, kspec, kspec],
            out_specs=pl.BlockSpec((1, tq, OUT_W), lambda bb, p, qt, kt: (bb, qt[p], 0)),
            scratch_shapes=[pltpu.VMEM((MLA_HEADS, tq, LANE), F32), pltpu.VMEM((MLA_HEADS, tq, LANE), F32)],
        ),
        out_shape=jax.ShapeDtypeStruct((b, s, OUT_W), BF16),
        compiler_params=_params(("parallel", "arbitrary")),
        name="mla_attention",
    )(qi_tab, ki_tab, q, k, v)


def _compress_kernel(h_ref, pos_ref, w1_ref, w1g_ref, w2_ref, ext_ref, o_ref, *, is_value):
    n16 = h_ref.shape[1]
    lane = _iota((n16, LANE), 1)
    if is_value:
        extra = jnp.where(lane == ONES_LANE, 1.0, 0.0)
    else:
        extra = _bias_pieces(ext_ref[0], lane)
    bias = _dot(pos_ref[...], w1_ref[...])[0:1]
    h = h_ref[0]
    for g in range(NSA_GROUPS):
        first = _dot(h, w1g_ref[g, 0])
        second = _dot(h, w1g_ref[g, 1])
        hid = first + pltpu.roll(second, n16 - 1, 0) + bias
        act = jax.nn.gelu(hid).astype(BF16)
        o_ref[0, g] = (_dot(act, w2_ref[...]) + extra).astype(BF16)


def nsa_compress(h, cmp_pos, w1, w2, dpos_cend, is_value):
    b, n16, hw = h.shape
    pos = jnp.broadcast_to(cmp_pos.reshape(1, CMP_LEN * NSA_DH), (8, CMP_LEN * NSA_DH)).astype(BF16)
    w2p = jnp.pad(w2, ((0, 0), (0, LANE - NSA_DH))).astype(BF16)
    halves = w1.reshape(2, CMP_STRIDE, 1, NSA_DH, CMP_HIDDEN)
    w1g = jnp.stack([jnp.pad(halves, ((0, 0), (0, 0), (g, NSA_GROUPS - 1 - g), (0, 0), (0, 0))).reshape(2, hw, CMP_HIDDEN)
                     for g in range(NSA_GROUPS)]).astype(BF16)
    w1 = w1.astype(BF16)
    return pl.pallas_call(
        functools.partial(_compress_kernel, is_value=is_value),
        grid=(b,),
        in_specs=[pl.BlockSpec((1, n16, hw), lambda i: (i, 0, 0)), _const_spec(pos.shape),
                  _const_spec(w1.shape), _const_spec(w1g.shape), _const_spec(w2p.shape),
                  pl.BlockSpec((1, n16, 1), lambda i: (i, 0, 0))],
        out_specs=pl.BlockSpec((1, NSA_GROUPS, n16, LANE), lambda i: (i, 0, 0, 0)),
        out_shape=jax.ShapeDtypeStruct((b, NSA_GROUPS, n16, LANE), BF16),
        compiler_params=_params(("parallel",)),
        name="nsa_compress",
    )(h, pos, w1, w1g, w2p, dpos_cend)


def _topk_mask_t(x, k):
    n = x.shape[0]
    ridx = _iota(x.shape, 0).astype(F32)
    sel = jnp.zeros(x.shape, F32)
    for _ in range(k):
        m = jnp.max(x, axis=0, keepdims=True)
        first = jnp.min(jnp.where(x == m, ridx, float(n)), axis=0, keepdims=True)
        hit = ridx == first
        sel = jnp.where(hit, 1.0, sel)
        x = jnp.where(hit, REMOVED, x)
    return sel


def _cmp_topk_kernel(q_ref, kc_ref, vc_ref, gate_ref, oc_ref, selm1_ref, imp_ref, *, tq, ncp, topk):
    qi = pl.program_id(1)
    t = qi * tq + _iota((tq, 1), 0)
    gates = gate_ref[0]
    blk = _iota((tq, LANE), 1)
    chunk = _shr(t, SEL_BLOCK)

    def attend(nc):
        cend = _iota((1, nc), 1) * CMP_STRIDE + (CMP_LEN - 1)
        mask = (cend <= t)[None]
        cstart = _iota((nc, LANE), 0) * CMP_STRIDE
        sstart = _iota((nc, LANE), 1) * SEL_BLOCK
        overlap = jnp.where((cstart < sstart + SEL_BLOCK) & (cstart + CMP_LEN > sstart)
                            & (cstart < (ncp - 1) * CMP_STRIDE), 1.0, 0.0).astype(BF16)
        for g in range(NSA_GROUPS):
            q = q_ref[g * NSA_HPG:(g + 1) * NSA_HPG].reshape(NSA_HPG * tq, LANE)
            s = _dot_nt(q, kc_ref[0, g, :nc]).reshape(NSA_HPG, tq, nc)
            s = jnp.where(mask, s, NEG)
            e = jnp.where(mask, jnp.exp2(s - jnp.max(s, axis=-1, keepdims=True)), 0.0)
            p = e / jnp.maximum(jnp.sum(e, axis=-1, keepdims=True), 1e-30)
            psum = jnp.sum(p, axis=0)
            o = _dot(p.reshape(NSA_HPG * tq, nc).astype(BF16), vc_ref[0, g, :nc])
            heads = [jnp.where(blk < NSA_DH, o[n * tq:(n + 1) * tq] * gates[:, g * NSA_HPG + n:g * NSA_HPG + n + 1], 0.0)
                     for n in range(NSA_HPG)]
            _store_head_pairs(oc_ref, slice(None), heads, first_pair=g * NSA_HPG // 2)
            hi = psum.astype(BF16)
            r1 = psum - hi.astype(F32)
            mid = r1.astype(BF16)
            lo = (r1 - mid.astype(F32)).astype(BF16)
            imp_ref[g] = _dot(hi, overlap) + _dot(mid, overlap) + _dot(lo, overlap)

    lane_tiles = ((qi + 1) * tq // CMP_STRIDE + LANE - 1) // LANE
    for v in range(1, ncp // LANE + 1):
        @pl.when(jnp.minimum(lane_tiles, ncp // LANE) == v)
        def _(v=v):
            attend(v * LANE)

    forced = (blk == 0) | (blk == chunk)
    imps = [jnp.where(forced, REMOVED, jnp.where(blk <= chunk, imp_ref[g], NEG)).T for g in range(NSA_GROUPS)]
    picked = _topk_mask_t(jnp.concatenate(imps, axis=1), topk - 2)
    for g in range(NSA_GROUPS):
        sel = jnp.where(forced, 1.0, picked[:, g * tq:(g + 1) * tq].T)
        selm1_ref[0, g] = (jnp.where(blk <= chunk, sel, 0.0) - 1.0).astype(BF16)


def nsa_cmp_topk(qn, kc, vc, gates, tq=256):
    _, b, s, _ = qn.shape
    tq = min(tq, s)
    ncp = kc.shape[2]
    assert s // SEL_BLOCK <= LANE
    topk = min(SEL_TOPK, s // SEL_BLOCK)
    kspec = pl.BlockSpec((1, NSA_GROUPS, ncp, LANE), lambda i, j: (i, 0, 0, 0))
    return pl.pallas_call(
        functools.partial(_cmp_topk_kernel, tq=tq, ncp=ncp, topk=topk),
        grid=(b, s // tq),
        in_specs=[pl.BlockSpec((NSA_HEADS, None, tq, LANE), lambda i, j: (0, i, j, 0)), kspec, kspec,
                  pl.BlockSpec((1, tq, LANE), lambda i, j: (i, j, 0))],
        out_specs=[pl.BlockSpec((1, tq, OUT_W), lambda i, j: (i, j, 0)),
                   pl.BlockSpec((1, NSA_GROUPS, tq, LANE), lambda i, j: (i, 0, j, 0))],
        out_shape=[jax.ShapeDtypeStruct((b, s, OUT_W), BF16), jax.ShapeDtypeStruct((b, NSA_GROUPS, s, LANE), BF16)],
        scratch_shapes=[pltpu.VMEM((NSA_GROUPS, tq, LANE), F32)],
        compiler_params=_params(("parallel", "parallel")),
        name="nsa_cmp_topk",
    )(qn, kc, vc, gates)


def _nsa_write(o_ref, gate_ref, acc_ref, tq, branch):
    gates = gate_ref[0]
    heads = []
    for h in range(NSA_HEADS):
        g, n = divmod(h, NSA_HPG)
        col = branch * NSA_HEADS + h
        heads.append(_normalized(acc_ref[g, n * tq:(n + 1) * tq]) * gates[:, col:col + 1])
    _store_head_pairs(o_ref, slice(None), heads)


def _sel_attn_kernel(flags, q_ref, k_ref, v_ref, selm1_ref, gate_ref, o_ref, lhs_ref, m_ref, acc_ref, *, t, nq):
    b = pl.program_id(0)
    qi = pl.program_id(1)
    _init_state(m_ref, acc_ref)
    for g in range(NSA_GROUPS):
        lhs_ref[g, :, :LANE] = q_ref[g * NSA_HPG:(g + 1) * NSA_HPG].reshape(NSA_HPG * t, LANE)
        lhs_ref[g, :, LANE:] = jnp.concatenate([selm1_ref[0, g]] * NSA_HPG, axis=0)

    half = NSA_HPG * t // 2
    chains = [(g, i) for g in range(NSA_GROUPS) for i in range(2)]

    def update(ki, diagonal):
        rows = pl.ds(pl.multiple_of(ki * t, t), t)
        scores = [_dot_nt(lhs_ref[g, i * half:(i + 1) * half], k_ref[rows, 2 * g * LANE:(2 * g + 2) * LANE])
                  for g, i in chains]
        if diagonal:
            causal = (_iota((t, t), 1) <= _iota((t, t), 0))[None]
            scores = [jnp.where(causal, s.reshape(NSA_HPG // 2, t, t), NEG).reshape(half, t) for s in scores]
        _flash_update_staged(scores, [v_ref[rows, g * LANE:(g + 1) * LANE] for g, _ in chains], m_ref, acc_ref,
                             [(g, slice(i * half, (i + 1) * half)) for g, i in chains])

    base = (b * nq + qi) * nq

    def body(ki, carry):
        @pl.when(flags[base + ki] > 0)
        def _():
            update(ki, False)
        return carry

    lax.fori_loop(0, qi, body, 0)
    update(qi, True)
    _nsa_write(o_ref, gate_ref, acc_ref, t, branch=1)


def nsa_selected(qn, ks, vs, selm1, gates, t=256):
    _, b, s, _ = qn.shape
    t = min(t, s)
    nq = s // t
    bpt = t // SEL_BLOCK
    flags = (selm1.reshape(b, NSA_GROUPS, nq, t, LANE // bpt, bpt) > -0.5).any(axis=(1, 3, 5))[..., :nq]
    flags = flags.astype(jnp.int32).reshape(-1)
    imap_q = lambda bb, i, fl: (bb, i, 0)
    rows = NSA_HPG * t
    return pl.pallas_call(
        functools.partial(_sel_attn_kernel, t=t, nq=nq),
        grid_spec=pltpu.PrefetchScalarGridSpec(
            num_scalar_prefetch=1,
            grid=(b, nq),
            in_specs=[pl.BlockSpec((NSA_HEADS, None, t, LANE), lambda bb, i, fl: (0, bb, i, 0)),
                      pl.BlockSpec((None, s, 4 * LANE), lambda bb, i, fl: (bb, 0, 0)),
                      pl.BlockSpec((None, s, 2 * LANE), lambda bb, i, fl: (bb, 0, 0)),
                      pl.BlockSpec((1, NSA_GROUPS, t, LANE), lambda bb, i, fl: (bb, 0, i, 0)),
                      pl.BlockSpec((1, t, LANE), imap_q)],
            out_specs=pl.BlockSpec((1, t, OUT_W), imap_q),
            scratch_shapes=[pltpu.VMEM((NSA_GROUPS, rows, 2 * LANE), BF16), pltpu.VMEM((NSA_GROUPS, rows, LANE), F32),
                            pltpu.VMEM((NSA_GROUPS, rows, LANE), F32)],
        ),
        out_shape=jax.ShapeDtypeStruct((b, s, OUT_W), BF16),
        compiler_params=_params(("parallel", "arbitrary")),
        name="nsa_selected",
    )(flags, qn, ks, vs, selm1, gates)


def _win_attn_kernel(q_ref, kp_ref, kc_ref, vp_ref, vc_ref, gate_ref, o_ref, *, tq):
    qi = pl.program_id(1)
    hq = tq // 2
    span = WIN + hq
    row = _iota((hq, span), 0)
    col = _iota((hq, span), 1)
    diff = WIN + row - col
    band = (diff >= 0) & (diff < WIN)
    gates = gate_ref[0]
    chains = [(g, rh) for g in range(NSA_GROUPS) for rh in range(2)]
    scores = []
    for g, rh in chains:
        gl = slice(g * LANE, (g + 1) * LANE)
        q = q_ref[g * NSA_HPG:(g + 1) * NSA_HPG, rh * hq:(rh + 1) * hq].reshape(NSA_HPG * hq, LANE)
        s = jnp.concatenate([_dot_nt(q, kp_ref[0, rh * hq:, gl]), _dot_nt(q, kc_ref[0, :(rh + 1) * hq, gl])], axis=1)
        valid = (band & (col >= WIN - qi * tq - rh * hq))[None]
        scores.append(jnp.where(valid, s.reshape(NSA_HPG, hq, span), NEG).reshape(NSA_HPG * hq, span))
    tops = [jnp.max(s, axis=-1, keepdims=True) for s in scores]
    probs = [jnp.exp2(s - m).astype(BF16) for s, m in zip(scores, tops)]
    for (g, rh), p in zip(chains, probs):
        gl = slice(g * LANE, (g + 1) * LANE)
        n_prev = tq - rh * hq
        acc = _dot(p[:, :n_prev], vp_ref[0, rh * hq:, gl]) + _dot(p[:, n_prev:], vc_ref[0, :(rh + 1) * hq, gl])
        rows = slice(rh * hq, (rh + 1) * hq)
        cols = [2 * NSA_HEADS + g * NSA_HPG + n for n in range(NSA_HPG)]
        heads = [_normalized(acc[n * hq:(n + 1) * hq]) * gates[rows, cols[n]:cols[n] + 1] for n in range(NSA_HPG)]
        _store_head_pairs(o_ref, rows, heads, first_pair=g * NSA_HPG // 2)


def nsa_window(qn, kw, vw, gates, tq=512):
    _, b, s, _ = qn.shape
    assert tq == WIN and s % tq == 0
    prev = pl.BlockSpec((1, tq, 2 * LANE), lambda bb, i: (bb, jnp.maximum(i - 1, 0), 0))
    cur = pl.BlockSpec((1, tq, 2 * LANE), lambda bb, i: (bb, i, 0))
    return pl.pallas_call(
        functools.partial(_win_attn_kernel, tq=tq),
        grid=(b, s // tq),
        in_specs=[pl.BlockSpec((NSA_HEADS, None, tq, LANE), lambda bb, i: (0, bb, i, 0)), prev, cur, prev, cur,
                  pl.BlockSpec((1, tq, LANE), lambda bb, i: (bb, i, 0))],
        out_specs=pl.BlockSpec((1, tq, OUT_W), lambda bb, i: (bb, i, 0)),
        out_shape=jax.ShapeDtypeStruct((b, s, OUT_W), BF16),
        compiler_params=_params(("parallel", "parallel")),
        name="nsa_window",
    )(qn, kw, kw, vw, vw, gates)


def _odd_out_ln_kernel(x_ref, o0_ref, o1_ref, o2_ref, l0_ref, l1_ref, l2_ref, w_ref, g_ref, b_ref, o_ref,
                       o_scr, l_scr):
    tm = x_ref.shape[0]
    for gi, (src, lsrc) in enumerate(((o0_ref, l0_ref), (o1_ref, l1_ref), (o2_ref, l2_ref))):
        dil = DIL_PATTERNS[gi][1]
        for r in range(dil):
            rows = pl.ds(r, tm // dil, stride=dil) if dil > 1 else slice(None)
            for h in range(DIL_GH):
                o_scr[gi, h, rows, :] = src[r, :, h * DIL_DH:(h + 1) * DIL_DH].astype(F32)
            l_scr[gi, rows, :] = lsrc[r]
    lses = [l_scr[gi] for gi in range(DIL_GROUPS)]
    top = jnp.maximum(jnp.maximum(lses[0], lses[1]), lses[2])
    es = [jnp.exp(l - top) for l in lses]
    den = es[0] + es[1] + es[2]
    wts = [e / den for e in es]
    cols = []
    for h in range(DIL_GH):
        merged = None
        for gi in range(DIL_GROUPS):
            term = wts[gi][:, h:h + 1] * o_scr[gi, h]
            merged = term if merged is None else merged + term
        cols.append(merged.astype(BF16))
    mix = _dot(jnp.concatenate(cols, axis=1), w_ref[...])
    o_ref[...] = _layer_norm(ALPHA * x_ref[...] + mix, g_ref[...], b_ref[...])


def odd_out_ln(x2, outs, lses, w, g, b, tm=512):
    m, d = x2.shape
    b_, _, s = outs[0].shape[0], None, outs[0].shape[1] * outs[0].shape[2]
    tm = min(tm, s)
    width = DIL_GH * DIL_DH
    row = pl.BlockSpec((None, tm, d), lambda bb, i: (bb, i, 0))
    cls = lambda gi, wd: pl.BlockSpec((None, DIL_PATTERNS[gi][1], tm // DIL_PATTERNS[gi][1], wd),
                                      lambda bb, i: (bb, 0, i, 0))
    out = pl.pallas_call(
        _odd_out_ln_kernel,
        grid=(b_, s // tm),
        in_specs=[row] + [cls(gi, width) for gi in range(DIL_GROUPS)] + [cls(gi, LANE) for gi in range(DIL_GROUPS)]
        + [_const_spec(w.shape), _const_spec((1, d)), _const_spec((1, d))],
        out_specs=row,
        out_shape=jax.ShapeDtypeStruct((b_, s, d), F32),
        scratch_shapes=[pltpu.VMEM((DIL_GROUPS, DIL_GH, tm, DIL_DH), F32), pltpu.VMEM((DIL_GROUPS, tm, LANE), F32)],
        compiler_params=_params(("parallel", "parallel")),
        name="odd_out_ln",
    )(x2.reshape(b_, s, d), *outs, *lses, w, g.reshape(1, d), b.reshape(1, d))
    return out.reshape(m, d)


def _odd_proj_kernel(x_ref, w_ref, o0_ref, o1_ref, o2_ref, xs_ref):
    tm, d_model = x_ref.shape
    width = DIL_GH * DIL_DH
    n_chunks = d_model // LANE
    for c in range(n_chunks):
        xs_ref[c] = x_ref[:, c * LANE:(c + 1) * LANE]
    for gi, o_ref in enumerate((o0_ref, o1_ref, o2_ref)):
        dil = DIL_PATTERNS[gi][1]
        if dil == 1:
            xg = x_ref[...]
        else:
            xg = jnp.concatenate(
                [jnp.concatenate([xs_ref[c, pl.ds(r, tm // dil, stride=dil), :] for r in range(dil)], axis=0)
                 for c in range(n_chunks)], axis=1)
        xg = xg.astype(BF16)
        for c in range(3):
            col = (gi * 3 + c) * width
            y = _dot(xg, w_ref[:, col:col + width])
            if c == 0:
                y = y * (DIL_SCALE * LOG2E)
            o_ref[:, :, c * width:(c + 1) * width] = y.astype(BF16).reshape(dil, tm // dil, width)


def odd_proj(x2, b, s, w_in, tm=512):
    m, d = x2.shape
    tm = min(tm, s)
    width = DIL_GH * DIL_DH
    w = w_in.reshape(d, 3, DIL_GROUPS, width).transpose(0, 2, 1, 3).reshape(d, 3 * DIL_GROUPS * width).astype(BF16)
    dils = [dil for _, dil in DIL_PATTERNS]
    return pl.pallas_call(
        _odd_proj_kernel,
        grid=(b, s // tm),
        in_specs=[pl.BlockSpec((None, tm, d), lambda bb, i: (bb, i, 0)), _const_spec(w.shape)],
        out_specs=[pl.BlockSpec((None, dil, tm // dil, 3 * width), lambda bb, i: (bb, 0, i, 0)) for dil in dils],
        out_shape=[jax.ShapeDtypeStruct((b, dil, s // dil, 3 * width), BF16) for dil in dils],
        scratch_shapes=[pltpu.VMEM((d // LANE, tm, LANE), F32)],
        compiler_params=_params(("parallel", "parallel")),
        name="odd_proj",
    )(x2.reshape(b, s, d), w)


def _dil_attn_kernel(slope_tab, pos0_tab, q_ref, kc_ref, kp_ref, vc_ref, vp_ref, pc_ref, pp_ref, pq_ref, o_ref,
                     lse_ref, *, group, tq):
    sub = DIL_SPAN
    b = pl.program_id(0)
    jt = pl.program_id(2)
    row = _iota((sub, 2 * sub), 0)
    col = _iota((sub, 2 * sub), 1)
    band = (col >= row) & (col <= row + sub)
    band_bias = jnp.where(band, 0.0, NEG)
    start_bias = jnp.where(band & (col >= sub), 0.0, NEG)
    pos0 = pos0_tab[b]
    lane = _iota((sub, LANE), 1)
    heads = range(DIL_GH)
    hs = [slice(h * DIL_DH, (h + 1) * DIL_DH) for h in heads]
    slopes = [slope_tab[group * DIL_GH + h] for h in heads]
    tiles = range(tq // sub)
    cur = [slice(i * sub, (i + 1) * sub) for i in tiles]
    biases, ks, vs = [], [], []
    for i in tiles:
        if i == 0:
            mask_bias = jnp.where(jt == 0, start_bias, band_bias)
            pk = jnp.concatenate([pp_ref[...], pc_ref[:, :sub]], axis=1)
            ks.append([jnp.concatenate([kp_ref[:, hs[h]], kc_ref[:sub, hs[h]]], axis=0) for h in heads])
            vs.append([jnp.concatenate([vp_ref[:, hs[h]], vc_ref[:sub, hs[h]]], axis=0) for h in heads])
        else:
            mask_bias = band_bias
            pk = pc_ref[:, (i - 1) * sub:(i + 1) * sub]
            ks.append([kc_ref[(i - 1) * sub:(i + 1) * sub, hs[h]] for h in heads])
            vs.append([vc_ref[(i - 1) * sub:(i + 1) * sub, hs[h]] for h in heads])
        dpos = pk - pos0
        biases.append([mask_bias + slopes[h] * dpos for h in heads])
    chains = [(i, h) for i in tiles for h in heads]
    ss = [_dot_nt(q_ref[cur[i], hs[h]], ks[i][h]) + biases[i][h] for i, h in chains]
    ms = [jnp.max(s, axis=-1, keepdims=True) for s in ss]
    ps = [jnp.exp2(s - m) for s, m in zip(ss, ms)]
    dens = [jnp.sum(p, axis=-1, keepdims=True) for p in ps]
    for n, (i, h) in enumerate(chains):
        o_ref[cur[i], hs[h]] = (_dot(ps[n].astype(BF16), vs[i][h]) / dens[n]).astype(BF16)
    for i in tiles:
        dq = (pq_ref[cur[i], :] - pos0) * LN2
        lse_tile = jnp.zeros((sub, LANE), F32)
        for h in heads:
            n = i * DIL_GH + h
            lse_tile = jnp.where(lane == h, ms[n] * LN2 + jnp.log(dens[n]) - slopes[h] * dq, lse_tile)
        lse_ref[cur[i], :] = lse_tile


def dilated_group_attention(qkv, pos_f, group, tq=512):
    b, d, sd, _ = qkv.shape
    w, dil = DIL_PATTERNS[group]
    assert w // dil == DIL_SPAN and dil == d
    tq = min(tq, sd)
    sub = DIL_SPAN
    width = DIL_GH * DIL_DH
    n_slopes = DIL_GROUPS * DIL_GH
    slopes = 2.0 ** (-ALIBI_MAX_BIAS * jnp.arange(1, n_slopes + 1, dtype=F32) / n_slopes) * LOG2E
    pos0 = pos_f[:, 0]
    posc = pos_f.reshape(b, sd, d).transpose(0, 2, 1).reshape(b, d, 1, sd)
    r = tq // sub
    cur = lambda c: pl.BlockSpec((None, None, tq, width), lambda bb, rr, j, *_: (bb, rr, j, c))
    prev = lambda c: pl.BlockSpec((None, None, sub, width),
                                  lambda bb, rr, j, *_: (bb, rr, jnp.maximum(j * r - 1, 0), c))
    return pl.pallas_call(
        functools.partial(_dil_attn_kernel, group=group, tq=tq),
        grid_spec=pltpu.PrefetchScalarGridSpec(
            num_scalar_prefetch=2,
            grid=(b, d, sd // tq),
            in_specs=[cur(0), cur(1), prev(1), cur(2), prev(2),
                      pl.BlockSpec((None, None, 1, tq), lambda bb, rr, j, *_: (bb, rr, 0, j)),
                      pl.BlockSpec((None, None, 1, sub), lambda bb, rr, j, *_: (bb, rr, 0, jnp.maximum(j * r - 1, 0))),
                      pl.BlockSpec((None, None, tq, 1), lambda bb, rr, j, *_: (bb, rr, j, 0))],
            out_specs=[pl.BlockSpec((None, None, tq, width), lambda bb, rr, j, *_: (bb, rr, j, 0)),
                       pl.BlockSpec((None, None, tq, LANE), lambda bb, rr, j, *_: (bb, rr, j, 0))],
        ),
        out_shape=[jax.ShapeDtypeStruct((b, d, sd, width), BF16), jax.ShapeDtypeStruct((b, d, sd, LANE), F32)],
        compiler_params=_params(("parallel", "parallel", "parallel")),
        name=f"dilated_attention_g{group}",
    )(slopes, pos0, qkv, qkv, qkv, qkv, qkv, posc, posc, posc.reshape(b, d, sd, 1))


def even_mixer_ln(x2, b, s, pos_f, w_in, q_norm_g, kv_norm_g, w_uq, w_uk, w_uv,
                  cmp_pos, cmp_k_w1, cmp_k_w2, cmp_v_w1, cmp_v_w2, w_out, ln_g, ln_b):
    m = b * s
    d = x2.shape[1]
    half = MLA_ROPE // 2
    inv = ROPE_THETA ** (-jnp.arange(half, dtype=F32) / half)
    ang = (pos_f[..., None] * inv).reshape(m, half)
    ones = jnp.ones((m, MLA_NOPE), F32)
    zeros = jnp.zeros((m, LANE - MLA_NOPE - MLA_ROPE), F32)
    cos_t = jnp.concatenate([ones, jnp.cos(ang), jnp.cos(ang), zeros], axis=1)
    sin_t = jnp.concatenate([0.0 * ones, jnp.sin(ang), jnp.sin(ang), zeros], axis=1)
    dpos = (pos_f - pos_f[:, :1]) * LOG2E
    ncp = s // CMP_STRIDE
    dpos_cend = jnp.pad(dpos[:, CMP_LEN - 1::CMP_STRIDE], ((0, 0), (0, 1))).reshape(b, ncp, 1)

    qm, km, vm, qn, kvc, ks, kw, vs, vw, gates = even_proj(
        x2, cos_t, sin_t, dpos.reshape(m, 1), s, w_in, q_norm_g, kv_norm_g, w_uq, w_uk, w_uv)
    o_mla = mla_attention(qm.reshape(b, s, HW), km.reshape(b, s, HW), vm.reshape(b, s, HW))

    kvc = kvc.reshape(2, b, ncp, CMP_STRIDE * LANE)
    kc = nsa_compress(kvc[0], cmp_pos, cmp_k_w1, cmp_k_w2, dpos_cend, is_value=False)
    vc = nsa_compress(kvc[1], cmp_pos, cmp_v_w1, cmp_v_w2, dpos_cend, is_value=True)
    qn = qn.reshape(NSA_HEADS, b, s, LANE)
    gates = gates.reshape(b, s, LANE)
    o_c, selm1 = nsa_cmp_topk(qn, kc, vc, gates)
    o_s = nsa_selected(qn, ks.reshape(b, s, 4 * LANE), vs.reshape(b, s, 2 * LANE), selm1, gates)
    o_w = nsa_window(qn, kw.reshape(b, s, 2 * LANE), vw.reshape(b, s, 2 * LANE), gates)

    outs = [o.reshape(m, OUT_W) for o in (o_mla, o_c, o_s, o_w)]
    return outs, [w_out[:OUT_W].astype(BF16), w_out[OUT_W:].astype(BF16)], ln_g, ln_b


def odd_mixer_ln(x2, b, s, pos_f, w_in, w_out, ln_g, ln_b):
    qkvs = odd_proj(x2, b, s, w_in)
    parts = [dilated_group_attention(qkvs[g], pos_f, g) for g in range(DIL_GROUPS)]
    return odd_out_ln(x2, [p[0] for p in parts], [p[1] for p in parts], w_out.astype(BF16), ln_g, ln_b)


def kernel(x, positions, ln1_g, ln1_b, ffn1_w_gate, ffn1_w_up, ffn1_w_down, mix_in_even, mla_q_norm, mla_kv_norm, mla_w_uq, mla_w_uk, mla_w_uv, nsa_cmp_pos, nsa_cmp_k_w1, nsa_cmp_k_w2, nsa_cmp_v_w1, nsa_cmp_v_w2, mix_out_even, mix_in_odd, mix_out_odd, ln2_g, ln2_b, ffn2_w_gate, ffn2_w_up, ffn2_w_down, ln3_g, ln3_b):
    b, s, d = x.shape
    x2 = x.reshape(b * s, d)
    pos_f = positions.astype(F32)
    for i in range(DEPTH):
        j = i // 2
        x2 = ffn_ln(x2, ffn1_w_gate[i].astype(BF16), ffn1_w_up[i].astype(BF16), ffn1_w_down[i].astype(BF16),
                    ln1_g[i], ln1_b[i])
        mix = None
        if i % 2 == 0:
            mix = even_mixer_ln(x2, b, s, pos_f, mix_in_even[j], mla_q_norm[j], mla_kv_norm[j], mla_w_uq[j],
                                mla_w_uk[j], mla_w_uv[j], nsa_cmp_pos[j], nsa_cmp_k_w1[j], nsa_cmp_k_w2[j],
                                nsa_cmp_v_w1[j], nsa_cmp_v_w2[j], mix_out_even[j], ln2_g[i], ln2_b[i])
        else:
            x2 = odd_mixer_ln(x2, b, s, pos_f, mix_in_odd[j], mix_out_odd[j], ln2_g[i], ln2_b[i])
        x2 = ffn_ln(x2, ffn2_w_gate[i].astype(BF16), ffn2_w_up[i].astype(BF16), ffn2_w_down[i].astype(BF16),
                    ln3_g[i], ln3_b[i], mix=mix)
    return x2.reshape(b, s, d)
```

```python
import functools
import math

import numpy as np
import jax
import jax.numpy as jnp
from jax import lax
from jax.experimental import pallas as pl
from jax.experimental.pallas import tpu as pltpu

F32 = jnp.float32
BF16 = jnp.bfloat16

DEPTH = 2
LN_EPS = 1e-5
RMS_EPS = 1e-6
ALPHA = (2 * DEPTH) ** 0.25
HALF_STEP = 0.5
NEG = -1e30
BIG = 1e9
MASK_BIG = 1e30
REMOVED = -3.0e38
ALIBI_MAX_BIAS = 8.0
LOG2E = math.log2(math.e)
LN2 = math.log(2.0)
LANE = 128

MLA_HEADS = 8
MLA_Q_RANK = 384
MLA_KV_RANK = 256
MLA_NOPE = 64
MLA_ROPE = 32
MLA_V = 64
ROPE_THETA = 10000.0
MLA_SCALE = (MLA_NOPE + MLA_ROPE) ** -0.5
MLA_CHAINS = 4

NSA_HEADS = 8
NSA_GROUPS = 2
NSA_HPG = 4
NSA_DH = 64
CMP_LEN = 32
CMP_STRIDE = 16
CMP_HIDDEN = 256
SEL_BLOCK = 64
SEL_TOPK = 16
WIN = 512
NSA_SCALE = NSA_DH ** -0.5
NSA_SLOPES = tuple(2.0 ** (-ALIBI_MAX_BIAS * (i + 1) / NSA_HEADS) for i in range(NSA_HEADS))
ONES_LANE = 64
BIAS_LANES = (64, 65, 66)

DIL_PATTERNS = ((128, 1), (512, 4), (2048, 16))
DIL_GROUPS = 3
DIL_GH = 4
DIL_DH = 128
DIL_SCALE = DIL_DH ** -0.5
DIL_SPAN = 128

VMEM_LIMIT = 48 * 1024 * 1024
MIXED_FFN_VMEM_LIMIT = 56 * 1024 * 1024


def _iota(shape, dim):
    return lax.broadcasted_iota(jnp.int32, shape, dim)


def _shr(x, pow2):
    return jnp.right_shift(x, int(pow2).bit_length() - 1)


def _dot(a, b):
    return jnp.dot(a, b, preferred_element_type=F32)


def _dot_nt(a, b):
    return lax.dot_general(a, b, (((1,), (1,)), ((), ())), preferred_element_type=F32)


def _const_spec(shape):
    zeros = (0,) * len(shape)
    return pl.BlockSpec(shape, lambda *_: zeros, pipeline_mode=pl.Buffered(1))


def _params(sem):
    return pltpu.CompilerParams(dimension_semantics=sem, vmem_limit_bytes=VMEM_LIMIT)


def _layer_norm(z, g, b):
    mu = jnp.mean(z, axis=-1, keepdims=True)
    zc = z - mu
    var = jnp.mean(zc * zc, axis=-1, keepdims=True)
    return zc * lax.rsqrt(var + LN_EPS) * g + b


def _rms_norm(z, g):
    return z * lax.rsqrt(jnp.mean(z * z, axis=-1, keepdims=True) + RMS_EPS) * g


def _bias_pieces(d, lane):
    x = jnp.broadcast_to(d, lane.shape)
    hi = x.astype(BF16).astype(F32)
    r = x - hi
    mid = r.astype(BF16).astype(F32)
    lo = (r - mid).astype(BF16).astype(F32)
    return jnp.where(lane == BIAS_LANES[0], hi,
                     jnp.where(lane == BIAS_LANES[1], mid, jnp.where(lane == BIAS_LANES[2], lo, 0.0)))


def _flash_update(s, v, m_ref, acc_ref, idx):
    m_old = m_ref[idx]
    m_new = jnp.maximum(m_old, jnp.max(s, axis=-1, keepdims=True))
    p = jnp.exp2(s - jnp.tile(m_new, (1, s.shape[1] // LANE))).astype(BF16)
    acc_ref[idx] = jnp.exp2(m_old - m_new) * acc_ref[idx] + _dot(p, v)
    m_ref[idx] = m_new


def _flash_update_staged(scores, values, m_ref, acc_ref, idxs):
    m_old = [m_ref[i] for i in idxs]
    m_new = [jnp.maximum(mo, jnp.max(s, axis=-1, keepdims=True)) for mo, s in zip(m_old, scores)]
    ps = [jnp.exp2(s - jnp.tile(mn, (1, s.shape[1] // LANE))).astype(BF16) for s, mn in zip(scores, m_new)]
    for i, mo, mn, p, v in zip(idxs, m_old, m_new, ps, values):
        acc_ref[i] = jnp.exp2(mo - mn) * acc_ref[i] + _dot(p, v)
        m_ref[i] = mn


def _init_state(m_ref, acc_ref):
    m_ref[...] = jnp.full(m_ref.shape, NEG, F32)
    acc_ref[...] = jnp.zeros(acc_ref.shape, F32)


def _normalized(acc):
    lane = _iota(acc.shape, 1)
    o = acc / jnp.maximum(acc[:, ONES_LANE:ONES_LANE + 1], 1e-30)
    return jnp.where(lane < ONES_LANE, o, 0.0)


def _store_head_pairs(o_ref, rows, heads, first_pair=0):
    for pr in range(len(heads) // 2):
        packed = heads[2 * pr] + pltpu.roll(heads[2 * pr + 1], LANE // 2, 1)
        o_ref[0, rows, (first_pair + pr) * LANE:(first_pair + pr + 1) * LANE] = packed.astype(BF16)


def _ffn_ln_kernel(*refs, n_chunks, mixed):
    if mixed:
        x_ref, om_ref, oc_ref, os_ref, ow_ref, wm_ref, wn_ref, g2_ref, b2_ref = refs[:9]
        nsa = (oc_ref[...].astype(F32) + os_ref[...].astype(F32) + ow_ref[...].astype(F32)).astype(BF16)
        mix = _dot(om_ref[...], wm_ref[...]) + _dot(nsa, wn_ref[...])
        x = _layer_norm(ALPHA * x_ref[...] + mix, g2_ref[...], b2_ref[...])
        refs = refs[9:]
    else:
        x = refs[0][...]
        refs = refs[1:]
    wg_ref, wu_ref, wd_ref, g_ref, b_ref, o_ref = refs
    xb = x.astype(BF16)
    c = wg_ref.shape[1] // n_chunks
    y = None
    for i in range(n_chunks):
        gt = _dot(xb, wg_ref[:, i * c:(i + 1) * c])
        up = _dot(xb, wu_ref[:, i * c:(i + 1) * c])
        h = (gt * jax.nn.sigmoid(gt) * up).astype(BF16)
        part = _dot(h, wd_ref[i * c:(i + 1) * c, :])
        y = part if y is None else y + part
    o_ref[...] = _layer_norm(ALPHA * x + HALF_STEP * y, g_ref[...], b_ref[...])


def ffn_ln(x2, wg, wu, wd, g, b, mix=None, tm=1024, n_chunks=11):
    m, d = x2.shape
    tm = min(tm, m)
    row = lambda w: pl.BlockSpec((tm, w), lambda i: (i, 0))
    args, specs = [x2], [row(d)]
    if mix is not None:
        outs, ws, g2, b2 = mix
        args += list(outs) + list(ws) + [g2.reshape(1, d), b2.reshape(1, d)]
        specs += [row(o.shape[1]) for o in outs] + [_const_spec(w.shape) for w in ws] + [_const_spec((1, d))] * 2
    args += [wg, wu, wd, g.reshape(1, d), b.reshape(1, d)]
    specs += [_const_spec(wg.shape), _const_spec(wu.shape), _const_spec(wd.shape), _const_spec((1, d)), _const_spec((1, d))]
    limit = VMEM_LIMIT if mix is None else MIXED_FFN_VMEM_LIMIT
    return pl.pallas_call(
        functools.partial(_ffn_ln_kernel, n_chunks=n_chunks, mixed=mix is not None),
        grid=(m // tm,),
        in_specs=specs,
        out_specs=row(d),
        out_shape=jax.ShapeDtypeStruct((m, d), F32),
        compiler_params=pltpu.CompilerParams(dimension_semantics=("parallel",), vmem_limit_bytes=limit),
        name="ffn_ln" if mix is None else "mix_ffn_ln",
    )(*args)


HW = MLA_HEADS * LANE
OUT_W = MLA_HEADS * MLA_V
EVEN_X_COLS = (MLA_Q_RANK, MLA_KV_RANK, LANE, LANE, HW, 2 * LANE, 8 * LANE, LANE)
EVEN_X_OFFS = tuple(int(v) for v in np.cumsum((0,) + EVEN_X_COLS))


def _even_proj_kernel(x_ref, cos_ref, sin_ref, dpos_ref, wx_ref, qg_ref, kvg_ref, wuq_ref, wuqs_ref, wuk_ref,
                      wuv_ref, slope_ref, qm_ref, km_ref, vm_ref, qn_ref, kvc_ref, ks_ref, kw_ref, vs_ref, vw_ref,
                      gate_ref, kvc_scr, *, tiles_per_seq):
    tm = x_ref.shape[0]
    xb = x_ref[...].astype(BF16)
    cos = cos_ref[...]
    sin = sin_ref[...]
    lane = _iota((tm, LANE), 1)
    ones_lane = jnp.where(lane == ONES_LANE, 1.0, 0.0)
    pos_term = _bias_pieces(dpos_ref[...], lane)
    tok = (pl.program_id(0) % tiles_per_seq) * tm + _iota((tm, LANE), 0)
    block_onehot = jnp.where(lane == _shr(tok, SEL_BLOCK), MASK_BIG, 0.0).astype(BF16)

    def xdot(i):
        return _dot(xb, wx_ref[:, EVEN_X_OFFS[i]:EVEN_X_OFFS[i + 1]])

    cq = _rms_norm(xdot(0), qg_ref[...]).astype(BF16)
    ckv = _rms_norm(xdot(1), kvg_ref[...]).astype(BF16)
    rope_pair = _dot(xb, wx_ref[:, EVEN_X_OFFS[2]:EVEN_X_OFFS[4]])
    k_rot = rope_pair[:, :LANE] * cos + rope_pair[:, LANE:] * sin
    q_all, q_swap, k_all, v_all = (_dot(a, w[...]) for a, w in ((cq, wuq_ref), (cq, wuqs_ref), (ckv, wuk_ref),
                                                                  (ckv, wuv_ref)))
    for h in range(MLA_HEADS):
        sl = slice(h * LANE, (h + 1) * LANE)
        q = q_all[:, sl] * cos + q_swap[:, sl] * sin
        qm_ref[:, sl] = (q * (MLA_SCALE * LOG2E)).astype(BF16)
        km_ref[:, sl] = (k_all[:, sl] + k_rot).astype(BF16)
        vm_ref[:, sl] = (v_all[:, sl] + ones_lane).astype(BF16)
    qn = xdot(4) * (NSA_SCALE * LOG2E) + slope_ref[...]
    for h in range(NSA_HEADS):
        qn_ref[h] = qn[:, h * LANE:(h + 1) * LANE].astype(BF16)
    kvc = xdot(5)
    for j in range(2):
        kvc_scr[j] = kvc[:, j * LANE:(j + 1) * LANE]
        for l in range(CMP_STRIDE):
            kvc_ref[j, :, l * LANE:(l + 1) * LANE] = kvc_scr[j, pl.ds(l, tm // CMP_STRIDE, stride=CMP_STRIDE), :].astype(BF16)
    kv8 = xdot(6)
    blk = lambda i: kv8[:, i * LANE:(i + 1) * LANE]
    for g in range(NSA_GROUPS):
        ks_ref[:, 2 * g * LANE:(2 * g + 1) * LANE] = (blk(g) + pos_term).astype(BF16)
        ks_ref[:, (2 * g + 1) * LANE:(2 * g + 2) * LANE] = block_onehot
        kw_ref[:, g * LANE:(g + 1) * LANE] = (blk(2 + g) + pos_term).astype(BF16)
        vs_ref[:, g * LANE:(g + 1) * LANE] = (blk(4 + g) + ones_lane).astype(BF16)
        vw_ref[:, g * LANE:(g + 1) * LANE] = (blk(6 + g) + ones_lane).astype(BF16)
    gate_ref[...] = jax.nn.sigmoid(xdot(7))


def _head_blocks(w, n_heads, width):
    k = w.shape[0]
    w = jnp.pad(w.reshape(k, n_heads, width), ((0, 0), (0, 0), (0, LANE - width)))
    return w.reshape(k, n_heads * LANE)


def even_proj(x2, cos_t, sin_t, dpos, seq, w_in, q_norm_g, kv_norm_g, w_uq, w_uk, w_uv, tm=512):
    m, d = x2.shape
    tm = min(tm, seq)
    half = MLA_ROPE // 2
    gw = NSA_GROUPS * NSA_DH
    cuts = np.cumsum((MLA_Q_RANK, MLA_KV_RANK, MLA_ROPE, NSA_HEADS * NSA_DH) + (gw,) * 6)
    cuts = [0] + [int(c) for c in cuts]
    w_cq, w_ckv, w_kpe, w_q = (w_in[:, cuts[i]:cuts[i + 1]] for i in range(4))
    w_kc, w_vc, w_ks, w_vs, w_kw, w_vw = (w_in[:, cuts[4 + i]:cuts[5 + i]] for i in range(6))
    w_gate = w_in[:, cuts[10]:]
    w_kpe_sw = jnp.concatenate([-w_kpe[:, half:], w_kpe[:, :half]], axis=1)
    rope_pad = ((0, 0), (MLA_NOPE, LANE - MLA_NOPE - MLA_ROPE))
    w_gate_blk = jnp.pad(w_gate, ((0, 0), (0, LANE - w_gate.shape[1])))
    wx = jnp.concatenate(
        [w_cq, w_ckv, jnp.pad(w_kpe, rope_pad), jnp.pad(w_kpe_sw, rope_pad), _head_blocks(w_q, NSA_HEADS, NSA_DH),
         w_kc, w_vc] + [_head_blocks(w, NSA_GROUPS, NSA_DH) for w in (w_ks, w_kw, w_vs, w_vw)] + [w_gate_blk],
        axis=1).astype(BF16)

    qd = MLA_NOPE + MLA_ROPE
    uq = w_uq.reshape(MLA_Q_RANK, MLA_HEADS, qd)
    uq_sw = jnp.concatenate([jnp.zeros_like(uq[..., :MLA_NOPE]), -uq[..., MLA_NOPE + half:],
                             uq[..., MLA_NOPE:MLA_NOPE + half]], axis=-1)
    wuq = _head_blocks(uq.reshape(MLA_Q_RANK, -1), MLA_HEADS, qd).astype(BF16)
    wuqs = _head_blocks(uq_sw.reshape(MLA_Q_RANK, -1), MLA_HEADS, qd).astype(BF16)
    wuk = _head_blocks(w_uk, MLA_HEADS, MLA_NOPE).astype(BF16)
    wuv = _head_blocks(w_uv, MLA_HEADS, MLA_V).astype(BF16)
    slope_row = np.zeros((1, HW), np.float32)
    for h in range(NSA_HEADS):
        for ln in BIAS_LANES:
            slope_row[0, h * LANE + ln] = NSA_SLOPES[h]

    row = lambda w: pl.BlockSpec((tm, w), lambda i: (i, 0))
    sds = jax.ShapeDtypeStruct
    return pl.pallas_call(
        functools.partial(_even_proj_kernel, tiles_per_seq=seq // tm),
        grid=(m // tm,),
        in_specs=[row(d), row(LANE), row(LANE), row(1), _const_spec(wx.shape), _const_spec((1, MLA_Q_RANK)),
                  _const_spec((1, MLA_KV_RANK)), _const_spec(wuq.shape), _const_spec(wuqs.shape),
                  _const_spec(wuk.shape), _const_spec(wuv.shape), _const_spec((1, HW))],
        out_specs=[row(HW), row(HW), row(HW), pl.BlockSpec((NSA_HEADS, tm, LANE), lambda i: (0, i, 0)),
                   pl.BlockSpec((2, tm // CMP_STRIDE, CMP_STRIDE * LANE), lambda i: (0, i, 0)), row(4 * LANE),
                   row(2 * LANE), row(2 * LANE), row(2 * LANE), row(LANE)],
        out_shape=[sds((m, HW), BF16)] * 3 + [sds((NSA_HEADS, m, LANE), BF16),
                                              sds((2, m // CMP_STRIDE, CMP_STRIDE * LANE), BF16),
                                              sds((m, 4 * LANE), BF16), sds((m, 2 * LANE), BF16),
                                              sds((m, 2 * LANE), BF16), sds((m, 2 * LANE), BF16), sds((m, LANE), F32)],
        scratch_shapes=[pltpu.VMEM((2, tm, LANE), F32)],
        compiler_params=_params(("parallel",)),
        name="even_proj",
    )(x2, cos_t, sin_t, dpos, wx, q_norm_g.reshape(1, -1), kv_norm_g.reshape(1, -1), wuq, wuqs, wuk, wuv,
      jnp.asarray(slope_row))


def _causal_pairs(nq, tq, tk):
    qi, ki = [], []
    for i in range(nq):
        for j in range(((i + 1) * tq - 1) // tk + 1):
            qi.append(i)
            ki.append(j)
    return jnp.asarray(qi, jnp.int32), jnp.asarray(ki, jnp.int32)


def _mla_kernel(qi_tab, ki_tab, q_ref, k_ref, v_ref, o_ref, m_ref, acc_ref, *, tq, tk):
    p = pl.program_id(1)
    qi = qi_tab[p]
    ki = ki_tab[p]

    @pl.when(ki == 0)
    def _():
        _init_state(m_ref, acc_ref)

    def run(rows, diagonal):
        if diagonal:
            mask = _iota((tk, tk), 0) >= _iota((tk, tk), 1)
        for h0 in range(0, MLA_HEADS, MLA_CHAINS):
            heads = range(h0, h0 + MLA_CHAINS)
            scores = [_dot_nt(q_ref[0, rows, h * LANE:(h + 1) * LANE], k_ref[0, :, h * LANE:(h + 1) * LANE])
                      for h in heads]
            if diagonal:
                scores = [jnp.where(mask, s, NEG) for s in scores]
            _flash_update_staged(scores, [v_ref[0, :, h * LANE:(h + 1) * LANE] for h in heads], m_ref, acc_ref,
                                 [(h, rows) for h in heads])

    blocks = tq // tk
    for r in range(blocks):
        rows = slice(r * tk, (r + 1) * tk)
        own = qi * blocks + r

        @pl.when(ki < own)
        def _(rows=rows):
            run(rows, False)

        @pl.when(ki == own)
        def _(rows=rows):
            run(rows, True)

    @pl.when(ki == ((qi + 1) * tq - 1) // tk)
    def _():
        _store_head_pairs(o_ref, slice(None), [_normalized(acc_ref[h]) for h in range(MLA_HEADS)])


def mla_attention(q, k, v, tq=1024, tk=512):
    b, s, hw = q.shape
    tq, tk = min(tq, s), min(tk, s)
    qi_tab, ki_tab = _causal_pairs(s // tq, tq, tk)
    qspec = pl.BlockSpec((1, tq, hw), lambda bb, p, qt, kt: (bb, qt[p], 0))
    kspec = pl.BlockSpec((1, tk, hw), lambda bb, p, qt, kt: (bb, kt[p], 0))
    return pl.pallas_call(
        functools.partial(_mla_kernel, tq=tq, tk=tk),
        grid_spec=pltpu.PrefetchScalarGridSpec(
            num_scalar_prefetch=2,
            grid=(b, int(qi_tab.shape[0])),
            in_specs=[qspec, kspec, kspec],
            out_specs=pl.BlockSpec((1, tq, OUT_W), lambda bb, p, qt, kt: (bb, qt[p], 0)),
            scratch_shapes=[pltpu.VMEM((MLA_HEADS, tq, LANE), F32), pltpu.VMEM((MLA_HEADS, tq, LANE), F32)],
        ),
        out_shape=jax.ShapeDtypeStruct((b, s, OUT_W), BF16),
        compiler_params=_params(("parallel", "arbitrary")),
        name="mla_attention",
    )(qi_tab, ki_tab, q, k, v)


def _compress_kernel(h_ref, pos_ref, w1_ref, w1g_ref, w2_ref, ext_ref, o_ref, *, is_value):
    n16 = h_ref.shape[1]
    lane = _iota((n16, LANE), 1)
    if is_value:
        extra = jnp.where(lane == ONES_LANE, 1.0, 0.0)
    else:
        extra = _bias_pieces(ext_ref[0], lane)
    bias = _dot(pos_ref[...], w1_ref[...])[0:1]
    h = h_ref[0]
    for g in range(NSA_GROUPS):
        first = _dot(h, w1g_ref[g, 0])
        second = _dot(h, w1g_ref[g, 1])
        hid = first + pltpu.roll(second, n16 - 1, 0) + bias
        act = jax.nn.gelu(hid).astype(BF16)
        o_ref[0, g] = (_dot(act, w2_ref[...]) + extra).astype(BF16)


def nsa_compress(h, cmp_pos, w1, w2, dpos_cend, is_value):
    b, n16, hw = h.shape
    pos = jnp.broadcast_to(cmp_pos.reshape(1, CMP_LEN * NSA_DH), (8, CMP_LEN * NSA_DH)).astype(BF16)
    w2p = jnp.pad(w2, ((0, 0), (0, LANE - NSA_DH))).astype(BF16)
    halves = w1.reshape(2, CMP_STRIDE, 1, NSA_DH, CMP_HIDDEN)
    w1g = jnp.stack([jnp.pad(halves, ((0, 0), (0, 0), (g, NSA_GROUPS - 1 - g), (0, 0), (0, 0))).reshape(2, hw, CMP_HIDDEN)
                     for g in range(NSA_GROUPS)]).astype(BF16)
    w1 = w1.astype(BF16)
    return pl.pallas_call(
        functools.partial(_compress_kernel, is_value=is_value),
        grid=(b,),
        in_specs=[pl.BlockSpec((1, n16, hw), lambda i: (i, 0, 0)), _const_spec(pos.shape),
                  _const_spec(w1.shape), _const_spec(w1g.shape), _const_spec(w2p.shape),
                  pl.BlockSpec((1, n16, 1), lambda i: (i, 0, 0))],
        out_specs=pl.BlockSpec((1, NSA_GROUPS, n16, LANE), lambda i: (i, 0, 0, 0)),
        out_shape=jax.ShapeDtypeStruct((b, NSA_GROUPS, n16, LANE), BF16),
        compiler_params=_params(("parallel",)),
        name="nsa_compress",
    )(h, pos, w1, w1g, w2p, dpos_cend)


def _topk_mask_t(x, k):
    n = x.shape[0]
    ridx = _iota(x.shape, 0).astype(F32)
    sel = jnp.zeros(x.shape, F32)
    for _ in range(k):
        m = jnp.max(x, axis=0, keepdims=True)
        first = jnp.min(jnp.where(x == m, ridx, float(n)), axis=0, keepdims=True)
        hit = ridx == first
        sel = jnp.where(hit, 1.0, sel)
        x = jnp.where(hit, REMOVED, x)
    return sel


def _cmp_topk_kernel(q_ref, kc_ref, vc_ref, gate_ref, oc_ref, selm1_ref, imp_ref, *, tq, ncp, topk):
    qi = pl.program_id(1)
    t = qi * tq + _iota((tq, 1), 0)
    gates = gate_ref[0]
    blk = _iota((tq, LANE), 1)
    chunk = _shr(t, SEL_BLOCK)

    def attend(nc):
        cend = _iota((1, nc), 1) * CMP_STRIDE + (CMP_LEN - 1)
        mask = (cend <= t)[None]
        cstart = _iota((nc, LANE), 0) * CMP_STRIDE
        sstart = _iota((nc, LANE), 1) * SEL_BLOCK
        overlap = jnp.where((cstart < sstart + SEL_BLOCK) & (cstart + CMP_LEN > sstart)
                            & (cstart < (ncp - 1) * CMP_STRIDE), 1.0, 0.0).astype(BF16)
        for g in range(NSA_GROUPS):
            q = q_ref[g * NSA_HPG:(g + 1) * NSA_HPG].reshape(NSA_HPG * tq, LANE)
            s = _dot_nt(q, kc_ref[0, g, :nc]).reshape(NSA_HPG, tq, nc)
            s = jnp.where(mask, s, NEG)
            e = jnp.where(mask, jnp.exp2(s - jnp.max(s, axis=-1, keepdims=True)), 0.0)
            p = e / jnp.maximum(jnp.sum(e, axis=-1, keepdims=True), 1e-30)
            psum = jnp.sum(p, axis=0)
            o = _dot(p.reshape(NSA_HPG * tq, nc).astype(BF16), vc_ref[0, g, :nc])
            heads = [jnp.where(blk < NSA_DH, o[n * tq:(n + 1) * tq] * gates[:, g * NSA_HPG + n:g * NSA_HPG + n + 1], 0.0)
                     for n in range(NSA_HPG)]
            _store_head_pairs(oc_ref, slice(None), heads, first_pair=g * NSA_HPG // 2)
            hi = psum.astype(BF16)
            r1 = psum - hi.astype(F32)
            mid = r1.astype(BF16)
            lo = (r1 - mid.astype(F32)).astype(BF16)
            imp_ref[g] = _dot(hi, overlap) + _dot(mid, overlap) + _dot(lo, overlap)

    lane_tiles = ((qi + 1) * tq // CMP_STRIDE + LANE - 1) // LANE
    for v in range(1, ncp // LANE + 1):
        @pl.when(jnp.minimum(lane_tiles, ncp // LANE) == v)
        def _(v=v):
            attend(v * LANE)

    forced = (blk == 0) | (blk == chunk)
    imps = [jnp.where(forced, REMOVED, jnp.where(blk <= chunk, imp_ref[g], NEG)).T for g in range(NSA_GROUPS)]
    picked = _topk_mask_t(jnp.concatenate(imps, axis=1), topk - 2)
    for g in range(NSA_GROUPS):
        sel = jnp.where(forced, 1.0, picked[:, g * tq:(g + 1) * tq].T)
        selm1_ref[0, g] = (jnp.where(blk <= chunk, sel, 0.0) - 1.0).astype(BF16)


def nsa_cmp_topk(qn, kc, vc, gates, tq=256):
    _, b, s, _ = qn.shape
    tq = min(tq, s)
    ncp = kc.shape[2]
    assert s // SEL_BLOCK <= LANE
    topk = min(SEL_TOPK, s // SEL_BLOCK)
    kspec = pl.BlockSpec((1, NSA_GROUPS, ncp, LANE), lambda i, j: (i, 0, 0, 0))
    return pl.pallas_call(
        functools.partial(_cmp_topk_kernel, tq=tq, ncp=ncp, topk=topk),
        grid=(b, s // tq),
        in_specs=[pl.BlockSpec((NSA_HEADS, None, tq, LANE), lambda i, j: (0, i, j, 0)), kspec, kspec,
                  pl.BlockSpec((1, tq, LANE), lambda i, j: (i, j, 0))],
        out_specs=[pl.BlockSpec((1, tq, OUT_W), lambda i, j: (i, j, 0)),
                   pl.BlockSpec((1, NSA_GROUPS, tq, LANE), lambda i, j: (i, 0, j, 0))],
        out_shape=[jax.ShapeDtypeStruct((b, s, OUT_W), BF16), jax.ShapeDtypeStruct((b, NSA_GROUPS, s, LANE), BF16)],
        scratch_shapes=[pltpu.VMEM((NSA_GROUPS, tq, LANE), F32)],
        compiler_params=_params(("parallel", "parallel")),
        name="nsa_cmp_topk",
    )(qn, kc, vc, gates)


def _nsa_write(o_ref, gate_ref, acc_ref, tq, branch):
    gates = gate_ref[0]
    heads = []
    for h in range(NSA_HEADS):
        g, n = divmod(h, NSA_HPG)
        col = branch * NSA_HEADS + h
        heads.append(_normalized(acc_ref[g, n * tq:(n + 1) * tq]) * gates[:, col:col + 1])
    _store_head_pairs(o_ref, slice(None), heads)


def _sel_attn_kernel(flags, q_ref, k_ref, v_ref, selm1_ref, gate_ref, o_ref, lhs_ref, m_ref, acc_ref, *, t, nq):
    b = pl.program_id(0)
    qi = pl.program_id(1)
    _init_state(m_ref, acc_ref)
    for g in range(NSA_GROUPS):
        lhs_ref[g, :, :LANE] = q_ref[g * NSA_HPG:(g + 1) * NSA_HPG].reshape(NSA_HPG * t, LANE)
        lhs_ref[g, :, LANE:] = jnp.concatenate([selm1_ref[0, g]] * NSA_HPG, axis=0)

    half = NSA_HPG * t // 2
    chains = [(g, i) for g in range(NSA_GROUPS) for i in range(2)]

    def update(ki, diagonal):
        rows = pl.ds(pl.multiple_of(ki * t, t), t)
        scores = [_dot_nt(lhs_ref[g, i * half:(i + 1) * half], k_ref[rows, 2 * g * LANE:(2 * g + 2) * LANE])
                  for g, i in chains]
        if diagonal:
            causal = (_iota((t, t), 1) <= _iota((t, t), 0))[None]
            scores = [jnp.where(causal, s.reshape(NSA_HPG // 2, t, t), NEG).reshape(half, t) for s in scores]
        _flash_update_staged(scores, [v_ref[rows, g * LANE:(g + 1) * LANE] for g, _ in chains], m_ref, acc_ref,
                             [(g, slice(i * half, (i + 1) * half)) for g, i in chains])

    base = (b * nq + qi) * nq

    def body(ki, carry):
        @pl.when(flags[base + ki] > 0)
        def _():
            update(ki, False)
        return carry

    lax.fori_loop(0, qi, body, 0)
    update(qi, True)
    _nsa_write(o_ref, gate_ref, acc_ref, t, branch=1)


def nsa_selected(qn, ks, vs, selm1, gates, t=256):
    _, b, s, _ = qn.shape
    t = min(t, s)
    nq = s // t
    bpt = t // SEL_BLOCK
    flags = (selm1.reshape(b, NSA_GROUPS, nq, t, LANE // bpt, bpt) > -0.5).any(axis=(1, 3, 5))[..., :nq]
    flags = flags.astype(jnp.int32).reshape(-1)
    imap_q = lambda bb, i, fl: (bb, i, 0)
    rows = NSA_HPG * t
    return pl.pallas_call(
        functools.partial(_sel_attn_kernel, t=t, nq=nq),
        grid_spec=pltpu.PrefetchScalarGridSpec(
            num_scalar_prefetch=1,
            grid=(b, nq),
            in_specs=[pl.BlockSpec((NSA_HEADS, None, t, LANE), lambda bb, i, fl: (0, bb, i, 0)),
                      pl.BlockSpec((None, s, 4 * LANE), lambda bb, i, fl: (bb, 0, 0)),
                      pl.BlockSpec((None, s, 2 * LANE), lambda bb, i, fl: (bb, 0, 0)),
                      pl.BlockSpec((1, NSA_GROUPS, t, LANE), lambda bb, i, fl: (bb, 0, i, 0)),
                      pl.BlockSpec((1, t, LANE), imap_q)],
            out_specs=pl.BlockSpec((1, t, OUT_W), imap_q),
            scratch_shapes=[pltpu.VMEM((NSA_GROUPS, rows, 2 * LANE), BF16), pltpu.VMEM((NSA_GROUPS, rows, LANE), F32),
                            pltpu.VMEM((NSA_GROUPS, rows, LANE), F32)],
        ),
        out_shape=jax.ShapeDtypeStruct((b, s, OUT_W), BF16),
        compiler_params=_params(("parallel", "arbitrary")),
        name="nsa_selected",
    )(flags, qn, ks, vs, selm1, gates)


def _win_attn_kernel(q_ref, kp_ref, kc_ref, vp_ref, vc_ref, gate_ref, o_ref, *, tq):
    qi = pl.program_id(1)
    hq = tq // 2
    span = WIN + hq
    row = _iota((hq, span), 0)
    col = _iota((hq, span), 1)
    diff = WIN + row - col
    band = (diff >= 0) & (diff < WIN)
    gates = gate_ref[0]
    chains = [(g, rh) for g in range(NSA_GROUPS) for rh in range(2)]
    scores = []
    for g, rh in chains:
        gl = slice(g * LANE, (g + 1) * LANE)
        q = q_ref[g * NSA_HPG:(g + 1) * NSA_HPG, rh * hq:(rh + 1) * hq].reshape(NSA_HPG * hq, LANE)
        s = jnp.concatenate([_dot_nt(q, kp_ref[0, rh * hq:, gl]), _dot_nt(q, kc_ref[0, :(rh + 1) * hq, gl])], axis=1)
        valid = (band & (col >= WIN - qi * tq - rh * hq))[None]
        scores.append(jnp.where(valid, s.reshape(NSA_HPG, hq, span), NEG).reshape(NSA_HPG * hq, span))
    tops = [jnp.max(s, axis=-1, keepdims=True) for s in scores]
    probs = [jnp.exp2(s - m).astype(BF16) for s, m in zip(scores, tops)]
    for (g, rh), p in zip(chains, probs):
        gl = slice(g * LANE, (g + 1) * LANE)
        n_prev = tq - rh * hq
        acc = _dot(p[:, :n_prev], vp_ref[0, rh * hq:, gl]) + _dot(p[:, n_prev:], vc_ref[0, :(rh + 1) * hq, gl])
        rows = slice(rh * hq, (rh + 1) * hq)
        cols = [2 * NSA_HEADS + g * NSA_HPG + n for n in range(NSA_HPG)]
        heads = [_normalized(acc[n * hq:(n + 1) * hq]) * gates[rows, cols[n]:cols[n] + 1] for n in range(NSA_HPG)]
        _store_head_pairs(o_ref, rows, heads, first_pair=g * NSA_HPG // 2)


def nsa_window(qn, kw, vw, gates, tq=512):
    _, b, s, _ = qn.shape
    assert tq == WIN and s % tq == 0
    prev = pl.BlockSpec((1, tq, 2 * LANE), lambda bb, i: (bb, jnp.maximum(i - 1, 0), 0))
    cur = pl.BlockSpec((1, tq, 2 * LANE), lambda bb, i: (bb, i, 0))
    return pl.pallas_call(
        functools.partial(_win_attn_kernel, tq=tq),
        grid=(b, s // tq),
        in_specs=[pl.BlockSpec((NSA_HEADS, None, tq, LANE), lambda bb, i: (0, bb, i, 0)), prev, cur, prev, cur,
                  pl.BlockSpec((1, tq, LANE), lambda bb, i: (bb, i, 0))],
        out_specs=pl.BlockSpec((1, tq, OUT_W), lambda bb, i: (bb, i, 0)),
        out_shape=jax.ShapeDtypeStruct((b, s, OUT_W), BF16),
        compiler_params=_params(("parallel", "parallel")),
        name="nsa_window",
    )(qn, kw, kw, vw, vw, gates)


def _odd_out_ln_kernel(x_ref, o0_ref, o1_ref, o2_ref, l0_ref, l1_ref, l2_ref, w_ref, g_ref, b_ref, o_ref,
                       o_scr, l_scr):
    tm = x_ref.shape[0]
    for gi, (src, lsrc) in enumerate(((o0_ref, l0_ref), (o1_ref, l1_ref), (o2_ref, l2_ref))):
        dil = DIL_PATTERNS[gi][1]
        for r in range(dil):
            rows = pl.ds(r, tm // dil, stride=dil) if dil > 1 else slice(None)
            for h in range(DIL_GH):
                o_scr[gi, h, rows, :] = src[r, :, h * DIL_DH:(h + 1) * DIL_DH].astype(F32)
            l_scr[gi, rows, :] = lsrc[r]
    lses = [l_scr[gi] for gi in range(DIL_GROUPS)]
    top = jnp.maximum(jnp.maximum(lses[0], lses[1]), lses[2])
    es = [jnp.exp(l - top) for l in lses]
    den = es[0] + es[1] + es[2]
    wts = [e / den for e in es]
    cols = []
    for h in range(DIL_GH):
        merged = None
        for gi in range(DIL_GROUPS):
            term = wts[gi][:, h:h + 1] * o_scr[gi, h]
            merged = term if merged is None else merged + term
        cols.append(merged.astype(BF16))
    mix = _dot(jnp.concatenate(cols, axis=1), w_ref[...])
    o_ref[...] = _layer_norm(ALPHA * x_ref[...] + mix, g_ref[...], b_ref[...])


def odd_out_ln(x2, outs, lses, w, g, b, tm=512):
    m, d = x2.shape
    b_, _, s = outs[0].shape[0], None, outs[0].shape[1] * outs[0].shape[2]
    tm = min(tm, s)
    width = DIL_GH * DIL_DH
    row = pl.BlockSpec((None, tm, d), lambda bb, i: (bb, i, 0))
    cls = lambda gi, wd: pl.BlockSpec((None, DIL_PATTERNS[gi][1], tm // DIL_PATTERNS[gi][1], wd),
                                      lambda bb, i: (bb, 0, i, 0))
    out = pl.pallas_call(
        _odd_out_ln_kernel,
        grid=(b_, s // tm),
        in_specs=[row] + [cls(gi, width) for gi in range(DIL_GROUPS)] + [cls(gi, LANE) for gi in range(DIL_GROUPS)]
        + [_const_spec(w.shape), _const_spec((1, d)), _const_spec((1, d))],
        out_specs=row,
        out_shape=jax.ShapeDtypeStruct((b_, s, d), F32),
        scratch_shapes=[pltpu.VMEM((DIL_GROUPS, DIL_GH, tm, DIL_DH), F32), pltpu.VMEM((DIL_GROUPS, tm, LANE), F32)],
        compiler_params=_params(("parallel", "parallel")),
        name="odd_out_ln",
    )(x2.reshape(b_, s, d), *outs, *lses, w, g.reshape(1, d), b.reshape(1, d))
    return out.reshape(m, d)


def _odd_proj_kernel(x_ref, w_ref, o0_ref, o1_ref, o2_ref, xs_ref):
    tm, d_model = x_ref.shape
    width = DIL_GH * DIL_DH
    n_chunks = d_model // LANE
    for c in range(n_chunks):
        xs_ref[c] = x_ref[:, c * LANE:(c + 1) * LANE]
    for gi, o_ref in enumerate((o0_ref, o1_ref, o2_ref)):
        dil = DIL_PATTERNS[gi][1]
        if dil == 1:
            xg = x_ref[...]
        else:
            xg = jnp.concatenate(
                [jnp.concatenate([xs_ref[c, pl.ds(r, tm // dil, stride=dil), :] for r in range(dil)], axis=0)
                 for c in range(n_chunks)], axis=1)
        xg = xg.astype(BF16)
        for c in range(3):
            col = (gi * 3 + c) * width
            y = _dot(xg, w_ref[:, col:col + width])
            if c == 0:
                y = y * (DIL_SCALE * LOG2E)
            o_ref[:, :, c * width:(c + 1) * width] = y.astype(BF16).reshape(dil, tm // dil, width)


def odd_proj(x2, b, s, w_in, tm=512):
    m, d = x2.shape
    tm = min(tm, s)
    width = DIL_GH * DIL_DH
    w = w_in.reshape(d, 3, DIL_GROUPS, width).transpose(0, 2, 1, 3).reshape(d, 3 * DIL_GROUPS * width).astype(BF16)
    dils = [dil for _, dil in DIL_PATTERNS]
    return pl.pallas_call(
        _odd_proj_kernel,
        grid=(b, s // tm),
        in_specs=[pl.BlockSpec((None, tm, d), lambda bb, i: (bb, i, 0)), _const_spec(w.shape)],
        out_specs=[pl.BlockSpec((None, dil, tm // dil, 3 * width), lambda bb, i: (bb, 0, i, 0)) for dil in dils],
        out_shape=[jax.ShapeDtypeStruct((b, dil, s // dil, 3 * width), BF16) for dil in dils],
        scratch_shapes=[pltpu.VMEM((d // LANE, tm, LANE), F32)],
        compiler_params=_params(("parallel", "parallel")),
        name="odd_proj",
    )(x2.reshape(b, s, d), w)


def _dil_attn_kernel(slope_tab, pos0_tab, q_ref, kc_ref, kp_ref, vc_ref, vp_ref, pc_ref, pp_ref, pq_ref, o_ref,
                     lse_ref, *, group, tq):
    sub = DIL_SPAN
    b = pl.program_id(0)
    jt = pl.program_id(2)
    row = _iota((sub, 2 * sub), 0)
    col = _iota((sub, 2 * sub), 1)
    band = (col >= row) & (col <= row + sub)
    band_bias = jnp.where(band, 0.0, NEG)
    start_bias = jnp.where(band & (col >= sub), 0.0, NEG)
    pos0 = pos0_tab[b]
    lane = _iota((sub, LANE), 1)
    heads = range(DIL_GH)
    hs = [slice(h * DIL_DH, (h + 1) * DIL_DH) for h in heads]
    slopes = [slope_tab[group * DIL_GH + h] for h in heads]
    tiles = range(tq // sub)
    cur = [slice(i * sub, (i + 1) * sub) for i in tiles]
    biases, ks, vs = [], [], []
    for i in tiles:
        if i == 0:
            mask_bias = jnp.where(jt == 0, start_bias, band_bias)
            pk = jnp.concatenate([pp_ref[...], pc_ref[:, :sub]], axis=1)
            ks.append([jnp.concatenate([kp_ref[:, hs[h]], kc_ref[:sub, hs[h]]], axis=0) for h in heads])
            vs.append([jnp.concatenate([vp_ref[:, hs[h]], vc_ref[:sub, hs[h]]], axis=0) for h in heads])
        else:
            mask_bias = band_bias
            pk = pc_ref[:, (i - 1) * sub:(i + 1) * sub]
            ks.append([kc_ref[(i - 1) * sub:(i + 1) * sub, hs[h]] for h in heads])
            vs.append([vc_ref[(i - 1) * sub:(i + 1) * sub, hs[h]] for h in heads])
        dpos = pk - pos0
        biases.append([mask_bias + slopes[h] * dpos for h in heads])
    chains = [(i, h) for i in tiles for h in heads]
    ss = [_dot_nt(q_ref[cur[i], hs[h]], ks[i][h]) + biases[i][h] for i, h in chains]
    ms = [jnp.max(s, axis=-1, keepdims=True) for s in ss]
    ps = [jnp.exp2(s - m) for s, m in zip(ss, ms)]
    dens = [jnp.sum(p, axis=-1, keepdims=True) for p in ps]
    for n, (i, h) in enumerate(chains):
        o_ref[cur[i], hs[h]] = (_dot(ps[n].astype(BF16), vs[i][h]) / dens[n]).astype(BF16)
    for i in tiles:
        dq = (pq_ref[cur[i], :] - pos0) * LN2
        lse_tile = jnp.zeros((sub, LANE), F32)
        for h in heads:
            n = i * DIL_GH + h
            lse_tile = jnp.where(lane == h, ms[n] * LN2 + jnp.log(dens[n]) - slopes[h] * dq, lse_tile)
        lse_ref[cur[i], :] = lse_tile


def dilated_group_attention(qkv, pos_f, group, tq=512):
    b, d, sd, _ = qkv.shape
    w, dil = DIL_PATTERNS[group]
    assert w // dil == DIL_SPAN and dil == d
    tq = min(tq, sd)
    sub = DIL_SPAN
    width = DIL_GH * DIL_DH
    n_slopes = DIL_GROUPS * DIL_GH
    slopes = 2.0 ** (-ALIBI_MAX_BIAS * jnp.arange(1, n_slopes + 1, dtype=F32) / n_slopes) * LOG2E
    pos0 = pos_f[:, 0]
    posc = pos_f.reshape(b, sd, d).transpose(0, 2, 1).reshape(b, d, 1, sd)
    r = tq // sub
    cur = lambda c: pl.BlockSpec((None, None, tq, width), lambda bb, rr, j, *_: (bb, rr, j, c))
    prev = lambda c: pl.BlockSpec((None, None, sub, width),
                                  lambda bb, rr, j, *_: (bb, rr, jnp.maximum(j * r - 1, 0), c))
    return pl.pallas_call(
        functools.partial(_dil_attn_kernel, group=group, tq=tq),
        grid_spec=pltpu.PrefetchScalarGridSpec(
            num_scalar_prefetch=2,
            grid=(b, d, sd // tq),
            in_specs=[cur(0), cur(1), prev(1), cur(2), prev(2),
                      pl.BlockSpec((None, None, 1, tq), lambda bb, rr, j, *_: (bb, rr, 0, j)),
                      pl.BlockSpec((None, None, 1, sub), lambda bb, rr, j, *_: (bb, rr, 0, jnp.maximum(j * r - 1, 0))),
                      pl.BlockSpec((None, None, tq, 1), lambda bb, rr, j, *_: (bb, rr, j, 0))],
            out_specs=[pl.BlockSpec((None, None, tq, width), lambda bb, rr, j, *_: (bb, rr, j, 0)),
                       pl.BlockSpec((None, None, tq, LANE), lambda bb, rr, j, *_: (bb, rr, j, 0))],
        ),
        out_shape=[jax.ShapeDtypeStruct((b, d, sd, width), BF16), jax.ShapeDtypeStruct((b, d, sd, LANE), F32)],
        compiler_params=_params(("parallel", "parallel", "parallel")),
        name=f"dilated_attention_g{group}",
    )(slopes, pos0, qkv, qkv, qkv, qkv, qkv, posc, posc, posc.reshape(b, d, sd, 1))


def even_mixer_ln(x2, b, s, pos_f, w_in, q_norm_g, kv_norm_g, w_uq, w_uk, w_uv,
                  cmp_pos, cmp_k_w1, cmp_k_w2, cmp_v_w1, cmp_v_w2, w_out, ln_g, ln_b):
    m = b * s
    d = x2.shape[1]
    half = MLA_ROPE // 2
    inv = ROPE_THETA ** (-jnp.arange(half, dtype=F32) / half)
    ang = (pos_f[..., None] * inv).reshape(m, half)
    ones = jnp.ones((m, MLA_NOPE), F32)
    zeros = jnp.zeros((m, LANE - MLA_NOPE - MLA_ROPE), F32)
    cos_t = jnp.concatenate([ones, jnp.cos(ang), jnp.cos(ang), zeros], axis=1)
    sin_t = jnp.concatenate([0.0 * ones, jnp.sin(ang), jnp.sin(ang), zeros], axis=1)
    dpos = (pos_f - pos_f[:, :1]) * LOG2E
    ncp = s // CMP_STRIDE
    dpos_cend = jnp.pad(dpos[:, CMP_LEN - 1::CMP_STRIDE], ((0, 0), (0, 1))).reshape(b, ncp, 1)

    qm, km, vm, qn, kvc, ks, kw, vs, vw, gates = even_proj(
        x2, cos_t, sin_t, dpos.reshape(m, 1), s, w_in, q_norm_g, kv_norm_g, w_uq, w_uk, w_uv)
    o_mla = mla_attention(qm.reshape(b, s, HW), km.reshape(b, s, HW), vm.reshape(b, s, HW))

    kvc = kvc.reshape(2, b, ncp, CMP_STRIDE * LANE)
    kc = nsa_compress(kvc[0], cmp_pos, cmp_k_w1, cmp_k_w2, dpos_cend, is_value=False)
    vc = nsa_compress(kvc[1], cmp_pos, cmp_v_w1, cmp_v_w2, dpos_cend, is_value=True)
    qn = qn.reshape(NSA_HEADS, b, s, LANE)
    gates = gates.reshape(b, s, LANE)
    o_c, selm1 = nsa_cmp_topk(qn, kc, vc, gates)
    o_s = nsa_selected(qn, ks.reshape(b, s, 4 * LANE), vs.reshape(b, s, 2 * LANE), selm1, gates)
    o_w = nsa_window(qn, kw.reshape(b, s, 2 * LANE), vw.reshape(b, s, 2 * LANE), gates)

    outs = [o.reshape(m, OUT_W) for o in (o_mla, o_c, o_s, o_w)]
    return outs, [w_out[:OUT_W].astype(BF16), w_out[OUT_W:].astype(BF16)], ln_g, ln_b


def odd_mixer_ln(x2, b, s, pos_f, w_in, w_out, ln_g, ln_b):
    qkvs = odd_proj(x2, b, s, w_in)
    parts = [dilated_group_attention(qkvs[g], pos_f, g) for g in range(DIL_GROUPS)]
    return odd_out_ln(x2, [p[0] for p in parts], [p[1] for p in parts], w_out.astype(BF16), ln_g, ln_b)


def kernel(x, positions, ln1_g, ln1_b, ffn1_w_gate, ffn1_w_up, ffn1_w_down, mix_in_even, mla_q_norm, mla_kv_norm, mla_w_uq, mla_w_uk, mla_w_uv, nsa_cmp_pos, nsa_cmp_k_w1, nsa_cmp_k_w2, nsa_cmp_v_w1, nsa_cmp_v_w2, mix_out_even, mix_in_odd, mix_out_odd, ln2_g, ln2_b, ffn2_w_gate, ffn2_w_up, ffn2_w_down, ln3_g, ln3_b):
    b, s, d = x.shape
    x2 = x.reshape(b * s, d)
    pos_f = positions.astype(F32)
    for i in range(DEPTH):
        j = i // 2
        x2 = ffn_ln(x2, ffn1_w_gate[i].astype(BF16), ffn1_w_up[i].astype(BF16), ffn1_w_down[i].astype(BF16),
                    ln1_g[i], ln1_b[i])
        mix = None
        if i % 2 == 0:
            mix = even_mixer_ln(x2, b, s, pos_f, mix_in_even[j], mla_q_norm[j], mla_kv_norm[j], mla_w_uq[j],
                                mla_w_uk[j], mla_w_uv[j], nsa_cmp_pos[j], nsa_cmp_k_w1[j], nsa_cmp_k_w2[j],
                                nsa_cmp_v_w1[j], nsa_cmp_v_w2[j], mix_out_even[j], ln2_g[i], ln2_b[i])
        else:
            x2 = odd_mixer_ln(x2, b, s, pos_f, mix_in_odd[j], mix_out_odd[j], ln2_g[i], ln2_b[i])
        x2 = ffn_ln(x2, ffn2_w_gate[i].astype(BF16), ffn2_w_up[i].astype(BF16), ffn2_w_down[i].astype(BF16),
                    ln3_g[i], ln3_b[i], mix=mix)
    return x2.reshape(b, s, d)
```

```python
import functools
import math

import numpy as np
import jax
import jax.numpy as jnp
from jax import lax
from jax.experimental import pallas as pl
from jax.experimental.pallas import tpu as pltpu

F32 = jnp.float32
BF16 = jnp.bfloat16

DEPTH = 2
LN_EPS = 1e-5
RMS_EPS = 1e-6
ALPHA = (2 * DEPTH) ** 0.25
HALF_STEP = 0.5
NEG = -1e30
BIG = 1e9
MASK_BIG = 1e30
REMOVED = -3.0e38
ALIBI_MAX_BIAS = 8.0
LOG2E = math.log2(math.e)
LN2 = math.log(2.0)
LANE = 128

MLA_HEADS = 8
MLA_Q_RANK = 384
MLA_KV_RANK = 256
MLA_NOPE = 64
MLA_ROPE = 32
MLA_V = 64
ROPE_THETA = 10000.0
MLA_SCALE = (MLA_NOPE + MLA_ROPE) ** -0.5
MLA_CHAINS = 4

NSA_HEADS = 8
NSA_GROUPS = 2
NSA_HPG = 4
NSA_DH = 64
CMP_LEN = 32
CMP_STRIDE = 16
CMP_HIDDEN = 256
SEL_BLOCK = 64
SEL_TOPK = 16
WIN = 512
NSA_SCALE = NSA_DH ** -0.5
SEL_SPLIT = 2
NSA_SLOPES = tuple(2.0 ** (-ALIBI_MAX_BIAS * (i + 1) / NSA_HEADS) for i in range(NSA_HEADS))
ONES_LANE = 64
BIAS_LANES = (64, 65, 66)

DIL_PATTERNS = ((128, 1), (512, 4), (2048, 16))
DIL_GROUPS = 3
DIL_GH = 4
DIL_DH = 128
DIL_SCALE = DIL_DH ** -0.5
DIL_SPAN = 128

VMEM_LIMIT = 48 * 1024 * 1024
MIXED_FFN_VMEM_LIMIT = 56 * 1024 * 1024


def _iota(shape, dim):
    return lax.broadcasted_iota(jnp.int32, shape, dim)


def _shr(x, pow2):
    return jnp.right_shift(x, int(pow2).bit_length() - 1)


def _dot(a, b):
    return jnp.dot(a, b, preferred_element_type=F32)


def _dot_nt(a, b):
    return lax.dot_general(a, b, (((1,), (1,)), ((), ())), preferred_element_type=F32)


def _const_spec(shape):
    zeros = (0,) * len(shape)
    return pl.BlockSpec(shape, lambda *_: zeros, pipeline_mode=pl.Buffered(1))


def _params(sem):
    return pltpu.CompilerParams(dimension_semantics=sem, vmem_limit_bytes=VMEM_LIMIT)


def _layer_norm(z, g, b):
    mu = jnp.mean(z, axis=-1, keepdims=True)
    zc = z - mu
    var = jnp.mean(zc * zc, axis=-1, keepdims=True)
    return zc * lax.rsqrt(var + LN_EPS) * g + b


def _rms_norm(z, g):
    return z * lax.rsqrt(jnp.mean(z * z, axis=-1, keepdims=True) + RMS_EPS) * g


def _bias_pieces(d, lane):
    x = jnp.broadcast_to(d, lane.shape)
    hi = x.astype(BF16).astype(F32)
    r = x - hi
    mid = r.astype(BF16).astype(F32)
    lo = (r - mid).astype(BF16).astype(F32)
    return jnp.where(lane == BIAS_LANES[0], hi,
                     jnp.where(lane == BIAS_LANES[1], mid, jnp.where(lane == BIAS_LANES[2], lo, 0.0)))


def _flash_update(s, v, m_ref, acc_ref, idx):
    m_old = m_ref[idx]
    m_new = jnp.maximum(m_old, jnp.max(s, axis=-1, keepdims=True))
    p = jnp.exp2(s - jnp.tile(m_new, (1, s.shape[1] // LANE))).astype(BF16)
    acc_ref[idx] = jnp.exp2(m_old - m_new) * acc_ref[idx] + _dot(p, v)
    m_ref[idx] = m_new


def _flash_update_staged(scores, values, m_ref, acc_ref, idxs):
    m_old = [m_ref[i] for i in idxs]
    m_new = [jnp.maximum(mo, jnp.max(s, axis=-1, keepdims=True)) for mo, s in zip(m_old, scores)]
    ps = [jnp.exp2(s - jnp.tile(mn, (1, s.shape[1] // LANE))).astype(BF16) for s, mn in zip(scores, m_new)]
    for i, mo, mn, p, v in zip(idxs, m_old, m_new, ps, values):
        acc_ref[i] = jnp.exp2(mo - mn) * acc_ref[i] + _dot(p, v)
        m_ref[i] = mn


def _init_state(m_ref, acc_ref):
    m_ref[...] = jnp.full(m_ref.shape, NEG, F32)
    acc_ref[...] = jnp.zeros(acc_ref.shape, F32)


def _normalized(acc):
    lane = _iota(acc.shape, 1)
    o = acc / jnp.maximum(acc[:, ONES_LANE:ONES_LANE + 1], 1e-30)
    return jnp.where(lane < ONES_LANE, o, 0.0)


def _store_head_pairs(o_ref, rows, heads, first_pair=0):
    for pr in range(len(heads) // 2):
        packed = heads[2 * pr] + pltpu.roll(heads[2 * pr + 1], LANE // 2, 1)
        o_ref[0, rows, (first_pair + pr) * LANE:(first_pair + pr + 1) * LANE] = packed.astype(BF16)


def _ffn_ln_kernel(*refs, n_chunks, mixed):
    if mixed:
        x_ref, om_ref, oc_ref, os_ref, ow_ref, wm_ref, wn_ref, g2_ref, b2_ref = refs[:9]
        nsa = (oc_ref[...].astype(F32) + os_ref[...].astype(F32) + ow_ref[...].astype(F32)).astype(BF16)
        mix = _dot(om_ref[...], wm_ref[...]) + _dot(nsa, wn_ref[...])
        x = _layer_norm(ALPHA * x_ref[...] + mix, g2_ref[...], b2_ref[...])
        refs = refs[9:]
    else:
        x = refs[0][...]
        refs = refs[1:]
    wg_ref, wu_ref, wd_ref, g_ref, b_ref, o_ref = refs
    xb = x.astype(BF16)
    c = wg_ref.shape[1] // n_chunks
    y = None
    for i in range(n_chunks):
        gt = _dot(xb, wg_ref[:, i * c:(i + 1) * c])
        up = _dot(xb, wu_ref[:, i * c:(i + 1) * c])
        h = (gt * jax.nn.sigmoid(gt) * up).astype(BF16)
        part = _dot(h, wd_ref[i * c:(i + 1) * c, :])
        y = part if y is None else y + part
    o_ref[...] = _layer_norm(ALPHA * x + HALF_STEP * y, g_ref[...], b_ref[...])


def ffn_ln(x2, wg, wu, wd, g, b, mix=None, tm=1024, n_chunks=11):
    m, d = x2.shape
    tm = min(tm, m)
    row = lambda w: pl.BlockSpec((tm, w), lambda i: (i, 0))
    args, specs = [x2], [row(d)]
    if mix is not None:
        outs, ws, g2, b2 = mix
        args += list(outs) + list(ws) + [g2.reshape(1, d), b2.reshape(1, d)]
        specs += [row(o.shape[1]) for o in outs] + [_const_spec(w.shape) for w in ws] + [_const_spec((1, d))] * 2
    args += [wg, wu, wd, g.reshape(1, d), b.reshape(1, d)]
    specs += [_const_spec(wg.shape), _const_spec(wu.shape), _const_spec(wd.shape), _const_spec((1, d)), _const_spec((1, d))]
    limit = VMEM_LIMIT if mix is None else MIXED_FFN_VMEM_LIMIT
    return pl.pallas_call(
        functools.partial(_ffn_ln_kernel, n_chunks=n_chunks, mixed=mix is not None),
        grid=(m // tm,),
        in_specs=specs,
        out_specs=row(d),
        out_shape=jax.ShapeDtypeStruct((m, d), F32),
        compiler_params=pltpu.CompilerParams(dimension_semantics=("parallel",), vmem_limit_bytes=limit),
        name="ffn_ln" if mix is None else "mix_ffn_ln",
    )(*args)


HW = MLA_HEADS * LANE
OUT_W = MLA_HEADS * MLA_V
EVEN_X_COLS = (MLA_Q_RANK, MLA_KV_RANK, LANE, LANE, HW, 2 * LANE, 8 * LANE, LANE)
EVEN_X_OFFS = tuple(int(v) for v in np.cumsum((0,) + EVEN_X_COLS))


def _even_proj_kernel(x_ref, cos_ref, sin_ref, dpos_ref, wx_ref, qg_ref, kvg_ref, wuq_ref, wuqs_ref, wuk_ref,
                      wuv_ref, slope_ref, qm_ref, km_ref, vm_ref, qn_ref, kvc_ref, ks_ref, kw_ref, vs_ref, vw_ref,
                      gate_ref, kvc_scr, *, tiles_per_seq):
    tm = x_ref.shape[0]
    xb = x_ref[...].astype(BF16)
    cos = cos_ref[...]
    sin = sin_ref[...]
    lane = _iota((tm, LANE), 1)
    ones_lane = jnp.where(lane == ONES_LANE, 1.0, 0.0)
    pos_term = _bias_pieces(dpos_ref[...], lane)
    tok = (pl.program_id(0) % tiles_per_seq) * tm + _iota((tm, LANE), 0)
    block_onehot = jnp.where(lane == _shr(tok, SEL_BLOCK), MASK_BIG, 0.0).astype(BF16)

    def xdot(i):
        return _dot(xb, wx_ref[:, EVEN_X_OFFS[i]:EVEN_X_OFFS[i + 1]])

    cq = _rms_norm(xdot(0), qg_ref[...]).astype(BF16)
    ckv = _rms_norm(xdot(1), kvg_ref[...]).astype(BF16)
    rope_pair = _dot(xb, wx_ref[:, EVEN_X_OFFS[2]:EVEN_X_OFFS[4]])
    k_rot = rope_pair[:, :LANE] * cos + rope_pair[:, LANE:] * sin
    q_all, q_swap, k_all, v_all = (_dot(a, w[...]) for a, w in ((cq, wuq_ref), (cq, wuqs_ref), (ckv, wuk_ref),
                                                                  (ckv, wuv_ref)))
    for h in range(MLA_HEADS):
        sl = slice(h * LANE, (h + 1) * LANE)
        q = q_all[:, sl] * cos + q_swap[:, sl] * sin
        qm_ref[:, sl] = (q * (MLA_SCALE * LOG2E)).astype(BF16)
        km_ref[:, sl] = (k_all[:, sl] + k_rot).astype(BF16)
        vm_ref[:, sl] = (v_all[:, sl] + ones_lane).astype(BF16)
    qn = xdot(4) * (NSA_SCALE * LOG2E) + slope_ref[...]
    for h in range(NSA_HEADS):
        qn_ref[h] = qn[:, h * LANE:(h + 1) * LANE].astype(BF16)
    kvc = xdot(5)
    for j in range(2):
        kvc_scr[j] = kvc[:, j * LANE:(j + 1) * LANE]
        for l in range(CMP_STRIDE):
            kvc_ref[j, :, l * LANE:(l + 1) * LANE] = kvc_scr[j, pl.ds(l, tm // CMP_STRIDE, stride=CMP_STRIDE), :].astype(BF16)
    kv8 = xdot(6)
    blk = lambda i: kv8[:, i * LANE:(i + 1) * LANE]
    for g in range(NSA_GROUPS):
        ks_ref[:, 2 * g * LANE:(2 * g + 1) * LANE] = (blk(g) + pos_term).astype(BF16)
        ks_ref[:, (2 * g + 1) * LANE:(2 * g + 2) * LANE] = block_onehot
        kw_ref[:, g * LANE:(g + 1) * LANE] = (blk(2 + g) + pos_term).astype(BF16)
        vs_ref[:, g * LANE:(g + 1) * LANE] = (blk(4 + g) + ones_lane).astype(BF16)
        vw_ref[:, g * LANE:(g + 1) * LANE] = (blk(6 + g) + ones_lane).astype(BF16)
    gate_ref[...] = jax.nn.sigmoid(xdot(7))


def _head_blocks(w, n_heads, width):
    k = w.shape[0]
    w = jnp.pad(w.reshape(k, n_heads, width), ((0, 0), (0, 0), (0, LANE - width)))
    return w.reshape(k, n_heads * LANE)


def even_proj(x2, cos_t, sin_t, dpos, seq, w_in, q_norm_g, kv_norm_g, w_uq, w_uk, w_uv, tm=512):
    m, d = x2.shape
    tm = min(tm, seq)
    half = MLA_ROPE // 2
    gw = NSA_GROUPS * NSA_DH
    cuts = np.cumsum((MLA_Q_RANK, MLA_KV_RANK, MLA_ROPE, NSA_HEADS * NSA_DH) + (gw,) * 6)
    cuts = [0] + [int(c) for c in cuts]
    w_cq, w_ckv, w_kpe, w_q = (w_in[:, cuts[i]:cuts[i + 1]] for i in range(4))
    w_kc, w_vc, w_ks, w_vs, w_kw, w_vw = (w_in[:, cuts[4 + i]:cuts[5 + i]] for i in range(6))
    w_gate = w_in[:, cuts[10]:]
    w_kpe_sw = jnp.concatenate([-w_kpe[:, half:], w_kpe[:, :half]], axis=1)
    rope_pad = ((0, 0), (MLA_NOPE, LANE - MLA_NOPE - MLA_ROPE))
    w_gate_blk = jnp.pad(w_gate, ((0, 0), (0, LANE - w_gate.shape[1])))
    wx = jnp.concatenate(
        [w_cq, w_ckv, jnp.pad(w_kpe, rope_pad), jnp.pad(w_kpe_sw, rope_pad), _head_blocks(w_q, NSA_HEADS, NSA_DH),
         w_kc, w_vc] + [_head_blocks(w, NSA_GROUPS, NSA_DH) for w in (w_ks, w_kw, w_vs, w_vw)] + [w_gate_blk],
        axis=1).astype(BF16)

    qd = MLA_NOPE + MLA_ROPE
    uq = w_uq.reshape(MLA_Q_RANK, MLA_HEADS, qd)
    uq_sw = jnp.concatenate([jnp.zeros_like(uq[..., :MLA_NOPE]), -uq[..., MLA_NOPE + half:],
                             uq[..., MLA_NOPE:MLA_NOPE + half]], axis=-1)
    wuq = _head_blocks(uq.reshape(MLA_Q_RANK, -1), MLA_HEADS, qd).astype(BF16)
    wuqs = _head_blocks(uq_sw.reshape(MLA_Q_RANK, -1), MLA_HEADS, qd).astype(BF16)
    wuk = _head_blocks(w_uk, MLA_HEADS, MLA_NOPE).astype(BF16)
    wuv = _head_blocks(w_uv, MLA_HEADS, MLA_V).astype(BF16)
    slope_row = np.zeros((1, HW), np.float32)
    for h in range(NSA_HEADS):
        for ln in BIAS_LANES:
            slope_row[0, h * LANE + ln] = NSA_SLOPES[h]

    row = lambda w: pl.BlockSpec((tm, w), lambda i: (i, 0))
    sds = jax.ShapeDtypeStruct
    return pl.pallas_call(
        functools.partial(_even_proj_kernel, tiles_per_seq=seq // tm),
        grid=(m // tm,),
        in_specs=[row(d), row(LANE), row(LANE), row(1), _const_spec(wx.shape), _const_spec((1, MLA_Q_RANK)),
                  _const_spec((1, MLA_KV_RANK)), _const_spec(wuq.shape), _const_spec(wuqs.shape),
                  _const_spec(wuk.shape), _const_spec(wuv.shape), _const_spec((1, HW))],
        out_specs=[row(HW), row(HW), row(HW), pl.BlockSpec((NSA_HEADS, tm, LANE), lambda i: (0, i, 0)),
                   pl.BlockSpec((2, tm // CMP_STRIDE, CMP_STRIDE * LANE), lambda i: (0, i, 0)), row(4 * LANE),
                   row(2 * LANE), row(2 * LANE), row(2 * LANE), row(LANE)],
        out_shape=[sds((m, HW), BF16)] * 3 + [sds((NSA_HEADS, m, LANE), BF16),
                                              sds((2, m // CMP_STRIDE, CMP_STRIDE * LANE), BF16),
                                              sds((m, 4 * LANE), BF16), sds((m, 2 * LANE), BF16),
                                              sds((m, 2 * LANE), BF16), sds((m, 2 * LANE), BF16), sds((m, LANE), F32)],
        scratch_shapes=[pltpu.VMEM((2, tm, LANE), F32)],
        compiler_params=_params(("parallel",)),
        name="even_proj",
    )(x2, cos_t, sin_t, dpos, wx, q_norm_g.reshape(1, -1), kv_norm_g.reshape(1, -1), wuq, wuqs, wuk, wuv,
      jnp.asarray(slope_row))


def _causal_pairs(nq, tq, tk):
    qi, ki = [], []
    for i in range(nq):
        for j in range(((i + 1) * tq - 1) // tk + 1):
            qi.append(i)
            ki.append(j)
    return jnp.asarray(qi, jnp.int32), jnp.asarray(ki, jnp.int32)


def _mla_kernel(qi_tab, ki_tab, q_ref, k_ref, v_ref, o_ref, m_ref, acc_ref, *, tq, tk):
    p = pl.program_id(1)
    qi = qi_tab[p]
    ki = ki_tab[p]

    @pl.when(ki == 0)
    def _():
        _init_state(m_ref, acc_ref)

    def run(rows, diagonal):
        if diagonal:
            mask = _iota((tk, tk), 0) >= _iota((tk, tk), 1)
        for h0 in range(0, MLA_HEADS, MLA_CHAINS):
            heads = range(h0, h0 + MLA_CHAINS)
            scores = [_dot_nt(q_ref[0, rows, h * LANE:(h + 1) * LANE], k_ref[0, :, h * LANE:(h + 1) * LANE])
                      for h in heads]
            if diagonal:
                scores = [jnp.where(mask, s, NEG) for s in scores]
            _flash_update_staged(scores, [v_ref[0, :, h * LANE:(h + 1) * LANE] for h in heads], m_ref, acc_ref,
                                 [(h, rows) for h in heads])

    blocks = tq // tk
    for r in range(blocks):
        rows = slice(r * tk, (r + 1) * tk)
        own = qi * blocks + r

        @pl.when(ki < own)
        def _(rows=rows):
            run(rows, False)

        @pl.when(ki == own)
        def _(rows=rows):
            run(rows, True)

    @pl.when(ki == ((qi + 1) * tq - 1) // tk)
    def _():
        _store_head_pairs(o_ref, slice(None), [_normalized(acc_ref[h]) for h in range(MLA_HEADS)])


def mla_attention(q, k, v, tq=1024, tk=512):
    b, s, hw = q.shape
    tq, tk = min(tq, s), min(tk, s)
    qi_tab, ki_tab = _causal_pairs(s // tq, tq, tk)
    qspec = pl.BlockSpec((1, tq, hw), lambda bb, p, qt, kt: (bb, qt[p], 0))
    kspec = pl.BlockSpec((1, tk, hw), lambda bb, p, qt, kt: (bb, kt[p], 0))
    return pl.pallas_call(
        functools.partial(_mla_kernel, tq=tq, tk=tk),
        grid_spec=pltpu.PrefetchScalarGridSpec(
            num_scalar_prefetch=2,
            grid=(b, int(qi_tab.shape[0])),
            in_specs=[qspec, kspec, kspec],
            out_specs=pl.BlockSpec((1, tq, OUT_W), lambda bb, p, qt, kt: (bb, qt[p], 0)),
            scratch_shapes=[pltpu.VMEM((MLA_HEADS, tq, LANE), F32), pltpu.VMEM((MLA_HEADS, tq, LANE), F32)],
        ),
        out_shape=jax.ShapeDtypeStruct((b, s, OUT_W), BF16),
        compiler_params=_params(("parallel", "arbitrary")),
        name="mla_attention",
    )(qi_tab, ki_tab, q, k, v)


def _compress_kernel(h_ref, pos_ref, w1_ref, w1g_ref, w2_ref, ext_ref, o_ref, *, is_value):
    n16 = h_ref.shape[1]
    lane = _iota((n16, LANE), 1)
    if is_value:
        extra = jnp.where(lane == ONES_LANE, 1.0, 0.0)
    else:
        extra = _bias_pieces(ext_ref[0], lane)
    bias = _dot(pos_ref[...], w1_ref[...])[0:1]
    h = h_ref[0]
    for g in range(NSA_GROUPS):
        first = _dot(h, w1g_ref[g, 0])
        second = _dot(h, w1g_ref[g, 1])
        hid = first + pltpu.roll(second, n16 - 1, 0) + bias
        act = jax.nn.gelu(hid).astype(BF16)
        o_ref[0, g] = (_dot(act, w2_ref[...]) + extra).astype(BF16)


def nsa_compress(h, cmp_pos, w1, w2, dpos_cend, is_value):
    b, n16, hw = h.shape
    pos = jnp.broadcast_to(cmp_pos.reshape(1, CMP_LEN * NSA_DH), (8, CMP_LEN * NSA_DH)).astype(BF16)
    w2p = jnp.pad(w2, ((0, 0), (0, LANE - NSA_DH))).astype(BF16)
    halves = w1.reshape(2, CMP_STRIDE, 1, NSA_DH, CMP_HIDDEN)
    w1g = jnp.stack([jnp.pad(halves, ((0, 0), (0, 0), (g, NSA_GROUPS - 1 - g), (0, 0), (0, 0))).reshape(2, hw, CMP_HIDDEN)
                     for g in range(NSA_GROUPS)]).astype(BF16)
    w1 = w1.astype(BF16)
    return pl.pallas_call(
        functools.partial(_compress_kernel, is_value=is_value),
        grid=(b,),
        in_specs=[pl.BlockSpec((1, n16, hw), lambda i: (i, 0, 0)), _const_spec(pos.shape),
                  _const_spec(w1.shape), _const_spec(w1g.shape), _const_spec(w2p.shape),
                  pl.BlockSpec((1, n16, 1), lambda i: (i, 0, 0))],
        out_specs=pl.BlockSpec((1, NSA_GROUPS, n16, LANE), lambda i: (i, 0, 0, 0)),
        out_shape=jax.ShapeDtypeStruct((b, NSA_GROUPS, n16, LANE), BF16),
        compiler_params=_params(("parallel",)),
        name="nsa_compress",
    )(h, pos, w1, w1g, w2p, dpos_cend)


def _topk_mask_t(x, k):
    n = x.shape[0]
    ridx = _iota(x.shape, 0).astype(F32)
    sel = jnp.zeros(x.shape, F32)
    for _ in range(k):
        m = jnp.max(x, axis=0, keepdims=True)
        first = jnp.min(jnp.where(x == m, ridx, float(n)), axis=0, keepdims=True)
        hit = ridx == first
        sel = jnp.where(hit, 1.0, sel)
        x = jnp.where(hit, REMOVED, x)
    return sel


def _cmp_topk_kernel(q_ref, kc_ref, vc_ref, gate_ref, oc_ref, selm1_ref, used_ref, imp_ref, *, tq, ncp, topk):
    qi = pl.program_id(1)
    t = qi * tq + _iota((tq, 1), 0)
    gates = gate_ref[0]
    blk = _iota((tq, LANE), 1)
    chunk = _shr(t, SEL_BLOCK)

    def attend(nc):
        cend = _iota((1, nc), 1) * CMP_STRIDE + (CMP_LEN - 1)
        mask = (cend <= t)[None]
        cstart = _iota((nc, LANE), 0) * CMP_STRIDE
        sstart = _iota((nc, LANE), 1) * SEL_BLOCK
        overlap = jnp.where((cstart < sstart + SEL_BLOCK) & (cstart + CMP_LEN > sstart)
                            & (cstart < (ncp - 1) * CMP_STRIDE), 1.0, 0.0).astype(BF16)
        for g in range(NSA_GROUPS):
            q = q_ref[g * NSA_HPG:(g + 1) * NSA_HPG].reshape(NSA_HPG * tq, LANE)
            s = _dot_nt(q, kc_ref[0, g, :nc]).reshape(NSA_HPG, tq, nc)
            s = jnp.where(mask, s, NEG)
            e = jnp.where(mask, jnp.exp2(s - jnp.max(s, axis=-1, keepdims=True)), 0.0)
            p = e / jnp.maximum(jnp.sum(e, axis=-1, keepdims=True), 1e-30)
            psum = jnp.sum(p, axis=0)
            o = _dot(p.reshape(NSA_HPG * tq, nc).astype(BF16), vc_ref[0, g, :nc])
            heads = [jnp.where(blk < NSA_DH, o[n * tq:(n + 1) * tq] * gates[:, g * NSA_HPG + n:g * NSA_HPG + n + 1], 0.0)
                     for n in range(NSA_HPG)]
            _store_head_pairs(oc_ref, slice(None), heads, first_pair=g * NSA_HPG // 2)
            hi = psum.astype(BF16)
            r1 = psum - hi.astype(F32)
            mid = r1.astype(BF16)
            lo = (r1 - mid.astype(F32)).astype(BF16)
            imp_ref[g] = _dot(hi, overlap) + _dot(mid, overlap) + _dot(lo, overlap)

    lane_tiles = ((qi + 1) * tq // CMP_STRIDE + LANE - 1) // LANE
    for v in range(1, ncp // LANE + 1):
        @pl.when(jnp.minimum(lane_tiles, ncp // LANE) == v)
        def _(v=v):
            attend(v * LANE)

    forced = (blk == 0) | (blk == chunk)
    imps = [jnp.where(forced, REMOVED, jnp.where(blk <= chunk, imp_ref[g], NEG)).T for g in range(NSA_GROUPS)]
    picked = _topk_mask_t(jnp.concatenate(imps, axis=1), topk - 2)
    used = jnp.zeros((1, LANE), F32)
    for g in range(NSA_GROUPS):
        sel = jnp.where(blk <= chunk, jnp.where(forced, 1.0, picked[:, g * tq:(g + 1) * tq].T), 0.0)
        selm1_ref[0, g] = (sel - 1.0).astype(BF16)
        used = jnp.maximum(used, jnp.max(sel, axis=0, keepdims=True))
    used_ref[0, 0] = jnp.broadcast_to(used, used_ref.shape[2:])


def nsa_cmp_topk(qn, kc, vc, gates, tq=256):
    _, b, s, _ = qn.shape
    tq = min(tq, s)
    ncp = kc.shape[2]
    assert s // SEL_BLOCK <= LANE
    topk = min(SEL_TOPK, s // SEL_BLOCK)
    kspec = pl.BlockSpec((1, NSA_GROUPS, ncp, LANE), lambda i, j: (i, 0, 0, 0))
    return pl.pallas_call(
        functools.partial(_cmp_topk_kernel, tq=tq, ncp=ncp, topk=topk),
        grid=(b, s // tq),
        in_specs=[pl.BlockSpec((NSA_HEADS, None, tq, LANE), lambda i, j: (0, i, j, 0)), kspec, kspec,
                  pl.BlockSpec((1, tq, LANE), lambda i, j: (i, j, 0))],
        out_specs=[pl.BlockSpec((1, tq, OUT_W), lambda i, j: (i, j, 0)),
                   pl.BlockSpec((1, NSA_GROUPS, tq, LANE), lambda i, j: (i, 0, j, 0)),
                   pl.BlockSpec((1, 1, 8, LANE), lambda i, j: (i, j, 0, 0))],
        out_shape=[jax.ShapeDtypeStruct((b, s, OUT_W), BF16), jax.ShapeDtypeStruct((b, NSA_GROUPS, s, LANE), BF16),
                   jax.ShapeDtypeStruct((b, s // tq, 8, LANE), F32)],
        scratch_shapes=[pltpu.VMEM((NSA_GROUPS, tq, LANE), F32)],
        compiler_params=_params(("parallel", "parallel")),
        name="nsa_cmp_topk",
    )(qn, kc, vc, gates)


def _nsa_write(o_ref, gate_ref, acc_ref, tq, branch):
    gates = gate_ref[0]
    heads = []
    for h in range(NSA_HEADS):
        g, n = divmod(h, NSA_HPG)
        col = branch * NSA_HEADS + h
        heads.append(_normalized(acc_ref[g, n * tq:(n + 1) * tq]) * gates[:, col:col + 1])
    _store_head_pairs(o_ref, slice(None), heads)


def _sel_attn_kernel(flags, q_ref, k_ref, v_ref, selm1_ref, gate_ref, o_ref, lhs_ref, m_ref, acc_ref, *, t, nq):
    b = pl.program_id(0)
    qi = pl.program_id(1)
    _init_state(m_ref, acc_ref)
    for g in range(NSA_GROUPS):
        lhs_ref[g, :, :LANE] = q_ref[g * NSA_HPG:(g + 1) * NSA_HPG].reshape(NSA_HPG * t, LANE)
        lhs_ref[g, :, LANE:] = jnp.concatenate([selm1_ref[0, g]] * NSA_HPG, axis=0)

    hpc = NSA_HPG // SEL_SPLIT
    half = hpc * t
    chains = [(g, i) for g in range(NSA_GROUPS) for i in range(SEL_SPLIT)]

    def update(ki, diagonal):
        rows = pl.ds(pl.multiple_of(ki * t, t), t)
        scores = [_dot_nt(lhs_ref[g, i * half:(i + 1) * half], k_ref[rows, 2 * g * LANE:(2 * g + 2) * LANE])
                  for g, i in chains]
        if diagonal:
            causal = (_iota((t, t), 1) <= _iota((t, t), 0))[None]
            scores = [jnp.where(causal, s.reshape(hpc, t, t), NEG).reshape(half, t) for s in scores]
        _flash_update_staged(scores, [v_ref[rows, g * LANE:(g + 1) * LANE] for g, _ in chains], m_ref, acc_ref,
                             [(g, slice(i * half, (i + 1) * half)) for g, i in chains])

    base = (b * nq + qi) * nq

    def body(ki, carry):
        @pl.when(flags[base + ki] > 0)
        def _():
            update(ki, False)
        return carry

    lax.fori_loop(0, qi, body, 0)
    update(qi, True)
    _nsa_write(o_ref, gate_ref, acc_ref, t, branch=1)


def nsa_selected(qn, ks, vs, selm1, used, gates):
    _, b, s, _ = qn.shape
    nq = used.shape[1]
    t = s // nq
    bpt = t // SEL_BLOCK
    flags = (used[:, :, 0].reshape(b, nq, LANE // bpt, bpt) > 0.5).any(axis=-1)[..., :nq]
    flags = flags.astype(jnp.int32).reshape(-1)
    imap_q = lambda bb, i, fl: (bb, i, 0)
    rows = NSA_HPG * t
    return pl.pallas_call(
        functools.partial(_sel_attn_kernel, t=t, nq=nq),
        grid_spec=pltpu.PrefetchScalarGridSpec(
            num_scalar_prefetch=1,
            grid=(b, nq),
            in_specs=[pl.BlockSpec((NSA_HEADS, None, t, LANE), lambda bb, i, fl: (0, bb, i, 0)),
                      pl.BlockSpec((None, s, 4 * LANE), lambda bb, i, fl: (bb, 0, 0)),
                      pl.BlockSpec((None, s, 2 * LANE), lambda bb, i, fl: (bb, 0, 0)),
                      pl.BlockSpec((1, NSA_GROUPS, t, LANE), lambda bb, i, fl: (bb, 0, i, 0)),
                      pl.BlockSpec((1, t, LANE), imap_q)],
            out_specs=pl.BlockSpec((1, t, OUT_W), imap_q),
            scratch_shapes=[pltpu.VMEM((NSA_GROUPS, rows, 2 * LANE), BF16), pltpu.VMEM((NSA_GROUPS, rows, LANE), F32),
                            pltpu.VMEM((NSA_GROUPS, rows, LANE), F32)],
        ),
        out_shape=jax.ShapeDtypeStruct((b, s, OUT_W), BF16),
        compiler_params=_params(("parallel", "arbitrary")),
        name="nsa_selected",
    )(flags, qn, ks, vs, selm1, gates)


def _win_attn_kernel(q_ref, kp_ref, kc_ref, vp_ref, vc_ref, gate_ref, o_ref, *, tq):
    qi = pl.program_id(1)
    hq = tq // 2
    span = WIN + hq
    row = _iota((hq, span), 0)
    col = _iota((hq, span), 1)
    diff = WIN + row - col
    band = (diff >= 0) & (diff < WIN)
    gates = gate_ref[0]
    chains = [(g, rh) for g in range(NSA_GROUPS) for rh in range(2)]
    scores = []
    for g, rh in chains:
        gl = slice(g * LANE, (g + 1) * LANE)
        q = q_ref[g * NSA_HPG:(g + 1) * NSA_HPG, rh * hq:(rh + 1) * hq].reshape(NSA_HPG * hq, LANE)
        s = jnp.concatenate([_dot_nt(q, kp_ref[0, rh * hq:, gl]), _dot_nt(q, kc_ref[0, :(rh + 1) * hq, gl])], axis=1)
        valid = (band & (col >= WIN - qi * tq - rh * hq))[None]
        scores.append(jnp.where(valid, s.reshape(NSA_HPG, hq, span), NEG).reshape(NSA_HPG * hq, span))
    tops = [jnp.max(s, axis=-1, keepdims=True) for s in scores]
    probs = [jnp.exp2(s - m).astype(BF16) for s, m in zip(scores, tops)]
    for (g, rh), p in zip(chains, probs):
        gl = slice(g * LANE, (g + 1) * LANE)
        n_prev = tq - rh * hq
        acc = _dot(p[:, :n_prev], vp_ref[0, rh * hq:, gl]) + _dot(p[:, n_prev:], vc_ref[0, :(rh + 1) * hq, gl])
        rows = slice(rh * hq, (rh + 1) * hq)
        cols = [2 * NSA_HEADS + g * NSA_HPG + n for n in range(NSA_HPG)]
        heads = [_normalized(acc[n * hq:(n + 1) * hq]) * gates[rows, cols[n]:cols[n] + 1] for n in range(NSA_HPG)]
        _store_head_pairs(o_ref, rows, heads, first_pair=g * NSA_HPG // 2)


def nsa_window(qn, kw, vw, gates, tq=512):
    _, b, s, _ = qn.shape
    assert tq == WIN and s % tq == 0
    prev = pl.BlockSpec((1, tq, 2 * LANE), lambda bb, i: (bb, jnp.maximum(i - 1, 0), 0))
    cur = pl.BlockSpec((1, tq, 2 * LANE), lambda bb, i: (bb, i, 0))
    return pl.pallas_call(
        functools.partial(_win_attn_kernel, tq=tq),
        grid=(b, s // tq),
        in_specs=[pl.BlockSpec((NSA_HEADS, None, tq, LANE), lambda bb, i: (0, bb, i, 0)), prev, cur, prev, cur,
                  pl.BlockSpec((1, tq, LANE), lambda bb, i: (bb, i, 0))],
        out_specs=pl.BlockSpec((1, tq, OUT_W), lambda bb, i: (bb, i, 0)),
        out_shape=jax.ShapeDtypeStruct((b, s, OUT_W), BF16),
        compiler_params=_params(("parallel", "parallel")),
        name="nsa_window",
    )(qn, kw, kw, vw, vw, gates)


def _odd_out_ln_kernel(x_ref, o0_ref, o1_ref, o2_ref, l0_ref, l1_ref, l2_ref, w_ref, g_ref, b_ref, o_ref,
                       o_scr, l_scr):
    tm = x_ref.shape[0]
    for gi, (src, lsrc) in enumerate(((o0_ref, l0_ref), (o1_ref, l1_ref), (o2_ref, l2_ref))):
        dil = DIL_PATTERNS[gi][1]
        for r in range(dil):
            rows = pl.ds(r, tm // dil, stride=dil) if dil > 1 else slice(None)
            for h in range(DIL_GH):
                o_scr[gi, h, rows, :] = src[r, :, h * DIL_DH:(h + 1) * DIL_DH].astype(F32)
            l_scr[gi, rows, :] = lsrc[r]
    lses = [l_scr[gi] for gi in range(DIL_GROUPS)]
    top = jnp.maximum(jnp.maximum(lses[0], lses[1]), lses[2])
    es = [jnp.exp(l - top) for l in lses]
    den = es[0] + es[1] + es[2]
    wts = [e / den for e in es]
    cols = []
    for h in range(DIL_GH):
        merged = None
        for gi in range(DIL_GROUPS):
            term = wts[gi][:, h:h + 1] * o_scr[gi, h]
            merged = term if merged is None else merged + term
        cols.append(merged.astype(BF16))
    mix = _dot(jnp.concatenate(cols, axis=1), w_ref[...])
    o_ref[...] = _layer_norm(ALPHA * x_ref[...] + mix, g_ref[...], b_ref[...])


def odd_out_ln(x2, outs, lses, w, g, b, tm=512):
    m, d = x2.shape
    b_, _, s = outs[0].shape[0], None, outs[0].shape[1] * outs[0].shape[2]
    tm = min(tm, s)
    width = DIL_GH * DIL_DH
    row = pl.BlockSpec((None, tm, d), lambda bb, i: (bb, i, 0))
    cls = lambda gi, wd: pl.BlockSpec((None, DIL_PATTERNS[gi][1], tm // DIL_PATTERNS[gi][1], wd),
                                      lambda bb, i: (bb, 0, i, 0))
    out = pl.pallas_call(
        _odd_out_ln_kernel,
        grid=(b_, s // tm),
        in_specs=[row] + [cls(gi, width) for gi in range(DIL_GROUPS)] + [cls(gi, LANE) for gi in range(DIL_GROUPS)]
        + [_const_spec(w.shape), _const_spec((1, d)), _const_spec((1, d))],
        out_specs=row,
        out_shape=jax.ShapeDtypeStruct((b_, s, d), F32),
        scratch_shapes=[pltpu.VMEM((DIL_GROUPS, DIL_GH, tm, DIL_DH), F32), pltpu.VMEM((DIL_GROUPS, tm, LANE), F32)],
        compiler_params=_params(("parallel", "parallel")),
        name="odd_out_ln",
    )(x2.reshape(b_, s, d), *outs, *lses, w, g.reshape(1, d), b.reshape(1, d))
    return out.reshape(m, d)


def _odd_proj_kernel(x_ref, w_ref, o0_ref, o1_ref, o2_ref, xs_ref):
    tm, d_model = x_ref.shape
    width = DIL_GH * DIL_DH
    n_chunks = d_model // LANE
    for c in range(n_chunks):
        xs_ref[c] = x_ref[:, c * LANE:(c + 1) * LANE]
    for gi, o_ref in enumerate((o0_ref, o1_ref, o2_ref)):
        dil = DIL_PATTERNS[gi][1]
        if dil == 1:
            xg = x_ref[...]
        else:
            xg = jnp.concatenate(
                [jnp.concatenate([xs_ref[c, pl.ds(r, tm // dil, stride=dil), :] for r in range(dil)], axis=0)
                 for c in range(n_chunks)], axis=1)
        xg = xg.astype(BF16)
        for c in range(3):
            col = (gi * 3 + c) * width
            y = _dot(xg, w_ref[:, col:col + width])
            if c == 0:
                y = y * (DIL_SCALE * LOG2E)
            o_ref[:, :, c * width:(c + 1) * width] = y.astype(BF16).reshape(dil, tm // dil, width)


def odd_proj(x2, b, s, w_in, tm=512):
    m, d = x2.shape
    tm = min(tm, s)
    width = DIL_GH * DIL_DH
    w = w_in.reshape(d, 3, DIL_GROUPS, width).transpose(0, 2, 1, 3).reshape(d, 3 * DIL_GROUPS * width).astype(BF16)
    dils = [dil for _, dil in DIL_PATTERNS]
    return pl.pallas_call(
        _odd_proj_kernel,
        grid=(b, s // tm),
        in_specs=[pl.BlockSpec((None, tm, d), lambda bb, i: (bb, i, 0)), _const_spec(w.shape)],
        out_specs=[pl.BlockSpec((None, dil, tm // dil, 3 * width), lambda bb, i: (bb, 0, i, 0)) for dil in dils],
        out_shape=[jax.ShapeDtypeStruct((b, dil, s // dil, 3 * width), BF16) for dil in dils],
        scratch_shapes=[pltpu.VMEM((d // LANE, tm, LANE), F32)],
        compiler_params=_params(("parallel", "parallel")),
        name="odd_proj",
    )(x2.reshape(b, s, d), w)


def _dil_attn_kernel(slope_tab, pos0_tab, q_ref, kc_ref, kp_ref, vc_ref, vp_ref, pc_ref, pp_ref, pq_ref, o_ref,
                     lse_ref, *, group, tq):
    sub = DIL_SPAN
    b = pl.program_id(0)
    jt = pl.program_id(2)
    row = _iota((sub, 2 * sub), 0)
    col = _iota((sub, 2 * sub), 1)
    band = (col >= row) & (col <= row + sub)
    band_bias = jnp.where(band, 0.0, NEG)
    start_bias = jnp.where(band & (col >= sub), 0.0, NEG)
    pos0 = pos0_tab[b]
    lane = _iota((sub, LANE), 1)
    heads = range(DIL_GH)
    hs = [slice(h * DIL_DH, (h + 1) * DIL_DH) for h in heads]
    slopes = [slope_tab[group * DIL_GH + h] for h in heads]
    tiles = range(tq // sub)
    cur = [slice(i * sub, (i + 1) * sub) for i in tiles]
    biases, ks, vs = [], [], []
    for i in tiles:
        if i == 0:
            mask_bias = jnp.where(jt == 0, start_bias, band_bias)
            pk = jnp.concatenate([pp_ref[...], pc_ref[:, :sub]], axis=1)
            ks.append([jnp.concatenate([kp_ref[:, hs[h]], kc_ref[:sub, hs[h]]], axis=0) for h in heads])
            vs.append([jnp.concatenate([vp_ref[:, hs[h]], vc_ref[:sub, hs[h]]], axis=0) for h in heads])
        else:
            mask_bias = band_bias
            pk = pc_ref[:, (i - 1) * sub:(i + 1) * sub]
            ks.append([kc_ref[(i - 1) * sub:(i + 1) * sub, hs[h]] for h in heads])
            vs.append([vc_ref[(i - 1) * sub:(i + 1) * sub, hs[h]] for h in heads])
        dpos = pk - pos0
        biases.append([mask_bias + slopes[h] * dpos for h in heads])
    chains = [(i, h) for i in tiles for h in heads]
    ss = [_dot_nt(q_ref[cur[i], hs[h]], ks[i][h]) + biases[i][h] for i, h in chains]
    ms = [jnp.max(s, axis=-1, keepdims=True) for s in ss]
    ps = [jnp.exp2(s - m) for s, m in zip(ss, ms)]
    dens = [jnp.sum(p, axis=-1, keepdims=True) for p in ps]
    for n, (i, h) in enumerate(chains):
        o_ref[cur[i], hs[h]] = (_dot(ps[n].astype(BF16), vs[i][h]) / dens[n]).astype(BF16)
    for i in tiles:
        dq = (pq_ref[cur[i], :] - pos0) * LN2
        lse_tile = jnp.zeros((sub, LANE), F32)
        for h in heads:
            n = i * DIL_GH + h
            lse_tile = jnp.where(lane == h, ms[n] * LN2 + jnp.log(dens[n]) - slopes[h] * dq, lse_tile)
        lse_ref[cur[i], :] = lse_tile


def dilated_group_attention(qkv, pos_f, group, tq=512):
    b, d, sd, _ = qkv.shape
    w, dil = DIL_PATTERNS[group]
    assert w // dil == DIL_SPAN and dil == d
    tq = min(tq, sd)
    sub = DIL_SPAN
    width = DIL_GH * DIL_DH
    n_slopes = DIL_GROUPS * DIL_GH
    slopes = 2.0 ** (-ALIBI_MAX_BIAS * jnp.arange(1, n_slopes + 1, dtype=F32) / n_slopes) * LOG2E
    pos0 = pos_f[:, 0]
    posc = pos_f.reshape(b, sd, d).transpose(0, 2, 1).reshape(b, d, 1, sd)
    r = tq // sub
    cur = lambda c: pl.BlockSpec((None, None, tq, width), lambda bb, rr, j, *_: (bb, rr, j, c))
    prev = lambda c: pl.BlockSpec((None, None, sub, width),
                                  lambda bb, rr, j, *_: (bb, rr, jnp.maximum(j * r - 1, 0), c))
    return pl.pallas_call(
        functools.partial(_dil_attn_kernel, group=group, tq=tq),
        grid_spec=pltpu.PrefetchScalarGridSpec(
            num_scalar_prefetch=2,
            grid=(b, d, sd // tq),
            in_specs=[cur(0), cur(1), prev(1), cur(2), prev(2),
                      pl.BlockSpec((None, None, 1, tq), lambda bb, rr, j, *_: (bb, rr, 0, j)),
                      pl.BlockSpec((None, None, 1, sub), lambda bb, rr, j, *_: (bb, rr, 0, jnp.maximum(j * r - 1, 0))),
                      pl.BlockSpec((None, None, tq, 1), lambda bb, rr, j, *_: (bb, rr, j, 0))],
            out_specs=[pl.BlockSpec((None, None, tq, width), lambda bb, rr, j, *_: (bb, rr, j, 0)),
                       pl.BlockSpec((None, None, tq, LANE), lambda bb, rr, j, *_: (bb, rr, j, 0))],
        ),
        out_shape=[jax.ShapeDtypeStruct((b, d, sd, width), BF16), jax.ShapeDtypeStruct((b, d, sd, LANE), F32)],
        compiler_params=_params(("parallel", "parallel", "parallel")),
        name=f"dilated_attention_g{group}",
    )(slopes, pos0, qkv, qkv, qkv, qkv, qkv, posc, posc, posc.reshape(b, d, sd, 1))


def even_mixer_ln(x2, b, s, pos_f, w_in, q_norm_g, kv_norm_g, w_uq, w_uk, w_uv,
                  cmp_pos, cmp_k_w1, cmp_k_w2, cmp_v_w1, cmp_v_w2, w_out, ln_g, ln_b):
    m = b * s
    d = x2.shape[1]
    half = MLA_ROPE // 2
    inv = ROPE_THETA ** (-jnp.arange(half, dtype=F32) / half)
    ang = (pos_f[..., None] * inv).reshape(m, half)
    ones = jnp.ones((m, MLA_NOPE), F32)
    zeros = jnp.zeros((m, LANE - MLA_NOPE - MLA_ROPE), F32)
    cos_t = jnp.concatenate([ones, jnp.cos(ang), jnp.cos(ang), zeros], axis=1)
    sin_t = jnp.concatenate([0.0 * ones, jnp.sin(ang), jnp.sin(ang), zeros], axis=1)
    dpos = (pos_f - pos_f[:, :1]) * LOG2E
    ncp = s // CMP_STRIDE
    dpos_cend = jnp.pad(dpos[:, CMP_LEN - 1::CMP_STRIDE], ((0, 0), (0, 1))).reshape(b, ncp, 1)

    qm, km, vm, qn, kvc, ks, kw, vs, vw, gates = even_proj(
        x2, cos_t, sin_t, dpos.reshape(m, 1), s, w_in, q_norm_g, kv_norm_g, w_uq, w_uk, w_uv)
    o_mla = mla_attention(qm.reshape(b, s, HW), km.reshape(b, s, HW), vm.reshape(b, s, HW))

    kvc = kvc.reshape(2, b, ncp, CMP_STRIDE * LANE)
    kc = nsa_compress(kvc[0], cmp_pos, cmp_k_w1, cmp_k_w2, dpos_cend, is_value=False)
    vc = nsa_compress(kvc[1], cmp_pos, cmp_v_w1, cmp_v_w2, dpos_cend, is_value=True)
    qn = qn.reshape(NSA_HEADS, b, s, LANE)
    gates = gates.reshape(b, s, LANE)
    o_c, selm1, used = nsa_cmp_topk(qn, kc, vc, gates)
    o_s = nsa_selected(qn, ks.reshape(b, s, 4 * LANE), vs.reshape(b, s, 2 * LANE), selm1, used, gates)
    o_w = nsa_window(qn, kw.reshape(b, s, 2 * LANE), vw.reshape(b, s, 2 * LANE), gates)

    outs = [o.reshape(m, OUT_W) for o in (o_mla, o_c, o_s, o_w)]
    return outs, [w_out[:OUT_W].astype(BF16), w_out[OUT_W:].astype(BF16)], ln_g, ln_b


def odd_mixer_ln(x2, b, s, pos_f, w_in, w_out, ln_g, ln_b):
    qkvs = odd_proj(x2, b, s, w_in)
    parts = [dilated_group_attention(qkvs[g], pos_f, g) for g in range(DIL_GROUPS)]
    return odd_out_ln(x2, [p[0] for p in parts], [p[1] for p in parts], w_out.astype(BF16), ln_g, ln_b)


def kernel(x, positions, ln1_g, ln1_b, ffn1_w_gate, ffn1_w_up, ffn1_w_down, mix_in_even, mla_q_norm, mla_kv_norm, mla_w_uq, mla_w_uk, mla_w_uv, nsa_cmp_pos, nsa_cmp_k_w1, nsa_cmp_k_w2, nsa_cmp_v_w1, nsa_cmp_v_w2, mix_out_even, mix_in_odd, mix_out_odd, ln2_g, ln2_b, ffn2_w_gate, ffn2_w_up, ffn2_w_down, ln3_g, ln3_b):
    b, s, d = x.shape
    x2 = x.reshape(b * s, d)
    pos_f = positions.astype(F32)
    for i in range(DEPTH):
        j = i // 2
        x2 = ffn_ln(x2, ffn1_w_gate[i].astype(BF16), ffn1_w_up[i].astype(BF16), ffn1_w_down[i].astype(BF16),
                    ln1_g[i], ln1_b[i])
        mix = None
        if i % 2 == 0:
            mix = even_mixer_ln(x2, b, s, pos_f, mix_in_even[j], mla_q_norm[j], mla_kv_norm[j], mla_w_uq[j],
                                mla_w_uk[j], mla_w_uv[j], nsa_cmp_pos[j], nsa_cmp_k_w1[j], nsa_cmp_k_w2[j],
                                nsa_cmp_v_w1[j], nsa_cmp_v_w2[j], mix_out_even[j], ln2_g[i], ln2_b[i])
        else:
            x2 = odd_mixer_ln(x2, b, s, pos_f, mix_in_odd[j], mix_out_odd[j], ln2_g[i], ln2_b[i])
        x2 = ffn_ln(x2, ffn2_w_gate[i].astype(BF16), ffn2_w_up[i].astype(BF16), ffn2_w_down[i].astype(BF16),
                    ln3_g[i], ln3_b[i], mix=mix)
    return x2.reshape(b, s, d)
```

```python
import functools
import math

import numpy as np
import jax
import jax.numpy as jnp
from jax import lax
from jax.experimental import pallas as pl
from jax.experimental.pallas import tpu as pltpu

F32 = jnp.float32
BF16 = jnp.bfloat16

DEPTH = 2
LN_EPS = 1e-5
RMS_EPS = 1e-6
ALPHA = (2 * DEPTH) ** 0.25
HALF_STEP = 0.5
NEG = -1e30
BIG = 1e9
MASK_BIG = 1e30
REMOVED = -3.0e38
ALIBI_MAX_BIAS = 8.0
LOG2E = math.log2(math.e)
LN2 = math.log(2.0)
LANE = 128

MLA_HEADS = 8
MLA_Q_RANK = 384
MLA_KV_RANK = 256
MLA_NOPE = 64
MLA_ROPE = 32
MLA_V = 64
ROPE_THETA = 10000.0
MLA_SCALE = (MLA_NOPE + MLA_ROPE) ** -0.5
MLA_CHAINS = 4

NSA_HEADS = 8
NSA_GROUPS = 2
NSA_HPG = 4
NSA_DH = 64
CMP_LEN = 32
CMP_STRIDE = 16
CMP_HIDDEN = 256
SEL_BLOCK = 64
SEL_TOPK = 16
WIN = 512
NSA_SCALE = NSA_DH ** -0.5
SEL_SPLIT = 2
NSA_SLOPES = tuple(2.0 ** (-ALIBI_MAX_BIAS * (i + 1) / NSA_HEADS) for i in range(NSA_HEADS))
ONES_LANE = 64
BIAS_LANES = (64, 65, 66)

DIL_PATTERNS = ((128, 1), (512, 4), (2048, 16))
DIL_GROUPS = 3
DIL_GH = 4
DIL_DH = 128
DIL_SCALE = DIL_DH ** -0.5
DIL_SPAN = 128

VMEM_LIMIT = 48 * 1024 * 1024
MIXED_FFN_VMEM_LIMIT = 56 * 1024 * 1024


def _iota(shape, dim):
    return lax.broadcasted_iota(jnp.int32, shape, dim)


def _shr(x, pow2):
    return jnp.right_shift(x, int(pow2).bit_length() - 1)


def _dot(a, b):
    return jnp.dot(a, b, preferred_element_type=F32)


def _dot_nt(a, b):
    return lax.dot_general(a, b, (((1,), (1,)), ((), ())), preferred_element_type=F32)


def _const_spec(shape):
    zeros = (0,) * len(shape)
    return pl.BlockSpec(shape, lambda *_: zeros, pipeline_mode=pl.Buffered(1))


def _params(sem):
    return pltpu.CompilerParams(dimension_semantics=sem, vmem_limit_bytes=VMEM_LIMIT)


def _layer_norm(z, g, b):
    mu = jnp.mean(z, axis=-1, keepdims=True)
    zc = z - mu
    var = jnp.mean(zc * zc, axis=-1, keepdims=True)
    return zc * lax.rsqrt(var + LN_EPS) * g + b


def _rms_norm(z, g):
    return z * lax.rsqrt(jnp.mean(z * z, axis=-1, keepdims=True) + RMS_EPS) * g


def _bias_pieces(d, lane):
    x = jnp.broadcast_to(d, lane.shape)
    hi = x.astype(BF16).astype(F32)
    r = x - hi
    mid = r.astype(BF16).astype(F32)
    lo = (r - mid).astype(BF16).astype(F32)
    return jnp.where(lane == BIAS_LANES[0], hi,
                     jnp.where(lane == BIAS_LANES[1], mid, jnp.where(lane == BIAS_LANES[2], lo, 0.0)))


def _flash_update_staged(scores, values, m_ref, acc_ref, idxs):
    m_old = [m_ref[i] for i in idxs]
    m_new = [jnp.maximum(mo, jnp.max(s, axis=-1, keepdims=True)) for mo, s in zip(m_old, scores)]
    ps = [jnp.exp2(s - jnp.tile(mn, (1, s.shape[1] // LANE))).astype(BF16) for s, mn in zip(scores, m_new)]
    for i, mo, mn, p, v in zip(idxs, m_old, m_new, ps, values):
        acc_ref[i] = jnp.exp2(mo - mn) * acc_ref[i] + _dot(p, v)
        m_ref[i] = mn


def _init_state(m_ref, acc_ref):
    m_ref[...] = jnp.full(m_ref.shape, NEG, F32)
    acc_ref[...] = jnp.zeros(acc_ref.shape, F32)


def _normalized(acc):
    lane = _iota(acc.shape, 1)
    o = acc / jnp.maximum(acc[:, ONES_LANE:ONES_LANE + 1], 1e-30)
    return jnp.where(lane < ONES_LANE, o, 0.0)


def _store_head_pairs(o_ref, rows, heads, first_pair=0):
    for pr in range(len(heads) // 2):
        packed = heads[2 * pr] + pltpu.roll(heads[2 * pr + 1], LANE // 2, 1)
        o_ref[0, rows, (first_pair + pr) * LANE:(first_pair + pr + 1) * LANE] = packed.astype(BF16)


def _ffn_ln_kernel(*refs, n_chunks, mixed):
    if mixed:
        x_ref, om_ref, oc_ref, os_ref, ow_ref, w_ref, g2_ref, b2_ref = refs[:8]
        nsa = (oc_ref[...].astype(F32) + os_ref[...].astype(F32) + ow_ref[...].astype(F32)).astype(BF16)
        mix = _dot(jnp.concatenate([om_ref[...], nsa], axis=1), w_ref[...])
        x = _layer_norm(ALPHA * x_ref[...] + mix, g2_ref[...], b2_ref[...])
        refs = refs[8:]
    else:
        x = refs[0][...]
        refs = refs[1:]
    wg_ref, wu_ref, wd_ref, g_ref, b_ref, o_ref = refs
    xb = x.astype(BF16)
    c = wg_ref.shape[1] // n_chunks
    y = None
    for i in range(n_chunks):
        gt = _dot(xb, wg_ref[:, i * c:(i + 1) * c])
        up = _dot(xb, wu_ref[:, i * c:(i + 1) * c])
        h = (gt * jax.nn.sigmoid(gt) * up).astype(BF16)
        part = _dot(h, wd_ref[i * c:(i + 1) * c, :])
        y = part if y is None else y + part
    o_ref[...] = _layer_norm(ALPHA * x + HALF_STEP * y, g_ref[...], b_ref[...])


def ffn_ln(x2, wg, wu, wd, g, b, mix=None, tm=1024, n_chunks=11):
    m, d = x2.shape
    tm = min(tm, m)
    row = lambda w: pl.BlockSpec((tm, w), lambda i: (i, 0))
    args, specs = [x2], [row(d)]
    if mix is not None:
        outs, ws, g2, b2 = mix
        args += list(outs) + list(ws) + [g2.reshape(1, d), b2.reshape(1, d)]
        specs += [row(o.shape[1]) for o in outs] + [_const_spec(w.shape) for w in ws] + [_const_spec((1, d))] * 2
    args += [wg, wu, wd, g.reshape(1, d), b.reshape(1, d)]
    specs += [_const_spec(wg.shape), _const_spec(wu.shape), _const_spec(wd.shape), _const_spec((1, d)), _const_spec((1, d))]
    limit = VMEM_LIMIT if mix is None else MIXED_FFN_VMEM_LIMIT
    return pl.pallas_call(
        functools.partial(_ffn_ln_kernel, n_chunks=n_chunks, mixed=mix is not None),
        grid=(m // tm,),
        in_specs=specs,
        out_specs=row(d),
        out_shape=jax.ShapeDtypeStruct((m, d), F32),
        compiler_params=pltpu.CompilerParams(dimension_semantics=("parallel",), vmem_limit_bytes=limit),
        name="ffn_ln" if mix is None else "mix_ffn_ln",
    )(*args)


HW = MLA_HEADS * LANE
OUT_W = MLA_HEADS * MLA_V
EVEN_X_COLS = (MLA_Q_RANK, MLA_KV_RANK, LANE, LANE, HW, 2 * LANE, 8 * LANE, LANE)
EVEN_X_OFFS = tuple(int(v) for v in np.cumsum((0,) + EVEN_X_COLS))


def _even_proj_kernel(x_ref, cos_ref, sin_ref, dpos_ref, wx_ref, qg_ref, kvg_ref, wuq_ref, wuqs_ref, wuk_ref,
                      wuv_ref, slope_ref, qm_ref, km_ref, vm_ref, qn_ref, kvc_ref, ks_ref, kw_ref, vs_ref, vw_ref,
                      gate_ref, kvc_scr, *, tiles_per_seq):
    tm = x_ref.shape[0]
    xb = x_ref[...].astype(BF16)
    cos = cos_ref[...]
    sin = sin_ref[...]
    lane = _iota((tm, LANE), 1)
    ones_lane = jnp.where(lane == ONES_LANE, 1.0, 0.0)
    pos_term = _bias_pieces(dpos_ref[...], lane)
    tok = (pl.program_id(0) % tiles_per_seq) * tm + _iota((tm, LANE), 0)
    block_onehot = jnp.where(lane == _shr(tok, SEL_BLOCK), MASK_BIG, 0.0).astype(BF16)

    def xdot(i):
        return _dot(xb, wx_ref[:, EVEN_X_OFFS[i]:EVEN_X_OFFS[i + 1]])

    cq = _rms_norm(xdot(0), qg_ref[...]).astype(BF16)
    ckv = _rms_norm(xdot(1), kvg_ref[...]).astype(BF16)
    rope_pair = _dot(xb, wx_ref[:, EVEN_X_OFFS[2]:EVEN_X_OFFS[4]])
    k_rot = rope_pair[:, :LANE] * cos + rope_pair[:, LANE:] * sin
    q_all, q_swap, k_all, v_all = (_dot(a, w[...]) for a, w in ((cq, wuq_ref), (cq, wuqs_ref), (ckv, wuk_ref),
                                                                  (ckv, wuv_ref)))
    for h in range(MLA_HEADS):
        sl = slice(h * LANE, (h + 1) * LANE)
        q = q_all[:, sl] * cos + q_swap[:, sl] * sin
        qm_ref[:, sl] = (q * (MLA_SCALE * LOG2E)).astype(BF16)
        km_ref[:, sl] = (k_all[:, sl] + k_rot).astype(BF16)
        vm_ref[:, sl] = (v_all[:, sl] + ones_lane).astype(BF16)
    qn = xdot(4) * (NSA_SCALE * LOG2E) + slope_ref[...]
    for h in range(NSA_HEADS):
        qn_ref[h] = qn[:, h * LANE:(h + 1) * LANE].astype(BF16)
    kvc = xdot(5)
    for j in range(2):
        kvc_scr[j] = kvc[:, j * LANE:(j + 1) * LANE]
        for l in range(CMP_STRIDE):
            kvc_ref[j, :, l * LANE:(l + 1) * LANE] = kvc_scr[j, pl.ds(l, tm // CMP_STRIDE, stride=CMP_STRIDE), :].astype(BF16)
    kv8 = xdot(6)
    blk = lambda i: kv8[:, i * LANE:(i + 1) * LANE]
    for g in range(NSA_GROUPS):
        ks_ref[:, 2 * g * LANE:(2 * g + 1) * LANE] = (blk(g) + pos_term).astype(BF16)
        ks_ref[:, (2 * g + 1) * LANE:(2 * g + 2) * LANE] = block_onehot
        kw_ref[:, g * LANE:(g + 1) * LANE] = (blk(2 + g) + pos_term).astype(BF16)
        vs_ref[:, g * LANE:(g + 1) * LANE] = (blk(4 + g) + ones_lane).astype(BF16)
        vw_ref[:, g * LANE:(g + 1) * LANE] = (blk(6 + g) + ones_lane).astype(BF16)
    gate_ref[...] = jax.nn.sigmoid(xdot(7))


def _head_blocks(w, n_heads, width):
    k = w.shape[0]
    w = jnp.pad(w.reshape(k, n_heads, width), ((0, 0), (0, 0), (0, LANE - width)))
    return w.reshape(k, n_heads * LANE)


def even_proj(x2, cos_t, sin_t, dpos, seq, w_in, q_norm_g, kv_norm_g, w_uq, w_uk, w_uv, tm=512):
    m, d = x2.shape
    tm = min(tm, seq)
    half = MLA_ROPE // 2
    gw = NSA_GROUPS * NSA_DH
    cuts = np.cumsum((MLA_Q_RANK, MLA_KV_RANK, MLA_ROPE, NSA_HEADS * NSA_DH) + (gw,) * 6)
    cuts = [0] + [int(c) for c in cuts]
    w_cq, w_ckv, w_kpe, w_q = (w_in[:, cuts[i]:cuts[i + 1]] for i in range(4))
    w_kc, w_vc, w_ks, w_vs, w_kw, w_vw = (w_in[:, cuts[4 + i]:cuts[5 + i]] for i in range(6))
    w_gate = w_in[:, cuts[10]:]
    w_kpe_sw = jnp.concatenate([-w_kpe[:, half:], w_kpe[:, :half]], axis=1)
    rope_pad = ((0, 0), (MLA_NOPE, LANE - MLA_NOPE - MLA_ROPE))
    w_gate_blk = jnp.pad(w_gate, ((0, 0), (0, LANE - w_gate.shape[1])))
    wx = jnp.concatenate(
        [w_cq, w_ckv, jnp.pad(w_kpe, rope_pad), jnp.pad(w_kpe_sw, rope_pad), _head_blocks(w_q, NSA_HEADS, NSA_DH),
         w_kc, w_vc] + [_head_blocks(w, NSA_GROUPS, NSA_DH) for w in (w_ks, w_kw, w_vs, w_vw)] + [w_gate_blk],
        axis=1).astype(BF16)

    qd = MLA_NOPE + MLA_ROPE
    uq = w_uq.reshape(MLA_Q_RANK, MLA_HEADS, qd)
    uq_sw = jnp.concatenate([jnp.zeros_like(uq[..., :MLA_NOPE]), -uq[..., MLA_NOPE + half:],
                             uq[..., MLA_NOPE:MLA_NOPE + half]], axis=-1)
    wuq = _head_blocks(uq.reshape(MLA_Q_RANK, -1), MLA_HEADS, qd).astype(BF16)
    wuqs = _head_blocks(uq_sw.reshape(MLA_Q_RANK, -1), MLA_HEADS, qd).astype(BF16)
    wuk = _head_blocks(w_uk, MLA_HEADS, MLA_NOPE).astype(BF16)
    wuv = _head_blocks(w_uv, MLA_HEADS, MLA_V).astype(BF16)
    slope_row = np.zeros((1, HW), np.float32)
    for h in range(NSA_HEADS):
        for ln in BIAS_LANES:
            slope_row[0, h * LANE + ln] = NSA_SLOPES[h]

    row = lambda w: pl.BlockSpec((tm, w), lambda i: (i, 0))
    sds = jax.ShapeDtypeStruct
    return pl.pallas_call(
        functools.partial(_even_proj_kernel, tiles_per_seq=seq // tm),
        grid=(m // tm,),
        in_specs=[row(d), row(LANE), row(LANE), row(1), _const_spec(wx.shape), _const_spec((1, MLA_Q_RANK)),
                  _const_spec((1, MLA_KV_RANK)), _const_spec(wuq.shape), _const_spec(wuqs.shape),
                  _const_spec(wuk.shape), _const_spec(wuv.shape), _const_spec((1, HW))],
        out_specs=[row(HW), row(HW), row(HW), pl.BlockSpec((NSA_HEADS, tm, LANE), lambda i: (0, i, 0)),
                   pl.BlockSpec((2, tm // CMP_STRIDE, CMP_STRIDE * LANE), lambda i: (0, i, 0)), row(4 * LANE),
                   row(2 * LANE), row(2 * LANE), row(2 * LANE), row(LANE)],
        out_shape=[sds((m, HW), BF16)] * 3 + [sds((NSA_HEADS, m, LANE), BF16),
                                              sds((2, m // CMP_STRIDE, CMP_STRIDE * LANE), BF16),
                                              sds((m, 4 * LANE), BF16), sds((m, 2 * LANE), BF16),
                                              sds((m, 2 * LANE), BF16), sds((m, 2 * LANE), BF16), sds((m, LANE), F32)],
        scratch_shapes=[pltpu.VMEM((2, tm, LANE), F32)],
        compiler_params=_params(("parallel",)),
        name="even_proj",
    )(x2, cos_t, sin_t, dpos, wx, q_norm_g.reshape(1, -1), kv_norm_g.reshape(1, -1), wuq, wuqs, wuk, wuv,
      jnp.asarray(slope_row))


def _causal_pairs(nq, tq, tk):
    qi, ki = [], []
    for i in range(nq):
        for j in range(((i + 1) * tq - 1) // tk + 1):
            qi.append(i)
            ki.append(j)
    return jnp.asarray(qi, jnp.int32), jnp.asarray(ki, jnp.int32)


def _mla_kernel(qi_tab, ki_tab, q_ref, k_ref, v_ref, o_ref, m_ref, acc_ref, *, tq, tk):
    p = pl.program_id(1)
    qi = qi_tab[p]
    ki = ki_tab[p]

    @pl.when(ki == 0)
    def _():
        _init_state(m_ref, acc_ref)

    def run(rows, diagonal):
        if diagonal:
            mask = _iota((tk, tk), 0) >= _iota((tk, tk), 1)
        for h0 in range(0, MLA_HEADS, MLA_CHAINS):
            heads = range(h0, h0 + MLA_CHAINS)
            scores = [_dot_nt(q_ref[0, rows, h * LANE:(h + 1) * LANE], k_ref[0, :, h * LANE:(h + 1) * LANE])
                      for h in heads]
            if diagonal:
                scores = [jnp.where(mask, s, NEG) for s in scores]
            _flash_update_staged(scores, [v_ref[0, :, h * LANE:(h + 1) * LANE] for h in heads], m_ref, acc_ref,
                                 [(h, rows) for h in heads])

    blocks = tq // tk
    for r in range(blocks):
        rows = slice(r * tk, (r + 1) * tk)
        own = qi * blocks + r

        @pl.when(ki < own)
        def _(rows=rows):
            run(rows, False)

        @pl.when(ki == own)
        def _(rows=rows):
            run(rows, True)

    @pl.when(ki == ((qi + 1) * tq - 1) // tk)
    def _():
        _store_head_pairs(o_ref, slice(None), [_normalized(acc_ref[h]) for h in range(MLA_HEADS)])


def mla_attention(q, k, v, tq=1024, tk=512):
    b, s, hw = q.shape
    tq, tk = min(tq, s), min(tk, s)
    qi_tab, ki_tab = _causal_pairs(s // tq, tq, tk)
    qspec = pl.BlockSpec((1, tq, hw), lambda bb, p, qt, kt: (bb, qt[p], 0))
    kspec = pl.BlockSpec((1, tk, hw), lambda bb, p, qt, kt: (bb, kt[p], 0))
    return pl.pallas_call(
        functools.partial(_mla_kernel, tq=tq, tk=tk),
        grid_spec=pltpu.PrefetchScalarGridSpec(
            num_scalar_prefetch=2,
            grid=(b, int(qi_tab.shape[0])),
            in_specs=[qspec, kspec, kspec],
            out_specs=pl.BlockSpec((1, tq, OUT_W), lambda bb, p, qt, kt: (bb, qt[p], 0)),
            scratch_shapes=[pltpu.VMEM((MLA_HEADS, tq, LANE), F32), pltpu.VMEM((MLA_HEADS, tq, LANE), F32)],
        ),
        out_shape=jax.ShapeDtypeStruct((b, s, OUT_W), BF16),
        compiler_params=_params(("parallel", "arbitrary")),
        name="mla_attention",
    )(qi_tab, ki_tab, q, k, v)


def _compress_kernel(h_ref, pos_ref, w1_ref, w1g_ref, w2_ref, ext_ref, o_ref, *, is_value):
    n16 = h_ref.shape[1]
    lane = _iota((n16, LANE), 1)
    if is_value:
        extra = jnp.where(lane == ONES_LANE, 1.0, 0.0)
    else:
        extra = _bias_pieces(ext_ref[0], lane)
    bias = _dot(pos_ref[...], w1_ref[...])[0:1]
    h = h_ref[0]
    for g in range(NSA_GROUPS):
        first = _dot(h, w1g_ref[g, 0])
        second = _dot(h, w1g_ref[g, 1])
        hid = first + pltpu.roll(second, n16 - 1, 0) + bias
        act = jax.nn.gelu(hid).astype(BF16)
        o_ref[0, g] = (_dot(act, w2_ref[...]) + extra).astype(BF16)


def nsa_compress(h, cmp_pos, w1, w2, dpos_cend, is_value):
    b, n16, hw = h.shape
    pos = jnp.broadcast_to(cmp_pos.reshape(1, CMP_LEN * NSA_DH), (8, CMP_LEN * NSA_DH)).astype(BF16)
    w2p = jnp.pad(w2, ((0, 0), (0, LANE - NSA_DH))).astype(BF16)
    halves = w1.reshape(2, CMP_STRIDE, 1, NSA_DH, CMP_HIDDEN)
    w1g = jnp.stack([jnp.pad(halves, ((0, 0), (0, 0), (g, NSA_GROUPS - 1 - g), (0, 0), (0, 0))).reshape(2, hw, CMP_HIDDEN)
                     for g in range(NSA_GROUPS)]).astype(BF16)
    w1 = w1.astype(BF16)
    return pl.pallas_call(
        functools.partial(_compress_kernel, is_value=is_value),
        grid=(b,),
        in_specs=[pl.BlockSpec((1, n16, hw), lambda i: (i, 0, 0)), _const_spec(pos.shape),
                  _const_spec(w1.shape), _const_spec(w1g.shape), _const_spec(w2p.shape),
                  pl.BlockSpec((1, n16, 1), lambda i: (i, 0, 0))],
        out_specs=pl.BlockSpec((1, NSA_GROUPS, n16, LANE), lambda i: (i, 0, 0, 0)),
        out_shape=jax.ShapeDtypeStruct((b, NSA_GROUPS, n16, LANE), BF16),
        compiler_params=_params(("parallel",)),
        name="nsa_compress",
    )(h, pos, w1, w1g, w2p, dpos_cend)


def _topk_mask_t(x, k):
    n = x.shape[0]
    ridx = _iota(x.shape, 0).astype(F32)
    sel = jnp.zeros(x.shape, F32)
    for _ in range(k):
        m = jnp.max(x, axis=0, keepdims=True)
        first = jnp.min(jnp.where(x == m, ridx, float(n)), axis=0, keepdims=True)
        hit = ridx == first
        sel = jnp.where(hit, 1.0, sel)
        x = jnp.where(hit, REMOVED, x)
    return sel


def _cmp_topk_kernel(q_ref, kc_ref, vc_ref, gate_ref, oc_ref, selm1_ref, used_ref, imp_ref, *, tq, ncp, topk):
    qi = pl.program_id(1)
    t = qi * tq + _iota((tq, 1), 0)
    gates = gate_ref[0]
    blk = _iota((tq, LANE), 1)
    chunk = _shr(t, SEL_BLOCK)

    def attend(nc):
        cend = _iota((1, nc), 1) * CMP_STRIDE + (CMP_LEN - 1)
        mask = (cend <= t)[None]
        cstart = _iota((nc, LANE), 0) * CMP_STRIDE
        sstart = _iota((nc, LANE), 1) * SEL_BLOCK
        overlap = jnp.where((cstart < sstart + SEL_BLOCK) & (cstart + CMP_LEN > sstart)
                            & (cstart < (ncp - 1) * CMP_STRIDE), 1.0, 0.0).astype(BF16)
        for g in range(NSA_GROUPS):
            q = q_ref[g * NSA_HPG:(g + 1) * NSA_HPG].reshape(NSA_HPG * tq, LANE)
            s = _dot_nt(q, kc_ref[0, g, :nc]).reshape(NSA_HPG, tq, nc)
            s = jnp.where(mask, s, NEG)
            e = jnp.where(mask, jnp.exp2(s - jnp.max(s, axis=-1, keepdims=True)), 0.0)
            p = e / jnp.maximum(jnp.sum(e, axis=-1, keepdims=True), 1e-30)
            psum = jnp.sum(p, axis=0)
            o = _dot(p.reshape(NSA_HPG * tq, nc).astype(BF16), vc_ref[0, g, :nc])
            heads = [jnp.where(blk < NSA_DH, o[n * tq:(n + 1) * tq] * gates[:, g * NSA_HPG + n:g * NSA_HPG + n + 1], 0.0)
                     for n in range(NSA_HPG)]
            _store_head_pairs(oc_ref, slice(None), heads, first_pair=g * NSA_HPG // 2)
            hi = psum.astype(BF16)
            r1 = psum - hi.astype(F32)
            mid = r1.astype(BF16)
            lo = (r1 - mid.astype(F32)).astype(BF16)
            imp_ref[g] = _dot(hi, overlap) + _dot(mid, overlap) + _dot(lo, overlap)

    lane_tiles = ((qi + 1) * tq // CMP_STRIDE + LANE - 1) // LANE
    for v in range(1, ncp // LANE + 1):
        @pl.when(jnp.minimum(lane_tiles, ncp // LANE) == v)
        def _(v=v):
            attend(v * LANE)

    forced = (blk == 0) | (blk == chunk)
    imps = [jnp.where(forced, REMOVED, jnp.where(blk <= chunk, imp_ref[g], NEG)).T for g in range(NSA_GROUPS)]
    picked = _topk_mask_t(jnp.concatenate(imps, axis=1), topk - 2)
    used = jnp.zeros((1, LANE), F32)
    for g in range(NSA_GROUPS):
        sel = jnp.where(blk <= chunk, jnp.where(forced, 1.0, picked[:, g * tq:(g + 1) * tq].T), 0.0)
        selm1_ref[0, g] = (sel - 1.0).astype(BF16)
        used = jnp.maximum(used, jnp.max(sel, axis=0, keepdims=True))
    used_ref[0, 0] = jnp.broadcast_to(used, used_ref.shape[2:])


def nsa_cmp_topk(qn, kc, vc, gates, tq=256):
    _, b, s, _ = qn.shape
    tq = min(tq, s)
    ncp = kc.shape[2]
    assert s // SEL_BLOCK <= LANE
    topk = min(SEL_TOPK, s // SEL_BLOCK)
    kspec = pl.BlockSpec((1, NSA_GROUPS, ncp, LANE), lambda i, j: (i, 0, 0, 0))
    return pl.pallas_call(
        functools.partial(_cmp_topk_kernel, tq=tq, ncp=ncp, topk=topk),
        grid=(b, s // tq),
        in_specs=[pl.BlockSpec((NSA_HEADS, None, tq, LANE), lambda i, j: (0, i, j, 0)), kspec, kspec,
                  pl.BlockSpec((1, tq, LANE), lambda i, j: (i, j, 0))],
        out_specs=[pl.BlockSpec((1, tq, OUT_W), lambda i, j: (i, j, 0)),
                   pl.BlockSpec((1, NSA_GROUPS, tq, LANE), lambda i, j: (i, 0, j, 0)),
                   pl.BlockSpec((1, 1, 8, LANE), lambda i, j: (i, j, 0, 0))],
        out_shape=[jax.ShapeDtypeStruct((b, s, OUT_W), BF16), jax.ShapeDtypeStruct((b, NSA_GROUPS, s, LANE), BF16),
                   jax.ShapeDtypeStruct((b, s // tq, 8, LANE), F32)],
        scratch_shapes=[pltpu.VMEM((NSA_GROUPS, tq, LANE), F32)],
        compiler_params=_params(("parallel", "parallel")),
        name="nsa_cmp_topk",
    )(qn, kc, vc, gates)


def _nsa_write(o_ref, gate_ref, acc_ref, tq, branch):
    gates = gate_ref[0]
    heads = []
    for h in range(NSA_HEADS):
        g, n = divmod(h, NSA_HPG)
        col = branch * NSA_HEADS + h
        heads.append(_normalized(acc_ref[g, n * tq:(n + 1) * tq]) * gates[:, col:col + 1])
    _store_head_pairs(o_ref, slice(None), heads)


def _sel_attn_kernel(flags, q_ref, k_ref, v_ref, selm1_ref, gate_ref, o_ref, lhs_ref, m_ref, acc_ref, *, t, nq):
    b = pl.program_id(0)
    qi = pl.program_id(1)
    _init_state(m_ref, acc_ref)
    for g in range(NSA_GROUPS):
        lhs_ref[g, :, :LANE] = q_ref[g * NSA_HPG:(g + 1) * NSA_HPG].reshape(NSA_HPG * t, LANE)
        lhs_ref[g, :, LANE:] = jnp.concatenate([selm1_ref[0, g]] * NSA_HPG, axis=0)

    hpc = NSA_HPG // SEL_SPLIT
    half = hpc * t
    chains = [(g, i) for g in range(NSA_GROUPS) for i in range(SEL_SPLIT)]

    def update(ki, diagonal):
        rows = pl.ds(pl.multiple_of(ki * t, t), t)
        scores = [_dot_nt(lhs_ref[g, i * half:(i + 1) * half], k_ref[rows, 2 * g * LANE:(2 * g + 2) * LANE])
                  for g, i in chains]
        if diagonal:
            causal = (_iota((t, t), 1) <= _iota((t, t), 0))[None]
            scores = [jnp.where(causal, s.reshape(hpc, t, t), NEG).reshape(half, t) for s in scores]
        _flash_update_staged(scores, [v_ref[rows, g * LANE:(g + 1) * LANE] for g, _ in chains], m_ref, acc_ref,
                             [(g, slice(i * half, (i + 1) * half)) for g, i in chains])

    base = (b * nq + qi) * nq

    def body(ki, carry):
        @pl.when(flags[base + ki] > 0)
        def _():
            update(ki, False)
        return carry

    lax.fori_loop(0, qi, body, 0)
    update(qi, True)
    _nsa_write(o_ref, gate_ref, acc_ref, t, branch=1)


def nsa_selected(qn, ks, vs, selm1, used, gates):
    _, b, s, _ = qn.shape
    nq = used.shape[1]
    t = s // nq
    bpt = t // SEL_BLOCK
    flags = (used[:, :, 0].reshape(b, nq, LANE // bpt, bpt) > 0.5).any(axis=-1)[..., :nq]
    flags = flags.astype(jnp.int32).reshape(-1)
    imap_q = lambda bb, i, fl: (bb, i, 0)
    rows = NSA_HPG * t
    return pl.pallas_call(
        functools.partial(_sel_attn_kernel, t=t, nq=nq),
        grid_spec=pltpu.PrefetchScalarGridSpec(
            num_scalar_prefetch=1,
            grid=(b, nq),
            in_specs=[pl.BlockSpec((NSA_HEADS, None, t, LANE), lambda bb, i, fl: (0, bb, i, 0)),
                      pl.BlockSpec((None, s, 4 * LANE), lambda bb, i, fl: (bb, 0, 0)),
                      pl.BlockSpec((None, s, 2 * LANE), lambda bb, i, fl: (bb, 0, 0)),
                      pl.BlockSpec((1, NSA_GROUPS, t, LANE), lambda bb, i, fl: (bb, 0, i, 0)),
                      pl.BlockSpec((1, t, LANE), imap_q)],
            out_specs=pl.BlockSpec((1, t, OUT_W), imap_q),
            scratch_shapes=[pltpu.VMEM((NSA_GROUPS, rows, 2 * LANE), BF16), pltpu.VMEM((NSA_GROUPS, rows, LANE), F32),
                            pltpu.VMEM((NSA_GROUPS, rows, LANE), F32)],
        ),
        out_shape=jax.ShapeDtypeStruct((b, s, OUT_W), BF16),
        compiler_params=_params(("parallel", "arbitrary")),
        name="nsa_selected",
    )(flags, qn, ks, vs, selm1, gates)


def _win_attn_kernel(q_ref, kp_ref, kc_ref, vp_ref, vc_ref, gate_ref, o_ref, *, tq):
    qi = pl.program_id(1)
    hq = tq // 2
    span = WIN + hq
    row = _iota((hq, span), 0)
    col = _iota((hq, span), 1)
    diff = WIN + row - col
    band = (diff >= 0) & (diff < WIN)
    gates = gate_ref[0]
    chains = [(g, rh) for g in range(NSA_GROUPS) for rh in range(2)]
    scores = []
    for g, rh in chains:
        gl = slice(g * LANE, (g + 1) * LANE)
        q = q_ref[g * NSA_HPG:(g + 1) * NSA_HPG, rh * hq:(rh + 1) * hq].reshape(NSA_HPG * hq, LANE)
        s = jnp.concatenate([_dot_nt(q, kp_ref[0, rh * hq:, gl]), _dot_nt(q, kc_ref[0, :(rh + 1) * hq, gl])], axis=1)
        valid = (band & (col >= WIN - qi * tq - rh * hq))[None]
        scores.append(jnp.where(valid, s.reshape(NSA_HPG, hq, span), NEG).reshape(NSA_HPG * hq, span))
    tops = [jnp.max(s, axis=-1, keepdims=True) for s in scores]
    probs = [jnp.exp2(s - m).astype(BF16) for s, m in zip(scores, tops)]
    for (g, rh), p in zip(chains, probs):
        gl = slice(g * LANE, (g + 1) * LANE)
        n_prev = tq - rh * hq
        acc = _dot(p[:, :n_prev], vp_ref[0, rh * hq:, gl]) + _dot(p[:, n_prev:], vc_ref[0, :(rh + 1) * hq, gl])
        rows = slice(rh * hq, (rh + 1) * hq)
        cols = [2 * NSA_HEADS + g * NSA_HPG + n for n in range(NSA_HPG)]
        heads = [_normalized(acc[n * hq:(n + 1) * hq]) * gates[rows, cols[n]:cols[n] + 1] for n in range(NSA_HPG)]
        _store_head_pairs(o_ref, rows, heads, first_pair=g * NSA_HPG // 2)


def nsa_window(qn, kw, vw, gates, tq=512):
    _, b, s, _ = qn.shape
    assert tq == WIN and s % tq == 0
    prev = pl.BlockSpec((1, tq, 2 * LANE), lambda bb, i: (bb, jnp.maximum(i - 1, 0), 0))
    cur = pl.BlockSpec((1, tq, 2 * LANE), lambda bb, i: (bb, i, 0))
    return pl.pallas_call(
        functools.partial(_win_attn_kernel, tq=tq),
        grid=(b, s // tq),
        in_specs=[pl.BlockSpec((NSA_HEADS, None, tq, LANE), lambda bb, i: (0, bb, i, 0)), prev, cur, prev, cur,
                  pl.BlockSpec((1, tq, LANE), lambda bb, i: (bb, i, 0))],
        out_specs=pl.BlockSpec((1, tq, OUT_W), lambda bb, i: (bb, i, 0)),
        out_shape=jax.ShapeDtypeStruct((b, s, OUT_W), BF16),
        compiler_params=_params(("parallel", "parallel")),
        name="nsa_window",
    )(qn, kw, kw, vw, vw, gates)


def _odd_out_ln_kernel(x_ref, o0_ref, o1_ref, o2_ref, l0_ref, l1_ref, l2_ref, w_ref, g_ref, b_ref, o_ref,
                       o_scr, l_scr):
    tm = x_ref.shape[0]
    for gi, (src, lsrc) in enumerate(((o0_ref, l0_ref), (o1_ref, l1_ref), (o2_ref, l2_ref))):
        dil = DIL_PATTERNS[gi][1]
        for r in range(dil):
            rows = pl.ds(r, tm // dil, stride=dil) if dil > 1 else slice(None)
            for h in range(DIL_GH):
                o_scr[gi, h, rows, :] = src[r, :, h * DIL_DH:(h + 1) * DIL_DH].astype(F32)
            l_scr[gi, rows, :] = lsrc[r]
    lses = [l_scr[gi] for gi in range(DIL_GROUPS)]
    top = jnp.maximum(jnp.maximum(lses[0], lses[1]), lses[2])
    es = [jnp.exp(l - top) for l in lses]
    den = es[0] + es[1] + es[2]
    wts = [e / den for e in es]
    cols = []
    for h in range(DIL_GH):
        merged = None
        for gi in range(DIL_GROUPS):
            term = wts[gi][:, h:h + 1] * o_scr[gi, h]
            merged = term if merged is None else merged + term
        cols.append(merged.astype(BF16))
    mix = _dot(jnp.concatenate(cols, axis=1), w_ref[...])
    o_ref[...] = _layer_norm(ALPHA * x_ref[...] + mix, g_ref[...], b_ref[...])


def odd_out_ln(x2, outs, lses, w, g, b, tm=1024):
    m, d = x2.shape
    b_, _, s = outs[0].shape[0], None, outs[0].shape[1] * outs[0].shape[2]
    tm = min(tm, s)
    width = DIL_GH * DIL_DH
    row = pl.BlockSpec((None, tm, d), lambda bb, i: (bb, i, 0))
    cls = lambda gi, wd: pl.BlockSpec((None, DIL_PATTERNS[gi][1], tm // DIL_PATTERNS[gi][1], wd),
                                      lambda bb, i: (bb, 0, i, 0))
    out = pl.pallas_call(
        _odd_out_ln_kernel,
        grid=(b_, s // tm),
        in_specs=[row] + [cls(gi, width) for gi in range(DIL_GROUPS)] + [cls(gi, LANE) for gi in range(DIL_GROUPS)]
        + [_const_spec(w.shape), _const_spec((1, d)), _const_spec((1, d))],
        out_specs=row,
        out_shape=jax.ShapeDtypeStruct((b_, s, d), F32),
        scratch_shapes=[pltpu.VMEM((DIL_GROUPS, DIL_GH, tm, DIL_DH), F32), pltpu.VMEM((DIL_GROUPS, tm, LANE), F32)],
        compiler_params=_params(("parallel", "parallel")),
        name="odd_out_ln",
    )(x2.reshape(b_, s, d), *outs, *lses, w, g.reshape(1, d), b.reshape(1, d))
    return out.reshape(m, d)


def _odd_proj_kernel(x_ref, w_ref, o0_ref, o1_ref, o2_ref, xs_ref):
    tm, d_model = x_ref.shape
    width = DIL_GH * DIL_DH
    n_chunks = d_model // LANE
    for c in range(n_chunks):
        xs_ref[c] = x_ref[:, c * LANE:(c + 1) * LANE]
    for gi, o_ref in enumerate((o0_ref, o1_ref, o2_ref)):
        dil = DIL_PATTERNS[gi][1]
        if dil == 1:
            xg = x_ref[...]
        else:
            xg = jnp.concatenate(
                [jnp.concatenate([xs_ref[c, pl.ds(r, tm // dil, stride=dil), :] for r in range(dil)], axis=0)
                 for c in range(n_chunks)], axis=1)
        xg = xg.astype(BF16)
        for c in range(3):
            col = (gi * 3 + c) * width
            y = _dot(xg, w_ref[:, col:col + width])
            if c == 0:
                y = y * (DIL_SCALE * LOG2E)
            o_ref[:, :, c * width:(c + 1) * width] = y.astype(BF16).reshape(dil, tm // dil, width)


def odd_proj(x2, b, s, w_in, tm=512):
    m, d = x2.shape
    tm = min(tm, s)
    width = DIL_GH * DIL_DH
    w = w_in.reshape(d, 3, DIL_GROUPS, width).transpose(0, 2, 1, 3).reshape(d, 3 * DIL_GROUPS * width).astype(BF16)
    dils = [dil for _, dil in DIL_PATTERNS]
    return pl.pallas_call(
        _odd_proj_kernel,
        grid=(b, s // tm),
        in_specs=[pl.BlockSpec((None, tm, d), lambda bb, i: (bb, i, 0)), _const_spec(w.shape)],
        out_specs=[pl.BlockSpec((None, dil, tm // dil, 3 * width), lambda bb, i: (bb, 0, i, 0)) for dil in dils],
        out_shape=[jax.ShapeDtypeStruct((b, dil, s // dil, 3 * width), BF16) for dil in dils],
        scratch_shapes=[pltpu.VMEM((d // LANE, tm, LANE), F32)],
        compiler_params=_params(("parallel", "parallel")),
        name="odd_proj",
    )(x2.reshape(b, s, d), w)


def _dil_attn_kernel(slope_tab, pos0_tab, q_ref, kc_ref, kp_ref, vc_ref, vp_ref, pc_ref, pp_ref, pq_ref, o_ref,
                     lse_ref, *, group, tq):
    sub = DIL_SPAN
    b = pl.program_id(0)
    jt = pl.program_id(2)
    row = _iota((sub, 2 * sub), 0)
    col = _iota((sub, 2 * sub), 1)
    band = (col >= row) & (col <= row + sub)
    band_bias = jnp.where(band, 0.0, NEG)
    start_bias = jnp.where(band & (col >= sub), 0.0, NEG)
    pos0 = pos0_tab[b]
    lane = _iota((sub, LANE), 1)
    heads = range(DIL_GH)
    hs = [slice(h * DIL_DH, (h + 1) * DIL_DH) for h in heads]
    slopes = [slope_tab[group * DIL_GH + h] for h in heads]
    tiles = range(tq // sub)
    cur = [slice(i * sub, (i + 1) * sub) for i in tiles]
    biases, ks, vs = [], [], []
    for i in tiles:
        if i == 0:
            mask_bias = jnp.where(jt == 0, start_bias, band_bias)
            pk = jnp.concatenate([pp_ref[...], pc_ref[:, :sub]], axis=1)
            ks.append([jnp.concatenate([kp_ref[:, hs[h]], kc_ref[:sub, hs[h]]], axis=0) for h in heads])
            vs.append([jnp.concatenate([vp_ref[:, hs[h]], vc_ref[:sub, hs[h]]], axis=0) for h in heads])
        else:
            mask_bias = band_bias
            pk = pc_ref[:, (i - 1) * sub:(i + 1) * sub]
            ks.append([kc_ref[(i - 1) * sub:(i + 1) * sub, hs[h]] for h in heads])
            vs.append([vc_ref[(i - 1) * sub:(i + 1) * sub, hs[h]] for h in heads])
        dpos = pk - pos0
        biases.append([mask_bias + slopes[h] * dpos for h in heads])
    chains = [(i, h) for i in tiles for h in heads]
    ss = [_dot_nt(q_ref[cur[i], hs[h]], ks[i][h]) + biases[i][h] for i, h in chains]
    ms = [jnp.max(s, axis=-1, keepdims=True) for s in ss]
    ps = [jnp.exp2(s - m) for s, m in zip(ss, ms)]
    dens = [jnp.sum(p, axis=-1, keepdims=True) for p in ps]
    for n, (i, h) in enumerate(chains):
        o_ref[cur[i], hs[h]] = (_dot(ps[n].astype(BF16), vs[i][h]) / dens[n]).astype(BF16)
    for i in tiles:
        dq = (pq_ref[cur[i], :] - pos0) * LN2
        lse_tile = jnp.zeros((sub, LANE), F32)
        for h in heads:
            n = i * DIL_GH + h
            lse_tile = jnp.where(lane == h, ms[n] * LN2 + jnp.log(dens[n]) - slopes[h] * dq, lse_tile)
        lse_ref[cur[i], :] = lse_tile


def dilated_group_attention(qkv, pos_f, group, tq=1024):
    b, d, sd, _ = qkv.shape
    w, dil = DIL_PATTERNS[group]
    assert w // dil == DIL_SPAN and dil == d
    tq = min(tq, sd)
    sub = DIL_SPAN
    width = DIL_GH * DIL_DH
    n_slopes = DIL_GROUPS * DIL_GH
    slopes = 2.0 ** (-ALIBI_MAX_BIAS * jnp.arange(1, n_slopes + 1, dtype=F32) / n_slopes) * LOG2E
    pos0 = pos_f[:, 0]
    posc = pos_f.reshape(b, sd, d).transpose(0, 2, 1).reshape(b, d, 1, sd)
    r = tq // sub
    cur = lambda c: pl.BlockSpec((None, None, tq, width), lambda bb, rr, j, *_: (bb, rr, j, c))
    prev = lambda c: pl.BlockSpec((None, None, sub, width),
                                  lambda bb, rr, j, *_: (bb, rr, jnp.maximum(j * r - 1, 0), c))
    return pl.pallas_call(
        functools.partial(_dil_attn_kernel, group=group, tq=tq),
        grid_spec=pltpu.PrefetchScalarGridSpec(
            num_scalar_prefetch=2,
            grid=(b, d, sd // tq),
            in_specs=[cur(0), cur(1), prev(1), cur(2), prev(2),
                      pl.BlockSpec((None, None, 1, tq), lambda bb, rr, j, *_: (bb, rr, 0, j)),
                      pl.BlockSpec((None, None, 1, sub), lambda bb, rr, j, *_: (bb, rr, 0, jnp.maximum(j * r - 1, 0))),
                      pl.BlockSpec((None, None, tq, 1), lambda bb, rr, j, *_: (bb, rr, j, 0))],
            out_specs=[pl.BlockSpec((None, None, tq, width), lambda bb, rr, j, *_: (bb, rr, j, 0)),
                       pl.BlockSpec((None, None, tq, LANE), lambda bb, rr, j, *_: (bb, rr, j, 0))],
        ),
        out_shape=[jax.ShapeDtypeStruct((b, d, sd, width), BF16), jax.ShapeDtypeStruct((b, d, sd, LANE), F32)],
        compiler_params=_params(("parallel", "parallel", "parallel")),
        name=f"dilated_attention_g{group}",
    )(slopes, pos0, qkv, qkv, qkv, qkv, qkv, posc, posc, posc.reshape(b, d, sd, 1))


def even_mixer(x2, b, s, pos_f, w_in, q_norm_g, kv_norm_g, w_uq, w_uk, w_uv,
               cmp_pos, cmp_k_w1, cmp_k_w2, cmp_v_w1, cmp_v_w2, w_out, ln_g, ln_b):
    m = b * s
    half = MLA_ROPE // 2
    inv = ROPE_THETA ** (-jnp.arange(half, dtype=F32) / half)
    ang = (pos_f[..., None] * inv).reshape(m, half)
    ones = jnp.ones((m, MLA_NOPE), F32)
    zeros = jnp.zeros((m, LANE - MLA_NOPE - MLA_ROPE), F32)
    cos_t = jnp.concatenate([ones, jnp.cos(ang), jnp.cos(ang), zeros], axis=1)
    sin_t = jnp.concatenate([0.0 * ones, jnp.sin(ang), jnp.sin(ang), zeros], axis=1)
    dpos = (pos_f - pos_f[:, :1]) * LOG2E
    ncp = s // CMP_STRIDE
    dpos_cend = jnp.pad(dpos[:, CMP_LEN - 1::CMP_STRIDE], ((0, 0), (0, 1))).reshape(b, ncp, 1)

    qm, km, vm, qn, kvc, ks, kw, vs, vw, gates = even_proj(
        x2, cos_t, sin_t, dpos.reshape(m, 1), s, w_in, q_norm_g, kv_norm_g, w_uq, w_uk, w_uv)
    o_mla = mla_attention(qm.reshape(b, s, HW), km.reshape(b, s, HW), vm.reshape(b, s, HW))

    kvc = kvc.reshape(2, b, ncp, CMP_STRIDE * LANE)
    kc = nsa_compress(kvc[0], cmp_pos, cmp_k_w1, cmp_k_w2, dpos_cend, is_value=False)
    vc = nsa_compress(kvc[1], cmp_pos, cmp_v_w1, cmp_v_w2, dpos_cend, is_value=True)
    qn = qn.reshape(NSA_HEADS, b, s, LANE)
    gates = gates.reshape(b, s, LANE)
    o_c, selm1, used = nsa_cmp_topk(qn, kc, vc, gates)
    o_s = nsa_selected(qn, ks.reshape(b, s, 4 * LANE), vs.reshape(b, s, 2 * LANE), selm1, used, gates)
    o_w = nsa_window(qn, kw.reshape(b, s, 2 * LANE), vw.reshape(b, s, 2 * LANE), gates)

    outs = [o.reshape(m, OUT_W) for o in (o_mla, o_c, o_s, o_w)]
    return outs, [w_out.astype(BF16)], ln_g, ln_b


def odd_mixer_ln(x2, b, s, pos_f, w_in, w_out, ln_g, ln_b):
    qkvs = odd_proj(x2, b, s, w_in)
    parts = [dilated_group_attention(qkvs[g], pos_f, g) for g in range(DIL_GROUPS)]
    return odd_out_ln(x2, [p[0] for p in parts], [p[1] for p in parts], w_out.astype(BF16), ln_g, ln_b)


def kernel(x, positions, ln1_g, ln1_b, ffn1_w_gate, ffn1_w_up, ffn1_w_down, mix_in_even, mla_q_norm, mla_kv_norm, mla_w_uq, mla_w_uk, mla_w_uv, nsa_cmp_pos, nsa_cmp_k_w1, nsa_cmp_k_w2, nsa_cmp_v_w1, nsa_cmp_v_w2, mix_out_even, mix_in_odd, mix_out_odd, ln2_g, ln2_b, ffn2_w_gate, ffn2_w_up, ffn2_w_down, ln3_g, ln3_b):
    b, s, d = x.shape
    x2 = x.reshape(b * s, d)
    pos_f = positions.astype(F32)
    for i in range(DEPTH):
        j = i // 2
        x2 = ffn_ln(x2, ffn1_w_gate[i].astype(BF16), ffn1_w_up[i].astype(BF16), ffn1_w_down[i].astype(BF16),
                    ln1_g[i], ln1_b[i])
        mix = None
        if i % 2 == 0:
            mix = even_mixer(x2, b, s, pos_f, mix_in_even[j], mla_q_norm[j], mla_kv_norm[j], mla_w_uq[j],
                                mla_w_uk[j], mla_w_uv[j], nsa_cmp_pos[j], nsa_cmp_k_w1[j], nsa_cmp_k_w2[j],
                                nsa_cmp_v_w1[j], nsa_cmp_v_w2[j], mix_out_even[j], ln2_g[i], ln2_b[i])
        else:
            x2 = odd_mixer_ln(x2, b, s, pos_f, mix_in_odd[j], mix_out_odd[j], ln2_g[i], ln2_b[i])
        x2 = ffn_ln(x2, ffn2_w_gate[i].astype(BF16), ffn2_w_up[i].astype(BF16), ffn2_w_down[i].astype(BF16),
                    ln3_g[i], ln3_b[i], mix=mix)
    return x2.reshape(b, s, d)
```

```python
import functools
import math

import numpy as np
import jax
import jax.numpy as jnp
from jax import lax
from jax.experimental import pallas as pl
from jax.experimental.pallas import tpu as pltpu

F32 = jnp.float32
BF16 = jnp.bfloat16

DEPTH = 2
LN_EPS = 1e-5
RMS_EPS = 1e-6
ALPHA = (2 * DEPTH) ** 0.25
HALF_STEP = 0.5
NEG = -1e30
BIG = 1e9
MASK_BIG = 1e30
REMOVED = -3.0e38
ALIBI_MAX_BIAS = 8.0
LOG2E = math.log2(math.e)
LN2 = math.log(2.0)
LANE = 128

MLA_HEADS = 8
MLA_Q_RANK = 384
MLA_KV_RANK = 256
MLA_NOPE = 64
MLA_ROPE = 32
MLA_V = 64
ROPE_THETA = 10000.0
MLA_SCALE = (MLA_NOPE + MLA_ROPE) ** -0.5
MLA_CHAINS = 4

NSA_HEADS = 8
NSA_GROUPS = 2
NSA_HPG = 4
NSA_DH = 64
CMP_LEN = 32
CMP_STRIDE = 16
CMP_HIDDEN = 256
SEL_BLOCK = 64
SEL_TOPK = 16
WIN = 512
NSA_SCALE = NSA_DH ** -0.5
SEL_SPLIT = 2
NSA_SLOPES = tuple(2.0 ** (-ALIBI_MAX_BIAS * (i + 1) / NSA_HEADS) for i in range(NSA_HEADS))
ONES_LANE = 64
BIAS_LANES = (64, 65, 66)

DIL_PATTERNS = ((128, 1), (512, 4), (2048, 16))
DIL_GROUPS = 3
DIL_GH = 4
DIL_DH = 128
DIL_SCALE = DIL_DH ** -0.5
DIL_SPAN = 128

VMEM_LIMIT = 48 * 1024 * 1024
MIXED_FFN_VMEM_LIMIT = 56 * 1024 * 1024


def _iota(shape, dim):
    return lax.broadcasted_iota(jnp.int32, shape, dim)


def _shr(x, pow2):
    return jnp.right_shift(x, int(pow2).bit_length() - 1)


def _dot(a, b):
    return jnp.dot(a, b, preferred_element_type=F32)


def _dot_nt(a, b):
    return lax.dot_general(a, b, (((1,), (1,)), ((), ())), preferred_element_type=F32)


def _const_spec(shape):
    zeros = (0,) * len(shape)
    return pl.BlockSpec(shape, lambda *_: zeros, pipeline_mode=pl.Buffered(1))


def _params(sem):
    return pltpu.CompilerParams(dimension_semantics=sem, vmem_limit_bytes=VMEM_LIMIT)


def _layer_norm(z, g, b):
    mu = jnp.mean(z, axis=-1, keepdims=True)
    zc = z - mu
    var = jnp.mean(zc * zc, axis=-1, keepdims=True)
    return zc * lax.rsqrt(var + LN_EPS) * g + b


def _rms_norm(z, g):
    return z * lax.rsqrt(jnp.mean(z * z, axis=-1, keepdims=True) + RMS_EPS) * g


def _bias_pieces(d, lane):
    x = jnp.broadcast_to(d, lane.shape)
    hi = x.astype(BF16).astype(F32)
    r = x - hi
    mid = r.astype(BF16).astype(F32)
    lo = (r - mid).astype(BF16).astype(F32)
    return jnp.where(lane == BIAS_LANES[0], hi,
                     jnp.where(lane == BIAS_LANES[1], mid, jnp.where(lane == BIAS_LANES[2], lo, 0.0)))


def _flash_update_staged(scores, values, m_ref, acc_ref, idxs):
    m_old = [m_ref[i] for i in idxs]
    m_new = [jnp.maximum(mo, jnp.max(s, axis=-1, keepdims=True)) for mo, s in zip(m_old, scores)]
    ps = [jnp.exp2(s - jnp.tile(mn, (1, s.shape[1] // LANE))).astype(BF16) for s, mn in zip(scores, m_new)]
    for i, mo, mn, p, v in zip(idxs, m_old, m_new, ps, values):
        acc_ref[i] = jnp.exp2(mo - mn) * acc_ref[i] + _dot(p, v)
        m_ref[i] = mn


def _init_state(m_ref, acc_ref):
    m_ref[...] = jnp.full(m_ref.shape, NEG, F32)
    acc_ref[...] = jnp.zeros(acc_ref.shape, F32)


def _normalized(acc):
    lane = _iota(acc.shape, 1)
    o = acc / jnp.maximum(acc[:, ONES_LANE:ONES_LANE + 1], 1e-30)
    return jnp.where(lane < ONES_LANE, o, 0.0)


def _store_head_pairs(o_ref, rows, heads, first_pair=0):
    for pr in range(len(heads) // 2):
        packed = heads[2 * pr] + pltpu.roll(heads[2 * pr + 1], LANE // 2, 1)
        o_ref[0, rows, (first_pair + pr) * LANE:(first_pair + pr + 1) * LANE] = packed.astype(BF16)


def _ffn_ln_kernel(*refs, n_chunks, mixed):
    if mixed:
        x_ref, om_ref, oc_ref, os_ref, ow_ref, w_ref, g2_ref, b2_ref = refs[:8]
        nsa = (oc_ref[...].astype(F32) + os_ref[...].astype(F32) + ow_ref[...].astype(F32)).astype(BF16)
        mix = _dot(jnp.concatenate([om_ref[...], nsa], axis=1), w_ref[...])
        x = _layer_norm(ALPHA * x_ref[...] + mix, g2_ref[...], b2_ref[...])
        refs = refs[8:]
    else:
        x = refs[0][...]
        refs = refs[1:]
    wg_ref, wu_ref, wd_ref, g_ref, b_ref, o_ref = refs
    xb = x.astype(BF16)
    c = wg_ref.shape[1] // n_chunks
    y = None
    for i in range(n_chunks):
        gt = _dot(xb, wg_ref[:, i * c:(i + 1) * c])
        up = _dot(xb, wu_ref[:, i * c:(i + 1) * c])
        h = (gt * jax.nn.sigmoid(gt) * up).astype(BF16)
        part = _dot(h, wd_ref[i * c:(i + 1) * c, :])
        y = part if y is None else y + part
    o_ref[...] = _layer_norm(ALPHA * x + HALF_STEP * y, g_ref[...], b_ref[...])


def ffn_ln(x2, wg, wu, wd, g, b, mix=None, tm=1024, n_chunks=11):
    m, d = x2.shape
    tm = min(tm, m)
    row = lambda w: pl.BlockSpec((tm, w), lambda i: (i, 0))
    args, specs = [x2], [row(d)]
    if mix is not None:
        outs, ws, g2, b2 = mix
        args += list(outs) + list(ws) + [g2.reshape(1, d), b2.reshape(1, d)]
        specs += [row(o.shape[1]) for o in outs] + [_const_spec(w.shape) for w in ws] + [_const_spec((1, d))] * 2
    args += [wg, wu, wd, g.reshape(1, d), b.reshape(1, d)]
    specs += [_const_spec(wg.shape), _const_spec(wu.shape), _const_spec(wd.shape), _const_spec((1, d)), _const_spec((1, d))]
    limit = VMEM_LIMIT if mix is None else MIXED_FFN_VMEM_LIMIT
    return pl.pallas_call(
        functools.partial(_ffn_ln_kernel, n_chunks=n_chunks, mixed=mix is not None),
        grid=(m // tm,),
        in_specs=specs,
        out_specs=row(d),
        out_shape=jax.ShapeDtypeStruct((m, d), F32),
        compiler_params=pltpu.CompilerParams(dimension_semantics=("parallel",), vmem_limit_bytes=limit),
        name="ffn_ln" if mix is None else "mix_ffn_ln",
    )(*args)


HW = MLA_HEADS * LANE
OUT_W = MLA_HEADS * MLA_V
EVEN_X_COLS = (MLA_Q_RANK, MLA_KV_RANK, LANE, LANE, HW, 2 * LANE, 8 * LANE, LANE)
EVEN_X_OFFS = tuple(int(v) for v in np.cumsum((0,) + EVEN_X_COLS))


def _even_proj_kernel(x_ref, tok_ref, wx_ref, qg_ref, kvg_ref, wuq_ref, wuqs_ref, wuk_ref,
                      wuv_ref, slope_ref, qm_ref, km_ref, vm_ref, qn_ref, kvc_ref, ks_ref, kw_ref, vs_ref, vw_ref,
                      gate_ref, kvc_scr, *, tiles_per_seq):
    tm = x_ref.shape[0]
    xb = x_ref[...].astype(BF16)
    lane = _iota((tm, LANE), 1)
    table = tok_ref[...]
    half = MLA_ROPE // 2
    lo = (lane >= MLA_NOPE) & (lane < MLA_NOPE + half)
    hi = (lane >= MLA_NOPE + half) & (lane < MLA_NOPE + MLA_ROPE)
    shifted = lambda k: pltpu.roll(table, k, 1)
    cos = jnp.where(lo, shifted(MLA_NOPE), jnp.where(hi, shifted(MLA_NOPE + half), jnp.where(lane < MLA_NOPE, 1.0, 0.0)))
    sin = jnp.where(lo, shifted(MLA_NOPE - half), jnp.where(hi, shifted(MLA_NOPE), 0.0))
    ones_lane = jnp.where(lane == ONES_LANE, 1.0, 0.0)
    pos_term = _bias_pieces(table[:, MLA_ROPE:MLA_ROPE + 1], lane)
    tok = (pl.program_id(0) % tiles_per_seq) * tm + _iota((tm, LANE), 0)
    block_onehot = jnp.where(lane == _shr(tok, SEL_BLOCK), MASK_BIG, 0.0).astype(BF16)

    def xdot(i):
        return _dot(xb, wx_ref[:, EVEN_X_OFFS[i]:EVEN_X_OFFS[i + 1]])

    cq = _rms_norm(xdot(0), qg_ref[...]).astype(BF16)
    ckv = _rms_norm(xdot(1), kvg_ref[...]).astype(BF16)
    rope_pair = _dot(xb, wx_ref[:, EVEN_X_OFFS[2]:EVEN_X_OFFS[4]])
    k_rot = rope_pair[:, :LANE] * cos + rope_pair[:, LANE:] * sin
    q_all, q_swap, k_all, v_all = (_dot(a, w[...]) for a, w in ((cq, wuq_ref), (cq, wuqs_ref), (ckv, wuk_ref),
                                                                  (ckv, wuv_ref)))
    for h in range(MLA_HEADS):
        sl = slice(h * LANE, (h + 1) * LANE)
        q = q_all[:, sl] * cos + q_swap[:, sl] * sin
        qm_ref[:, sl] = (q * (MLA_SCALE * LOG2E)).astype(BF16)
        km_ref[:, sl] = (k_all[:, sl] + k_rot).astype(BF16)
        vm_ref[:, sl] = (v_all[:, sl] + ones_lane).astype(BF16)
    qn = xdot(4) * (NSA_SCALE * LOG2E) + slope_ref[...]
    for h in range(NSA_HEADS):
        qn_ref[h] = qn[:, h * LANE:(h + 1) * LANE].astype(BF16)
    kvc = xdot(5)
    for j in range(2):
        kvc_scr[j] = kvc[:, j * LANE:(j + 1) * LANE]
        for l in range(CMP_STRIDE):
            kvc_ref[j, :, l * LANE:(l + 1) * LANE] = kvc_scr[j, pl.ds(l, tm // CMP_STRIDE, stride=CMP_STRIDE), :].astype(BF16)
    kv8 = xdot(6)
    blk = lambda i: kv8[:, i * LANE:(i + 1) * LANE]
    for g in range(NSA_GROUPS):
        ks_ref[:, 2 * g * LANE:(2 * g + 1) * LANE] = (blk(g) + pos_term).astype(BF16)
        ks_ref[:, (2 * g + 1) * LANE:(2 * g + 2) * LANE] = block_onehot
        kw_ref[:, g * LANE:(g + 1) * LANE] = (blk(2 + g) + pos_term).astype(BF16)
        vs_ref[:, g * LANE:(g + 1) * LANE] = (blk(4 + g) + ones_lane).astype(BF16)
        vw_ref[:, g * LANE:(g + 1) * LANE] = (blk(6 + g) + ones_lane).astype(BF16)
    gate_ref[...] = jax.nn.sigmoid(xdot(7))


def _head_blocks(w, n_heads, width):
    k = w.shape[0]
    w = jnp.pad(w.reshape(k, n_heads, width), ((0, 0), (0, 0), (0, LANE - width)))
    return w.reshape(k, n_heads * LANE)


def even_proj(x2, tok_table, seq, w_in, q_norm_g, kv_norm_g, w_uq, w_uk, w_uv, tm=512):
    m, d = x2.shape
    tm = min(tm, seq)
    half = MLA_ROPE // 2
    gw = NSA_GROUPS * NSA_DH
    cuts = np.cumsum((MLA_Q_RANK, MLA_KV_RANK, MLA_ROPE, NSA_HEADS * NSA_DH) + (gw,) * 6)
    cuts = [0] + [int(c) for c in cuts]
    w_cq, w_ckv, w_kpe, w_q = (w_in[:, cuts[i]:cuts[i + 1]] for i in range(4))
    w_kc, w_vc, w_ks, w_vs, w_kw, w_vw = (w_in[:, cuts[4 + i]:cuts[5 + i]] for i in range(6))
    w_gate = w_in[:, cuts[10]:]
    w_kpe_sw = jnp.concatenate([-w_kpe[:, half:], w_kpe[:, :half]], axis=1)
    rope_pad = ((0, 0), (MLA_NOPE, LANE - MLA_NOPE - MLA_ROPE))
    w_gate_blk = jnp.pad(w_gate, ((0, 0), (0, LANE - w_gate.shape[1])))
    wx = jnp.concatenate(
        [w_cq, w_ckv, jnp.pad(w_kpe, rope_pad), jnp.pad(w_kpe_sw, rope_pad), _head_blocks(w_q, NSA_HEADS, NSA_DH),
         w_kc, w_vc] + [_head_blocks(w, NSA_GROUPS, NSA_DH) for w in (w_ks, w_kw, w_vs, w_vw)] + [w_gate_blk],
        axis=1).astype(BF16)

    qd = MLA_NOPE + MLA_ROPE
    uq = w_uq.reshape(MLA_Q_RANK, MLA_HEADS, qd)
    uq_sw = jnp.concatenate([jnp.zeros_like(uq[..., :MLA_NOPE]), -uq[..., MLA_NOPE + half:],
                             uq[..., MLA_NOPE:MLA_NOPE + half]], axis=-1)
    wuq = _head_blocks(uq.reshape(MLA_Q_RANK, -1), MLA_HEADS, qd).astype(BF16)
    wuqs = _head_blocks(uq_sw.reshape(MLA_Q_RANK, -1), MLA_HEADS, qd).astype(BF16)
    wuk = _head_blocks(w_uk, MLA_HEADS, MLA_NOPE).astype(BF16)
    wuv = _head_blocks(w_uv, MLA_HEADS, MLA_V).astype(BF16)
    slope_row = np.zeros((1, HW), np.float32)
    for h in range(NSA_HEADS):
        for ln in BIAS_LANES:
            slope_row[0, h * LANE + ln] = NSA_SLOPES[h]

    row = lambda w: pl.BlockSpec((tm, w), lambda i: (i, 0))
    sds = jax.ShapeDtypeStruct
    return pl.pallas_call(
        functools.partial(_even_proj_kernel, tiles_per_seq=seq // tm),
        grid=(m // tm,),
        in_specs=[row(d), row(LANE), _const_spec(wx.shape), _const_spec((1, MLA_Q_RANK)),
                  _const_spec((1, MLA_KV_RANK)), _const_spec(wuq.shape), _const_spec(wuqs.shape),
                  _const_spec(wuk.shape), _const_spec(wuv.shape), _const_spec((1, HW))],
        out_specs=[row(HW), row(HW), row(HW), pl.BlockSpec((NSA_HEADS, tm, LANE), lambda i: (0, i, 0)),
                   pl.BlockSpec((2, tm // CMP_STRIDE, CMP_STRIDE * LANE), lambda i: (0, i, 0)), row(4 * LANE),
                   row(2 * LANE), row(2 * LANE), row(2 * LANE), row(LANE)],
        out_shape=[sds((m, HW), BF16)] * 3 + [sds((NSA_HEADS, m, LANE), BF16),
                                              sds((2, m // CMP_STRIDE, CMP_STRIDE * LANE), BF16),
                                              sds((m, 4 * LANE), BF16), sds((m, 2 * LANE), BF16),
                                              sds((m, 2 * LANE), BF16), sds((m, 2 * LANE), BF16), sds((m, LANE), F32)],
        scratch_shapes=[pltpu.VMEM((2, tm, LANE), F32)],
        compiler_params=_params(("parallel",)),
        name="even_proj",
    )(x2, tok_table, wx, q_norm_g.reshape(1, -1), kv_norm_g.reshape(1, -1), wuq, wuqs, wuk, wuv,
      jnp.asarray(slope_row))


def _causal_pairs(nq, tq, tk):
    qi, ki = [], []
    for i in range(nq):
        for j in range(((i + 1) * tq - 1) // tk + 1):
            qi.append(i)
            ki.append(j)
    return jnp.asarray(qi, jnp.int32), jnp.asarray(ki, jnp.int32)


def _mla_kernel(qi_tab, ki_tab, q_ref, k_ref, v_ref, o_ref, m_ref, acc_ref, *, tq, tk):
    p = pl.program_id(1)
    qi = qi_tab[p]
    ki = ki_tab[p]

    @pl.when(ki == 0)
    def _():
        _init_state(m_ref, acc_ref)

    def run(rows, diagonal):
        if diagonal:
            mask = _iota((tk, tk), 0) >= _iota((tk, tk), 1)
        for h0 in range(0, MLA_HEADS, MLA_CHAINS):
            heads = range(h0, h0 + MLA_CHAINS)
            scores = [_dot_nt(q_ref[0, rows, h * LANE:(h + 1) * LANE], k_ref[0, :, h * LANE:(h + 1) * LANE])
                      for h in heads]
            if diagonal:
                scores = [jnp.where(mask, s, NEG) for s in scores]
            _flash_update_staged(scores, [v_ref[0, :, h * LANE:(h + 1) * LANE] for h in heads], m_ref, acc_ref,
                                 [(h, rows) for h in heads])

    blocks = tq // tk
    for r in range(blocks):
        rows = slice(r * tk, (r + 1) * tk)
        own = qi * blocks + r

        @pl.when(ki < own)
        def _(rows=rows):
            run(rows, False)

        @pl.when(ki == own)
        def _(rows=rows):
            run(rows, True)

    @pl.when(ki == ((qi + 1) * tq - 1) // tk)
    def _():
        _store_head_pairs(o_ref, slice(None), [_normalized(acc_ref[h]) for h in range(MLA_HEADS)])


def mla_attention(q, k, v, tq=1024, tk=512):
    b, s, hw = q.shape
    tq, tk = min(tq, s), min(tk, s)
    qi_tab, ki_tab = _causal_pairs(s // tq, tq, tk)
    qspec = pl.BlockSpec((1, tq, hw), lambda bb, p, qt, kt: (bb, qt[p], 0))
    kspec = pl.BlockSpec((1, tk, hw), lambda bb, p, qt, kt: (bb, kt[p], 0))
    return pl.pallas_call(
        functools.partial(_mla_kernel, tq=tq, tk=tk),
        grid_spec=pltpu.PrefetchScalarGridSpec(
            num_scalar_prefetch=2,
            grid=(b, int(qi_tab.shape[0])),
            in_specs=[qspec, kspec, kspec],
            out_specs=pl.BlockSpec((1, tq, OUT_W), lambda bb, p, qt, kt: (bb, qt[p], 0)),
            scratch_shapes=[pltpu.VMEM((MLA_HEADS, tq, LANE), F32), pltpu.VMEM((MLA_HEADS, tq, LANE), F32)],
        ),
        out_shape=jax.ShapeDtypeStruct((b, s, OUT_W), BF16),
        compiler_params=_params(("parallel", "arbitrary")),
        name="mla_attention",
    )(qi_tab, ki_tab, q, k, v)


def _compress_kernel(h_ref, pos_ref, w1_ref, w1g_ref, w2_ref, ext_ref, o_ref, *, is_value):
    n16 = h_ref.shape[1]
    lane = _iota((n16, LANE), 1)
    if is_value:
        extra = jnp.where(lane == ONES_LANE, 1.0, 0.0)
    else:
        extra = _bias_pieces(ext_ref[0], lane)
    bias = _dot(pos_ref[...], w1_ref[...])[0:1]
    h = h_ref[0]
    for g in range(NSA_GROUPS):
        first = _dot(h, w1g_ref[g, 0])
        second = _dot(h, w1g_ref[g, 1])
        hid = first + pltpu.roll(second, n16 - 1, 0) + bias
        act = jax.nn.gelu(hid).astype(BF16)
        o_ref[0, g] = (_dot(act, w2_ref[...]) + extra).astype(BF16)


def nsa_compress(h, cmp_pos, w1, w2, dpos_cend, is_value):
    b, n16, hw = h.shape
    pos = jnp.broadcast_to(cmp_pos.reshape(1, CMP_LEN * NSA_DH), (8, CMP_LEN * NSA_DH)).astype(BF16)
    w2p = jnp.pad(w2, ((0, 0), (0, LANE - NSA_DH))).astype(BF16)
    halves = w1.reshape(2, CMP_STRIDE, 1, NSA_DH, CMP_HIDDEN)
    w1g = jnp.stack([jnp.pad(halves, ((0, 0), (0, 0), (g, NSA_GROUPS - 1 - g), (0, 0), (0, 0))).reshape(2, hw, CMP_HIDDEN)
                     for g in range(NSA_GROUPS)]).astype(BF16)
    w1 = w1.astype(BF16)
    return pl.pallas_call(
        functools.partial(_compress_kernel, is_value=is_value),
        grid=(b,),
        in_specs=[pl.BlockSpec((1, n16, hw), lambda i: (i, 0, 0)), _const_spec(pos.shape),
                  _const_spec(w1.shape), _const_spec(w1g.shape), _const_spec(w2p.shape),
                  pl.BlockSpec((1, n16, 1), lambda i: (i, 0, 0))],
        out_specs=pl.BlockSpec((1, NSA_GROUPS, n16, LANE), lambda i: (i, 0, 0, 0)),
        out_shape=jax.ShapeDtypeStruct((b, NSA_GROUPS, n16, LANE), BF16),
        compiler_params=_params(("parallel",)),
        name="nsa_compress",
    )(h, pos, w1, w1g, w2p, dpos_cend)


def _topk_mask_t(x, k):
    n = x.shape[0]
    ridx = _iota(x.shape, 0).astype(F32)
    sel = jnp.zeros(x.shape, F32)
    for _ in range(k):
        m = jnp.max(x, axis=0, keepdims=True)
        first = jnp.min(jnp.where(x == m, ridx, float(n)), axis=0, keepdims=True)
        hit = ridx == first
        sel = jnp.where(hit, 1.0, sel)
        x = jnp.where(hit, REMOVED, x)
    return sel


def _cmp_topk_kernel(q_ref, kc_ref, vc_ref, gate_ref, oc_ref, selm1_ref, used_ref, imp_ref, *, tq, ncp, topk):
    qi = pl.program_id(1)
    t = qi * tq + _iota((tq, 1), 0)
    gates = gate_ref[0]
    blk = _iota((tq, LANE), 1)
    chunk = _shr(t, SEL_BLOCK)

    def attend(nc):
        cend = _iota((1, nc), 1) * CMP_STRIDE + (CMP_LEN - 1)
        mask = (cend <= t)[None]
        cstart = _iota((nc, LANE), 0) * CMP_STRIDE
        sstart = _iota((nc, LANE), 1) * SEL_BLOCK
        overlap = jnp.where((cstart < sstart + SEL_BLOCK) & (cstart + CMP_LEN > sstart)
                            & (cstart < (ncp - 1) * CMP_STRIDE), 1.0, 0.0).astype(BF16)
        for g in range(NSA_GROUPS):
            q = q_ref[g * NSA_HPG:(g + 1) * NSA_HPG].reshape(NSA_HPG * tq, LANE)
            s = _dot_nt(q, kc_ref[0, g, :nc]).reshape(NSA_HPG, tq, nc)
            s = jnp.where(mask, s, NEG)
            e = jnp.where(mask, jnp.exp2(s - jnp.max(s, axis=-1, keepdims=True)), 0.0)
            p = e / jnp.maximum(jnp.sum(e, axis=-1, keepdims=True), 1e-30)
            psum = jnp.sum(p, axis=0)
            o = _dot(p.reshape(NSA_HPG * tq, nc).astype(BF16), vc_ref[0, g, :nc])
            heads = [jnp.where(blk < NSA_DH, o[n * tq:(n + 1) * tq] * gates[:, g * NSA_HPG + n:g * NSA_HPG + n + 1], 0.0)
                     for n in range(NSA_HPG)]
            _store_head_pairs(oc_ref, slice(None), heads, first_pair=g * NSA_HPG // 2)
            hi = psum.astype(BF16)
            r1 = psum - hi.astype(F32)
            mid = r1.astype(BF16)
            lo = (r1 - mid.astype(F32)).astype(BF16)
            imp_ref[g] = _dot(hi, overlap) + _dot(mid, overlap) + _dot(lo, overlap)

    lane_tiles = ((qi + 1) * tq // CMP_STRIDE + LANE - 1) // LANE
    for v in range(1, ncp // LANE + 1):
        @pl.when(jnp.minimum(lane_tiles, ncp // LANE) == v)
        def _(v=v):
            attend(v * LANE)

    forced = (blk == 0) | (blk == chunk)
    imps = [jnp.where(forced, REMOVED, jnp.where(blk <= chunk, imp_ref[g], NEG)).T for g in range(NSA_GROUPS)]
    picked = _topk_mask_t(jnp.concatenate(imps, axis=1), topk - 2)
    used = jnp.zeros((1, LANE), F32)
    for g in range(NSA_GROUPS):
        sel = jnp.where(blk <= chunk, jnp.where(forced, 1.0, picked[:, g * tq:(g + 1) * tq].T), 0.0)
        selm1_ref[0, g] = (sel - 1.0).astype(BF16)
        used = jnp.maximum(used, jnp.max(sel, axis=0, keepdims=True))
    used_ref[0, 0] = jnp.broadcast_to(used, used_ref.shape[2:])


def nsa_cmp_topk(qn, kc, vc, gates, tq=256):
    _, b, s, _ = qn.shape
    tq = min(tq, s)
    ncp = kc.shape[2]
    assert s // SEL_BLOCK <= LANE
    topk = min(SEL_TOPK, s // SEL_BLOCK)
    kspec = pl.BlockSpec((1, NSA_GROUPS, ncp, LANE), lambda i, j: (i, 0, 0, 0))
    return pl.pallas_call(
        functools.partial(_cmp_topk_kernel, tq=tq, ncp=ncp, topk=topk),
        grid=(b, s // tq),
        in_specs=[pl.BlockSpec((NSA_HEADS, None, tq, LANE), lambda i, j: (0, i, j, 0)), kspec, kspec,
                  pl.BlockSpec((1, tq, LANE), lambda i, j: (i, j, 0))],
        out_specs=[pl.BlockSpec((1, tq, OUT_W), lambda i, j: (i, j, 0)),
                   pl.BlockSpec((1, NSA_GROUPS, tq, LANE), lambda i, j: (i, 0, j, 0)),
                   pl.BlockSpec((1, 1, 8, LANE), lambda i, j: (i, j, 0, 0))],
        out_shape=[jax.ShapeDtypeStruct((b, s, OUT_W), BF16), jax.ShapeDtypeStruct((b, NSA_GROUPS, s, LANE), BF16),
                   jax.ShapeDtypeStruct((b, s // tq, 8, LANE), F32)],
        scratch_shapes=[pltpu.VMEM((NSA_GROUPS, tq, LANE), F32)],
        compiler_params=_params(("parallel", "parallel")),
        name="nsa_cmp_topk",
    )(qn, kc, vc, gates)


def _nsa_write(o_ref, gate_ref, acc_ref, tq, branch):
    gates = gate_ref[0]
    heads = []
    for h in range(NSA_HEADS):
        g, n = divmod(h, NSA_HPG)
        col = branch * NSA_HEADS + h
        heads.append(_normalized(acc_ref[g, n * tq:(n + 1) * tq]) * gates[:, col:col + 1])
    _store_head_pairs(o_ref, slice(None), heads)


def _sel_attn_kernel(flags, q_ref, k_ref, v_ref, selm1_ref, gate_ref, o_ref, lhs_ref, m_ref, acc_ref, *, t, nq):
    b = pl.program_id(0)
    qi = pl.program_id(1)
    _init_state(m_ref, acc_ref)
    for g in range(NSA_GROUPS):
        lhs_ref[g, :, :LANE] = q_ref[g * NSA_HPG:(g + 1) * NSA_HPG].reshape(NSA_HPG * t, LANE)
        lhs_ref[g, :, LANE:] = jnp.concatenate([selm1_ref[0, g]] * NSA_HPG, axis=0)

    hpc = NSA_HPG // SEL_SPLIT
    half = hpc * t
    chains = [(g, i) for g in range(NSA_GROUPS) for i in range(SEL_SPLIT)]

    def update(ki, diagonal):
        rows = pl.ds(pl.multiple_of(ki * t, t), t)
        scores = [_dot_nt(lhs_ref[g, i * half:(i + 1) * half], k_ref[rows, 2 * g * LANE:(2 * g + 2) * LANE])
                  for g, i in chains]
        if diagonal:
            causal = (_iota((t, t), 1) <= _iota((t, t), 0))[None]
            scores = [jnp.where(causal, s.reshape(hpc, t, t), NEG).reshape(half, t) for s in scores]
        _flash_update_staged(scores, [v_ref[rows, g * LANE:(g + 1) * LANE] for g, _ in chains], m_ref, acc_ref,
                             [(g, slice(i * half, (i + 1) * half)) for g, i in chains])

    base = (b * nq + qi) * nq

    def body(ki, carry):
        @pl.when(flags[base + ki] > 0)
        def _():
            update(ki, False)
        return carry

    lax.fori_loop(0, qi, body, 0)
    update(qi, True)
    _nsa_write(o_ref, gate_ref, acc_ref, t, branch=1)


def nsa_selected(qn, ks, vs, selm1, used, gates):
    _, b, s, _ = qn.shape
    nq = used.shape[1]
    t = s // nq
    bpt = t // SEL_BLOCK
    flags = (used[:, :, 0].reshape(b, nq, LANE // bpt, bpt) > 0.5).any(axis=-1)[..., :nq]
    flags = flags.astype(jnp.int32).reshape(-1)
    imap_q = lambda bb, i, fl: (bb, i, 0)
    rows = NSA_HPG * t
    return pl.pallas_call(
        functools.partial(_sel_attn_kernel, t=t, nq=nq),
        grid_spec=pltpu.PrefetchScalarGridSpec(
            num_scalar_prefetch=1,
            grid=(b, nq),
            in_specs=[pl.BlockSpec((NSA_HEADS, None, t, LANE), lambda bb, i, fl: (0, bb, i, 0)),
                      pl.BlockSpec((None, s, 4 * LANE), lambda bb, i, fl: (bb, 0, 0)),
                      pl.BlockSpec((None, s, 2 * LANE), lambda bb, i, fl: (bb, 0, 0)),
                      pl.BlockSpec((1, NSA_GROUPS, t, LANE), lambda bb, i, fl: (bb, 0, i, 0)),
                      pl.BlockSpec((1, t, LANE), imap_q)],
            out_specs=pl.BlockSpec((1, t, OUT_W), imap_q),
            scratch_shapes=[pltpu.VMEM((NSA_GROUPS, rows, 2 * LANE), BF16), pltpu.VMEM((NSA_GROUPS, rows, LANE), F32),
                            pltpu.VMEM((NSA_GROUPS, rows, LANE), F32)],
        ),
        out_shape=jax.ShapeDtypeStruct((b, s, OUT_W), BF16),
        compiler_params=_params(("parallel", "arbitrary")),
        name="nsa_selected",
    )(flags, qn, ks, vs, selm1, gates)


def _win_attn_kernel(q_ref, kp_ref, kc_ref, vp_ref, vc_ref, gate_ref, o_ref, *, tq):
    qi = pl.program_id(1)
    hq = tq // 2
    span = WIN + hq
    row = _iota((hq, span), 0)
    col = _iota((hq, span), 1)
    diff = WIN + row - col
    band = (diff >= 0) & (diff < WIN)
    gates = gate_ref[0]
    chains = [(g, rh) for g in range(NSA_GROUPS) for rh in range(2)]
    scores = []
    for g, rh in chains:
        gl = slice(g * LANE, (g + 1) * LANE)
        q = q_ref[g * NSA_HPG:(g + 1) * NSA_HPG, rh * hq:(rh + 1) * hq].reshape(NSA_HPG * hq, LANE)
        s = jnp.concatenate([_dot_nt(q, kp_ref[0, rh * hq:, gl]), _dot_nt(q, kc_ref[0, :(rh + 1) * hq, gl])], axis=1)
        valid = (band & (col >= WIN - qi * tq - rh * hq))[None]
        scores.append(jnp.where(valid, s.reshape(NSA_HPG, hq, span), NEG).reshape(NSA_HPG * hq, span))
    tops = [jnp.max(s, axis=-1, keepdims=True) for s in scores]
    probs = [jnp.exp2(s - m).astype(BF16) for s, m in zip(scores, tops)]
    for (g, rh), p in zip(chains, probs):
        gl = slice(g * LANE, (g + 1) * LANE)
        n_prev = tq - rh * hq
        acc = _dot(p[:, :n_prev], vp_ref[0, rh * hq:, gl]) + _dot(p[:, n_prev:], vc_ref[0, :(rh + 1) * hq, gl])
        rows = slice(rh * hq, (rh + 1) * hq)
        cols = [2 * NSA_HEADS + g * NSA_HPG + n for n in range(NSA_HPG)]
        heads = [_normalized(acc[n * hq:(n + 1) * hq]) * gates[rows, cols[n]:cols[n] + 1] for n in range(NSA_HPG)]
        _store_head_pairs(o_ref, rows, heads, first_pair=g * NSA_HPG // 2)


def nsa_window(qn, kw, vw, gates, tq=512):
    _, b, s, _ = qn.shape
    assert tq == WIN and s % tq == 0
    prev = pl.BlockSpec((1, tq, 2 * LANE), lambda bb, i: (bb, jnp.maximum(i - 1, 0), 0))
    cur = pl.BlockSpec((1, tq, 2 * LANE), lambda bb, i: (bb, i, 0))
    return pl.pallas_call(
        functools.partial(_win_attn_kernel, tq=tq),
        grid=(b, s // tq),
        in_specs=[pl.BlockSpec((NSA_HEADS, None, tq, LANE), lambda bb, i: (0, bb, i, 0)), prev, cur, prev, cur,
                  pl.BlockSpec((1, tq, LANE), lambda bb, i: (bb, i, 0))],
        out_specs=pl.BlockSpec((1, tq, OUT_W), lambda bb, i: (bb, i, 0)),
        out_shape=jax.ShapeDtypeStruct((b, s, OUT_W), BF16),
        compiler_params=_params(("parallel", "parallel")),
        name="nsa_window",
    )(qn, kw, kw, vw, vw, gates)


def _odd_out_ln_kernel(x_ref, o0_ref, o1_ref, o2_ref, l0_ref, l1_ref, l2_ref, w_ref, g_ref, b_ref, o_ref,
                       o_scr, l_scr):
    tm = x_ref.shape[0]
    for gi, (src, lsrc) in enumerate(((o0_ref, l0_ref), (o1_ref, l1_ref), (o2_ref, l2_ref))):
        dil = DIL_PATTERNS[gi][1]
        for r in range(dil):
            rows = pl.ds(r, tm // dil, stride=dil) if dil > 1 else slice(None)
            for h in range(DIL_GH):
                o_scr[gi, h, rows, :] = src[r, :, h * DIL_DH:(h + 1) * DIL_DH].astype(F32)
            l_scr[gi, rows, :] = lsrc[r]
    lses = [l_scr[gi] for gi in range(DIL_GROUPS)]
    top = jnp.maximum(jnp.maximum(lses[0], lses[1]), lses[2])
    es = [jnp.exp(l - top) for l in lses]
    den = es[0] + es[1] + es[2]
    wts = [e / den for e in es]
    cols = []
    for h in range(DIL_GH):
        merged = None
        for gi in range(DIL_GROUPS):
            term = wts[gi][:, h:h + 1] * o_scr[gi, h]
            merged = term if merged is None else merged + term
        cols.append(merged.astype(BF16))
    mix = _dot(jnp.concatenate(cols, axis=1), w_ref[...])
    o_ref[...] = _layer_norm(ALPHA * x_ref[...] + mix, g_ref[...], b_ref[...])


def odd_out_ln(x2, outs, lses, w, g, b, tm=1024):
    m, d = x2.shape
    b_, _, s = outs[0].shape[0], None, outs[0].shape[1] * outs[0].shape[2]
    tm = min(tm, s)
    width = DIL_GH * DIL_DH
    row = pl.BlockSpec((None, tm, d), lambda bb, i: (bb, i, 0))
    cls = lambda gi, wd: pl.BlockSpec((None, DIL_PATTERNS[gi][1], tm // DIL_PATTERNS[gi][1], wd),
                                      lambda bb, i: (bb, 0, i, 0))
    out = pl.pallas_call(
        _odd_out_ln_kernel,
        grid=(b_, s // tm),
        in_specs=[row] + [cls(gi, width) for gi in range(DIL_GROUPS)] + [cls(gi, LANE) for gi in range(DIL_GROUPS)]
        + [_const_spec(w.shape), _const_spec((1, d)), _const_spec((1, d))],
        out_specs=row,
        out_shape=jax.ShapeDtypeStruct((b_, s, d), F32),
        scratch_shapes=[pltpu.VMEM((DIL_GROUPS, DIL_GH, tm, DIL_DH), F32), pltpu.VMEM((DIL_GROUPS, tm, LANE), F32)],
        compiler_params=_params(("parallel", "parallel")),
        name="odd_out_ln",
    )(x2.reshape(b_, s, d), *outs, *lses, w, g.reshape(1, d), b.reshape(1, d))
    return out.reshape(m, d)


def _odd_proj_kernel(x_ref, w_ref, o0_ref, o1_ref, o2_ref, xs_ref):
    tm, d_model = x_ref.shape
    width = DIL_GH * DIL_DH
    n_chunks = d_model // LANE
    for c in range(n_chunks):
        xs_ref[c] = x_ref[:, c * LANE:(c + 1) * LANE]
    for gi, o_ref in enumerate((o0_ref, o1_ref, o2_ref)):
        dil = DIL_PATTERNS[gi][1]
        if dil == 1:
            xg = x_ref[...]
        else:
            xg = jnp.concatenate(
                [jnp.concatenate([xs_ref[c, pl.ds(r, tm // dil, stride=dil), :] for r in range(dil)], axis=0)
                 for c in range(n_chunks)], axis=1)
        xg = xg.astype(BF16)
        for c in range(3):
            col = (gi * 3 + c) * width
            y = _dot(xg, w_ref[:, col:col + width])
            if c == 0:
                y = y * (DIL_SCALE * LOG2E)
            o_ref[:, :, c * width:(c + 1) * width] = y.astype(BF16).reshape(dil, tm // dil, width)


def odd_proj(x2, b, s, w_in, tm=512):
    m, d = x2.shape
    tm = min(tm, s)
    width = DIL_GH * DIL_DH
    w = w_in.reshape(d, 3, DIL_GROUPS, width).transpose(0, 2, 1, 3).reshape(d, 3 * DIL_GROUPS * width).astype(BF16)
    dils = [dil for _, dil in DIL_PATTERNS]
    return pl.pallas_call(
        _odd_proj_kernel,
        grid=(b, s // tm),
        in_specs=[pl.BlockSpec((None, tm, d), lambda bb, i: (bb, i, 0)), _const_spec(w.shape)],
        out_specs=[pl.BlockSpec((None, dil, tm // dil, 3 * width), lambda bb, i: (bb, 0, i, 0)) for dil in dils],
        out_shape=[jax.ShapeDtypeStruct((b, dil, s // dil, 3 * width), BF16) for dil in dils],
        scratch_shapes=[pltpu.VMEM((d // LANE, tm, LANE), F32)],
        compiler_params=_params(("parallel", "parallel")),
        name="odd_proj",
    )(x2.reshape(b, s, d), w)


def _dil_attn_kernel(slope_tab, pos0_tab, q_ref, kc_ref, kp_ref, vc_ref, vp_ref, pc_ref, pp_ref, pq_ref, o_ref,
                     lse_ref, *, group, tq):
    sub = DIL_SPAN
    b = pl.program_id(0)
    jt = pl.program_id(2)
    row = _iota((sub, 2 * sub), 0)
    col = _iota((sub, 2 * sub), 1)
    band = (col >= row) & (col <= row + sub)
    band_bias = jnp.where(band, 0.0, NEG)
    start_bias = jnp.where(band & (col >= sub), 0.0, NEG)
    pos0 = pos0_tab[b]
    lane = _iota((sub, LANE), 1)
    heads = range(DIL_GH)
    hs = [slice(h * DIL_DH, (h + 1) * DIL_DH) for h in heads]
    slopes = [slope_tab[group * DIL_GH + h] for h in heads]
    tiles = range(tq // sub)
    cur = [slice(i * sub, (i + 1) * sub) for i in tiles]
    biases, ks, vs = [], [], []
    for i in tiles:
        if i == 0:
            mask_bias = jnp.where(jt == 0, start_bias, band_bias)
            pk = jnp.concatenate([pp_ref[...], pc_ref[:, :sub]], axis=1)
            ks.append([jnp.concatenate([kp_ref[:, hs[h]], kc_ref[:sub, hs[h]]], axis=0) for h in heads])
            vs.append([jnp.concatenate([vp_ref[:, hs[h]], vc_ref[:sub, hs[h]]], axis=0) for h in heads])
        else:
            mask_bias = band_bias
            pk = pc_ref[:, (i - 1) * sub:(i + 1) * sub]
            ks.append([kc_ref[(i - 1) * sub:(i + 1) * sub, hs[h]] for h in heads])
            vs.append([vc_ref[(i - 1) * sub:(i + 1) * sub, hs[h]] for h in heads])
        dpos = pk - pos0
        biases.append([mask_bias + slopes[h] * dpos for h in heads])
    chains = [(i, h) for i in tiles for h in heads]
    ss = [_dot_nt(q_ref[cur[i], hs[h]], ks[i][h]) + biases[i][h] for i, h in chains]
    ms = [jnp.max(s, axis=-1, keepdims=True) for s in ss]
    ps = [jnp.exp2(s - m) for s, m in zip(ss, ms)]
    dens = [jnp.sum(p, axis=-1, keepdims=True) for p in ps]
    for n, (i, h) in enumerate(chains):
        o_ref[cur[i], hs[h]] = (_dot(ps[n].astype(BF16), vs[i][h]) / dens[n]).astype(BF16)
    for i in tiles:
        dq = (pq_ref[cur[i], :] - pos0) * LN2
        lse_tile = jnp.zeros((sub, LANE), F32)
        for h in heads:
            n = i * DIL_GH + h
            lse_tile = jnp.where(lane == h, ms[n] * LN2 + jnp.log(dens[n]) - slopes[h] * dq, lse_tile)
        lse_ref[cur[i], :] = lse_tile


def dilated_group_attention(qkv, pos_f, group, tq=1024):
    b, d, sd, _ = qkv.shape
    w, dil = DIL_PATTERNS[group]
    assert w // dil == DIL_SPAN and dil == d
    tq = min(tq, sd)
    sub = DIL_SPAN
    width = DIL_GH * DIL_DH
    n_slopes = DIL_GROUPS * DIL_GH
    slopes = 2.0 ** (-ALIBI_MAX_BIAS * jnp.arange(1, n_slopes + 1, dtype=F32) / n_slopes) * LOG2E
    pos0 = pos_f[:, 0]
    posc = pos_f.reshape(b, sd, d).transpose(0, 2, 1).reshape(b, d, 1, sd)
    r = tq // sub
    cur = lambda c: pl.BlockSpec((None, None, tq, width), lambda bb, rr, j, *_: (bb, rr, j, c))
    prev = lambda c: pl.BlockSpec((None, None, sub, width),
                                  lambda bb, rr, j, *_: (bb, rr, jnp.maximum(j * r - 1, 0), c))
    return pl.pallas_call(
        functools.partial(_dil_attn_kernel, group=group, tq=tq),
        grid_spec=pltpu.PrefetchScalarGridSpec(
            num_scalar_prefetch=2,
            grid=(b, d, sd // tq),
            in_specs=[cur(0), cur(1), prev(1), cur(2), prev(2),
                      pl.BlockSpec((None, None, 1, tq), lambda bb, rr, j, *_: (bb, rr, 0, j)),
                      pl.BlockSpec((None, None, 1, sub), lambda bb, rr, j, *_: (bb, rr, 0, jnp.maximum(j * r - 1, 0))),
                      pl.BlockSpec((None, None, tq, 1), lambda bb, rr, j, *_: (bb, rr, j, 0))],
            out_specs=[pl.BlockSpec((None, None, tq, width), lambda bb, rr, j, *_: (bb, rr, j, 0)),
                       pl.BlockSpec((None, None, tq, LANE), lambda bb, rr, j, *_: (bb, rr, j, 0))],
        ),
        out_shape=[jax.ShapeDtypeStruct((b, d, sd, width), BF16), jax.ShapeDtypeStruct((b, d, sd, LANE), F32)],
        compiler_params=_params(("parallel", "parallel", "parallel")),
        name=f"dilated_attention_g{group}",
    )(slopes, pos0, qkv, qkv, qkv, qkv, qkv, posc, posc, posc.reshape(b, d, sd, 1))


def even_mixer(x2, b, s, pos_f, w_in, q_norm_g, kv_norm_g, w_uq, w_uk, w_uv,
               cmp_pos, cmp_k_w1, cmp_k_w2, cmp_v_w1, cmp_v_w2, w_out, ln_g, ln_b):
    m = b * s
    half = MLA_ROPE // 2
    inv = ROPE_THETA ** (-jnp.arange(half, dtype=F32) / half)
    ang = (pos_f[..., None] * inv).reshape(m, half)
    dpos = (pos_f - pos_f[:, :1]) * LOG2E
    ncp = s // CMP_STRIDE
    dpos_cend = jnp.pad(dpos[:, CMP_LEN - 1::CMP_STRIDE], ((0, 0), (0, 1))).reshape(b, ncp, 1)
    tok_table = jnp.concatenate([jnp.cos(ang), jnp.sin(ang), dpos.reshape(m, 1),
                                 jnp.zeros((m, LANE - MLA_ROPE - 1), F32)], axis=1)

    qm, km, vm, qn, kvc, ks, kw, vs, vw, gates = even_proj(
        x2, tok_table, s, w_in, q_norm_g, kv_norm_g, w_uq, w_uk, w_uv)
    o_mla = mla_attention(qm.reshape(b, s, HW), km.reshape(b, s, HW), vm.reshape(b, s, HW))

    kvc = kvc.reshape(2, b, ncp, CMP_STRIDE * LANE)
    kc = nsa_compress(kvc[0], cmp_pos, cmp_k_w1, cmp_k_w2, dpos_cend, is_value=False)
    vc = nsa_compress(kvc[1], cmp_pos, cmp_v_w1, cmp_v_w2, dpos_cend, is_value=True)
    qn = qn.reshape(NSA_HEADS, b, s, LANE)
    gates = gates.reshape(b, s, LANE)
    o_c, selm1, used = nsa_cmp_topk(qn, kc, vc, gates)
    o_s = nsa_selected(qn, ks.reshape(b, s, 4 * LANE), vs.reshape(b, s, 2 * LANE), selm1, used, gates)
    o_w = nsa_window(qn, kw.reshape(b, s, 2 * LANE), vw.reshape(b, s, 2 * LANE), gates)

    outs = [o.reshape(m, OUT_W) for o in (o_mla, o_c, o_s, o_w)]
    return outs, [w_out.astype(BF16)], ln_g, ln_b


def odd_mixer_ln(x2, b, s, pos_f, w_in, w_out, ln_g, ln_b):
    qkvs = odd_proj(x2, b, s, w_in)
    parts = [dilated_group_attention(qkvs[g], pos_f, g) for g in range(DIL_GROUPS)]
    return odd_out_ln(x2, [p[0] for p in parts], [p[1] for p in parts], w_out.astype(BF16), ln_g, ln_b)


def kernel(x, positions, ln1_g, ln1_b, ffn1_w_gate, ffn1_w_up, ffn1_w_down, mix_in_even, mla_q_norm, mla_kv_norm, mla_w_uq, mla_w_uk, mla_w_uv, nsa_cmp_pos, nsa_cmp_k_w1, nsa_cmp_k_w2, nsa_cmp_v_w1, nsa_cmp_v_w2, mix_out_even, mix_in_odd, mix_out_odd, ln2_g, ln2_b, ffn2_w_gate, ffn2_w_up, ffn2_w_down, ln3_g, ln3_b):
    b, s, d = x.shape
    x2 = x.reshape(b * s, d)
    pos_f = positions.astype(F32)
    for i in range(DEPTH):
        j = i // 2
        x2 = ffn_ln(x2, ffn1_w_gate[i].astype(BF16), ffn1_w_up[i].astype(BF16), ffn1_w_down[i].astype(BF16),
                    ln1_g[i], ln1_b[i])
        mix = None
        if i % 2 == 0:
            mix = even_mixer(x2, b, s, pos_f, mix_in_even[j], mla_q_norm[j], mla_kv_norm[j], mla_w_uq[j],
                                mla_w_uk[j], mla_w_uv[j], nsa_cmp_pos[j], nsa_cmp_k_w1[j], nsa_cmp_k_w2[j],
                                nsa_cmp_v_w1[j], nsa_cmp_v_w2[j], mix_out_even[j], ln2_g[i], ln2_b[i])
        else:
            x2 = odd_mixer_ln(x2, b, s, pos_f, mix_in_odd[j], mix_out_odd[j], ln2_g[i], ln2_b[i])
        x2 = ffn_ln(x2, ffn2_w_gate[i].astype(BF16), ffn2_w_up[i].astype(BF16), ffn2_w_down[i].astype(BF16),
                    ln3_g[i], ln3_b[i], mix=mix)
    return x2.reshape(b, s, d)
```

```python
import functools
import math

import numpy as np
import jax
import jax.numpy as jnp
from jax import lax
from jax.experimental import pallas as pl
from jax.experimental.pallas import tpu as pltpu

F32 = jnp.float32
BF16 = jnp.bfloat16

DEPTH = 2
LN_EPS = 1e-5
RMS_EPS = 1e-6
ALPHA = (2 * DEPTH) ** 0.25
HALF_STEP = 0.5
NEG = -1e30
BIG = 1e9
MASK_BIG = 1e30
REMOVED = -3.0e38
ALIBI_MAX_BIAS = 8.0
LOG2E = math.log2(math.e)
LN2 = math.log(2.0)
LANE = 128

MLA_HEADS = 8
MLA_Q_RANK = 384
MLA_KV_RANK = 256
MLA_NOPE = 64
MLA_ROPE = 32
MLA_V = 64
ROPE_THETA = 10000.0
MLA_SCALE = (MLA_NOPE + MLA_ROPE) ** -0.5
MLA_CHAINS = 4

NSA_HEADS = 8
NSA_GROUPS = 2
NSA_HPG = 4
NSA_DH = 64
CMP_LEN = 32
CMP_STRIDE = 16
CMP_HIDDEN = 256
SEL_BLOCK = 64
SEL_TOPK = 16
WIN = 512
NSA_SCALE = NSA_DH ** -0.5
SEL_SPLIT = 2
NSA_SLOPES = tuple(2.0 ** (-ALIBI_MAX_BIAS * (i + 1) / NSA_HEADS) for i in range(NSA_HEADS))
ONES_LANE = 64
BIAS_LANES = (64, 65, 66)

DIL_PATTERNS = ((128, 1), (512, 4), (2048, 16))
DIL_GROUPS = 3
DIL_GH = 4
DIL_DH = 128
DIL_SCALE = DIL_DH ** -0.5
DIL_SPAN = 128

VMEM_LIMIT = 48 * 1024 * 1024
MIXED_FFN_VMEM_LIMIT = 56 * 1024 * 1024


def _iota(shape, dim):
    return lax.broadcasted_iota(jnp.int32, shape, dim)


def _shr(x, pow2):
    return jnp.right_shift(x, int(pow2).bit_length() - 1)


def _dot(a, b):
    return jnp.dot(a, b, preferred_element_type=F32)


def _dot_nt(a, b):
    return lax.dot_general(a, b, (((1,), (1,)), ((), ())), preferred_element_type=F32)


def _const_spec(shape):
    zeros = (0,) * len(shape)
    return pl.BlockSpec(shape, lambda *_: zeros, pipeline_mode=pl.Buffered(1))


def _params(sem):
    return pltpu.CompilerParams(dimension_semantics=sem, vmem_limit_bytes=VMEM_LIMIT)


def _layer_norm(z, g, b):
    mu = jnp.mean(z, axis=-1, keepdims=True)
    zc = z - mu
    var = jnp.mean(zc * zc, axis=-1, keepdims=True)
    return zc * lax.rsqrt(var + LN_EPS) * g + b


def _rms_norm(z, g):
    return z * lax.rsqrt(jnp.mean(z * z, axis=-1, keepdims=True) + RMS_EPS) * g


def _bias_pieces(d, lane):
    x = jnp.broadcast_to(d, lane.shape)
    hi = x.astype(BF16).astype(F32)
    r = x - hi
    mid = r.astype(BF16).astype(F32)
    lo = (r - mid).astype(BF16).astype(F32)
    return jnp.where(lane == BIAS_LANES[0], hi,
                     jnp.where(lane == BIAS_LANES[1], mid, jnp.where(lane == BIAS_LANES[2], lo, 0.0)))


def _flash_update_staged(scores, values, m_ref, acc_ref, idxs):
    m_old = [m_ref[i] for i in idxs]
    m_new = [jnp.maximum(mo, jnp.max(s, axis=-1, keepdims=True)) for mo, s in zip(m_old, scores)]
    ps = [jnp.exp2(s - jnp.tile(mn, (1, s.shape[1] // LANE))).astype(BF16) for s, mn in zip(scores, m_new)]
    for i, mo, mn, p, v in zip(idxs, m_old, m_new, ps, values):
        acc_ref[i] = jnp.exp2(mo - mn) * acc_ref[i] + _dot(p, v)
        m_ref[i] = mn


def _init_state(m_ref, acc_ref):
    m_ref[...] = jnp.full(m_ref.shape, NEG, F32)
    acc_ref[...] = jnp.zeros(acc_ref.shape, F32)


def _normalized(acc):
    lane = _iota(acc.shape, 1)
    o = acc / jnp.maximum(acc[:, ONES_LANE:ONES_LANE + 1], 1e-30)
    return jnp.where(lane < ONES_LANE, o, 0.0)


def _store_head_pairs(o_ref, rows, heads, first_pair=0):
    for pr in range(len(heads) // 2):
        packed = heads[2 * pr] + pltpu.roll(heads[2 * pr + 1], LANE // 2, 1)
        o_ref[0, rows, (first_pair + pr) * LANE:(first_pair + pr + 1) * LANE] = packed.astype(BF16)


def _ffn_ln_kernel(*refs, n_chunks, mixed):
    if mixed:
        x_ref, om_ref, oc_ref, os_ref, ow_ref, w_ref, g2_ref, b2_ref = refs[:8]
        nsa = (oc_ref[...].astype(F32) + os_ref[...].astype(F32) + ow_ref[...].astype(F32)).astype(BF16)
        mix = _dot(jnp.concatenate([om_ref[...], nsa], axis=1), w_ref[...])
        x = _layer_norm(ALPHA * x_ref[...] + mix, g2_ref[...], b2_ref[...])
        refs = refs[8:]
    else:
        x = refs[0][...]
        refs = refs[1:]
    wg_ref, wu_ref, wd_ref, g_ref, b_ref, o_ref = refs
    xb = x.astype(BF16)
    c = wg_ref.shape[1] // n_chunks
    y = None
    for i in range(n_chunks):
        gt = _dot(xb, wg_ref[:, i * c:(i + 1) * c])
        up = _dot(xb, wu_ref[:, i * c:(i + 1) * c])
        h = (gt * jax.nn.sigmoid(gt) * up).astype(BF16)
        part = _dot(h, wd_ref[i * c:(i + 1) * c, :])
        y = part if y is None else y + part
    o_ref[...] = _layer_norm(ALPHA * x + HALF_STEP * y, g_ref[...], b_ref[...])


def ffn_ln(x2, wg, wu, wd, g, b, mix=None, tm=1024, n_chunks=11):
    m, d = x2.shape
    tm = min(tm, m)
    row = lambda w: pl.BlockSpec((tm, w), lambda i: (i, 0))
    args, specs = [x2], [row(d)]
    if mix is not None:
        outs, ws, g2, b2 = mix
        args += list(outs) + list(ws) + [g2.reshape(1, d), b2.reshape(1, d)]
        specs += [row(o.shape[1]) for o in outs] + [_const_spec(w.shape) for w in ws] + [_const_spec((1, d))] * 2
    args += [wg, wu, wd, g.reshape(1, d), b.reshape(1, d)]
    specs += [_const_spec(wg.shape), _const_spec(wu.shape), _const_spec(wd.shape), _const_spec((1, d)), _const_spec((1, d))]
    limit = VMEM_LIMIT if mix is None else MIXED_FFN_VMEM_LIMIT
    return pl.pallas_call(
        functools.partial(_ffn_ln_kernel, n_chunks=n_chunks, mixed=mix is not None),
        grid=(m // tm,),
        in_specs=specs,
        out_specs=row(d),
        out_shape=jax.ShapeDtypeStruct((m, d), F32),
        compiler_params=pltpu.CompilerParams(dimension_semantics=("parallel",), vmem_limit_bytes=limit),
        name="ffn_ln" if mix is None else "mix_ffn_ln",
    )(*args)


HW = MLA_HEADS * LANE
OUT_W = MLA_HEADS * MLA_V
EVEN_X_COLS = (MLA_Q_RANK, MLA_KV_RANK, LANE, LANE, HW, 2 * LANE, 8 * LANE, LANE)
EVEN_X_OFFS = tuple(int(v) for v in np.cumsum((0,) + EVEN_X_COLS))


def _even_proj_kernel(x_ref, tok_ref, wx_ref, qg_ref, kvg_ref, wuq_ref, wuqs_ref, wuk_ref,
                      wuv_ref, slope_ref, qm_ref, km_ref, vm_ref, qn_ref, kvc_ref, ks_ref, kw_ref, vs_ref, vw_ref,
                      gate_ref, kvc_scr, *, tiles_per_seq):
    tm = x_ref.shape[0]
    xb = x_ref[...].astype(BF16)
    lane = _iota((tm, LANE), 1)
    table = tok_ref[...]
    half = MLA_ROPE // 2
    lo = (lane >= MLA_NOPE) & (lane < MLA_NOPE + half)
    hi = (lane >= MLA_NOPE + half) & (lane < MLA_NOPE + MLA_ROPE)
    shifted = lambda k: pltpu.roll(table, k, 1)
    cos = jnp.where(lo, shifted(MLA_NOPE), jnp.where(hi, shifted(MLA_NOPE + half), jnp.where(lane < MLA_NOPE, 1.0, 0.0)))
    sin = jnp.where(lo, shifted(MLA_NOPE - half), jnp.where(hi, shifted(MLA_NOPE), 0.0))
    ones_lane = jnp.where(lane == ONES_LANE, 1.0, 0.0)
    pos_term = _bias_pieces(table[:, MLA_ROPE:MLA_ROPE + 1], lane)
    tok = (pl.program_id(0) % tiles_per_seq) * tm + _iota((tm, LANE), 0)
    block_onehot = jnp.where(lane == _shr(tok, SEL_BLOCK), MASK_BIG, 0.0).astype(BF16)

    def xdot(i):
        return _dot(xb, wx_ref[:, EVEN_X_OFFS[i]:EVEN_X_OFFS[i + 1]])

    cq = _rms_norm(xdot(0), qg_ref[...]).astype(BF16)
    ckv = _rms_norm(xdot(1), kvg_ref[...]).astype(BF16)
    rope_pair = _dot(xb, wx_ref[:, EVEN_X_OFFS[2]:EVEN_X_OFFS[4]])
    k_rot = rope_pair[:, :LANE] * cos + rope_pair[:, LANE:] * sin
    q_all, q_swap, k_all, v_all = (_dot(a, w[...]) for a, w in ((cq, wuq_ref), (cq, wuqs_ref), (ckv, wuk_ref),
                                                                  (ckv, wuv_ref)))
    for h in range(MLA_HEADS):
        sl = slice(h * LANE, (h + 1) * LANE)
        q = q_all[:, sl] * cos + q_swap[:, sl] * sin
        qm_ref[:, sl] = (q * (MLA_SCALE * LOG2E)).astype(BF16)
        km_ref[:, sl] = (k_all[:, sl] + k_rot).astype(BF16)
        vm_ref[:, sl] = (v_all[:, sl] + ones_lane).astype(BF16)
    qn = xdot(4) * (NSA_SCALE * LOG2E) + slope_ref[...]
    for h in range(NSA_HEADS):
        qn_ref[h] = qn[:, h * LANE:(h + 1) * LANE].astype(BF16)
    kvc = xdot(5)
    for j in range(2):
        kvc_scr[j] = kvc[:, j * LANE:(j + 1) * LANE]
        for l in range(CMP_STRIDE):
            kvc_ref[j, :, l * LANE:(l + 1) * LANE] = kvc_scr[j, pl.ds(l, tm // CMP_STRIDE, stride=CMP_STRIDE), :].astype(BF16)
    kv8 = xdot(6)
    blk = lambda i: kv8[:, i * LANE:(i + 1) * LANE]
    for g in range(NSA_GROUPS):
        ks_ref[:, 2 * g * LANE:(2 * g + 1) * LANE] = (blk(g) + pos_term).astype(BF16)
        ks_ref[:, (2 * g + 1) * LANE:(2 * g + 2) * LANE] = block_onehot
        kw_ref[:, g * LANE:(g + 1) * LANE] = (blk(2 + g) + pos_term).astype(BF16)
        vs_ref[:, g * LANE:(g + 1) * LANE] = (blk(4 + g) + ones_lane).astype(BF16)
        vw_ref[:, g * LANE:(g + 1) * LANE] = (blk(6 + g) + ones_lane).astype(BF16)
    gate_ref[...] = jax.nn.sigmoid(xdot(7))


def _head_blocks(w, n_heads, width):
    k = w.shape[0]
    w = jnp.pad(w.reshape(k, n_heads, width), ((0, 0), (0, 0), (0, LANE - width)))
    return w.reshape(k, n_heads * LANE)


def even_proj(x2, tok_table, seq, w_in, q_norm_g, kv_norm_g, w_uq, w_uk, w_uv, tm=512):
    m, d = x2.shape
    tm = min(tm, seq)
    half = MLA_ROPE // 2
    gw = NSA_GROUPS * NSA_DH
    cuts = np.cumsum((MLA_Q_RANK, MLA_KV_RANK, MLA_ROPE, NSA_HEADS * NSA_DH) + (gw,) * 6)
    cuts = [0] + [int(c) for c in cuts]
    w_cq, w_ckv, w_kpe, w_q = (w_in[:, cuts[i]:cuts[i + 1]] for i in range(4))
    w_kc, w_vc, w_ks, w_vs, w_kw, w_vw = (w_in[:, cuts[4 + i]:cuts[5 + i]] for i in range(6))
    w_gate = w_in[:, cuts[10]:]
    w_kpe_sw = jnp.concatenate([-w_kpe[:, half:], w_kpe[:, :half]], axis=1)
    rope_pad = ((0, 0), (MLA_NOPE, LANE - MLA_NOPE - MLA_ROPE))
    w_gate_blk = jnp.pad(w_gate, ((0, 0), (0, LANE - w_gate.shape[1])))
    wx = jnp.concatenate(
        [w_cq, w_ckv, jnp.pad(w_kpe, rope_pad), jnp.pad(w_kpe_sw, rope_pad), _head_blocks(w_q, NSA_HEADS, NSA_DH),
         w_kc, w_vc] + [_head_blocks(w, NSA_GROUPS, NSA_DH) for w in (w_ks, w_kw, w_vs, w_vw)] + [w_gate_blk],
        axis=1).astype(BF16)

    qd = MLA_NOPE + MLA_ROPE
    uq = w_uq.reshape(MLA_Q_RANK, MLA_HEADS, qd)
    uq_sw = jnp.concatenate([jnp.zeros_like(uq[..., :MLA_NOPE]), -uq[..., MLA_NOPE + half:],
                             uq[..., MLA_NOPE:MLA_NOPE + half]], axis=-1)
    wuq = _head_blocks(uq.reshape(MLA_Q_RANK, -1), MLA_HEADS, qd).astype(BF16)
    wuqs = _head_blocks(uq_sw.reshape(MLA_Q_RANK, -1), MLA_HEADS, qd).astype(BF16)
    wuk = _head_blocks(w_uk, MLA_HEADS, MLA_NOPE).astype(BF16)
    wuv = _head_blocks(w_uv, MLA_HEADS, MLA_V).astype(BF16)
    slope_row = np.zeros((1, HW), np.float32)
    for h in range(NSA_HEADS):
        for ln in BIAS_LANES:
            slope_row[0, h * LANE + ln] = NSA_SLOPES[h]

    row = lambda w: pl.BlockSpec((tm, w), lambda i: (i, 0))
    sds = jax.ShapeDtypeStruct
    return pl.pallas_call(
        functools.partial(_even_proj_kernel, tiles_per_seq=seq // tm),
        grid=(m // tm,),
        in_specs=[row(d), row(LANE), _const_spec(wx.shape), _const_spec((1, MLA_Q_RANK)),
                  _const_spec((1, MLA_KV_RANK)), _const_spec(wuq.shape), _const_spec(wuqs.shape),
                  _const_spec(wuk.shape), _const_spec(wuv.shape), _const_spec((1, HW))],
        out_specs=[row(HW), row(HW), row(HW), pl.BlockSpec((NSA_HEADS, tm, LANE), lambda i: (0, i, 0)),
                   pl.BlockSpec((2, tm // CMP_STRIDE, CMP_STRIDE * LANE), lambda i: (0, i, 0)), row(4 * LANE),
                   row(2 * LANE), row(2 * LANE), row(2 * LANE), row(LANE)],
        out_shape=[sds((m, HW), BF16)] * 3 + [sds((NSA_HEADS, m, LANE), BF16),
                                              sds((2, m // CMP_STRIDE, CMP_STRIDE * LANE), BF16),
                                              sds((m, 4 * LANE), BF16), sds((m, 2 * LANE), BF16),
                                              sds((m, 2 * LANE), BF16), sds((m, 2 * LANE), BF16), sds((m, LANE), F32)],
        scratch_shapes=[pltpu.VMEM((2, tm, LANE), F32)],
        compiler_params=_params(("parallel",)),
        name="even_proj",
    )(x2, tok_table, wx, q_norm_g.reshape(1, -1), kv_norm_g.reshape(1, -1), wuq, wuqs, wuk, wuv,
      jnp.asarray(slope_row))


def _causal_pairs(nq, tq, tk):
    qi, ki = [], []
    for i in range(nq):
        for j in range(((i + 1) * tq - 1) // tk + 1):
            qi.append(i)
            ki.append(j)
    return jnp.asarray(qi, jnp.int32), jnp.asarray(ki, jnp.int32)


def _mla_kernel(qi_tab, ki_tab, q_ref, k_ref, v_ref, o_ref, m_ref, acc_ref, *, tq, tk):
    p = pl.program_id(1)
    qi = qi_tab[p]
    ki = ki_tab[p]

    @pl.when(ki == 0)
    def _():
        _init_state(m_ref, acc_ref)

    def run(rows, diagonal):
        if diagonal:
            mask = _iota((tk, tk), 0) >= _iota((tk, tk), 1)
        for h0 in range(0, MLA_HEADS, MLA_CHAINS):
            heads = range(h0, h0 + MLA_CHAINS)
            scores = [_dot_nt(q_ref[0, rows, h * LANE:(h + 1) * LANE], k_ref[0, :, h * LANE:(h + 1) * LANE])
                      for h in heads]
            if diagonal:
                scores = [jnp.where(mask, s, NEG) for s in scores]
            _flash_update_staged(scores, [v_ref[0, :, h * LANE:(h + 1) * LANE] for h in heads], m_ref, acc_ref,
                                 [(h, rows) for h in heads])

    blocks = tq // tk
    for r in range(blocks):
        rows = slice(r * tk, (r + 1) * tk)
        own = qi * blocks + r

        @pl.when(ki < own)
        def _(rows=rows):
            run(rows, False)

        @pl.when(ki == own)
        def _(rows=rows):
            run(rows, True)

    @pl.when(ki == ((qi + 1) * tq - 1) // tk)
    def _():
        _store_head_pairs(o_ref, slice(None), [_normalized(acc_ref[h]) for h in range(MLA_HEADS)])


def mla_attention(q, k, v, tq=1024, tk=512):
    b, s, hw = q.shape
    tq, tk = min(tq, s), min(tk, s)
    qi_tab, ki_tab = _causal_pairs(s // tq, tq, tk)
    qspec = pl.BlockSpec((1, tq, hw), lambda bb, p, qt, kt: (bb, qt[p], 0))
    kspec = pl.BlockSpec((1, tk, hw), lambda bb, p, qt, kt: (bb, kt[p], 0))
    return pl.pallas_call(
        functools.partial(_mla_kernel, tq=tq, tk=tk),
        grid_spec=pltpu.PrefetchScalarGridSpec(
            num_scalar_prefetch=2,
            grid=(b, int(qi_tab.shape[0])),
            in_specs=[qspec, kspec, kspec],
            out_specs=pl.BlockSpec((1, tq, OUT_W), lambda bb, p, qt, kt: (bb, qt[p], 0)),
            scratch_shapes=[pltpu.VMEM((MLA_HEADS, tq, LANE), F32), pltpu.VMEM((MLA_HEADS, tq, LANE), F32)],
        ),
        out_shape=jax.ShapeDtypeStruct((b, s, OUT_W), BF16),
        compiler_params=_params(("parallel", "arbitrary")),
        name="mla_attention",
    )(qi_tab, ki_tab, q, k, v)


def _compress_kernel(h_ref, pos_ref, w1_ref, w1g_ref, w2_ref, ext_ref, o_ref, *, is_value):
    n16 = h_ref.shape[1]
    lane = _iota((n16, LANE), 1)
    if is_value:
        extra = jnp.where(lane == ONES_LANE, 1.0, 0.0)
    else:
        extra = _bias_pieces(ext_ref[0], lane)
    bias = _dot(pos_ref[...], w1_ref[...])[0:1]
    h = h_ref[0]
    for g in range(NSA_GROUPS):
        first = _dot(h, w1g_ref[g, 0])
        second = _dot(h, w1g_ref[g, 1])
        hid = first + pltpu.roll(second, n16 - 1, 0) + bias
        act = jax.nn.gelu(hid).astype(BF16)
        o_ref[0, g] = (_dot(act, w2_ref[...]) + extra).astype(BF16)


def nsa_compress(h, cmp_pos, w1, w2, dpos_cend, is_value):
    b, n16, hw = h.shape
    pos = jnp.broadcast_to(cmp_pos.reshape(1, CMP_LEN * NSA_DH), (8, CMP_LEN * NSA_DH)).astype(BF16)
    w2p = jnp.pad(w2, ((0, 0), (0, LANE - NSA_DH))).astype(BF16)
    halves = w1.reshape(2, CMP_STRIDE, 1, NSA_DH, CMP_HIDDEN)
    w1g = jnp.stack([jnp.pad(halves, ((0, 0), (0, 0), (g, NSA_GROUPS - 1 - g), (0, 0), (0, 0))).reshape(2, hw, CMP_HIDDEN)
                     for g in range(NSA_GROUPS)]).astype(BF16)
    w1 = w1.astype(BF16)
    return pl.pallas_call(
        functools.partial(_compress_kernel, is_value=is_value),
        grid=(b,),
        in_specs=[pl.BlockSpec((1, n16, hw), lambda i: (i, 0, 0)), _const_spec(pos.shape),
                  _const_spec(w1.shape), _const_spec(w1g.shape), _const_spec(w2p.shape),
                  pl.BlockSpec((1, n16, 1), lambda i: (i, 0, 0))],
        out_specs=pl.BlockSpec((1, NSA_GROUPS, n16, LANE), lambda i: (i, 0, 0, 0)),
        out_shape=jax.ShapeDtypeStruct((b, NSA_GROUPS, n16, LANE), BF16),
        compiler_params=_params(("parallel",)),
        name="nsa_compress",
    )(h, pos, w1, w1g, w2p, dpos_cend)


def _topk_mask_t(x, k):
    n = x.shape[0]
    ridx = _iota(x.shape, 0).astype(F32)
    sel = jnp.zeros(x.shape, F32)
    for _ in range(k):
        m = jnp.max(x, axis=0, keepdims=True)
        first = jnp.min(jnp.where(x == m, ridx, float(n)), axis=0, keepdims=True)
        hit = ridx == first
        sel = jnp.where(hit, 1.0, sel)
        x = jnp.where(hit, REMOVED, x)
    return sel


def _cmp_topk_kernel(q_ref, kc_ref, vc_ref, gate_ref, oc_ref, selm1_ref, used_ref, imp_ref, *, tq, ncp, topk):
    qi = pl.program_id(1)
    t = qi * tq + _iota((tq, 1), 0)
    gates = gate_ref[0]
    blk = _iota((tq, LANE), 1)
    chunk = _shr(t, SEL_BLOCK)

    def attend(nc):
        cend = _iota((1, nc), 1) * CMP_STRIDE + (CMP_LEN - 1)
        mask = (cend <= t)[None]
        cstart = _iota((nc, LANE), 0) * CMP_STRIDE
        sstart = _iota((nc, LANE), 1) * SEL_BLOCK
        overlap = jnp.where((cstart < sstart + SEL_BLOCK) & (cstart + CMP_LEN > sstart)
                            & (cstart < (ncp - 1) * CMP_STRIDE), 1.0, 0.0).astype(BF16)
        for g in range(NSA_GROUPS):
            q = q_ref[g * NSA_HPG:(g + 1) * NSA_HPG].reshape(NSA_HPG * tq, LANE)
            s = _dot_nt(q, kc_ref[0, g, :nc]).reshape(NSA_HPG, tq, nc)
            s = jnp.where(mask, s, NEG)
            e = jnp.where(mask, jnp.exp2(s - jnp.max(s, axis=-1, keepdims=True)), 0.0)
            p = e / jnp.maximum(jnp.sum(e, axis=-1, keepdims=True), 1e-30)
            psum = jnp.sum(p, axis=0)
            o = _dot(p.reshape(NSA_HPG * tq, nc).astype(BF16), vc_ref[0, g, :nc])
            heads = [jnp.where(blk < NSA_DH, o[n * tq:(n + 1) * tq] * gates[:, g * NSA_HPG + n:g * NSA_HPG + n + 1], 0.0)
                     for n in range(NSA_HPG)]
            _store_head_pairs(oc_ref, slice(None), heads, first_pair=g * NSA_HPG // 2)
            hi = psum.astype(BF16)
            r1 = psum - hi.astype(F32)
            mid = r1.astype(BF16)
            lo = (r1 - mid.astype(F32)).astype(BF16)
            imp_ref[g] = _dot(hi, overlap) + _dot(mid, overlap) + _dot(lo, overlap)

    lane_tiles = ((qi + 1) * tq // CMP_STRIDE + LANE - 1) // LANE
    for v in range(1, ncp // LANE + 1):
        @pl.when(jnp.minimum(lane_tiles, ncp // LANE) == v)
        def _(v=v):
            attend(v * LANE)

    forced = (blk == 0) | (blk == chunk)
    imps = [jnp.where(forced, REMOVED, jnp.where(blk <= chunk, imp_ref[g], NEG)).T for g in range(NSA_GROUPS)]
    picked = _topk_mask_t(jnp.concatenate(imps, axis=1), topk - 2)
    used = jnp.zeros((1, LANE), F32)
    for g in range(NSA_GROUPS):
        sel = jnp.where(blk <= chunk, jnp.where(forced, 1.0, picked[:, g * tq:(g + 1) * tq].T), 0.0)
        selm1_ref[0, g] = (sel - 1.0).astype(BF16)
        used = jnp.maximum(used, jnp.max(sel, axis=0, keepdims=True))
    used_ref[0, 0] = jnp.broadcast_to(used, used_ref.shape[2:])


def nsa_cmp_topk(qn, kc, vc, gates, tq=256):
    _, b, s, _ = qn.shape
    tq = min(tq, s)
    ncp = kc.shape[2]
    assert s // SEL_BLOCK <= LANE
    topk = min(SEL_TOPK, s // SEL_BLOCK)
    kspec = pl.BlockSpec((1, NSA_GROUPS, ncp, LANE), lambda i, j: (i, 0, 0, 0))
    return pl.pallas_call(
        functools.partial(_cmp_topk_kernel, tq=tq, ncp=ncp, topk=topk),
        grid=(b, s // tq),
        in_specs=[pl.BlockSpec((NSA_HEADS, None, tq, LANE), lambda i, j: (0, i, j, 0)), kspec, kspec,
                  pl.BlockSpec((1, tq, LANE), lambda i, j: (i, j, 0))],
        out_specs=[pl.BlockSpec((1, tq, OUT_W), lambda i, j: (i, j, 0)),
                   pl.BlockSpec((1, NSA_GROUPS, tq, LANE), lambda i, j: (i, 0, j, 0)),
                   pl.BlockSpec((1, 1, 8, LANE), lambda i, j: (i, j, 0, 0))],
        out_shape=[jax.ShapeDtypeStruct((b, s, OUT_W), BF16), jax.ShapeDtypeStruct((b, NSA_GROUPS, s, LANE), BF16),
                   jax.ShapeDtypeStruct((b, s // tq, 8, LANE), F32)],
        scratch_shapes=[pltpu.VMEM((NSA_GROUPS, tq, LANE), F32)],
        compiler_params=_params(("parallel", "parallel")),
        name="nsa_cmp_topk",
    )(qn, kc, vc, gates)


def _nsa_write(o_ref, gate_ref, acc_ref, tq, branch):
    gates = gate_ref[0]
    heads = []
    for h in range(NSA_HEADS):
        g, n = divmod(h, NSA_HPG)
        col = branch * NSA_HEADS + h
        heads.append(_normalized(acc_ref[g, n * tq:(n + 1) * tq]) * gates[:, col:col + 1])
    _store_head_pairs(o_ref, slice(None), heads)


def _sel_attn_kernel(flags, q_ref, k_ref, v_ref, selm1_ref, gate_ref, o_ref, lhs_ref, m_ref, acc_ref, *, t, nq):
    b = pl.program_id(0)
    qi = pl.program_id(1)
    _init_state(m_ref, acc_ref)
    for g in range(NSA_GROUPS):
        lhs_ref[g, :, :LANE] = q_ref[g * NSA_HPG:(g + 1) * NSA_HPG].reshape(NSA_HPG * t, LANE)
        lhs_ref[g, :, LANE:] = jnp.concatenate([selm1_ref[0, g]] * NSA_HPG, axis=0)

    hpc = NSA_HPG // SEL_SPLIT
    half = hpc * t
    chains = [(g, i) for g in range(NSA_GROUPS) for i in range(SEL_SPLIT)]

    def update(ki, diagonal):
        rows = pl.ds(pl.multiple_of(ki * t, t), t)
        scores = [_dot_nt(lhs_ref[g, i * half:(i + 1) * half], k_ref[rows, 2 * g * LANE:(2 * g + 2) * LANE])
                  for g, i in chains]
        if diagonal:
            causal = (_iota((t, t), 1) <= _iota((t, t), 0))[None]
            scores = [jnp.where(causal, s.reshape(hpc, t, t), NEG).reshape(half, t) for s in scores]
        _flash_update_staged(scores, [v_ref[rows, g * LANE:(g + 1) * LANE] for g, _ in chains], m_ref, acc_ref,
                             [(g, slice(i * half, (i + 1) * half)) for g, i in chains])

    base = (b * nq + qi) * nq

    def body(ki, carry):
        @pl.when(flags[base + ki] > 0)
        def _():
            update(ki, False)
        return carry

    lax.fori_loop(0, qi, body, 0)
    update(qi, True)
    _nsa_write(o_ref, gate_ref, acc_ref, t, branch=1)


def nsa_selected(qn, ks, vs, selm1, used, gates):
    _, b, s, _ = qn.shape
    nq = used.shape[1]
    t = s // nq
    bpt = t // SEL_BLOCK
    flags = (used[:, :, 0].reshape(b, nq, LANE // bpt, bpt) > 0.5).any(axis=-1)[..., :nq]
    flags = flags.astype(jnp.int32).reshape(-1)
    imap_q = lambda bb, i, fl: (bb, i, 0)
    rows = NSA_HPG * t
    return pl.pallas_call(
        functools.partial(_sel_attn_kernel, t=t, nq=nq),
        grid_spec=pltpu.PrefetchScalarGridSpec(
            num_scalar_prefetch=1,
            grid=(b, nq),
            in_specs=[pl.BlockSpec((NSA_HEADS, None, t, LANE), lambda bb, i, fl: (0, bb, i, 0)),
                      pl.BlockSpec((None, s, 4 * LANE), lambda bb, i, fl: (bb, 0, 0)),
                      pl.BlockSpec((None, s, 2 * LANE), lambda bb, i, fl: (bb, 0, 0)),
                      pl.BlockSpec((1, NSA_GROUPS, t, LANE), lambda bb, i, fl: (bb, 0, i, 0)),
                      pl.BlockSpec((1, t, LANE), imap_q)],
            out_specs=pl.BlockSpec((1, t, OUT_W), imap_q),
            scratch_shapes=[pltpu.VMEM((NSA_GROUPS, rows, 2 * LANE), BF16), pltpu.VMEM((NSA_GROUPS, rows, LANE), F32),
                            pltpu.VMEM((NSA_GROUPS, rows, LANE), F32)],
        ),
        out_shape=jax.ShapeDtypeStruct((b, s, OUT_W), BF16),
        compiler_params=_params(("parallel", "arbitrary")),
        name="nsa_selected",
    )(flags, qn, ks, vs, selm1, gates)


def _win_attn_kernel(q_ref, kp_ref, kc_ref, vp_ref, vc_ref, gate_ref, o_ref, *, tq):
    qi = pl.program_id(1)
    hq = tq // 2
    span = WIN + hq
    row = _iota((hq, span), 0)
    col = _iota((hq, span), 1)
    diff = WIN + row - col
    band = (diff >= 0) & (diff < WIN)
    gates = gate_ref[0]
    chains = [(g, rh) for g in range(NSA_GROUPS) for rh in range(2)]
    scores = []
    for g, rh in chains:
        gl = slice(g * LANE, (g + 1) * LANE)
        q = q_ref[g * NSA_HPG:(g + 1) * NSA_HPG, rh * hq:(rh + 1) * hq].reshape(NSA_HPG * hq, LANE)
        s = jnp.concatenate([_dot_nt(q, kp_ref[0, rh * hq:, gl]), _dot_nt(q, kc_ref[0, :(rh + 1) * hq, gl])], axis=1)
        valid = (band & (col >= WIN - qi * tq - rh * hq))[None]
        scores.append(jnp.where(valid, s.reshape(NSA_HPG, hq, span), NEG).reshape(NSA_HPG * hq, span))
    tops = [jnp.max(s, axis=-1, keepdims=True) for s in scores]
    probs = [jnp.exp2(s - m).astype(BF16) for s, m in zip(scores, tops)]
    for (g, rh), p in zip(chains, probs):
        gl = slice(g * LANE, (g + 1) * LANE)
        n_prev = tq - rh * hq
        acc = _dot(p[:, :n_prev], vp_ref[0, rh * hq:, gl]) + _dot(p[:, n_prev:], vc_ref[0, :(rh + 1) * hq, gl])
        rows = slice(rh * hq, (rh + 1) * hq)
        cols = [2 * NSA_HEADS + g * NSA_HPG + n for n in range(NSA_HPG)]
        heads = [_normalized(acc[n * hq:(n + 1) * hq]) * gates[rows, cols[n]:cols[n] + 1] for n in range(NSA_HPG)]
        _store_head_pairs(o_ref, rows, heads, first_pair=g * NSA_HPG // 2)


def nsa_window(qn, kw, vw, gates, tq=512):
    _, b, s, _ = qn.shape
    assert tq == WIN and s % tq == 0
    prev = pl.BlockSpec((1, tq, 2 * LANE), lambda bb, i: (bb, jnp.maximum(i - 1, 0), 0))
    cur = pl.BlockSpec((1, tq, 2 * LANE), lambda bb, i: (bb, i, 0))
    return pl.pallas_call(
        functools.partial(_win_attn_kernel, tq=tq),
        grid=(b, s // tq),
        in_specs=[pl.BlockSpec((NSA_HEADS, None, tq, LANE), lambda bb, i: (0, bb, i, 0)), prev, cur, prev, cur,
                  pl.BlockSpec((1, tq, LANE), lambda bb, i: (bb, i, 0))],
        out_specs=pl.BlockSpec((1, tq, OUT_W), lambda bb, i: (bb, i, 0)),
        out_shape=jax.ShapeDtypeStruct((b, s, OUT_W), BF16),
        compiler_params=_params(("parallel", "parallel")),
        name="nsa_window",
    )(qn, kw, kw, vw, vw, gates)


def _odd_out_ln_kernel(x_ref, o0_ref, o1_ref, o2_ref, l0_ref, l1_ref, l2_ref, w_ref, g_ref, b_ref, o_ref,
                       o_scr, l_scr):
    tm = x_ref.shape[0]
    for gi, (src, lsrc) in enumerate(((o0_ref, l0_ref), (o1_ref, l1_ref), (o2_ref, l2_ref))):
        dil = DIL_PATTERNS[gi][1]
        for r in range(dil):
            rows = pl.ds(r, tm // dil, stride=dil) if dil > 1 else slice(None)
            for h in range(DIL_GH):
                o_scr[gi, h, rows, :] = src[r, :, h * DIL_DH:(h + 1) * DIL_DH].astype(F32)
            l_scr[gi, rows, :] = lsrc[r]
    lses = [l_scr[gi] for gi in range(DIL_GROUPS)]
    top = jnp.maximum(jnp.maximum(lses[0], lses[1]), lses[2])
    es = [jnp.exp(l - top) for l in lses]
    den = es[0] + es[1] + es[2]
    wts = [e / den for e in es]
    cols = []
    for h in range(DIL_GH):
        merged = None
        for gi in range(DIL_GROUPS):
            term = wts[gi][:, h:h + 1] * o_scr[gi, h]
            merged = term if merged is None else merged + term
        cols.append(merged.astype(BF16))
    mix = _dot(jnp.concatenate(cols, axis=1), w_ref[...])
    o_ref[...] = _layer_norm(ALPHA * x_ref[...] + mix, g_ref[...], b_ref[...])


def odd_out_ln(x2, outs, lses, w, g, b, tm=1024):
    m, d = x2.shape
    b_, _, s = outs[0].shape[0], None, outs[0].shape[1] * outs[0].shape[2]
    tm = min(tm, s)
    width = DIL_GH * DIL_DH
    row = pl.BlockSpec((None, tm, d), lambda bb, i: (bb, i, 0))
    cls = lambda gi, wd: pl.BlockSpec((None, DIL_PATTERNS[gi][1], tm // DIL_PATTERNS[gi][1], wd),
                                      lambda bb, i: (bb, 0, i, 0))
    out = pl.pallas_call(
        _odd_out_ln_kernel,
        grid=(b_, s // tm),
        in_specs=[row] + [cls(gi, width) for gi in range(DIL_GROUPS)] + [cls(gi, LANE) for gi in range(DIL_GROUPS)]
        + [_const_spec(w.shape), _const_spec((1, d)), _const_spec((1, d))],
        out_specs=row,
        out_shape=jax.ShapeDtypeStruct((b_, s, d), F32),
        scratch_shapes=[pltpu.VMEM((DIL_GROUPS, DIL_GH, tm, DIL_DH), F32), pltpu.VMEM((DIL_GROUPS, tm, LANE), F32)],
        compiler_params=_params(("parallel", "parallel")),
        name="odd_out_ln",
    )(x2.reshape(b_, s, d), *outs, *lses, w, g.reshape(1, d), b.reshape(1, d))
    return out.reshape(m, d)


def _odd_proj_kernel(x_ref, w_ref, o0_ref, o1_ref, o2_ref, xs_ref):
    tm, d_model = x_ref.shape
    width = DIL_GH * DIL_DH
    n_chunks = d_model // LANE
    for c in range(n_chunks):
        xs_ref[c] = x_ref[:, c * LANE:(c + 1) * LANE]
    for gi, o_ref in enumerate((o0_ref, o1_ref, o2_ref)):
        dil = DIL_PATTERNS[gi][1]
        if dil == 1:
            xg = x_ref[...]
        else:
            xg = jnp.concatenate(
                [jnp.concatenate([xs_ref[c, pl.ds(r, tm // dil, stride=dil), :] for r in range(dil)], axis=0)
                 for c in range(n_chunks)], axis=1)
        xg = xg.astype(BF16)
        for c in range(3):
            col = (gi * 3 + c) * width
            y = _dot(xg, w_ref[:, col:col + width])
            if c == 0:
                y = y * (DIL_SCALE * LOG2E)
            o_ref[:, :, c * width:(c + 1) * width] = y.astype(BF16).reshape(dil, tm // dil, width)


def odd_proj(x2, b, s, w_in, tm=512):
    m, d = x2.shape
    tm = min(tm, s)
    width = DIL_GH * DIL_DH
    w = w_in.reshape(d, 3, DIL_GROUPS, width).transpose(0, 2, 1, 3).reshape(d, 3 * DIL_GROUPS * width).astype(BF16)
    dils = [dil for _, dil in DIL_PATTERNS]
    return pl.pallas_call(
        _odd_proj_kernel,
        grid=(b, s // tm),
        in_specs=[pl.BlockSpec((None, tm, d), lambda bb, i: (bb, i, 0)), _const_spec(w.shape)],
        out_specs=[pl.BlockSpec((None, dil, tm // dil, 3 * width), lambda bb, i: (bb, 0, i, 0)) for dil in dils],
        out_shape=[jax.ShapeDtypeStruct((b, dil, s // dil, 3 * width), BF16) for dil in dils],
        scratch_shapes=[pltpu.VMEM((d // LANE, tm, LANE), F32)],
        compiler_params=_params(("parallel", "parallel")),
        name="odd_proj",
    )(x2.reshape(b, s, d), w)


def _dil_attn_kernel(slope_tab, pos0_tab, q_ref, kc_ref, kp_ref, vc_ref, vp_ref, pc_ref, pp_ref, pq_ref, o_ref,
                     lse_ref, *, group, tq):
    sub = DIL_SPAN
    b = pl.program_id(0)
    jt = pl.program_id(2)
    row = _iota((sub, 2 * sub), 0)
    col = _iota((sub, 2 * sub), 1)
    band = (col >= row) & (col <= row + sub)
    band_bias = jnp.where(band, 0.0, NEG)
    start_bias = jnp.where(band & (col >= sub), 0.0, NEG)
    pos0 = pos0_tab[b]
    lane = _iota((sub, LANE), 1)
    heads = range(DIL_GH)
    hs = [slice(h * DIL_DH, (h + 1) * DIL_DH) for h in heads]
    slopes = [slope_tab[group * DIL_GH + h] for h in heads]
    tiles = range(tq // sub)
    cur = [slice(i * sub, (i + 1) * sub) for i in tiles]
    biases, ks, vs = [], [], []
    for i in tiles:
        if i == 0:
            mask_bias = jnp.where(jt == 0, start_bias, band_bias)
            pk = jnp.concatenate([pp_ref[...], pc_ref[:, :sub]], axis=1)
            ks.append([jnp.concatenate([kp_ref[:, hs[h]], kc_ref[:sub, hs[h]]], axis=0) for h in heads])
            vs.append([jnp.concatenate([vp_ref[:, hs[h]], vc_ref[:sub, hs[h]]], axis=0) for h in heads])
        else:
            mask_bias = band_bias
            pk = pc_ref[:, (i - 1) * sub:(i + 1) * sub]
            ks.append([kc_ref[(i - 1) * sub:(i + 1) * sub, hs[h]] for h in heads])
            vs.append([vc_ref[(i - 1) * sub:(i + 1) * sub, hs[h]] for h in heads])
        dpos = pk - pos0
        biases.append([mask_bias + slopes[h] * dpos for h in heads])
    chains = [(i, h) for i in tiles for h in heads]
    ss = [_dot_nt(q_ref[cur[i], hs[h]], ks[i][h]) + biases[i][h] for i, h in chains]
    ms = [jnp.max(s, axis=-1, keepdims=True) for s in ss]
    ps = [jnp.exp2(s - m) for s, m in zip(ss, ms)]
    dens = [jnp.sum(p, axis=-1, keepdims=True) for p in ps]
    for n, (i, h) in enumerate(chains):
        o_ref[cur[i], hs[h]] = (_dot(ps[n].astype(BF16), vs[i][h]) / dens[n]).astype(BF16)
    for i in tiles:
        dq = (pq_ref[cur[i], :] - pos0) * LN2
        lse_tile = jnp.zeros((sub, LANE), F32)
        for h in heads:
            n = i * DIL_GH + h
            lse_tile = jnp.where(lane == h, ms[n] * LN2 + jnp.log(dens[n]) - slopes[h] * dq, lse_tile)
        lse_ref[cur[i], :] = lse_tile


def dilated_group_attention(qkv, pos_f, group, tq=1024):
    b, d, sd, _ = qkv.shape
    w, dil = DIL_PATTERNS[group]
    assert w // dil == DIL_SPAN and dil == d
    tq = min(tq, sd)
    sub = DIL_SPAN
    width = DIL_GH * DIL_DH
    n_slopes = DIL_GROUPS * DIL_GH
    slopes = 2.0 ** (-ALIBI_MAX_BIAS * jnp.arange(1, n_slopes + 1, dtype=F32) / n_slopes) * LOG2E
    pos0 = pos_f[:, 0]
    posc = pos_f.reshape(b, sd, d).transpose(0, 2, 1).reshape(b, d, 1, sd)
    r = tq // sub
    cur = lambda c: pl.BlockSpec((None, None, tq, width), lambda bb, rr, j, *_: (bb, rr, j, c))
    prev = lambda c: pl.BlockSpec((None, None, sub, width),
                                  lambda bb, rr, j, *_: (bb, rr, jnp.maximum(j * r - 1, 0), c))
    return pl.pallas_call(
        functools.partial(_dil_attn_kernel, group=group, tq=tq),
        grid_spec=pltpu.PrefetchScalarGridSpec(
            num_scalar_prefetch=2,
            grid=(b, d, sd // tq),
            in_specs=[cur(0), cur(1), prev(1), cur(2), prev(2),
                      pl.BlockSpec((None, None, 1, tq), lambda bb, rr, j, *_: (bb, rr, 0, j)),
                      pl.BlockSpec((None, None, 1, sub), lambda bb, rr, j, *_: (bb, rr, 0, jnp.maximum(j * r - 1, 0))),
                      pl.BlockSpec((None, None, tq, 1), lambda bb, rr, j, *_: (bb, rr, j, 0))],
            out_specs=[pl.BlockSpec((None, None, tq, width), lambda bb, rr, j, *_: (bb, rr, j, 0)),
                       pl.BlockSpec((None, None, tq, LANE), lambda bb, rr, j, *_: (bb, rr, j, 0))],
        ),
        out_shape=[jax.ShapeDtypeStruct((b, d, sd, width), BF16), jax.ShapeDtypeStruct((b, d, sd, LANE), F32)],
        compiler_params=_params(("parallel", "parallel", "parallel")),
        name=f"dilated_attention_g{group}",
    )(slopes, pos0, qkv, qkv, qkv, qkv, qkv, posc, posc, posc.reshape(b, d, sd, 1))


def even_mixer(x2, b, s, pos_f, w_in, q_norm_g, kv_norm_g, w_uq, w_uk, w_uv,
               cmp_pos, cmp_k_w1, cmp_k_w2, cmp_v_w1, cmp_v_w2, w_out, ln_g, ln_b):
    m = b * s
    half = MLA_ROPE // 2
    inv = ROPE_THETA ** (-jnp.arange(half, dtype=F32) / half)
    ang = (pos_f[..., None] * inv).reshape(m * half // LANE, LANE)
    dpos = (pos_f - pos_f[:, :1]) * LOG2E
    ncp = s // CMP_STRIDE
    dpos_cend = jnp.pad(dpos[:, CMP_LEN - 1::CMP_STRIDE], ((0, 0), (0, 1))).reshape(b, ncp, 1)
    tok_table = jnp.concatenate([jnp.cos(ang).reshape(m, half), jnp.sin(ang).reshape(m, half), dpos.reshape(m, 1),
                                 jnp.zeros((m, LANE - MLA_ROPE - 1), F32)], axis=1)

    qm, km, vm, qn, kvc, ks, kw, vs, vw, gates = even_proj(
        x2, tok_table, s, w_in, q_norm_g, kv_norm_g, w_uq, w_uk, w_uv)
    o_mla = mla_attention(qm.reshape(b, s, HW), km.reshape(b, s, HW), vm.reshape(b, s, HW))

    kvc = kvc.reshape(2, b, ncp, CMP_STRIDE * LANE)
    kc = nsa_compress(kvc[0], cmp_pos, cmp_k_w1, cmp_k_w2, dpos_cend, is_value=False)
    vc = nsa_compress(kvc[1], cmp_pos, cmp_v_w1, cmp_v_w2, dpos_cend, is_value=True)
    qn = qn.reshape(NSA_HEADS, b, s, LANE)
    gates = gates.reshape(b, s, LANE)
    o_c, selm1, used = nsa_cmp_topk(qn, kc, vc, gates)
    o_s = nsa_selected(qn, ks.reshape(b, s, 4 * LANE), vs.reshape(b, s, 2 * LANE), selm1, used, gates)
    o_w = nsa_window(qn, kw.reshape(b, s, 2 * LANE), vw.reshape(b, s, 2 * LANE), gates)

    outs = [o.reshape(m, OUT_W) for o in (o_mla, o_c, o_s, o_w)]
    return outs, [w_out.astype(BF16)], ln_g, ln_b


def odd_mixer_ln(x2, b, s, pos_f, w_in, w_out, ln_g, ln_b):
    qkvs = odd_proj(x2, b, s, w_in)
    parts = [dilated_group_attention(qkvs[g], pos_f, g) for g in range(DIL_GROUPS)]
    return odd_out_ln(x2, [p[0] for p in parts], [p[1] for p in parts], w_out.astype(BF16), ln_g, ln_b)


def kernel(x, positions, ln1_g, ln1_b, ffn1_w_gate, ffn1_w_up, ffn1_w_down, mix_in_even, mla_q_norm, mla_kv_norm, mla_w_uq, mla_w_uk, mla_w_uv, nsa_cmp_pos, nsa_cmp_k_w1, nsa_cmp_k_w2, nsa_cmp_v_w1, nsa_cmp_v_w2, mix_out_even, mix_in_odd, mix_out_odd, ln2_g, ln2_b, ffn2_w_gate, ffn2_w_up, ffn2_w_down, ln3_g, ln3_b):
    b, s, d = x.shape
    x2 = x.reshape(b * s, d)
    pos_f = positions.astype(F32)
    for i in range(DEPTH):
        j = i // 2
        x2 = ffn_ln(x2, ffn1_w_gate[i].astype(BF16), ffn1_w_up[i].astype(BF16), ffn1_w_down[i].astype(BF16),
                    ln1_g[i], ln1_b[i])
        mix = None
        if i % 2 == 0:
            mix = even_mixer(x2, b, s, pos_f, mix_in_even[j], mla_q_norm[j], mla_kv_norm[j], mla_w_uq[j],
                                mla_w_uk[j], mla_w_uv[j], nsa_cmp_pos[j], nsa_cmp_k_w1[j], nsa_cmp_k_w2[j],
                                nsa_cmp_v_w1[j], nsa_cmp_v_w2[j], mix_out_even[j], ln2_g[i], ln2_b[i])
        else:
            x2 = odd_mixer_ln(x2, b, s, pos_f, mix_in_odd[j], mix_out_odd[j], ln2_g[i], ln2_b[i])
        x2 = ffn_ln(x2, ffn2_w_gate[i].astype(BF16), ffn2_w_up[i].astype(BF16), ffn2_w_down[i].astype(BF16),
                    ln3_g[i], ln3_b[i], mix=mix)
    return x2.reshape(b, s, d)
```

```python
import functools
import math

import numpy as np
import jax
import jax.numpy as jnp
from jax import lax
from jax.experimental import pallas as pl
from jax.experimental.pallas import tpu as pltpu

F32 = jnp.float32
BF16 = jnp.bfloat16

DEPTH = 2
LN_EPS = 1e-5
RMS_EPS = 1e-6
ALPHA = (2 * DEPTH) ** 0.25
HALF_STEP = 0.5
NEG = -1e30
BIG = 1e9
MASK_BIG = 1e30
REMOVED = -3.0e38
ALIBI_MAX_BIAS = 8.0
LOG2E = math.log2(math.e)
LN2 = math.log(2.0)
LANE = 128

MLA_HEADS = 8
MLA_Q_RANK = 384
MLA_KV_RANK = 256
MLA_NOPE = 64
MLA_ROPE = 32
MLA_V = 64
ROPE_THETA = 10000.0
MLA_SCALE = (MLA_NOPE + MLA_ROPE) ** -0.5
MLA_CHAINS = 4

NSA_HEADS = 8
NSA_GROUPS = 2
NSA_HPG = 4
NSA_DH = 64
CMP_LEN = 32
CMP_STRIDE = 16
CMP_HIDDEN = 256
SEL_BLOCK = 64
SEL_TOPK = 16
WIN = 512
NSA_SCALE = NSA_DH ** -0.5
SEL_SPLIT = 2
NSA_SLOPES = tuple(2.0 ** (-ALIBI_MAX_BIAS * (i + 1) / NSA_HEADS) for i in range(NSA_HEADS))
ONES_LANE = 64
BIAS_LANES = (64, 65, 66)

DIL_PATTERNS = ((128, 1), (512, 4), (2048, 16))
DIL_GROUPS = 3
DIL_GH = 4
DIL_DH = 128
DIL_SCALE = DIL_DH ** -0.5
DIL_SPAN = 128

VMEM_LIMIT = 48 * 1024 * 1024
MIXED_FFN_VMEM_LIMIT = 56 * 1024 * 1024


def _iota(shape, dim):
    return lax.broadcasted_iota(jnp.int32, shape, dim)


def _shr(x, pow2):
    return jnp.right_shift(x, int(pow2).bit_length() - 1)


def _dot(a, b):
    return jnp.dot(a, b, preferred_element_type=F32)


def _dot_nt(a, b):
    return lax.dot_general(a, b, (((1,), (1,)), ((), ())), preferred_element_type=F32)


def _const_spec(shape):
    zeros = (0,) * len(shape)
    return pl.BlockSpec(shape, lambda *_: zeros, pipeline_mode=pl.Buffered(1))


def _params(sem):
    return pltpu.CompilerParams(dimension_semantics=sem, vmem_limit_bytes=VMEM_LIMIT)


def _layer_norm(z, g, b):
    mu = jnp.mean(z, axis=-1, keepdims=True)
    zc = z - mu
    var = jnp.mean(zc * zc, axis=-1, keepdims=True)
    return zc * lax.rsqrt(var + LN_EPS) * g + b


def _rms_norm(z, g):
    return z * lax.rsqrt(jnp.mean(z * z, axis=-1, keepdims=True) + RMS_EPS) * g


def _bias_pieces(d, lane):
    x = jnp.broadcast_to(d, lane.shape)
    hi = x.astype(BF16).astype(F32)
    r = x - hi
    mid = r.astype(BF16).astype(F32)
    lo = (r - mid).astype(BF16).astype(F32)
    return jnp.where(lane == BIAS_LANES[0], hi,
                     jnp.where(lane == BIAS_LANES[1], mid, jnp.where(lane == BIAS_LANES[2], lo, 0.0)))


def _flash_update_staged(scores, values, m_ref, acc_ref, idxs):
    m_old = [m_ref[i] for i in idxs]
    m_new = [jnp.maximum(mo, jnp.max(s, axis=-1, keepdims=True)) for mo, s in zip(m_old, scores)]
    ps = [jnp.exp2(s - jnp.tile(mn, (1, s.shape[1] // LANE))).astype(BF16) for s, mn in zip(scores, m_new)]
    for i, mo, mn, p, v in zip(idxs, m_old, m_new, ps, values):
        acc_ref[i] = jnp.exp2(mo - mn) * acc_ref[i] + _dot(p, v)
        m_ref[i] = mn


def _init_state(m_ref, acc_ref):
    m_ref[...] = jnp.full(m_ref.shape, NEG, F32)
    acc_ref[...] = jnp.zeros(acc_ref.shape, F32)


def _normalized(acc):
    lane = _iota(acc.shape, 1)
    o = acc / jnp.maximum(acc[:, ONES_LANE:ONES_LANE + 1], 1e-30)
    return jnp.where(lane < ONES_LANE, o, 0.0)


def _store_head_pairs(o_ref, rows, heads, first_pair=0):
    for pr in range(len(heads) // 2):
        packed = heads[2 * pr] + pltpu.roll(heads[2 * pr + 1], LANE // 2, 1)
        o_ref[0, rows, (first_pair + pr) * LANE:(first_pair + pr + 1) * LANE] = packed.astype(BF16)


def _ffn_ln_kernel(*refs, n_chunks, mixed):
    if mixed:
        x_ref, om_ref, oc_ref, os_ref, ow_ref, w_ref, g2_ref, b2_ref = refs[:8]
        nsa = (oc_ref[...].astype(F32) + os_ref[...].astype(F32) + ow_ref[...].astype(F32)).astype(BF16)
        mix = _dot(jnp.concatenate([om_ref[...], nsa], axis=1), w_ref[...])
        x = _layer_norm(ALPHA * x_ref[...] + mix, g2_ref[...], b2_ref[...])
        refs = refs[8:]
    else:
        x = refs[0][...]
        refs = refs[1:]
    wg_ref, wu_ref, wd_ref, g_ref, b_ref, o_ref = refs
    xb = x.astype(BF16)
    c = wg_ref.shape[1] // n_chunks
    y = None
    for i in range(n_chunks):
        gt = _dot(xb, wg_ref[:, i * c:(i + 1) * c])
        up = _dot(xb, wu_ref[:, i * c:(i + 1) * c])
        h = (gt * jax.nn.sigmoid(gt) * up).astype(BF16)
        part = _dot(h, wd_ref[i * c:(i + 1) * c, :])
        y = part if y is None else y + part
    o_ref[...] = _layer_norm(ALPHA * x + HALF_STEP * y, g_ref[...], b_ref[...])


def ffn_ln(x2, wg, wu, wd, g, b, mix=None, tm=1024, n_chunks=11):
    m, d = x2.shape
    tm = min(tm, m)
    row = lambda w: pl.BlockSpec((tm, w), lambda i: (i, 0))
    args, specs = [x2], [row(d)]
    if mix is not None:
        outs, ws, g2, b2 = mix
        args += list(outs) + list(ws) + [g2.reshape(1, d), b2.reshape(1, d)]
        specs += [row(o.shape[1]) for o in outs] + [_const_spec(w.shape) for w in ws] + [_const_spec((1, d))] * 2
    args += [wg, wu, wd, g.reshape(1, d), b.reshape(1, d)]
    specs += [_const_spec(wg.shape), _const_spec(wu.shape), _const_spec(wd.shape), _const_spec((1, d)), _const_spec((1, d))]
    limit = VMEM_LIMIT if mix is None else MIXED_FFN_VMEM_LIMIT
    return pl.pallas_call(
        functools.partial(_ffn_ln_kernel, n_chunks=n_chunks, mixed=mix is not None),
        grid=(m // tm,),
        in_specs=specs,
        out_specs=row(d),
        out_shape=jax.ShapeDtypeStruct((m, d), F32),
        compiler_params=pltpu.CompilerParams(dimension_semantics=("parallel",), vmem_limit_bytes=limit),
        name="ffn_ln" if mix is None else "mix_ffn_ln",
    )(*args)


HW = MLA_HEADS * LANE
OUT_W = MLA_HEADS * MLA_V
EVEN_X_COLS = (MLA_Q_RANK, MLA_KV_RANK, LANE, LANE, HW, 2 * LANE, 8 * LANE, LANE)
EVEN_X_OFFS = tuple(int(v) for v in np.cumsum((0,) + EVEN_X_COLS))


def _even_proj_kernel(x_ref, tok_ref, wx_ref, qg_ref, kvg_ref, wuq_ref, wuqs_ref, wuk_ref,
                      wuv_ref, slope_ref, qm_ref, km_ref, vm_ref, qn_ref, kvc_ref, ks_ref, kw_ref, vs_ref, vw_ref,
                      gate_ref, kvc_scr, *, tiles_per_seq):
    tm = x_ref.shape[0]
    xb = x_ref[...].astype(BF16)
    lane = _iota((tm, LANE), 1)
    table = tok_ref[...]
    half = MLA_ROPE // 2
    lo = (lane >= MLA_NOPE) & (lane < MLA_NOPE + half)
    hi = (lane >= MLA_NOPE + half) & (lane < MLA_NOPE + MLA_ROPE)
    shifted = lambda k: pltpu.roll(table, k, 1)
    cos = jnp.where(lo, shifted(MLA_NOPE), jnp.where(hi, shifted(MLA_NOPE + half), jnp.where(lane < MLA_NOPE, 1.0, 0.0)))
    sin = jnp.where(lo, shifted(MLA_NOPE - half), jnp.where(hi, shifted(MLA_NOPE), 0.0))
    ones_lane = jnp.where(lane == ONES_LANE, 1.0, 0.0)
    pos_term = _bias_pieces(table[:, MLA_ROPE:MLA_ROPE + 1], lane)
    tok = (pl.program_id(0) % tiles_per_seq) * tm + _iota((tm, LANE), 0)
    block_onehot = jnp.where(lane == _shr(tok, SEL_BLOCK), MASK_BIG, 0.0).astype(BF16)

    def xdot(i):
        return _dot(xb, wx_ref[:, EVEN_X_OFFS[i]:EVEN_X_OFFS[i + 1]])

    cq = _rms_norm(xdot(0), qg_ref[...]).astype(BF16)
    ckv = _rms_norm(xdot(1), kvg_ref[...]).astype(BF16)
    rope_pair = _dot(xb, wx_ref[:, EVEN_X_OFFS[2]:EVEN_X_OFFS[4]])
    k_rot = rope_pair[:, :LANE] * cos + rope_pair[:, LANE:] * sin
    q_all, q_swap, k_all, v_all = (_dot(a, w[...]) for a, w in ((cq, wuq_ref), (cq, wuqs_ref), (ckv, wuk_ref),
                                                                  (ckv, wuv_ref)))
    for h in range(MLA_HEADS):
        sl = slice(h * LANE, (h + 1) * LANE)
        q = q_all[:, sl] * cos + q_swap[:, sl] * sin
        qm_ref[:, sl] = (q * (MLA_SCALE * LOG2E)).astype(BF16)
        km_ref[:, sl] = (k_all[:, sl] + k_rot).astype(BF16)
        vm_ref[:, sl] = (v_all[:, sl] + ones_lane).astype(BF16)
    qn = xdot(4) * (NSA_SCALE * LOG2E) + slope_ref[...]
    for h in range(NSA_HEADS):
        qn_ref[h] = qn[:, h * LANE:(h + 1) * LANE].astype(BF16)
    kvc = xdot(5)
    for j in range(2):
        kvc_scr[j] = kvc[:, j * LANE:(j + 1) * LANE]
        for l in range(CMP_STRIDE):
            kvc_ref[j, :, l * LANE:(l + 1) * LANE] = kvc_scr[j, pl.ds(l, tm // CMP_STRIDE, stride=CMP_STRIDE), :].astype(BF16)
    kv8 = xdot(6)
    blk = lambda i: kv8[:, i * LANE:(i + 1) * LANE]
    for g in range(NSA_GROUPS):
        ks_ref[:, 2 * g * LANE:(2 * g + 1) * LANE] = (blk(g) + pos_term).astype(BF16)
        ks_ref[:, (2 * g + 1) * LANE:(2 * g + 2) * LANE] = block_onehot
        kw_ref[:, g * LANE:(g + 1) * LANE] = (blk(2 + g) + pos_term).astype(BF16)
        vs_ref[:, g * LANE:(g + 1) * LANE] = (blk(4 + g) + ones_lane).astype(BF16)
        vw_ref[:, g * LANE:(g + 1) * LANE] = (blk(6 + g) + ones_lane).astype(BF16)
    gate_ref[...] = jax.nn.sigmoid(xdot(7))


def _head_blocks(w, n_heads, width):
    k = w.shape[0]
    w = jnp.pad(w.reshape(k, n_heads, width), ((0, 0), (0, 0), (0, LANE - width)))
    return w.reshape(k, n_heads * LANE)


def even_proj(x2, tok_table, seq, w_in, q_norm_g, kv_norm_g, w_uq, w_uk, w_uv, tm=512):
    m, d = x2.shape
    tm = min(tm, seq)
    half = MLA_ROPE // 2
    gw = NSA_GROUPS * NSA_DH
    cuts = np.cumsum((MLA_Q_RANK, MLA_KV_RANK, MLA_ROPE, NSA_HEADS * NSA_DH) + (gw,) * 6)
    cuts = [0] + [int(c) for c in cuts]
    w_cq, w_ckv, w_kpe, w_q = (w_in[:, cuts[i]:cuts[i + 1]] for i in range(4))
    w_kc, w_vc, w_ks, w_vs, w_kw, w_vw = (w_in[:, cuts[4 + i]:cuts[5 + i]] for i in range(6))
    w_gate = w_in[:, cuts[10]:]
    w_kpe_sw = jnp.concatenate([-w_kpe[:, half:], w_kpe[:, :half]], axis=1)
    rope_pad = ((0, 0), (MLA_NOPE, LANE - MLA_NOPE - MLA_ROPE))
    w_gate_blk = jnp.pad(w_gate, ((0, 0), (0, LANE - w_gate.shape[1])))
    wx = jnp.concatenate(
        [w_cq, w_ckv, jnp.pad(w_kpe, rope_pad), jnp.pad(w_kpe_sw, rope_pad), _head_blocks(w_q, NSA_HEADS, NSA_DH),
         w_kc, w_vc] + [_head_blocks(w, NSA_GROUPS, NSA_DH) for w in (w_ks, w_kw, w_vs, w_vw)] + [w_gate_blk],
        axis=1).astype(BF16)

    qd = MLA_NOPE + MLA_ROPE
    uq = w_uq.reshape(MLA_Q_RANK, MLA_HEADS, qd)
    uq_sw = jnp.concatenate([jnp.zeros_like(uq[..., :MLA_NOPE]), -uq[..., MLA_NOPE + half:],
                             uq[..., MLA_NOPE:MLA_NOPE + half]], axis=-1)
    wuq = _head_blocks(uq.reshape(MLA_Q_RANK, -1), MLA_HEADS, qd).astype(BF16)
    wuqs = _head_blocks(uq_sw.reshape(MLA_Q_RANK, -1), MLA_HEADS, qd).astype(BF16)
    wuk = _head_blocks(w_uk, MLA_HEADS, MLA_NOPE).astype(BF16)
    wuv = _head_blocks(w_uv, MLA_HEADS, MLA_V).astype(BF16)
    slope_row = np.zeros((1, HW), np.float32)
    for h in range(NSA_HEADS):
        for ln in BIAS_LANES:
            slope_row[0, h * LANE + ln] = NSA_SLOPES[h]

    row = lambda w: pl.BlockSpec((tm, w), lambda i: (i, 0))
    sds = jax.ShapeDtypeStruct
    return pl.pallas_call(
        functools.partial(_even_proj_kernel, tiles_per_seq=seq // tm),
        grid=(m // tm,),
        in_specs=[row(d), row(LANE), _const_spec(wx.shape), _const_spec((1, MLA_Q_RANK)),
                  _const_spec((1, MLA_KV_RANK)), _const_spec(wuq.shape), _const_spec(wuqs.shape),
                  _const_spec(wuk.shape), _const_spec(wuv.shape), _const_spec((1, HW))],
        out_specs=[row(HW), row(HW), row(HW), pl.BlockSpec((NSA_HEADS, tm, LANE), lambda i: (0, i, 0)),
                   pl.BlockSpec((2, tm // CMP_STRIDE, CMP_STRIDE * LANE), lambda i: (0, i, 0)), row(4 * LANE),
                   row(2 * LANE), row(2 * LANE), row(2 * LANE), row(LANE)],
        out_shape=[sds((m, HW), BF16)] * 3 + [sds((NSA_HEADS, m, LANE), BF16),
                                              sds((2, m // CMP_STRIDE, CMP_STRIDE * LANE), BF16),
                                              sds((m, 4 * LANE), BF16), sds((m, 2 * LANE), BF16),
                                              sds((m, 2 * LANE), BF16), sds((m, 2 * LANE), BF16), sds((m, LANE), F32)],
        scratch_shapes=[pltpu.VMEM((2, tm, LANE), F32)],
        compiler_params=_params(("parallel",)),
        name="even_proj",
    )(x2, tok_table, wx, q_norm_g.reshape(1, -1), kv_norm_g.reshape(1, -1), wuq, wuqs, wuk, wuv,
      jnp.asarray(slope_row))


def _causal_pairs(nq, tq, tk):
    qi, ki = [], []
    for i in range(nq):
        for j in range(((i + 1) * tq - 1) // tk + 1):
            qi.append(i)
            ki.append(j)
    return jnp.asarray(qi, jnp.int32), jnp.asarray(ki, jnp.int32)


def _mla_kernel(qi_tab, ki_tab, q_ref, k_ref, v_ref, o_ref, m_ref, acc_ref, *, tq, tk):
    p = pl.program_id(1)
    qi = qi_tab[p]
    ki = ki_tab[p]

    @pl.when(ki == 0)
    def _():
        _init_state(m_ref, acc_ref)

    def run(rows, diagonal):
        if diagonal:
            mask = _iota((tk, tk), 0) >= _iota((tk, tk), 1)
        for h0 in range(0, MLA_HEADS, MLA_CHAINS):
            heads = range(h0, h0 + MLA_CHAINS)
            scores = [_dot_nt(q_ref[0, rows, h * LANE:(h + 1) * LANE], k_ref[0, :, h * LANE:(h + 1) * LANE])
                      for h in heads]
            if diagonal:
                scores = [jnp.where(mask, s, NEG) for s in scores]
            _flash_update_staged(scores, [v_ref[0, :, h * LANE:(h + 1) * LANE] for h in heads], m_ref, acc_ref,
                                 [(h, rows) for h in heads])

    blocks = tq // tk
    for r in range(blocks):
        rows = slice(r * tk, (r + 1) * tk)
        own = qi * blocks + r

        @pl.when(ki < own)
        def _(rows=rows):
            run(rows, False)

        @pl.when(ki == own)
        def _(rows=rows):
            run(rows, True)

    @pl.when(ki == ((qi + 1) * tq - 1) // tk)
    def _():
        _store_head_pairs(o_ref, slice(None), [_normalized(acc_ref[h]) for h in range(MLA_HEADS)])


def mla_attention(q, k, v, tq=1024, tk=512):
    b, s, hw = q.shape
    tq, tk = min(tq, s), min(tk, s)
    qi_tab, ki_tab = _causal_pairs(s // tq, tq, tk)
    qspec = pl.BlockSpec((1, tq, hw), lambda bb, p, qt, kt: (bb, qt[p], 0))
    kspec = pl.BlockSpec((1, tk, hw), lambda bb, p, qt, kt: (bb, kt[p], 0))
    return pl.pallas_call(
        functools.partial(_mla_kernel, tq=tq, tk=tk),
        grid_spec=pltpu.PrefetchScalarGridSpec(
            num_scalar_prefetch=2,
            grid=(b, int(qi_tab.shape[0])),
            in_specs=[qspec, kspec, kspec],
            out_specs=pl.BlockSpec((1, tq, OUT_W), lambda bb, p, qt, kt: (bb, qt[p], 0)),
            scratch_shapes=[pltpu.VMEM((MLA_HEADS, tq, LANE), F32), pltpu.VMEM((MLA_HEADS, tq, LANE), F32)],
        ),
        out_shape=jax.ShapeDtypeStruct((b, s, OUT_W), BF16),
        compiler_params=_params(("parallel", "arbitrary")),
        name="mla_attention",
    )(qi_tab, ki_tab, q, k, v)


def _compress_kernel(h_ref, pos_ref, w1_ref, w1g_ref, w2_ref, ext_ref, kc_ref, vc_ref):
    n16 = h_ref.shape[1]
    lane = _iota((n16, LANE), 1)
    extras = (_bias_pieces(ext_ref[0], lane), jnp.where(lane == ONES_LANE, 1.0, 0.0))
    for j, o_ref in enumerate((kc_ref, vc_ref)):
        bias = _dot(pos_ref[...], w1_ref[j])[0:1]
        h = h_ref[j]
        for g in range(NSA_GROUPS):
            first = _dot(h, w1g_ref[j, g, 0])
            second = _dot(h, w1g_ref[j, g, 1])
            hid = first + pltpu.roll(second, n16 - 1, 0) + bias
            act = jax.nn.gelu(hid).astype(BF16)
            o_ref[0, g] = (_dot(act, w2_ref[j]) + extras[j]).astype(BF16)


def nsa_compress(h, cmp_pos, w1s, w2s, dpos_cend):
    _, b, n16, hw = h.shape
    pos = jnp.broadcast_to(cmp_pos.reshape(1, CMP_LEN * NSA_DH), (8, CMP_LEN * NSA_DH)).astype(BF16)
    w2p = jnp.stack([jnp.pad(w2, ((0, 0), (0, LANE - NSA_DH))) for w2 in w2s]).astype(BF16)

    def scattered(w1):
        halves = w1.reshape(2, CMP_STRIDE, 1, NSA_DH, CMP_HIDDEN)
        return jnp.stack([jnp.pad(halves, ((0, 0), (0, 0), (g, NSA_GROUPS - 1 - g), (0, 0), (0, 0)))
                          .reshape(2, hw, CMP_HIDDEN) for g in range(NSA_GROUPS)])

    w1g = jnp.stack([scattered(w1) for w1 in w1s]).astype(BF16)
    w1 = jnp.stack(w1s).astype(BF16)
    out = jax.ShapeDtypeStruct((b, NSA_GROUPS, n16, LANE), BF16)
    ospec = pl.BlockSpec((1, NSA_GROUPS, n16, LANE), lambda i: (i, 0, 0, 0))
    return pl.pallas_call(
        _compress_kernel,
        grid=(b,),
        in_specs=[pl.BlockSpec((2, None, n16, hw), lambda i: (0, i, 0, 0)), _const_spec(pos.shape),
                  _const_spec(w1.shape), _const_spec(w1g.shape), _const_spec(w2p.shape),
                  pl.BlockSpec((1, n16, 1), lambda i: (i, 0, 0))],
        out_specs=[ospec, ospec],
        out_shape=[out, out],
        compiler_params=_params(("parallel",)),
        name="nsa_compress",
    )(h, pos, w1, w1g, w2p, dpos_cend)


def _topk_mask_t(x, k):
    n = x.shape[0]
    ridx = _iota(x.shape, 0).astype(F32)
    sel = jnp.zeros(x.shape, F32)
    for _ in range(k):
        m = jnp.max(x, axis=0, keepdims=True)
        first = jnp.min(jnp.where(x == m, ridx, float(n)), axis=0, keepdims=True)
        hit = ridx == first
        sel = jnp.where(hit, 1.0, sel)
        x = jnp.where(hit, REMOVED, x)
    return sel


def _cmp_topk_kernel(q_ref, kc_ref, vc_ref, gate_ref, oc_ref, selm1_ref, used_ref, imp_ref, *, tq, ncp, topk):
    qi = pl.program_id(1)
    t = qi * tq + _iota((tq, 1), 0)
    gates = gate_ref[0]
    blk = _iota((tq, LANE), 1)
    chunk = _shr(t, SEL_BLOCK)

    def attend(nc):
        cend = _iota((1, nc), 1) * CMP_STRIDE + (CMP_LEN - 1)
        mask = (cend <= t)[None]
        cstart = _iota((nc, LANE), 0) * CMP_STRIDE
        sstart = _iota((nc, LANE), 1) * SEL_BLOCK
        overlap = jnp.where((cstart < sstart + SEL_BLOCK) & (cstart + CMP_LEN > sstart)
                            & (cstart < (ncp - 1) * CMP_STRIDE), 1.0, 0.0).astype(BF16)
        for g in range(NSA_GROUPS):
            q = q_ref[g * NSA_HPG:(g + 1) * NSA_HPG].reshape(NSA_HPG * tq, LANE)
            s = _dot_nt(q, kc_ref[0, g, :nc]).reshape(NSA_HPG, tq, nc)
            s = jnp.where(mask, s, NEG)
            e = jnp.where(mask, jnp.exp2(s - jnp.max(s, axis=-1, keepdims=True)), 0.0)
            p = e / jnp.maximum(jnp.sum(e, axis=-1, keepdims=True), 1e-30)
            psum = jnp.sum(p, axis=0)
            o = _dot(p.reshape(NSA_HPG * tq, nc).astype(BF16), vc_ref[0, g, :nc])
            heads = [jnp.where(blk < NSA_DH, o[n * tq:(n + 1) * tq] * gates[:, g * NSA_HPG + n:g * NSA_HPG + n + 1], 0.0)
                     for n in range(NSA_HPG)]
            _store_head_pairs(oc_ref, slice(None), heads, first_pair=g * NSA_HPG // 2)
            hi = psum.astype(BF16)
            r1 = psum - hi.astype(F32)
            mid = r1.astype(BF16)
            lo = (r1 - mid.astype(F32)).astype(BF16)
            imp_ref[g] = _dot(hi, overlap) + _dot(mid, overlap) + _dot(lo, overlap)

    lane_tiles = ((qi + 1) * tq // CMP_STRIDE + LANE - 1) // LANE
    for v in range(1, ncp // LANE + 1):
        @pl.when(jnp.minimum(lane_tiles, ncp // LANE) == v)
        def _(v=v):
            attend(v * LANE)

    forced = (blk == 0) | (blk == chunk)
    imps = [jnp.where(forced, REMOVED, jnp.where(blk <= chunk, imp_ref[g], NEG)).T for g in range(NSA_GROUPS)]
    picked = _topk_mask_t(jnp.concatenate(imps, axis=1), topk - 2)
    used = jnp.zeros((1, LANE), F32)
    for g in range(NSA_GROUPS):
        sel = jnp.where(blk <= chunk, jnp.where(forced, 1.0, picked[:, g * tq:(g + 1) * tq].T), 0.0)
        selm1_ref[0, g] = (sel - 1.0).astype(BF16)
        used = jnp.maximum(used, jnp.max(sel, axis=0, keepdims=True))
    used_ref[0, 0] = jnp.broadcast_to(used, used_ref.shape[2:])


def nsa_cmp_topk(qn, kc, vc, gates, tq=256):
    _, b, s, _ = qn.shape
    tq = min(tq, s)
    ncp = kc.shape[2]
    assert s // SEL_BLOCK <= LANE
    topk = min(SEL_TOPK, s // SEL_BLOCK)
    kspec = pl.BlockSpec((1, NSA_GROUPS, ncp, LANE), lambda i, j: (i, 0, 0, 0))
    return pl.pallas_call(
        functools.partial(_cmp_topk_kernel, tq=tq, ncp=ncp, topk=topk),
        grid=(b, s // tq),
        in_specs=[pl.BlockSpec((NSA_HEADS, None, tq, LANE), lambda i, j: (0, i, j, 0)), kspec, kspec,
                  pl.BlockSpec((1, tq, LANE), lambda i, j: (i, j, 0))],
        out_specs=[pl.BlockSpec((1, tq, OUT_W), lambda i, j: (i, j, 0)),
                   pl.BlockSpec((1, NSA_GROUPS, tq, LANE), lambda i, j: (i, 0, j, 0)),
                   pl.BlockSpec((1, 1, 8, LANE), lambda i, j: (i, j, 0, 0))],
        out_shape=[jax.ShapeDtypeStruct((b, s, OUT_W), BF16), jax.ShapeDtypeStruct((b, NSA_GROUPS, s, LANE), BF16),
                   jax.ShapeDtypeStruct((b, s // tq, 8, LANE), F32)],
        scratch_shapes=[pltpu.VMEM((NSA_GROUPS, tq, LANE), F32)],
        compiler_params=_params(("parallel", "parallel")),
        name="nsa_cmp_topk",
    )(qn, kc, vc, gates)


def _nsa_write(o_ref, gate_ref, acc_ref, tq, branch):
    gates = gate_ref[0]
    heads = []
    for h in range(NSA_HEADS):
        g, n = divmod(h, NSA_HPG)
        col = branch * NSA_HEADS + h
        heads.append(_normalized(acc_ref[g, n * tq:(n + 1) * tq]) * gates[:, col:col + 1])
    _store_head_pairs(o_ref, slice(None), heads)


def _sel_attn_kernel(flags, q_ref, k_ref, v_ref, selm1_ref, gate_ref, o_ref, lhs_ref, m_ref, acc_ref, *, t, nq):
    b = pl.program_id(0)
    qi = pl.program_id(1)
    _init_state(m_ref, acc_ref)
    for g in range(NSA_GROUPS):
        lhs_ref[g, :, :LANE] = q_ref[g * NSA_HPG:(g + 1) * NSA_HPG].reshape(NSA_HPG * t, LANE)
        lhs_ref[g, :, LANE:] = jnp.concatenate([selm1_ref[0, g]] * NSA_HPG, axis=0)

    hpc = NSA_HPG // SEL_SPLIT
    half = hpc * t
    chains = [(g, i) for g in range(NSA_GROUPS) for i in range(SEL_SPLIT)]

    def update(ki, diagonal):
        rows = pl.ds(pl.multiple_of(ki * t, t), t)
        scores = [_dot_nt(lhs_ref[g, i * half:(i + 1) * half], k_ref[rows, 2 * g * LANE:(2 * g + 2) * LANE])
                  for g, i in chains]
        if diagonal:
            causal = (_iota((t, t), 1) <= _iota((t, t), 0))[None]
            scores = [jnp.where(causal, s.reshape(hpc, t, t), NEG).reshape(half, t) for s in scores]
        _flash_update_staged(scores, [v_ref[rows, g * LANE:(g + 1) * LANE] for g, _ in chains], m_ref, acc_ref,
                             [(g, slice(i * half, (i + 1) * half)) for g, i in chains])

    base = (b * nq + qi) * nq

    def body(ki, carry):
        @pl.when(flags[base + ki] > 0)
        def _():
            update(ki, False)
        return carry

    lax.fori_loop(0, qi, body, 0)
    update(qi, True)
    _nsa_write(o_ref, gate_ref, acc_ref, t, branch=1)


def nsa_selected(qn, ks, vs, selm1, used, gates):
    _, b, s, _ = qn.shape
    nq = used.shape[1]
    t = s // nq
    bpt = t // SEL_BLOCK
    flags = (used[:, :, 0].reshape(b, nq, LANE // bpt, bpt) > 0.5).any(axis=-1)[..., :nq]
    flags = flags.astype(jnp.int32).reshape(-1)
    imap_q = lambda bb, i, fl: (bb, i, 0)
    rows = NSA_HPG * t
    return pl.pallas_call(
        functools.partial(_sel_attn_kernel, t=t, nq=nq),
        grid_spec=pltpu.PrefetchScalarGridSpec(
            num_scalar_prefetch=1,
            grid=(b, nq),
            in_specs=[pl.BlockSpec((NSA_HEADS, None, t, LANE), lambda bb, i, fl: (0, bb, i, 0)),
                      pl.BlockSpec((None, s, 4 * LANE), lambda bb, i, fl: (bb, 0, 0)),
                      pl.BlockSpec((None, s, 2 * LANE), lambda bb, i, fl: (bb, 0, 0)),
                      pl.BlockSpec((1, NSA_GROUPS, t, LANE), lambda bb, i, fl: (bb, 0, i, 0)),
                      pl.BlockSpec((1, t, LANE), imap_q)],
            out_specs=pl.BlockSpec((1, t, OUT_W), imap_q),
            scratch_shapes=[pltpu.VMEM((NSA_GROUPS, rows, 2 * LANE), BF16), pltpu.VMEM((NSA_GROUPS, rows, LANE), F32),
                            pltpu.VMEM((NSA_GROUPS, rows, LANE), F32)],
        ),
        out_shape=jax.ShapeDtypeStruct((b, s, OUT_W), BF16),
        compiler_params=_params(("parallel", "arbitrary")),
        name="nsa_selected",
    )(flags, qn, ks, vs, selm1, gates)


def _win_attn_kernel(q_ref, kp_ref, kc_ref, vp_ref, vc_ref, gate_ref, o_ref, *, tq):
    qi = pl.program_id(1)
    hq = tq // 2
    span = WIN + hq
    row = _iota((hq, span), 0)
    col = _iota((hq, span), 1)
    diff = WIN + row - col
    band = (diff >= 0) & (diff < WIN)
    gates = gate_ref[0]
    chains = [(g, rh) for g in range(NSA_GROUPS) for rh in range(2)]
    scores = []
    for g, rh in chains:
        gl = slice(g * LANE, (g + 1) * LANE)
        q = q_ref[g * NSA_HPG:(g + 1) * NSA_HPG, rh * hq:(rh + 1) * hq].reshape(NSA_HPG * hq, LANE)
        s = jnp.concatenate([_dot_nt(q, kp_ref[0, rh * hq:, gl]), _dot_nt(q, kc_ref[0, :(rh + 1) * hq, gl])], axis=1)
        valid = (band & (col >= WIN - qi * tq - rh * hq))[None]
        scores.append(jnp.where(valid, s.reshape(NSA_HPG, hq, span), NEG).reshape(NSA_HPG * hq, span))
    tops = [jnp.max(s, axis=-1, keepdims=True) for s in scores]
    probs = [jnp.exp2(s - m).astype(BF16) for s, m in zip(scores, tops)]
    for (g, rh), p in zip(chains, probs):
        gl = slice(g * LANE, (g + 1) * LANE)
        n_prev = tq - rh * hq
        acc = _dot(p[:, :n_prev], vp_ref[0, rh * hq:, gl]) + _dot(p[:, n_prev:], vc_ref[0, :(rh + 1) * hq, gl])
        rows = slice(rh * hq, (rh + 1) * hq)
        cols = [2 * NSA_HEADS + g * NSA_HPG + n for n in range(NSA_HPG)]
        heads = [_normalized(acc[n * hq:(n + 1) * hq]) * gates[rows, cols[n]:cols[n] + 1] for n in range(NSA_HPG)]
        _store_head_pairs(o_ref, rows, heads, first_pair=g * NSA_HPG // 2)


def nsa_window(qn, kw, vw, gates, tq=512):
    _, b, s, _ = qn.shape
    assert tq == WIN and s % tq == 0
    prev = pl.BlockSpec((1, tq, 2 * LANE), lambda bb, i: (bb, jnp.maximum(i - 1, 0), 0))
    cur = pl.BlockSpec((1, tq, 2 * LANE), lambda bb, i: (bb, i, 0))
    return pl.pallas_call(
        functools.partial(_win_attn_kernel, tq=tq),
        grid=(b, s // tq),
        in_specs=[pl.BlockSpec((NSA_HEADS, None, tq, LANE), lambda bb, i: (0, bb, i, 0)), prev, cur, prev, cur,
                  pl.BlockSpec((1, tq, LANE), lambda bb, i: (bb, i, 0))],
        out_specs=pl.BlockSpec((1, tq, OUT_W), lambda bb, i: (bb, i, 0)),
        out_shape=jax.ShapeDtypeStruct((b, s, OUT_W), BF16),
        compiler_params=_params(("parallel", "parallel")),
        name="nsa_window",
    )(qn, kw, kw, vw, vw, gates)


def _odd_out_ln_kernel(x_ref, o0_ref, o1_ref, o2_ref, l0_ref, l1_ref, l2_ref, w_ref, g_ref, b_ref, o_ref,
                       o_scr, l_scr):
    tm = x_ref.shape[0]
    for gi, (src, lsrc) in enumerate(((o0_ref, l0_ref), (o1_ref, l1_ref), (o2_ref, l2_ref))):
        dil = DIL_PATTERNS[gi][1]
        for r in range(dil):
            rows = pl.ds(r, tm // dil, stride=dil) if dil > 1 else slice(None)
            for h in range(DIL_GH):
                o_scr[gi, h, rows, :] = src[r, :, h * DIL_DH:(h + 1) * DIL_DH].astype(F32)
            l_scr[gi, rows, :] = lsrc[r]
    lses = [l_scr[gi] for gi in range(DIL_GROUPS)]
    top = jnp.maximum(jnp.maximum(lses[0], lses[1]), lses[2])
    es = [jnp.exp(l - top) for l in lses]
    den = es[0] + es[1] + es[2]
    wts = [e / den for e in es]
    cols = []
    for h in range(DIL_GH):
        merged = None
        for gi in range(DIL_GROUPS):
            term = wts[gi][:, h:h + 1] * o_scr[gi, h]
            merged = term if merged is None else merged + term
        cols.append(merged.astype(BF16))
    mix = _dot(jnp.concatenate(cols, axis=1), w_ref[...])
    o_ref[...] = _layer_norm(ALPHA * x_ref[...] + mix, g_ref[...], b_ref[...])


def odd_out_ln(x2, outs, lses, w, g, b, tm=1024):
    m, d = x2.shape
    b_, _, s = outs[0].shape[0], None, outs[0].shape[1] * outs[0].shape[2]
    tm = min(tm, s)
    width = DIL_GH * DIL_DH
    row = pl.BlockSpec((None, tm, d), lambda bb, i: (bb, i, 0))
    cls = lambda gi, wd: pl.BlockSpec((None, DIL_PATTERNS[gi][1], tm // DIL_PATTERNS[gi][1], wd),
                                      lambda bb, i: (bb, 0, i, 0))
    out = pl.pallas_call(
        _odd_out_ln_kernel,
        grid=(b_, s // tm),
        in_specs=[row] + [cls(gi, width) for gi in range(DIL_GROUPS)] + [cls(gi, LANE) for gi in range(DIL_GROUPS)]
        + [_const_spec(w.shape), _const_spec((1, d)), _const_spec((1, d))],
        out_specs=row,
        out_shape=jax.ShapeDtypeStruct((b_, s, d), F32),
        scratch_shapes=[pltpu.VMEM((DIL_GROUPS, DIL_GH, tm, DIL_DH), F32), pltpu.VMEM((DIL_GROUPS, tm, LANE), F32)],
        compiler_params=_params(("parallel", "parallel")),
        name="odd_out_ln",
    )(x2.reshape(b_, s, d), *outs, *lses, w, g.reshape(1, d), b.reshape(1, d))
    return out.reshape(m, d)


def _odd_proj_kernel(x_ref, w_ref, o0_ref, o1_ref, o2_ref, xs_ref):
    tm, d_model = x_ref.shape
    width = DIL_GH * DIL_DH
    n_chunks = d_model // LANE
    for c in range(n_chunks):
        xs_ref[c] = x_ref[:, c * LANE:(c + 1) * LANE]
    for gi, o_ref in enumerate((o0_ref, o1_ref, o2_ref)):
        dil = DIL_PATTERNS[gi][1]
        if dil == 1:
            xg = x_ref[...]
        else:
            xg = jnp.concatenate(
                [jnp.concatenate([xs_ref[c, pl.ds(r, tm // dil, stride=dil), :] for r in range(dil)], axis=0)
                 for c in range(n_chunks)], axis=1)
        xg = xg.astype(BF16)
        for c in range(3):
            col = (gi * 3 + c) * width
            y = _dot(xg, w_ref[:, col:col + width])
            if c == 0:
                y = y * (DIL_SCALE * LOG2E)
            o_ref[:, :, c * width:(c + 1) * width] = y.astype(BF16).reshape(dil, tm // dil, width)


def odd_proj(x2, b, s, w_in, tm=512):
    m, d = x2.shape
    tm = min(tm, s)
    width = DIL_GH * DIL_DH
    w = w_in.reshape(d, 3, DIL_GROUPS, width).transpose(0, 2, 1, 3).reshape(d, 3 * DIL_GROUPS * width).astype(BF16)
    dils = [dil for _, dil in DIL_PATTERNS]
    return pl.pallas_call(
        _odd_proj_kernel,
        grid=(b, s // tm),
        in_specs=[pl.BlockSpec((None, tm, d), lambda bb, i: (bb, i, 0)), _const_spec(w.shape)],
        out_specs=[pl.BlockSpec((None, dil, tm // dil, 3 * width), lambda bb, i: (bb, 0, i, 0)) for dil in dils],
        out_shape=[jax.ShapeDtypeStruct((b, dil, s // dil, 3 * width), BF16) for dil in dils],
        scratch_shapes=[pltpu.VMEM((d // LANE, tm, LANE), F32)],
        compiler_params=_params(("parallel", "parallel")),
        name="odd_proj",
    )(x2.reshape(b, s, d), w)


def _dil_attn_kernel(slope_tab, pos0_tab, q_ref, kc_ref, kp_ref, vc_ref, vp_ref, pc_ref, pp_ref, pq_ref, o_ref,
                     lse_ref, *, group, tq):
    sub = DIL_SPAN
    b = pl.program_id(0)
    jt = pl.program_id(2)
    row = _iota((sub, 2 * sub), 0)
    col = _iota((sub, 2 * sub), 1)
    band = (col >= row) & (col <= row + sub)
    band_bias = jnp.where(band, 0.0, NEG)
    start_bias = jnp.where(band & (col >= sub), 0.0, NEG)
    pos0 = pos0_tab[b]
    lane = _iota((sub, LANE), 1)
    heads = range(DIL_GH)
    hs = [slice(h * DIL_DH, (h + 1) * DIL_DH) for h in heads]
    slopes = [slope_tab[group * DIL_GH + h] for h in heads]
    tiles = range(tq // sub)
    cur = [slice(i * sub, (i + 1) * sub) for i in tiles]
    biases, ks, vs = [], [], []
    for i in tiles:
        if i == 0:
            mask_bias = jnp.where(jt == 0, start_bias, band_bias)
            pk = jnp.concatenate([pp_ref[...], pc_ref[:, :sub]], axis=1)
            ks.append([jnp.concatenate([kp_ref[:, hs[h]], kc_ref[:sub, hs[h]]], axis=0) for h in heads])
            vs.append([jnp.concatenate([vp_ref[:, hs[h]], vc_ref[:sub, hs[h]]], axis=0) for h in heads])
        else:
            mask_bias = band_bias
            pk = pc_ref[:, (i - 1) * sub:(i + 1) * sub]
            ks.append([kc_ref[(i - 1) * sub:(i + 1) * sub, hs[h]] for h in heads])
            vs.append([vc_ref[(i - 1) * sub:(i + 1) * sub, hs[h]] for h in heads])
        dpos = pk - pos0
        biases.append([mask_bias + slopes[h] * dpos for h in heads])
    chains = [(i, h) for i in tiles for h in heads]
    ss = [_dot_nt(q_ref[cur[i], hs[h]], ks[i][h]) + biases[i][h] for i, h in chains]
    ms = [jnp.max(s, axis=-1, keepdims=True) for s in ss]
    ps = [jnp.exp2(s - m) for s, m in zip(ss, ms)]
    dens = [jnp.sum(p, axis=-1, keepdims=True) for p in ps]
    for n, (i, h) in enumerate(chains):
        o_ref[cur[i], hs[h]] = (_dot(ps[n].astype(BF16), vs[i][h]) / dens[n]).astype(BF16)
    for i in tiles:
        dq = (pq_ref[cur[i], :] - pos0) * LN2
        lse_tile = jnp.zeros((sub, LANE), F32)
        for h in heads:
            n = i * DIL_GH + h
            lse_tile = jnp.where(lane == h, ms[n] * LN2 + jnp.log(dens[n]) - slopes[h] * dq, lse_tile)
        lse_ref[cur[i], :] = lse_tile


def dilated_group_attention(qkv, pos_f, group, tq=1024):
    b, d, sd, _ = qkv.shape
    w, dil = DIL_PATTERNS[group]
    assert w // dil == DIL_SPAN and dil == d
    tq = min(tq, sd)
    sub = DIL_SPAN
    width = DIL_GH * DIL_DH
    n_slopes = DIL_GROUPS * DIL_GH
    slopes = 2.0 ** (-ALIBI_MAX_BIAS * jnp.arange(1, n_slopes + 1, dtype=F32) / n_slopes) * LOG2E
    pos0 = pos_f[:, 0]
    posc = pos_f.reshape(b, sd, d).transpose(0, 2, 1).reshape(b, d, 1, sd)
    r = tq // sub
    cur = lambda c: pl.BlockSpec((None, None, tq, width), lambda bb, rr, j, *_: (bb, rr, j, c))
    prev = lambda c: pl.BlockSpec((None, None, sub, width),
                                  lambda bb, rr, j, *_: (bb, rr, jnp.maximum(j * r - 1, 0), c))
    return pl.pallas_call(
        functools.partial(_dil_attn_kernel, group=group, tq=tq),
        grid_spec=pltpu.PrefetchScalarGridSpec(
            num_scalar_prefetch=2,
            grid=(b, d, sd // tq),
            in_specs=[cur(0), cur(1), prev(1), cur(2), prev(2),
                      pl.BlockSpec((None, None, 1, tq), lambda bb, rr, j, *_: (bb, rr, 0, j)),
                      pl.BlockSpec((None, None, 1, sub), lambda bb, rr, j, *_: (bb, rr, 0, jnp.maximum(j * r - 1, 0))),
                      pl.BlockSpec((None, None, tq, 1), lambda bb, rr, j, *_: (bb, rr, j, 0))],
            out_specs=[pl.BlockSpec((None, None, tq, width), lambda bb, rr, j, *_: (bb, rr, j, 0)),
                       pl.BlockSpec((None, None, tq, LANE), lambda bb, rr, j, *_: (bb, rr, j, 0))],
        ),
        out_shape=[jax.ShapeDtypeStruct((b, d, sd, width), BF16), jax.ShapeDtypeStruct((b, d, sd, LANE), F32)],
        compiler_params=_params(("parallel", "parallel", "parallel")),
        name=f"dilated_attention_g{group}",
    )(slopes, pos0, qkv, qkv, qkv, qkv, qkv, posc, posc, posc.reshape(b, d, sd, 1))


def even_mixer(x2, b, s, pos_f, w_in, q_norm_g, kv_norm_g, w_uq, w_uk, w_uv,
               cmp_pos, cmp_k_w1, cmp_k_w2, cmp_v_w1, cmp_v_w2, w_out, ln_g, ln_b):
    m = b * s
    half = MLA_ROPE // 2
    inv = ROPE_THETA ** (-jnp.arange(half, dtype=F32) / half)
    ang = (pos_f[..., None] * inv).reshape(m, half)
    dpos = (pos_f - pos_f[:, :1]) * LOG2E
    ncp = s // CMP_STRIDE
    dpos_cend = jnp.pad(dpos[:, CMP_LEN - 1::CMP_STRIDE], ((0, 0), (0, 1))).reshape(b, ncp, 1)
    tok_table = jnp.concatenate([jnp.cos(ang), jnp.sin(ang), dpos.reshape(m, 1),
                                 jnp.zeros((m, LANE - MLA_ROPE - 1), F32)], axis=1)

    qm, km, vm, qn, kvc, ks, kw, vs, vw, gates = even_proj(
        x2, tok_table, s, w_in, q_norm_g, kv_norm_g, w_uq, w_uk, w_uv)
    o_mla = mla_attention(qm.reshape(b, s, HW), km.reshape(b, s, HW), vm.reshape(b, s, HW))

    kvc = kvc.reshape(2, b, ncp, CMP_STRIDE * LANE)
    kc, vc = nsa_compress(kvc, cmp_pos, (cmp_k_w1, cmp_v_w1), (cmp_k_w2, cmp_v_w2), dpos_cend)
    qn = qn.reshape(NSA_HEADS, b, s, LANE)
    gates = gates.reshape(b, s, LANE)
    o_c, selm1, used = nsa_cmp_topk(qn, kc, vc, gates)
    o_s = nsa_selected(qn, ks.reshape(b, s, 4 * LANE), vs.reshape(b, s, 2 * LANE), selm1, used, gates)
    o_w = nsa_window(qn, kw.reshape(b, s, 2 * LANE), vw.reshape(b, s, 2 * LANE), gates)

    outs = [o.reshape(m, OUT_W) for o in (o_mla, o_c, o_s, o_w)]
    return outs, [w_out.astype(BF16)], ln_g, ln_b


def odd_mixer_ln(x2, b, s, pos_f, w_in, w_out, ln_g, ln_b):
    qkvs = odd_proj(x2, b, s, w_in)
    parts = [dilated_group_attention(qkvs[g], pos_f, g) for g in range(DIL_GROUPS)]
    return odd_out_ln(x2, [p[0] for p in parts], [p[1] for p in parts], w_out.astype(BF16), ln_g, ln_b)


def kernel(x, positions, ln1_g, ln1_b, ffn1_w_gate, ffn1_w_up, ffn1_w_down, mix_in_even, mla_q_norm, mla_kv_norm, mla_w_uq, mla_w_uk, mla_w_uv, nsa_cmp_pos, nsa_cmp_k_w1, nsa_cmp_k_w2, nsa_cmp_v_w1, nsa_cmp_v_w2, mix_out_even, mix_in_odd, mix_out_odd, ln2_g, ln2_b, ffn2_w_gate, ffn2_w_up, ffn2_w_down, ln3_g, ln3_b):
    b, s, d = x.shape
    x2 = x.reshape(b * s, d)
    pos_f = positions.astype(F32)
    for i in range(DEPTH):
        j = i // 2
        x2 = ffn_ln(x2, ffn1_w_gate[i].astype(BF16), ffn1_w_up[i].astype(BF16), ffn1_w_down[i].astype(BF16),
                    ln1_g[i], ln1_b[i])
        mix = None
        if i % 2 == 0:
            mix = even_mixer(x2, b, s, pos_f, mix_in_even[j], mla_q_norm[j], mla_kv_norm[j], mla_w_uq[j],
                                mla_w_uk[j], mla_w_uv[j], nsa_cmp_pos[j], nsa_cmp_k_w1[j], nsa_cmp_k_w2[j],
                                nsa_cmp_v_w1[j], nsa_cmp_v_w2[j], mix_out_even[j], ln2_g[i], ln2_b[i])
        else:
            x2 = odd_mixer_ln(x2, b, s, pos_f, mix_in_odd[j], mix_out_odd[j], ln2_g[i], ln2_b[i])
        x2 = ffn_ln(x2, ffn2_w_gate[i].astype(BF16), ffn2_w_up[i].astype(BF16), ffn2_w_down[i].astype(BF16),
                    ln3_g[i], ln3_b[i], mix=mix)
    return x2.reshape(b, s, d)
```
